```python
import math
import jax, jax.numpy as jnp
from jax import lax
import numpy as np

D_MODEL = 1024
BATCH = 32
SEQ = 256
DEPTH = 2
DEC_BATCH = 8
DEC_SEQ = 4096
PAST_LEN = 256

GRID_W = 64
N_EVEN = (DEPTH + 1) // 2
N_ODD = DEPTH // 2
DA_HEADS = 4
DA_HD = 64
DA_VD = 2 * DA_HD
HG_HEADS = 4
HG_DK = 128
HG_DV = 128
HG_W = HG_HEADS * HG_DK
HG_CHUNK = 64
Q_BLOCK = 128
ROPE_THETA = 10000.0
ROPE_HALF = DA_HD // 4
CM_WIDTH = D_MODEL
CM_CHUNK = 128
CM_GROUPS = 8
CM_GD = CM_WIDTH // CM_GROUPS
N_EXPERTS = 32
TOP_K = 4
D_EXPERT = D_MODEL
SWIGLU_ALPHA = 1.702
SWIGLU_LIMIT = 7.0
MOE_BLOCK = 128
EPS = 1e-6
EVEN_SPLITS = (DA_HEADS * 2 * DA_HD, DA_HEADS * 2 * DA_HD, DA_HEADS * DA_VD,
               HG_W, HG_W, HG_W, HG_HEADS * HG_DV, HG_HEADS * HG_DV)
EVEN_IN = sum(EVEN_SPLITS)
MIX_OUT = DA_HEADS * DA_VD + HG_HEADS * HG_DV

kernel_name = 'diff_hgrn2_gmlp_moe_dit_step'


def _split(a, sizes):
    idx = np.cumsum(sizes)[:-1].tolist()
    return jnp.split(a, idx, axis=-1)


def rms_norm(x, g):
    xf = x.astype(jnp.float32)
    y = xf * lax.rsqrt(jnp.mean(xf * xf, axis=-1, keepdims=True) + EPS)
    return (y * g.astype(jnp.float32)).astype(x.dtype)


def layer_norm(x):
    xf = x.astype(jnp.float32)
    mu = jnp.mean(xf, axis=-1, keepdims=True)
    var = jnp.mean(jnp.square(xf - mu), axis=-1, keepdims=True)
    return ((xf - mu) * lax.rsqrt(var + EPS)).astype(x.dtype)


def modulation(cond, w, b):
    m = jax.nn.silu(cond) @ w + b
    return jnp.split(m[:, None, :], 6, axis=-1)


def modulate(h, shift, scale):
    return h * (1 + scale) + shift


def axial_rope_tables(rows):
    r, cl = jnp.meshgrid(jnp.arange(rows), jnp.arange(GRID_W), indexing='ij')
    pos_r = r.reshape(-1).astype(jnp.float32)
    pos_c = cl.reshape(-1).astype(jnp.float32)
    inv = 1.0 / (ROPE_THETA ** (jnp.arange(ROPE_HALF, dtype=jnp.float32) / ROPE_HALF))
    ang_r = pos_r[:, None] * inv
    ang_c = pos_c[:, None] * inv
    return (jnp.cos(ang_r), jnp.sin(ang_r), jnp.cos(ang_c), jnp.sin(ang_c))


def _rotate(x, cos, sin):
    x1, x2 = jnp.split(x, 2, axis=-1)
    return jnp.concatenate([x1 * cos - x2 * sin, x2 * cos + x1 * sin], axis=-1)


def apply_axial_rope(x, tabs):
    cr, sr, cc, sc = [t[:, None, :].astype(x.dtype) for t in tabs]
    xr, xc = jnp.split(x, 2, axis=-1)
    return jnp.concatenate([_rotate(xr, cr, sr), _rotate(xc, cc, sc)], axis=-1)


def diff_attention(q, k, v, lam):
    B, H, Lq = q.shape[:3]
    nb = Lq // Q_BLOCK
    scale = DA_HD ** -0.5
    qb = q.reshape(B, H, nb, Q_BLOCK, 2, DA_HD).transpose(2, 0, 1, 3, 4, 5)

    def block(qi):
        s = jnp.einsum('bhqjd,bhkjd->bhjqk', qi, k).astype(jnp.float32) * scale
        a = jax.nn.softmax(s, axis=-1)
        p = a[:, :, 0] - lam * a[:, :, 1]
        return jnp.einsum('bhqk,bhkv->bhqv', p.astype(v.dtype), v)

    o = lax.map(block, qb)
    return o.transpose(1, 2, 0, 3, 4).reshape(B, H, Lq, DA_VD)


def hgrn_scan(q, k, v, log_f, s0):
    B, H, L, DK = q.shape
    n = L // HG_CHUNK
    f32 = jnp.float32

    def to_chunks(a):
        return a.astype(f32).reshape(B, H, n, HG_CHUNK, a.shape[-1]).transpose(2, 0, 1, 3, 4)

    mask = jnp.tril(jnp.ones((HG_CHUNK, HG_CHUNK), dtype=bool))[:, :, None]

    def step(S, inp):
        qc, kc, vc, lf = inp
        G = jnp.cumsum(lf, axis=-2)
        o_inter = jnp.einsum('bhtk,bhkv->bhtv', qc * jnp.exp(G), S)
        rel = jnp.where(mask, G[:, :, :, None, :] - G[:, :, None, :, :], -jnp.inf)
        A = jnp.einsum('bhtk,bhtsk,bhsk->bhts', qc, jnp.exp(rel), kc)
        o = o_inter + jnp.einsum('bhts,bhsv->bhtv', A, vc)
        g_last = G[:, :, -1:, :]
        S = jnp.exp(g_last[:, :, 0, :, None]) * S + jnp.einsum('bhsk,bhsv->bhkv', kc * jnp.exp(g_last - G), vc)
        return S, o

    S, o = lax.scan(step, s0.astype(f32), (to_chunks(q), to_chunks(k), to_chunks(v), to_chunks(log_f)))
    o = o.transpose(1, 2, 0, 3, 4).reshape(B, H, L, v.shape[-1])
    return o, S


def hgrn_bidir(q, i, zf, zb, lb_f, lb_b, s0_f, s0_b):
    def gate(z, lb):
        f = lb + (1.0 - lb) * jax.nn.sigmoid(z.astype(jnp.float32))
        return jnp.log(f), 1.0 - f

    logf_f, k_f = gate(zf, lb_f)
    logf_b, k_b = gate(zb, lb_b)
    o_f, S_f = hgrn_scan(q, k_f, i, logf_f, s0_f)
    flip = lambda a: jnp.flip(a, axis=2)
    o_b, S_b = hgrn_scan(flip(q), flip(k_b), flip(i), flip(logf_b), s0_b)
    return o_f + flip(o_b), S_f, S_b


def even_mixer(h, w_in, w_out, lam_p, subln_g, hg_norm_g, lb_f, lb_b, lam_init,
               rope_tabs, ctx_k, ctx_v, s0_f, s0_b):
    B, L, _ = h.shape
    proj = h @ w_in
    dq, dk, dv, hq, hzf, hzb, hi, hg = _split(proj, EVEN_SPLITS)
    q = dq.reshape(B, L, DA_HEADS, 2, DA_HD).transpose(0, 2, 1, 3, 4)
    k = dk.reshape(B, L, DA_HEADS, 2, DA_HD).transpose(0, 2, 1, 3, 4)
    v = dv.reshape(B, L, DA_HEADS, DA_VD).transpose(0, 2, 1, 3)
    k_out = k.reshape(B, DA_HEADS, L, 2 * DA_HD)
    if rope_tabs is not None:
        q = apply_axial_rope(q, rope_tabs)
        k = apply_axial_rope(k, rope_tabs)
    if ctx_k is not None:
        Lc = ctx_k.shape[2]
        k_all = jnp.concatenate([ctx_k.reshape(B, DA_HEADS, Lc, 2, DA_HD).astype(k.dtype), k], axis=2)
        v_all = jnp.concatenate([ctx_v.astype(v.dtype), v], axis=2)
    else:
        k_all, v_all = k, v
    lp = lam_p.astype(jnp.float32)
    lam = jnp.exp(jnp.sum(lp[0] * lp[1])) - jnp.exp(jnp.sum(lp[2] * lp[3])) + lam_init
    o_da = diff_attention(q, k_all, v_all, lam)
    o_da = (rms_norm(o_da, subln_g) * (1.0 - lam_init)).transpose(0, 2, 1, 3).reshape(B, L, DA_HEADS * DA_VD)
    heads = lambda a, d: a.reshape(B, L, HG_HEADS, d).transpose(0, 2, 1, 3)
    o_h, S_f, S_b = hgrn_bidir(heads(jax.nn.silu(hq), HG_DK), heads(hi, HG_DV), heads(hzf, HG_DK),
                               heads(hzb, HG_DK), lb_f, lb_b, s0_f, s0_b)
    o_h = rms_norm(o_h.astype(h.dtype).transpose(0, 2, 1, 3), hg_norm_g)
    o_h = (o_h * jax.nn.silu(hg.reshape(B, L, HG_HEADS, HG_DV))).reshape(B, L, HG_HEADS * HG_DV)
    out = jnp.concatenate([o_da, o_h], axis=-1) @ w_out
    return out, k_out, v, S_f, S_b


def odd_mixer(h, w_in, sgu_w, sgu_b, w_out):
    B, L, _ = h.shape
    z = jax.nn.gelu(h @ w_in, approximate=False)
    u, v = jnp.split(z, 2, axis=-1)
    v = layer_norm(v)
    n = L // CM_CHUNK
    vb = v.reshape(B, n, CM_CHUNK, CM_GROUPS, CM_GD)
    sv = jnp.einsum('gpq,bnqgc->bnpgc', sgu_w, vb) + sgu_b.T[None, None, :, :, None]
    return (u * sv.reshape(B, L, CM_WIDTH)) @ w_out


def moe(h, router_w, router_b, w1, b1, w2, b2):
    shp = h.shape
    x = h.reshape(-1, D_MODEL)
    T = x.shape[0]
    logits = (x @ router_w + router_b).astype(jnp.float32)
    top_v, top_i = lax.top_k(logits, TOP_K)
    gates = jax.nn.softmax(top_v, axis=-1)
    n_asg = T * TOP_K
    e_flat = top_i.reshape(-1)
    tok_flat = jnp.arange(n_asg, dtype=jnp.int32) // TOP_K
    order = jnp.argsort(e_flat)
    e_sorted = e_flat[order]
    counts = jnp.bincount(e_flat, length=N_EXPERTS)
    padded = (counts + MOE_BLOCK - 1) // MOE_BLOCK * MOE_BLOCK
    start = jnp.cumsum(counts) - counts
    pend = jnp.cumsum(padded)
    pstart = pend - padded
    dest = pstart[e_sorted] + (jnp.arange(n_asg, dtype=jnp.int32) - start[e_sorted])
    n_blocks = -(-n_asg // MOE_BLOCK) + N_EXPERTS
    P = n_blocks * MOE_BLOCK
    buf_tok = jnp.zeros((P,), jnp.int32).at[dest].set(tok_flat[order])
    buf_gate = jnp.zeros((P,), jnp.float32).at[dest].set(gates.reshape(-1)[order])
    blk_exp = jnp.minimum(jnp.searchsorted(pend, jnp.arange(n_blocks) * MOE_BLOCK, side='right'), N_EXPERTS - 1)

    def run_block(args):
        toks, e = args
        hb = x[toks] @ w1[e] + b1[e]
        a, lin = jnp.split(hb, 2, axis=-1)
        a = jnp.minimum(a, SWIGLU_LIMIT)
        lin = jnp.clip(lin, -SWIGLU_LIMIT, SWIGLU_LIMIT)
        act = a * jax.nn.sigmoid(SWIGLU_ALPHA * a) * (lin + 1)
        return act @ w2[e] + b2[e]

    y = lax.map(run_block, (buf_tok.reshape(n_blocks, MOE_BLOCK), blk_exp)).reshape(P, D_MODEL)
    y = y * buf_gate[:, None].astype(y.dtype)
    out = jnp.zeros_like(x).at[buf_tok].add(y.astype(x.dtype))
    return out.reshape(shp)


def setup_inputs(seed: int = 0) -> dict:
    key = jax.random.key(seed)
    ks = iter(jax.random.split(key, 32))
    nrm = lambda shape, s=1.0: jax.random.normal(next(ks), shape, jnp.float32) * s
    return {
        'x_prompt': nrm((BATCH, SEQ, D_MODEL)),
        'x_sample': nrm((DEC_BATCH, DEC_SEQ, D_MODEL)),
        'c': nrm((DEC_BATCH, D_MODEL)),
        'cache_k': nrm((DEC_BATCH, N_EVEN, DA_HEADS, PAST_LEN, 2 * DA_HD)),
        'cache_v': nrm((DEC_BATCH, N_EVEN, DA_HEADS, PAST_LEN, DA_VD)),
        'state_hgrn': nrm((DEC_BATCH, N_EVEN, 2, HG_HEADS, HG_DK, HG_DV)),
        'c_ctx': nrm((D_MODEL,)),
        'mod_w': nrm((DEPTH, D_MODEL, 6 * D_MODEL), 0.5 * D_MODEL ** -0.5),
        'mod_b': nrm((DEPTH, 6 * D_MODEL), 0.02),
        'norm_g': 1.0 + nrm((DEPTH, 2, D_MODEL), 0.02),
        'final_norm_g': 1.0 + nrm((D_MODEL,), 0.02),
        'w_in_even': nrm((N_EVEN, D_MODEL, EVEN_IN), D_MODEL ** -0.5),
        'w_out_even': nrm((N_EVEN, MIX_OUT, D_MODEL), MIX_OUT ** -0.5),
        'da_lambda': nrm((N_EVEN, 4, DA_HD), 0.1),
        'da_subln_g': 1.0 + nrm((N_EVEN, DA_VD), 0.02),
        'hg_norm_g': 1.0 + nrm((N_EVEN, HG_DV), 0.02),
        'hg_lb': nrm((2, DEPTH + 1, HG_W), 0.5),
        'w_in_odd': nrm((N_ODD, D_MODEL, 2 * CM_WIDTH), D_MODEL ** -0.5),
        'sgu_w': nrm((N_ODD, CM_GROUPS, CM_CHUNK, CM_CHUNK), CM_CHUNK ** -0.5),
        'sgu_b': 1.0 + nrm((N_ODD, CM_GROUPS, CM_CHUNK), 0.02),
        'w_out_odd': nrm((N_ODD, CM_WIDTH, D_MODEL), CM_WIDTH ** -0.5),
        'router_w': nrm((DEPTH, D_MODEL, N_EXPERTS), D_MODEL ** -0.5),
        'router_b': nrm((DEPTH, N_EXPERTS), 0.01),
        'ex_w1': nrm((DEPTH, N_EXPERTS, D_MODEL, 2 * D_EXPERT), D_MODEL ** -0.5),
        'ex_b1': nrm((DEPTH, N_EXPERTS, 2 * D_EXPERT), 0.02),
        'ex_w2': nrm((DEPTH, N_EXPERTS, D_EXPERT, D_MODEL), D_EXPERT ** -0.5),
        'ex_b2': nrm((DEPTH, N_EXPERTS, D_MODEL), 0.02),
    }


def reference(x_prompt, x_sample, c, cache_k, cache_v, state_hgrn, c_ctx, mod_w, mod_b, norm_g,
              final_norm_g, w_in_even, w_out_even, da_lambda, da_subln_g, hg_norm_g, hg_lb,
              w_in_odd, sgu_w, sgu_b, w_out_odd, router_w, router_b, ex_w1, ex_b1, ex_w2, ex_b2):
    rows = x_sample.shape[1] // GRID_W
    rope_tabs = axial_rope_tables(rows)
    lb = jnp.cumsum(jax.nn.softmax(hg_lb.astype(jnp.float32), axis=1), axis=1)
    cond_ctx = c_ctx[None, :]
    zero_state = jnp.zeros((x_prompt.shape[0], HG_HEADS, HG_DK, HG_DV), jnp.float32)
    xp, xs = x_prompt, x_sample
    new_k, new_v, new_s = [], [], []
    for l in range(DEPTH):
        sh1_p, sc1_p, g1_p, sh2_p, sc2_p, g2_p = modulation(cond_ctx, mod_w[l], mod_b[l])
        sh1_s, sc1_s, g1_s, sh2_s, sc2_s, g2_s = modulation(c, mod_w[l], mod_b[l])
        hp = modulate(rms_norm(xp, norm_g[l, 0]), sh1_p, sc1_p)
        hs = modulate(rms_norm(xs, norm_g[l, 0]), sh1_s, sc1_s)
        if l % 2 == 0:
            e = l // 2
            lam_init = 0.8 - 0.6 * math.exp(-0.3 * l)
            lb_f = lb[0, l].reshape(HG_HEADS, 1, HG_DK)
            lb_b = lb[1, l].reshape(HG_HEADS, 1, HG_DK)
            shared = (w_in_even[e], w_out_even[e], da_lambda[e], da_subln_g[e], hg_norm_g[e], lb_f, lb_b, lam_init)
            mp, kc, vc, sf, sb = even_mixer(hp, *shared, None, None, None, zero_state, zero_state)
            ms, _, _, _, _ = even_mixer(hs, *shared, rope_tabs, cache_k[:, e], cache_v[:, e],
                                        state_hgrn[:, e, 0], state_hgrn[:, e, 1])
            new_k.append(kc)
            new_v.append(vc)
            new_s.append(jnp.stack([sf, sb], axis=1))
        else:
            o = l // 2
            mp = odd_mixer(hp, w_in_odd[o], sgu_w[o], sgu_b[o], w_out_odd[o])
            ms = odd_mixer(hs, w_in_odd[o], sgu_w[o], sgu_b[o], w_out_odd[o])
        xp = xp + g1_p * mp
        xs = xs + g1_s * ms
        hp = modulate(rms_norm(xp, norm_g[l, 1]), sh2_p, sc2_p)
        hs = modulate(rms_norm(xs, norm_g[l, 1]), sh2_s, sc2_s)
        xp = xp + g2_p * moe(hp, router_w[l], router_b[l], ex_w1[l], ex_b1[l], ex_w2[l], ex_b2[l])
        xs = xs + g2_s * moe(hs, router_w[l], router_b[l], ex_w1[l], ex_b1[l], ex_w2[l], ex_b2[l])
    y_prompt = rms_norm(xp, final_norm_g)
    y_sample = rms_norm(xs, final_norm_g)
    new_cache_k = jnp.stack(new_k, axis=1)
    new_cache_v = jnp.stack(new_v, axis=1)
    new_state_hgrn = jnp.stack(new_s, axis=1)
    return (y_prompt, y_sample, new_cache_k, new_cache_v, new_state_hgrn)
```

```python
import functools
import math

import jax
import jax.numpy as jnp
from jax import lax
from jax.experimental import pallas as pl
from jax.experimental.pallas import tpu as pltpu

F32 = jnp.float32
BF16 = jnp.bfloat16
I32 = jnp.int32

D_MODEL = 1024
DEPTH = 2
GRID_W = 64
DA_HEADS = 4
DA_HD = 64
HG_HEADS = 4
HG_DK = 128
ROPE_THETA = 10000.0
ROPE_HALF = DA_HD // 4
CM_CHUNK = 128
CM_GROUPS = 8
N_EXPERTS = 32
TOP_K = 4
SWIGLU_ALPHA = 1.702
SWIGLU_LIMIT = 7.0
EPS = 1e-6
EVEN_IN = 4096
HEAD_W = 128

TM = 256
HG_C = 64
HG_SB = 16
MOE_BM = 256
DEST_ROWS = 8
MOD_ROWS = 16
V7X_VMEM_LIMIT = 56 * 1024 * 1024

NT_DIMS = (((1,), (1,)), ((), ()))
TN_DIMS = (((0,), (0,)), ((), ()))


def _params(sem):
    return pltpu.CompilerParams(dimension_semantics=sem, vmem_limit_bytes=V7X_VMEM_LIMIT)


def _sigmoid(x):
    return 1.0 / (1.0 + jnp.exp(-x))


def _silu(x):
    return x * _sigmoid(x)


def _rms(x, g):
    ms = jnp.mean(x * x, axis=-1, keepdims=True)
    return x * lax.rsqrt(ms + EPS) * g


def _norm_mod(x, g, shift, scale):
    return _rms(x, g) * (1.0 + scale) + shift


def _mod_kernel(c_ref, w_ref, b_ref, o_ref):
    s = _silu(c_ref[...])
    o_ref[...] = jnp.dot(s.astype(BF16), w_ref[...].astype(BF16), preferred_element_type=F32) + b_ref[...]


def _modulation(cond, mod_w, mod_b):
    depth = mod_w.shape[0]
    m = pl.pallas_call(
        _mod_kernel,
        out_shape=jax.ShapeDtypeStruct((depth, 6, MOD_ROWS, D_MODEL), F32),
        grid=(depth, 6),
        in_specs=[
            pl.BlockSpec((MOD_ROWS, D_MODEL), lambda l, j: (0, 0)),
            pl.BlockSpec((None, D_MODEL, D_MODEL), lambda l, j: (l, 0, j)),
            pl.BlockSpec((None, 1, D_MODEL), lambda l, j: (l, 0, j)),
        ],
        out_specs=pl.BlockSpec((None, None, MOD_ROWS, D_MODEL), lambda l, j: (l, j, 0, 0)),
        compiler_params=_params(("arbitrary", "arbitrary")),
        name="modulation",
    )(cond, mod_w, mod_b.reshape(depth, 1, 6 * D_MODEL))
    return m.transpose(0, 2, 1, 3)[:, :, :, None, :]


def _mod_spec(n_prompt, l_sample, ctx_row):
    def index(i):
        t = i * TM
        return (jnp.where(t < n_prompt, ctx_row, (t - n_prompt) // l_sample), 0, 0, 0)
    return pl.BlockSpec((None, 6, 1, D_MODEL), index)


def _inproj_kernel(x_ref, m_ref, g_ref, w_ref, o_ref):
    h = _norm_mod(x_ref[...], g_ref[...], m_ref[0], m_ref[1])
    o_ref[...] = jnp.dot(h.astype(BF16), w_ref[...], preferred_element_type=F32)


def _inproj(x, m, g, w, mod_spec):
    nt, n_out = x.shape[0], w.shape[1]
    return pl.pallas_call(
        _inproj_kernel,
        out_shape=jax.ShapeDtypeStruct((nt, n_out), F32),
        grid=(nt // TM,),
        in_specs=[
            pl.BlockSpec((TM, D_MODEL), lambda i: (i, 0)),
            mod_spec,
            pl.BlockSpec((1, D_MODEL), lambda i: (0, 0)),
            pl.BlockSpec((D_MODEL, n_out), lambda i: (0, 0)),
        ],
        out_specs=pl.BlockSpec((TM, n_out), lambda i: (i, 0)),
        compiler_params=_params(("arbitrary",)),
        name="even_inproj",
    )(x, m, g, w)


def _rope(x, cos, sin_signed):
    lane = lax.broadcasted_iota(I32, x.shape, 1)
    first = ((lane // ROPE_HALF) % 2) == 0
    partner = jnp.where(first, pltpu.roll(x, HEAD_W - ROPE_HALF, 1), pltpu.roll(x, ROPE_HALF, 1))
    return x * cos + partner * sin_signed


def _attn_kernel(*refs, rope, ctx, bq, chunks, lam_init):
    it = iter(refs)
    lam_ref, g_ref, q_ref, k_ref, v_ref = (next(it) for _ in range(5))
    if rope:
        cq_ref, sq_ref, ck_ref, sk_ref = (next(it) for _ in range(4))
    if ctx:
        kc_ref, vc_ref = next(it), next(it)
    next(it)
    o_ref, kt_ref, vx_ref = next(it), next(it), next(it)
    l_new = k_ref.shape[0]
    l_ctx = kc_ref.shape[0] if ctx else 0

    @pl.when(pl.program_id(2) == 0)
    def _prepare_keys():
        if ctx:
            kt_ref[:, 0:l_ctx] = kc_ref[...].T.astype(BF16)
            vx_ref[0:l_ctx, 0:HEAD_W] = vc_ref[...].astype(BF16)
        step = min(512, l_new)
        for c0 in range(0, l_new, step):
            k = k_ref[c0:c0 + step, :]
            if rope:
                k = _rope(k, ck_ref[c0:c0 + step, :], sk_ref[c0:c0 + step, :])
            kt_ref[:, l_ctx + c0:l_ctx + c0 + step] = k.T.astype(BF16)
            vx_ref[l_ctx + c0:l_ctx + c0 + step, 0:HEAD_W] = v_ref[c0:c0 + step, :].astype(BF16)
        vx_ref[:, HEAD_W:2 * HEAD_W] = jnp.ones((l_ctx + l_new, HEAD_W), BF16)

    q = q_ref[...]
    if rope:
        q = _rope(q, cq_ref[...], sq_ref[...])
    q = q * (DA_HD ** -0.5)
    lane = lax.broadcasted_iota(I32, q.shape, 1)
    lo = lane < DA_HD
    zero = jnp.zeros_like(q)
    qs = jnp.concatenate([jnp.where(lo, q, zero), jnp.where(lo, zero, q)], axis=0).astype(BF16)

    m = jnp.full((2 * bq, 1), -jnp.inf, F32)
    l = jnp.zeros((2 * bq, 1), F32)
    acc = jnp.zeros((2 * bq, HEAD_W), F32)
    for c0, cs in chunks:
        s = jnp.dot(qs, kt_ref[:, c0:c0 + cs], preferred_element_type=F32)
        mn = jnp.maximum(m, jnp.max(s, axis=-1, keepdims=True))
        alpha = jnp.exp(m - mn)
        p = jnp.exp(s - mn).astype(BF16)
        pv = jnp.dot(p, vx_ref[c0:c0 + cs, :], preferred_element_type=F32)
        acc = alpha * acc + pv[:, 0:HEAD_W]
        l = alpha * l + pv[:, HEAD_W:HEAD_W + 1]
        m = mn
    o2 = acc / l
    lp = lam_ref[...]
    lam = (jnp.exp(jnp.sum(lp[0:1] * lp[1:2], axis=-1, keepdims=True))
           - jnp.exp(jnp.sum(lp[2:3] * lp[3:4], axis=-1, keepdims=True)) + lam_init)
    o = o2[0:bq] - lam * o2[bq:2 * bq]
    o_ref[...] = _rms(o, g_ref[...]) * (1.0 - lam_init)


def _attn_chunks(l_ctx, l_new):
    total = l_ctx + l_new
    if total <= 1280:
        return ((0, total),)
    chunks, c0 = [], 0
    first = l_ctx + 1024
    chunks.append((0, first))
    c0 = first
    while c0 < total:
        cs = min(1024, total - c0)
        chunks.append((c0, cs))
        c0 += cs
    return tuple(chunks)


def _diff_attention(proj, mix, lam_p, subln_g, *, row0, n_batch, seq, lam_init, rope_tabs=None,
                    ctx_k=None, ctx_v=None, layer_e=0):
    rope, ctx = rope_tabs is not None, ctx_k is not None
    bq = min(256, seq)
    nq = seq // bq
    l_ctx = ctx_k.shape[3] if ctx else 0
    qb0, kb0 = row0 // bq, row0 // seq
    in_specs = [
        pl.BlockSpec((4, DA_HD), lambda b, h, i: (0, 0)),
        pl.BlockSpec((1, HEAD_W), lambda b, h, i: (0, 0)),
        pl.BlockSpec((bq, HEAD_W), lambda b, h, i: (qb0 + b * nq + i, h)),
        pl.BlockSpec((seq, HEAD_W), lambda b, h, i: (kb0 + b, DA_HEADS + h)),
        pl.BlockSpec((seq, HEAD_W), lambda b, h, i: (kb0 + b, 2 * DA_HEADS + h)),
    ]
    args = [lam_p, subln_g, proj, proj, proj]
    if rope:
        cos, sin = rope_tabs
        in_specs += [
            pl.BlockSpec((bq, HEAD_W), lambda b, h, i: (i, 0)),
            pl.BlockSpec((bq, HEAD_W), lambda b, h, i: (i, 0)),
            pl.BlockSpec((seq, HEAD_W), lambda b, h, i: (0, 0)),
            pl.BlockSpec((seq, HEAD_W), lambda b, h, i: (0, 0)),
        ]
        args += [cos, sin, cos, sin]
    if ctx:
        in_specs += [
            pl.BlockSpec((None, None, None, l_ctx, HEAD_W), lambda b, h, i: (b, layer_e, h, 0, 0)),
            pl.BlockSpec((None, None, None, l_ctx, HEAD_W), lambda b, h, i: (b, layer_e, h, 0, 0)),
        ]
        args += [ctx_k, ctx_v]
    in_specs.append(pl.BlockSpec(memory_space=pl.ANY))
    args.append(mix)
    kernel = functools.partial(_attn_kernel, rope=rope, ctx=ctx, bq=bq,
                               chunks=_attn_chunks(l_ctx, seq), lam_init=lam_init)
    return pl.pallas_call(
        kernel,
        out_shape=jax.ShapeDtypeStruct(mix.shape, mix.dtype),
        grid=(n_batch, DA_HEADS, nq),
        in_specs=in_specs,
        out_specs=pl.BlockSpec((bq, HEAD_W), lambda b, h, i: (qb0 + b * nq + i, h)),
        scratch_shapes=[pltpu.VMEM((HEAD_W, l_ctx + seq), BF16), pltpu.VMEM((l_ctx + seq, 2 * HEAD_W), BF16)],
        input_output_aliases={len(args) - 1: 0},
        compiler_params=_params(("arbitrary", "arbitrary", "arbitrary")),
        name="diff_attention_ctx" if ctx else "diff_attention",
    )(*args)


def _hgrn_chunk(q, k, v, lf, st, rev):
    c = HG_C
    row = lax.broadcasted_iota(I32, (c, c), 0)
    col = lax.broadcasted_iota(I32, (c, c), 1)
    tri = jnp.where((col >= row) if rev else (col <= row), 1.0, 0.0).astype(BF16)
    hi = lf.astype(BF16)
    r1 = lf - hi.astype(F32)
    mid = r1.astype(BF16)
    lo = (r1 - mid.astype(F32)).astype(BF16)
    g3 = jnp.dot(tri, jnp.concatenate([hi, mid, lo], axis=1), preferred_element_type=F32)
    g = g3[:, 0:HG_DK] + g3[:, HG_DK:2 * HG_DK] + g3[:, 2 * HG_DK:3 * HG_DK]
    g_last = g[0:1] if rev else g[c - 1:c]

    o_inter = lax.dot_general((q * jnp.exp(g)).astype(BF16), st.astype(BF16), NT_DIMS,
                              preferred_element_type=F32)
    ones = jnp.ones((HG_DK, HG_DK), BF16)
    t_idx = lax.broadcasted_iota(I32, (HG_SB, 1), 0)
    outs = []
    for i in range(c // HG_SB):
        r0 = i * HG_SB
        gi, qi, ki, vi = g[r0:r0 + HG_SB], q[r0:r0 + HG_SB], k[r0:r0 + HG_SB], v[r0:r0 + HG_SB]
        if rev:
            s_lo, s_hi = r0 + HG_SB, c
            ref_g = g[s_lo:s_lo + 1] if s_lo < c else None
        else:
            s_lo, s_hi = 0, r0
            ref_g = g[r0 - 1:r0] if r0 > 0 else None
        oi = jnp.zeros((HG_SB, v.shape[1]), F32)
        if ref_g is not None:
            qd = (qi * jnp.exp(gi - ref_g)).astype(BF16)
            kd = (k[s_lo:s_hi] * jnp.exp(ref_g - g[s_lo:s_hi])).astype(BF16)
            a = lax.dot_general(qd, kd, NT_DIMS, preferred_element_type=F32)
            oi = jnp.dot(a.astype(BF16), v[s_lo:s_hi].astype(BF16), preferred_element_type=F32)
        ws = []
        for s in range(HG_SB):
            valid = (t_idx <= s) if rev else (t_idx >= s)
            e = jnp.exp(jnp.where(valid, gi - gi[s:s + 1], -1e30))
            ws.append(qi * (ki[s:s + 1] * e))
        a_rep = jnp.dot(jnp.concatenate(ws, axis=0).astype(BF16), ones, preferred_element_type=F32)
        for s in range(HG_SB):
            oi = oi + a_rep[s * HG_SB:(s + 1) * HG_SB] * vi[s:s + 1]
        outs.append(oi)
    o = o_inter + jnp.concatenate(outs, axis=0)

    kd = (k * jnp.exp(g_last - g)).astype(BF16)
    st_new = st * jnp.exp(g_last) + lax.dot_general(v.astype(BF16), kd, TN_DIMS, preferred_element_type=F32)
    return o, st_new


def _hgrn_kernel(*refs, layer, has_s0):
    it = iter(refs)
    hq_ref, zf_ref, zb_ref, hi_ref, hg_ref, lbp_ref, g_ref = (next(it) for _ in range(7))
    s0_ref = next(it) if has_s0 else None
    next(it)
    o_ref, sout_ref, st_ref = next(it), next(it), next(it)
    n = hq_ref.shape[0] // HG_C

    for d, z_ref in enumerate((zf_ref, zb_ref)):
        p = lbp_ref[d]
        pmax = jnp.max(p, axis=0)
        e = jnp.exp(p - pmax)
        lb = jnp.sum(e[0:layer + 1], axis=0) / jnp.sum(e, axis=0)

        if has_s0:
            st_ref[...] = s0_ref[d].T
        else:
            st_ref[...] = jnp.zeros(st_ref.shape, F32)

        def body(ci, carry, d=d, z_ref=z_ref, lb=lb):
            c = (n - 1 - ci) if d == 1 else ci
            rows = pl.ds(pl.multiple_of(c * HG_C, HG_C), HG_C)
            f = lb + (1.0 - lb) * _sigmoid(z_ref[rows, :])
            o, st_new = _hgrn_chunk(_silu(hq_ref[rows, :]), 1.0 - f, hi_ref[rows, :], jnp.log(f),
                                    st_ref[...], rev=(d == 1))
            st_ref[...] = st_new
            if d == 0:
                o_ref[rows, :] = o
            else:
                o_ref[rows, :] = _rms(o_ref[rows, :] + o, g_ref[...]) * _silu(hg_ref[rows, :])
            return carry

        lax.fori_loop(0, n, body, 0)
        sout_ref[d] = st_ref[...].T


def _hgrn(proj, mix, hg_lb_l, hg_norm_g, *, row0, n_batch, seq, layer, s0=None):
    has_s0 = s0 is not None
    rb0 = row0 // seq
    col0 = 3 * DA_HEADS

    def col_spec(j):
        return pl.BlockSpec((seq, HEAD_W), lambda b, h: (rb0 + b, col0 + j * HG_HEADS + h))

    in_specs = [col_spec(0), col_spec(1), col_spec(2), col_spec(3), col_spec(4),
                pl.BlockSpec((2, DEPTH + 1, None, 1, HG_DK), lambda b, h: (0, 0, h, 0, 0)),
                pl.BlockSpec((1, HEAD_W), lambda b, h: (0, 0))]
    args = [proj, proj, proj, proj, proj, hg_lb_l, hg_norm_g]
    if has_s0:
        in_specs.append(pl.BlockSpec((None, 2, None, HG_DK, HEAD_W), lambda b, h: (b, 0, h, 0, 0)))
        args.append(s0)
    in_specs.append(pl.BlockSpec(memory_space=pl.ANY))
    args.append(mix)
    return pl.pallas_call(
        functools.partial(_hgrn_kernel, layer=layer, has_s0=has_s0),
        out_shape=(jax.ShapeDtypeStruct(mix.shape, mix.dtype),
                   jax.ShapeDtypeStruct((n_batch, 2, HG_HEADS, HG_DK, HEAD_W), F32)),
        grid=(n_batch, HG_HEADS),
        in_specs=in_specs,
        out_specs=(pl.BlockSpec((seq, HEAD_W), lambda b, h: (rb0 + b, DA_HEADS + h)),
                   pl.BlockSpec((None, 2, None, HG_DK, HEAD_W), lambda b, h: (b, 0, h, 0, 0))),
        scratch_shapes=[pltpu.VMEM((HEAD_W, HG_DK), F32)],
        input_output_aliases={len(args) - 1: 0},
        compiler_params=_params(("arbitrary", "arbitrary")),
        name="hgrn2_state" if has_s0 else "hgrn2",
    )(*args)


def _odd_kernel(x_ref, m_ref, g_ref, win_ref, ws_ref, sb_ref, o_ref):
    h = _norm_mod(x_ref[...], g_ref[...], m_ref[0], m_ref[1])
    z = jnp.dot(h.astype(BF16), win_ref[...], preferred_element_type=F32)
    z = 0.5 * z * (1.0 + lax.erf(z * (2.0 ** -0.5)))
    width = z.shape[1] // 2
    u, v = z[:, 0:width], z[:, width:2 * width]
    mu = jnp.mean(v, axis=-1, keepdims=True)
    vc = v - mu
    var = jnp.mean(vc * vc, axis=-1, keepdims=True)
    vn = (vc * lax.rsqrt(var + EPS)).astype(BF16)
    gd = width // CM_GROUPS
    for r in range(x_ref.shape[0] // CM_CHUNK):
        rows = slice(r * CM_CHUNK, (r + 1) * CM_CHUNK)
        for gi in range(CM_GROUPS):
            cols = slice(gi * gd, (gi + 1) * gd)
            sv = jnp.dot(ws_ref[gi], vn[rows, cols], preferred_element_type=F32) + sb_ref[:, gi:gi + 1]
            o_ref[rows, cols] = u[rows, cols] * sv


def _odd_mixer(x, m, g, w_in, sgu_w, sgu_bt, mod_spec):
    nt = x.shape[0]
    return pl.pallas_call(
        _odd_kernel,
        out_shape=jax.ShapeDtypeStruct((nt, D_MODEL), F32),
        grid=(nt // TM,),
        in_specs=[
            pl.BlockSpec((TM, D_MODEL), lambda i: (i, 0)),
            mod_spec,
            pl.BlockSpec((1, D_MODEL), lambda i: (0, 0)),
            pl.BlockSpec(w_in.shape, lambda i: (0, 0)),
            pl.BlockSpec(sgu_w.shape, lambda i: (0, 0, 0)),
            pl.BlockSpec(sgu_bt.shape, lambda i: (0, 0)),
        ],
        out_specs=pl.BlockSpec((TM, D_MODEL), lambda i: (i, 0)),
        compiler_params=_params(("arbitrary",)),
        name="odd_mixer",
    )(x, m, g, w_in, sgu_w, sgu_bt)


def _post_kernel(x_ref, mix_ref, m_ref, g_ref, w_ref, rwt_ref, rb_ref, xo_ref, h2_ref, lg_ref):
    mo = jnp.dot(mix_ref[...].astype(BF16), w_ref[...], preferred_element_type=F32)
    xn = x_ref[...] + m_ref[2] * mo
    xo_ref[...] = xn
    h2 = _norm_mod(xn, g_ref[...], m_ref[3], m_ref[4])
    h2_ref[...] = h2
    hh = h2.astype(BF16)
    hl = (h2 - hh.astype(F32)).astype(BF16)
    rw = rwt_ref[...]
    rh = rw.astype(BF16)
    rl = (rw - rh.astype(F32)).astype(BF16)
    lg = (lax.dot_general(rh, hh, NT_DIMS, preferred_element_type=F32)
          + lax.dot_general(rh, hl, NT_DIMS, preferred_element_type=F32)
          + lax.dot_general(rl, hh, NT_DIMS, preferred_element_type=F32))
    lg_ref[...] = lg + rb_ref[...]


def _post(x, mix, m, g2, w_out, router_wt, router_b, mod_spec):
    nt = x.shape[0]
    row = lambda i: (i, 0)
    return pl.pallas_call(
        _post_kernel,
        out_shape=(jax.ShapeDtypeStruct((nt, D_MODEL), F32),
                   jax.ShapeDtypeStruct((nt, D_MODEL), F32),
                   jax.ShapeDtypeStruct((N_EXPERTS, nt), F32)),
        grid=(nt // TM,),
        in_specs=[
            pl.BlockSpec((TM, D_MODEL), row),
            pl.BlockSpec((TM, D_MODEL), row),
            mod_spec,
            pl.BlockSpec((1, D_MODEL), lambda i: (0, 0)),
            pl.BlockSpec((D_MODEL, D_MODEL), lambda i: (0, 0)),
            pl.BlockSpec((N_EXPERTS, D_MODEL), lambda i: (0, 0)),
            pl.BlockSpec((N_EXPERTS, 1), lambda i: (0, 0)),
        ],
        out_specs=(pl.BlockSpec((TM, D_MODEL), row), pl.BlockSpec((TM, D_MODEL), row),
                   pl.BlockSpec((N_EXPERTS, TM), lambda i: (0, i))),
        compiler_params=_params(("arbitrary",)),
        name="mixer_out_router",
    )(x, mix, m, g2, w_out, router_wt, router_b)


def _route_a_kernel(lg_ref, idx_ref, gate_ref, rank_ref, cnt_ref, carry_ref):
    i = pl.program_id(0)

    @pl.when(i == 0)
    def _():
        carry_ref[...] = jnp.zeros(carry_ref.shape, F32)

    l = lg_ref[...]
    tn = l.shape[1]
    eio = lax.broadcasted_iota(I32, l.shape, 0)
    vals, idxs = [], []
    for _ in range(TOP_K):
        mk = jnp.max(l, axis=0, keepdims=True)
        ik = jnp.min(jnp.where(l == mk, eio, N_EXPERTS), axis=0, keepdims=True)
        vals.append(mk)
        idxs.append(ik)
        l = jnp.where(eio == ik, -jnp.inf, l)
    es = [jnp.exp(v - vals[0]) for v in vals]
    den = es[0] + es[1] + es[2] + es[3]
    sel = jnp.zeros(l.shape, F32)
    for ik in idxs:
        sel = sel + jnp.where(eio == ik, 1.0, 0.0)
    upper = jnp.where(lax.broadcasted_iota(I32, (tn, tn), 0) < lax.broadcasted_iota(I32, (tn, tn), 1),
                      1.0, 0.0).astype(BF16)
    base = jnp.dot(sel.astype(BF16), upper, preferred_element_type=F32) + carry_ref[...]
    for k in range(TOP_K):
        idx_ref[k:k + 1, :] = idxs[k]
        gate_ref[k:k + 1, :] = es[k] / den
        rk = jnp.sum(jnp.where(eio == idxs[k], base, 0.0), axis=0, keepdims=True)
        rank_ref[k:k + 1, :] = rk.astype(I32)
    carry_ref[...] = carry_ref[...] + jnp.sum(sel, axis=1, keepdims=True)
    cnt_ref[...] = carry_ref[...]


def _route_b_kernel(cnt_ref, idx_ref, rank_ref, gate_ref, dest_ref, gt_ref, be_ref, nv_ref, *, n_blk_pad):
    cnt = cnt_ref[...]
    padded = jnp.floor((cnt + (MOE_BM - 1)) / MOE_BM) * MOE_BM
    r = lax.broadcasted_iota(I32, (N_EXPERTS, N_EXPERTS), 0)
    c = lax.broadcasted_iota(I32, (N_EXPERTS, N_EXPERTS), 1)
    padded_row = jnp.sum(jnp.where(r == c, padded, 0.0), axis=0, keepdims=True)
    pend = jnp.sum(jnp.where(c <= r, padded_row, 0.0), axis=1, keepdims=True)
    pstart = pend - padded
    tn = idx_ref.shape[1]
    eio = lax.broadcasted_iota(I32, (N_EXPERTS, tn), 0)
    for k in range(TOP_K):
        off = jnp.sum(jnp.where(eio == idx_ref[k:k + 1, :], pstart, 0.0), axis=0, keepdims=True)
        dest_ref[k:k + 1, :] = off.astype(I32) + rank_ref[k:k + 1, :]
    dest_ref[TOP_K:DEST_ROWS, :] = jnp.zeros((DEST_ROWS - TOP_K, tn), I32)
    gp = jnp.concatenate([gate_ref[...], jnp.zeros((HEAD_W - TOP_K, tn), F32)], axis=0)
    gt_ref[...] = gp.T
    blk_start = (lax.broadcasted_iota(I32, (N_EXPERTS, n_blk_pad), 1) * MOE_BM).astype(F32)
    be = jnp.sum(jnp.where(pend <= blk_start, 1.0, 0.0), axis=0, keepdims=True)
    be_ref[...] = jnp.minimum(be, N_EXPERTS - 1.0).astype(I32)
    total = jnp.sum(padded, axis=0, keepdims=True)
    nv_ref[...] = jnp.broadcast_to(total / MOE_BM, nv_ref.shape).astype(I32)


def _route(logits_t, n_blk_pad):
    nt = logits_t.shape[1]
    tn = TM
    blk = lambda i: (0, i)
    idx, gate, rank, cnt = pl.pallas_call(
        _route_a_kernel,
        out_shape=(jax.ShapeDtypeStruct((TOP_K, nt), I32), jax.ShapeDtypeStruct((TOP_K, nt), F32),
                   jax.ShapeDtypeStruct((TOP_K, nt), I32), jax.ShapeDtypeStruct((N_EXPERTS, 1), F32)),
        grid=(nt // tn,),
        in_specs=[pl.BlockSpec((N_EXPERTS, tn), blk)],
        out_specs=(pl.BlockSpec((TOP_K, tn), blk), pl.BlockSpec((TOP_K, tn), blk),
                   pl.BlockSpec((TOP_K, tn), blk), pl.BlockSpec((N_EXPERTS, 1), lambda i: (0, 0))),
        scratch_shapes=[pltpu.VMEM((N_EXPERTS, 1), F32)],
        compiler_params=_params(("arbitrary",)),
        name="route_topk_rank",
    )(logits_t)
    dest, gate_t, blk_exp, n_valid = pl.pallas_call(
        functools.partial(_route_b_kernel, n_blk_pad=n_blk_pad),
        out_shape=(jax.ShapeDtypeStruct((nt // tn * DEST_ROWS, tn), I32), jax.ShapeDtypeStruct((nt, HEAD_W), F32),
                   jax.ShapeDtypeStruct((1, n_blk_pad), I32), jax.ShapeDtypeStruct((1, HEAD_W), I32)),
        grid=(nt // tn,),
        in_specs=[pl.BlockSpec((N_EXPERTS, 1), lambda i: (0, 0)), pl.BlockSpec((TOP_K, tn), blk),
                  pl.BlockSpec((TOP_K, tn), blk), pl.BlockSpec((TOP_K, tn), blk)],
        out_specs=(pl.BlockSpec((DEST_ROWS, tn), lambda i: (i, 0)), pl.BlockSpec((tn, HEAD_W), lambda i: (i, 0)),
                   pl.BlockSpec((1, n_blk_pad), lambda i: (0, 0)), pl.BlockSpec((1, HEAD_W), lambda i: (0, 0))),
        compiler_params=_params(("arbitrary",)),
        name="route_dest",
    )(cnt, idx, rank, gate)
    return dest, gate_t, blk_exp.reshape(n_blk_pad), n_valid[0, 0:1]


def _row_copy_out(h_ref, xs_ref, sem, r, d):
    return pltpu.make_async_copy(h_ref.at[pl.ds(r, 1)], xs_ref.at[pl.ds(d, 1)], sem)


def _dispatch_kernel(h_ref, dest_ref, xs_ref, dsm_ref, sem_idx, sem_rows):
    cp = pltpu.make_async_copy(dest_ref, dsm_ref, sem_idx)
    cp.start()
    cp.wait()
    tn = h_ref.shape[0]

    def issue(r, carry):
        for k in range(TOP_K):
            _row_copy_out(h_ref, xs_ref, sem_rows, r, dsm_ref[k, r]).start()
        return carry

    lax.fori_loop(0, tn, issue, 0)

    def drain(r, carry):
        for k in range(TOP_K):
            _row_copy_out(h_ref, xs_ref, sem_rows, r, dsm_ref[k, r]).wait()
        return carry

    lax.fori_loop(0, tn, drain, 0)


def _dispatch(h2, dest, n_rows):
    nt = h2.shape[0]
    return pl.pallas_call(
        _dispatch_kernel,
        out_shape=jax.ShapeDtypeStruct((n_rows, D_MODEL), F32),
        grid=(nt // TM,),
        in_specs=[pl.BlockSpec((TM, D_MODEL), lambda i: (i, 0)),
                  pl.BlockSpec((DEST_ROWS, TM), lambda i: (i, 0))],
        out_specs=pl.BlockSpec(memory_space=pl.ANY),
        scratch_shapes=[pltpu.SMEM((DEST_ROWS, TM), I32), pltpu.SemaphoreType.DMA, pltpu.SemaphoreType.DMA],
        compiler_params=_params(("arbitrary",)),
        name="moe_dispatch",
    )(h2, dest)


def _expert_kernel(be_ref, nv_ref, xs_ref, w1_ref, b1_ref, w2_ref, b2_ref, ys_ref, w1b_ref, w2b_ref):
    i = pl.program_id(0)
    valid = i < nv_ref[0]
    prev = be_ref[jnp.maximum(i - 1, 0)]
    changed = jnp.logical_or(i == 0, be_ref[i] != prev)

    @pl.when(jnp.logical_and(valid, changed))
    def _cast_weights():
        step = 128
        def cast(j, carry):
            rows = pl.ds(pl.multiple_of(j * step, step), step)
            w1b_ref[rows, :] = w1_ref[rows, :].astype(BF16)
            w2b_ref[rows, :] = w2_ref[rows, :].astype(BF16)
            return carry
        lax.fori_loop(0, w1_ref.shape[0] // step, cast, 0)

    @pl.when(valid)
    def _mlp():
        h = jnp.dot(xs_ref[...].astype(BF16), w1b_ref[...], preferred_element_type=F32) + b1_ref[...]
        half = h.shape[1] // 2
        a = jnp.minimum(h[:, 0:half], SWIGLU_LIMIT)
        lin = jnp.clip(h[:, half:2 * half], -SWIGLU_LIMIT, SWIGLU_LIMIT)
        act = a * _sigmoid(SWIGLU_ALPHA * a) * (lin + 1.0)
        ys_ref[...] = jnp.dot(act.astype(BF16), w2b_ref[...], preferred_element_type=F32) + b2_ref[...]


def _experts(xs, blk_exp, n_valid, w1, b1, w2, b2):
    n_rows = xs.shape[0]
    n_blk = n_rows // MOE_BM
    d_e2 = w1.shape[2]

    def blk(i, be, nv):
        return jnp.minimum(i, nv[0] - 1)

    grid_spec = pltpu.PrefetchScalarGridSpec(
        num_scalar_prefetch=2,
        grid=(n_blk,),
        in_specs=[
            pl.BlockSpec((MOE_BM, D_MODEL), lambda i, be, nv: (blk(i, be, nv), 0)),
            pl.BlockSpec((None, D_MODEL, d_e2), lambda i, be, nv: (be[blk(i, be, nv)], 0, 0)),
            pl.BlockSpec((None, 1, d_e2), lambda i, be, nv: (be[blk(i, be, nv)], 0, 0)),
            pl.BlockSpec((None, d_e2 // 2, D_MODEL), lambda i, be, nv: (be[blk(i, be, nv)], 0, 0)),
            pl.BlockSpec((None, 1, D_MODEL), lambda i, be, nv: (be[blk(i, be, nv)], 0, 0)),
        ],
        out_specs=pl.BlockSpec((MOE_BM, D_MODEL), lambda i, be, nv: (i, 0)),
        scratch_shapes=[pltpu.VMEM((D_MODEL, d_e2), BF16), pltpu.VMEM((d_e2 // 2, D_MODEL), BF16)],
    )
    return pl.pallas_call(
        _expert_kernel,
        out_shape=jax.ShapeDtypeStruct((n_rows, D_MODEL), F32),
        grid_spec=grid_spec,
        compiler_params=_params(("arbitrary",)),
        name="moe_experts",
    )(blk_exp, n_valid, xs, w1, b1.reshape(N_EXPERTS, 1, d_e2), w2, b2.reshape(N_EXPERTS, 1, D_MODEL))


def _row_copy_in(ys_ref, buf_ref, sem, k, r, d):
    return pltpu.make_async_copy(ys_ref.at[pl.ds(d, 1)], buf_ref.at[k, pl.ds(r, 1)], sem)


def _combine_kernel(x_ref, dest_ref, gt_ref, m_ref, fg_ref, ys_ref, o_ref, dsm_ref, buf_ref, sem_idx, sem_rows,
                    *, final):
    cp = pltpu.make_async_copy(dest_ref, dsm_ref, sem_idx)
    cp.start()
    cp.wait()
    tn = x_ref.shape[0]

    def issue(r, carry):
        for k in range(TOP_K):
            _row_copy_in(ys_ref, buf_ref, sem_rows, k, r, dsm_ref[k, r]).start()
        return carry

    lax.fori_loop(0, tn, issue, 0)

    def drain(r, carry):
        for k in range(TOP_K):
            _row_copy_in(ys_ref, buf_ref, sem_rows, k, r, dsm_ref[k, r]).wait()
        return carry

    lax.fori_loop(0, tn, drain, 0)

    gt = gt_ref[...]
    acc = gt[:, 0:1] * buf_ref[0]
    for k in range(1, TOP_K):
        acc = acc + gt[:, k:k + 1] * buf_ref[k]
    xn = x_ref[...] + m_ref[5] * acc
    if final:
        xn = _rms(xn, fg_ref[...])
    o_ref[...] = xn


def _combine(x, dest, gate_t, m, final_g, ys, mod_spec, final):
    nt = x.shape[0]
    return pl.pallas_call(
        functools.partial(_combine_kernel, final=final),
        out_shape=jax.ShapeDtypeStruct((nt, D_MODEL), F32),
        grid=(nt // TM,),
        in_specs=[pl.BlockSpec((TM, D_MODEL), lambda i: (i, 0)),
                  pl.BlockSpec((DEST_ROWS, TM), lambda i: (i, 0)),
                  pl.BlockSpec((TM, HEAD_W), lambda i: (i, 0)),
                  mod_spec,
                  pl.BlockSpec((1, D_MODEL), lambda i: (0, 0)),
                  pl.BlockSpec(memory_space=pl.ANY)],
        out_specs=pl.BlockSpec((TM, D_MODEL), lambda i: (i, 0)),
        scratch_shapes=[pltpu.SMEM((DEST_ROWS, TM), I32), pltpu.VMEM((TOP_K, TM, D_MODEL), F32),
                        pltpu.SemaphoreType.DMA, pltpu.SemaphoreType.DMA],
        compiler_params=_params(("arbitrary",)),
        name="moe_combine_final" if final else "moe_combine",
    )(x, dest, gate_t, m, final_g, ys)


def _moe(x, h2, logits_t, m, final_g, w1, b1, w2, b2, mod_spec, final):
    nt = x.shape[0]
    n_blk = nt * TOP_K // MOE_BM + N_EXPERTS
    n_blk_pad = -(-n_blk // HEAD_W) * HEAD_W
    dest, gate_t, blk_exp, n_valid = _route(logits_t, n_blk_pad)
    xs = _dispatch(h2, dest, n_blk * MOE_BM)
    ys = _experts(xs, blk_exp, n_valid, w1, b1, w2, b2)
    return _combine(x, dest, gate_t, m, final_g, ys, mod_spec, final)


def _rope_tables(seq):
    t = jnp.arange(seq)
    inv = 1.0 / (ROPE_THETA ** (jnp.arange(ROPE_HALF, dtype=F32) / ROPE_HALF))
    ang_r = (t // GRID_W).astype(F32)[:, None] * inv
    ang_c = (t % GRID_W).astype(F32)[:, None] * inv
    cr, sr, cc, sc = jnp.cos(ang_r), jnp.sin(ang_r), jnp.cos(ang_c), jnp.sin(ang_c)
    cos = jnp.concatenate([cr, cr, cc, cc] * 2, axis=1)
    sin = jnp.concatenate([-sr, sr, -sc, sc] * 2, axis=1)
    return cos, sin


def kernel(x_prompt, x_sample, c, cache_k, cache_v, state_hgrn, c_ctx, mod_w, mod_b, norm_g, final_norm_g,
           w_in_even, w_out_even, da_lambda, da_subln_g, hg_norm_g, hg_lb, w_in_odd, sgu_w, sgu_b, w_out_odd,
           router_w, router_b, ex_w1, ex_b1, ex_w2, ex_b2):
    bp, lp, _ = x_prompt.shape
    bs, ls, _ = x_sample.shape
    n_p, n_s = bp * lp, bs * ls
    assert lp % TM == 0 and ls % TM == 0 and n_p % ls == 0 and bs < MOD_ROWS
    depth = mod_w.shape[0]

    x = jnp.concatenate([x_prompt.reshape(n_p, D_MODEL), x_sample.reshape(n_s, D_MODEL)], axis=0)
    cond = jnp.zeros((MOD_ROWS, D_MODEL), F32).at[0:bs].set(c).at[bs].set(c_ctx)
    mod = _modulation(cond, mod_w, mod_b)
    mod_spec = _mod_spec(n_p, ls, bs)
    rope_tabs = _rope_tables(ls)
    hg_lb_h = hg_lb.reshape(2, depth + 1, HG_HEADS, 1, HG_DK)
    final_g = final_norm_g.reshape(1, D_MODEL)

    new_k, new_v, new_s = [], [], []
    for l in range(depth):
        m = mod[l]
        g1 = norm_g[l, 0].reshape(1, D_MODEL)
        g2 = norm_g[l, 1].reshape(1, D_MODEL)
        if l % 2 == 0:
            e = l // 2
            lam_init = 0.8 - 0.6 * math.exp(-0.3 * l)
            proj = _inproj(x, m, g1, w_in_even[e].astype(BF16), mod_spec)
            subln = da_subln_g[e].reshape(1, HEAD_W)
            hgn = hg_norm_g[e].reshape(1, HEAD_W)
            mix = jnp.zeros((n_p + n_s, D_MODEL), F32)
            mix = _diff_attention(proj, mix, da_lambda[e], subln, row0=0, n_batch=bp, seq=lp, lam_init=lam_init)
            mix = _diff_attention(proj, mix, da_lambda[e], subln, row0=n_p, n_batch=bs, seq=ls, lam_init=lam_init,
                                  rope_tabs=rope_tabs, ctx_k=cache_k, ctx_v=cache_v, layer_e=e)
            mix, s_p = _hgrn(proj, mix, hg_lb_h, hgn, row0=0, n_batch=bp, seq=lp, layer=l)
            mix, _ = _hgrn(proj, mix, hg_lb_h, hgn, row0=n_p, n_batch=bs, seq=ls, layer=l, s0=state_hgrn[:, e])
            w_out = w_out_even[e].astype(BF16)
            kv = proj[0:n_p, DA_HEADS * HEAD_W:3 * DA_HEADS * HEAD_W].reshape(bp, lp, 2, DA_HEADS, HEAD_W)
            new_k.append(kv[:, :, 0].transpose(0, 2, 1, 3))
            new_v.append(kv[:, :, 1].transpose(0, 2, 1, 3))
            new_s.append(s_p)
        else:
            o = l // 2
            mix = _odd_mixer(x, m, g1, w_in_odd[o].astype(BF16), sgu_w[o].astype(BF16), sgu_b[o].T, mod_spec)
            w_out = w_out_odd[o].astype(BF16)
        x, h2, logits_t = _post(x, mix, m, g2, w_out, router_w[l].T, router_b[l].reshape(N_EXPERTS, 1), mod_spec)
        x = _moe(x, h2, logits_t, m, final_g, ex_w1[l], ex_b1[l], ex_w2[l], ex_b2[l], mod_spec,
                 final=(l == depth - 1))

    y_prompt = x[0:n_p].reshape(bp, lp, D_MODEL)
    y_sample = x[n_p:].reshape(bs, ls, D_MODEL)
    return (y_prompt, y_sample, jnp.stack(new_k, axis=1), jnp.stack(new_v, axis=1), jnp.stack(new_s, axis=1))
```

```python
import functools
import math

import jax
import jax.numpy as jnp
from jax import lax
from jax.experimental import pallas as pl
from jax.experimental.pallas import tpu as pltpu

F32 = jnp.float32
BF16 = jnp.bfloat16
I32 = jnp.int32

D_MODEL = 1024
DEPTH = 2
GRID_W = 64
DA_HEADS = 4
DA_HD = 64
HG_HEADS = 4
HG_DK = 128
ROPE_THETA = 10000.0
ROPE_HALF = DA_HD // 4
CM_CHUNK = 128
CM_GROUPS = 8
N_EXPERTS = 32
TOP_K = 4
SWIGLU_ALPHA = 1.702
SWIGLU_LIMIT = 7.0
EPS = 1e-6
EVEN_IN = 4096
HEAD_W = 128

TM = 256
HG_C = 64
HG_SB = 16
HG_HPS = 2
HG_FIN = 256
ROW_UNROLL = 8
MOE_BM = 256
DEST_ROWS = 8
MOD_ROWS = 16
V7X_VMEM_LIMIT = 56 * 1024 * 1024

NT_DIMS = (((1,), (1,)), ((), ()))
TN_DIMS = (((0,), (0,)), ((), ()))


def _params(sem):
    return pltpu.CompilerParams(dimension_semantics=sem, vmem_limit_bytes=V7X_VMEM_LIMIT)


def _sigmoid(x):
    return 1.0 / (1.0 + jnp.exp(-x))


def _silu(x):
    return x * _sigmoid(x)


def _rms(x, g):
    ms = jnp.mean(x * x, axis=-1, keepdims=True)
    return x * lax.rsqrt(ms + EPS) * g


def _norm_mod(x, g, shift, scale):
    return _rms(x, g) * (1.0 + scale) + shift


def _mod_kernel(c_ref, w_ref, b_ref, o_ref):
    s = _silu(c_ref[...])
    o_ref[...] = jnp.dot(s.astype(BF16), w_ref[...].astype(BF16), preferred_element_type=F32) + b_ref[...]


def _modulation(cond, mod_w, mod_b):
    depth = mod_w.shape[0]
    m = pl.pallas_call(
        _mod_kernel,
        out_shape=jax.ShapeDtypeStruct((depth, 6, MOD_ROWS, D_MODEL), F32),
        grid=(depth, 6),
        in_specs=[
            pl.BlockSpec((MOD_ROWS, D_MODEL), lambda l, j: (0, 0)),
            pl.BlockSpec((None, D_MODEL, D_MODEL), lambda l, j: (l, 0, j)),
            pl.BlockSpec((None, 1, D_MODEL), lambda l, j: (l, 0, j)),
        ],
        out_specs=pl.BlockSpec((None, None, MOD_ROWS, D_MODEL), lambda l, j: (l, j, 0, 0)),
        compiler_params=_params(("arbitrary", "arbitrary")),
        name="modulation",
    )(cond, mod_w, mod_b.reshape(depth, 1, 6 * D_MODEL))
    return m.transpose(0, 2, 1, 3)[:, :, :, None, :]


def _mod_spec(n_prompt, l_sample, ctx_row):
    def index(i):
        t = i * TM
        return (jnp.where(t < n_prompt, ctx_row, (t - n_prompt) // l_sample), 0, 0, 0)
    return pl.BlockSpec((None, 6, 1, D_MODEL), index)


def _inproj_kernel(x_ref, m_ref, g_ref, w_ref, o_ref):
    h = _norm_mod(x_ref[...], g_ref[...], m_ref[0], m_ref[1])
    o_ref[...] = jnp.dot(h.astype(BF16), w_ref[...], preferred_element_type=F32)


def _inproj(x, m, g, w, mod_spec):
    nt, n_out = x.shape[0], w.shape[1]
    return pl.pallas_call(
        _inproj_kernel,
        out_shape=jax.ShapeDtypeStruct((nt, n_out), F32),
        grid=(nt // TM,),
        in_specs=[
            pl.BlockSpec((TM, D_MODEL), lambda i: (i, 0)),
            mod_spec,
            pl.BlockSpec((1, D_MODEL), lambda i: (0, 0)),
            pl.BlockSpec((D_MODEL, n_out), lambda i: (0, 0)),
        ],
        out_specs=pl.BlockSpec((TM, n_out), lambda i: (i, 0)),
        compiler_params=_params(("arbitrary",)),
        name="even_inproj",
    )(x, m, g, w)


def _rope(x, cos, sin_signed):
    lane = lax.broadcasted_iota(I32, x.shape, 1)
    first = ((lane // ROPE_HALF) % 2) == 0
    partner = jnp.where(first, pltpu.roll(x, HEAD_W - ROPE_HALF, 1), pltpu.roll(x, ROPE_HALF, 1))
    return x * cos + partner * sin_signed


def _attn_kernel(*refs, rope, ctx, bq, chunks, lam_init):
    it = iter(refs)
    lam_ref, g_ref, q_ref, k_ref, v_ref = (next(it) for _ in range(5))
    if rope:
        cq_ref, sq_ref, ck_ref, sk_ref = (next(it) for _ in range(4))
    if ctx:
        kc_ref, vc_ref = next(it), next(it)
    next(it)
    o_ref = next(it)
    if not ctx:
        ko_ref, vo_ref = next(it), next(it)
    kt_ref, vx_ref = next(it), next(it)
    l_new = k_ref.shape[0]
    l_ctx = kc_ref.shape[0] if ctx else 0

    @pl.when(pl.program_id(2) == 0)
    def _prepare_keys():
        if ctx:
            kt_ref[:, 0:l_ctx] = kc_ref[...].T.astype(BF16)
            vx_ref[0:l_ctx, 0:HEAD_W] = vc_ref[...].astype(BF16)
        else:
            ko_ref[...] = k_ref[...]
            vo_ref[...] = v_ref[...]
        step = min(512, l_new)
        for c0 in range(0, l_new, step):
            k = k_ref[c0:c0 + step, :]
            if rope:
                k = _rope(k, ck_ref[c0:c0 + step, :], sk_ref[c0:c0 + step, :])
            kt_ref[:, l_ctx + c0:l_ctx + c0 + step] = k.T.astype(BF16)
            vx_ref[l_ctx + c0:l_ctx + c0 + step, 0:HEAD_W] = v_ref[c0:c0 + step, :].astype(BF16)
        vx_ref[:, HEAD_W:2 * HEAD_W] = jnp.ones((l_ctx + l_new, HEAD_W), BF16)

    q = q_ref[...]
    if rope:
        q = _rope(q, cq_ref[...], sq_ref[...])
    q = q * (DA_HD ** -0.5)
    lane = lax.broadcasted_iota(I32, q.shape, 1)
    lo = lane < DA_HD
    zero = jnp.zeros_like(q)
    qs = jnp.concatenate([jnp.where(lo, q, zero), jnp.where(lo, zero, q)], axis=0).astype(BF16)

    m = jnp.full((2 * bq, 1), -jnp.inf, F32)
    l = jnp.zeros((2 * bq, 1), F32)
    acc = jnp.zeros((2 * bq, HEAD_W), F32)
    def scores(chunk):
        return jnp.dot(qs, kt_ref[:, chunk[0]:chunk[0] + chunk[1]], preferred_element_type=F32)

    s_next = scores(chunks[0])
    for ci, (c0, cs) in enumerate(chunks):
        s = s_next
        if ci + 1 < len(chunks):
            s_next = scores(chunks[ci + 1])
        mn = jnp.maximum(m, jnp.max(s, axis=-1, keepdims=True))
        alpha = jnp.exp(m - mn)
        p = jnp.exp(s - mn).astype(BF16)
        pv = jnp.dot(p, vx_ref[c0:c0 + cs, :], preferred_element_type=F32)
        acc = alpha * acc + pv[:, 0:HEAD_W]
        l = alpha * l + pv[:, HEAD_W:HEAD_W + 1]
        m = mn
    o2 = acc / l
    lp = lam_ref[...]
    lam = (jnp.exp(jnp.sum(lp[0:1] * lp[1:2], axis=-1, keepdims=True))
           - jnp.exp(jnp.sum(lp[2:3] * lp[3:4], axis=-1, keepdims=True)) + lam_init)
    o = o2[0:bq] - lam * o2[bq:2 * bq]
    o_ref[...] = _rms(o, g_ref[...]) * (1.0 - lam_init)


def _attn_chunks(l_ctx, l_new):
    total = l_ctx + l_new
    if total <= 1280:
        return ((0, total),)
    chunks, c0 = [], 0
    first = l_ctx + 1024
    chunks.append((0, first))
    c0 = first
    while c0 < total:
        cs = min(1024, total - c0)
        chunks.append((c0, cs))
        c0 += cs
    return tuple(chunks)


def _diff_attention(proj, mix, lam_p, subln_g, *, row0, n_batch, seq, lam_init, rope_tabs=None,
                    ctx_k=None, ctx_v=None, layer_e=0):
    rope, ctx = rope_tabs is not None, ctx_k is not None
    n_rows = proj.shape[0]
    bq = min(256, seq)
    nq = seq // bq
    l_ctx = ctx_k.shape[3] if ctx else 0
    qb0, kb0 = row0 // bq, row0 // seq
    in_specs = [
        pl.BlockSpec((4, DA_HD), lambda b, h, i: (0, 0)),
        pl.BlockSpec((1, HEAD_W), lambda b, h, i: (0, 0)),
        pl.BlockSpec((bq, HEAD_W), lambda b, h, i: (qb0 + b * nq + i, h)),
        pl.BlockSpec((seq, HEAD_W), lambda b, h, i: (kb0 + b, DA_HEADS + h)),
        pl.BlockSpec((seq, HEAD_W), lambda b, h, i: (kb0 + b, 2 * DA_HEADS + h)),
    ]
    args = [lam_p, subln_g, proj, proj, proj]
    if rope:
        cos, sin = rope_tabs
        in_specs += [
            pl.BlockSpec((bq, HEAD_W), lambda b, h, i: (i, 0)),
            pl.BlockSpec((bq, HEAD_W), lambda b, h, i: (i, 0)),
            pl.BlockSpec((seq, HEAD_W), lambda b, h, i: (0, 0)),
            pl.BlockSpec((seq, HEAD_W), lambda b, h, i: (0, 0)),
        ]
        args += [cos, sin, cos, sin]
    if ctx:
        in_specs += [
            pl.BlockSpec((None, None, None, l_ctx, HEAD_W), lambda b, h, i: (b, layer_e, h, 0, 0)),
            pl.BlockSpec((None, None, None, l_ctx, HEAD_W), lambda b, h, i: (b, layer_e, h, 0, 0)),
        ]
        args += [ctx_k, ctx_v]
    mix_sds = jax.ShapeDtypeStruct((n_rows, D_MODEL), F32)
    mix_spec = pl.BlockSpec((bq, HEAD_W), lambda b, h, i: (qb0 + b * nq + i, h))
    in_specs.append(pl.BlockSpec(memory_space=pl.ANY))
    args.append(mix)
    aliases = {len(args) - 1: 0}
    if ctx:
        out_shape, out_specs = mix_sds, mix_spec
    else:
        cache_sds = jax.ShapeDtypeStruct((n_batch, DA_HEADS, seq, HEAD_W), F32)
        cache_spec = pl.BlockSpec((None, None, seq, HEAD_W), lambda b, h, i: (b, h, 0, 0))
        out_shape, out_specs = (mix_sds, cache_sds, cache_sds), (mix_spec, cache_spec, cache_spec)
    kernel = functools.partial(_attn_kernel, rope=rope, ctx=ctx, bq=bq,
                               chunks=_attn_chunks(l_ctx, seq), lam_init=lam_init)
    return pl.pallas_call(
        kernel,
        out_shape=out_shape,
        grid=(n_batch, DA_HEADS, nq),
        in_specs=in_specs,
        out_specs=out_specs,
        scratch_shapes=[pltpu.VMEM((HEAD_W, l_ctx + seq), BF16), pltpu.VMEM((l_ctx + seq, 2 * HEAD_W), BF16)],
        input_output_aliases=aliases,
        compiler_params=_params(("arbitrary", "arbitrary", "arbitrary")),
        name="diff_attention_ctx" if ctx else "diff_attention",
    )(*args)


def _hgrn_chunks(streams):
    c, nsb = HG_C, HG_C // HG_SB
    row = lax.broadcasted_iota(I32, (c, c), 0)
    col = lax.broadcasted_iota(I32, (c, c), 1)
    tri = {r: jnp.where((col >= row) if r else (col <= row), 1.0, 0.0).astype(BF16) for r in (False, True)}
    ones = jnp.ones((HG_DK, HG_DK), BF16)
    t_idx = lax.broadcasted_iota(I32, (HG_SB, 1), 0)

    g3s = []
    for q, k, v, lf, st, rev in streams:
        hi = lf.astype(BF16)
        r1 = lf - hi.astype(F32)
        mid = r1.astype(BF16)
        lo = (r1 - mid.astype(F32)).astype(BF16)
        g3s.append(jnp.dot(tri[rev], jnp.concatenate([hi, mid, lo], axis=1), preferred_element_type=F32))
    gs = [g3[:, 0:HG_DK] + g3[:, HG_DK:2 * HG_DK] + g3[:, 2 * HG_DK:3 * HG_DK] for g3 in g3s]
    g_lasts = [g[0:1] if s[5] else g[c - 1:c] for g, s in zip(gs, streams)]

    o_inters = [lax.dot_general((s[0] * jnp.exp(g)).astype(BF16), s[4].astype(BF16), NT_DIMS,
                                preferred_element_type=F32) for g, s in zip(gs, streams)]

    def outside(i, rev):
        r0 = i * HG_SB
        if rev:
            return (r0 + HG_SB, c, r0 + HG_SB) if r0 + HG_SB < c else None
        return (0, r0, r0 - 1) if r0 > 0 else None

    a_offs = {}
    for i in range(nsb):
        for n, (g, s) in enumerate(zip(gs, streams)):
            span = outside(i, s[5])
            if span is None:
                continue
            s_lo, s_hi, rb = span
            r0 = i * HG_SB
            ref_g = g[rb:rb + 1]
            qd = (s[0][r0:r0 + HG_SB] * jnp.exp(g[r0:r0 + HG_SB] - ref_g)).astype(BF16)
            kd = (s[1][s_lo:s_hi] * jnp.exp(ref_g - g[s_lo:s_hi])).astype(BF16)
            a_offs[i, n] = lax.dot_general(qd, kd, NT_DIMS, preferred_element_type=F32)
    o_offs = {}
    for (i, n), a in a_offs.items():
        s_lo, s_hi, _ = outside(i, streams[n][5])
        o_offs[i, n] = jnp.dot(a.astype(BF16), streams[n][2][s_lo:s_hi].astype(BF16), preferred_element_type=F32)

    a_reps = {}
    for i in range(nsb):
        r0 = i * HG_SB
        for n, (g, s) in enumerate(zip(gs, streams)):
            gi, qi, ki = g[r0:r0 + HG_SB], s[0][r0:r0 + HG_SB], s[1][r0:r0 + HG_SB]
            ws = []
            for j in range(HG_SB):
                valid = (t_idx <= j) if s[5] else (t_idx >= j)
                e = jnp.exp(jnp.where(valid, gi - gi[j:j + 1], -1e30))
                ws.append(qi * (ki[j:j + 1] * e))
            a_reps[i, n] = jnp.dot(jnp.concatenate(ws, axis=0).astype(BF16), ones, preferred_element_type=F32)

    results = []
    for n, (g, s) in enumerate(zip(gs, streams)):
        q, k, v, lf, st, rev = s
        outs = []
        for i in range(nsb):
            r0 = i * HG_SB
            vi = v[r0:r0 + HG_SB]
            oi = o_offs.get((i, n), jnp.zeros((HG_SB, v.shape[1]), F32))
            for j in range(HG_SB):
                oi = oi + a_reps[i, n][j * HG_SB:(j + 1) * HG_SB] * vi[j:j + 1]
            outs.append(oi)
        o = o_inters[n] + jnp.concatenate(outs, axis=0)
        kd = (k * jnp.exp(g_lasts[n] - g)).astype(BF16)
        st_new = st * jnp.exp(g_lasts[n]) + lax.dot_general(v.astype(BF16), kd, TN_DIMS, preferred_element_type=F32)
        results.append((o, st_new))
    return results


def _hgrn_kernel(*refs, layer, has_s0):
    it = iter(refs)
    hq_ref, zf_ref, zb_ref, hi_ref, hg_ref, lbp_ref, g_ref = (next(it) for _ in range(7))
    s0_ref = next(it) if has_s0 else None
    next(it)
    o_ref, sout_ref, st_ref, ob_ref = next(it), next(it), next(it), next(it)
    seq = hq_ref.shape[0]
    n = seq // HG_C
    z_refs = (zf_ref, zb_ref)

    lbs = []
    for hh in range(HG_HPS):
        for d in range(2):
            p = lbp_ref[d, :, hh]
            e = jnp.exp(p - jnp.max(p, axis=0))
            lbs.append(jnp.sum(e[0:layer + 1], axis=0) / jnp.sum(e, axis=0))
            if has_s0:
                st_ref[2 * hh + d] = s0_ref[d, hh].T
            else:
                st_ref[2 * hh + d] = jnp.zeros((HEAD_W, HG_DK), F32)

    def body(ci, carry):
        streams, dests = [], []
        for hh in range(HG_HPS):
            lanes = slice(hh * HEAD_W, (hh + 1) * HEAD_W)
            for d in range(2):
                c = (n - 1 - ci) if d == 1 else ci
                rows = pl.ds(pl.multiple_of(c * HG_C, HG_C), HG_C)
                lb = lbs[2 * hh + d]
                f = lb + (1.0 - lb) * _sigmoid(z_refs[d][rows, lanes])
                streams.append((_silu(hq_ref[rows, lanes]), 1.0 - f, hi_ref[rows, lanes], jnp.log(f),
                                st_ref[2 * hh + d], d == 1))
                dests.append((o_ref if d == 0 else ob_ref, rows, lanes, 2 * hh + d))
        for (o, st_new), (dst_ref, rows, lanes, slot) in zip(_hgrn_chunks(streams), dests):
            st_ref[slot] = st_new
            dst_ref[rows, lanes] = o
        return carry

    lax.fori_loop(0, n, body, 0)
    for hh in range(HG_HPS):
        for d in range(2):
            sout_ref[d, hh] = st_ref[2 * hh + d].T

    def finish(ci, carry):
        rows = pl.ds(pl.multiple_of(ci * HG_FIN, HG_FIN), HG_FIN)
        for hh in range(HG_HPS):
            lanes = slice(hh * HEAD_W, (hh + 1) * HEAD_W)
            tot = o_ref[rows, lanes] + ob_ref[rows, lanes]
            o_ref[rows, lanes] = _rms(tot, g_ref[...]) * _silu(hg_ref[rows, lanes])
        return carry

    lax.fori_loop(0, seq // HG_FIN, finish, 0)


def _hgrn(proj, mix, hg_lb_l, hg_norm_g, *, row0, n_batch, seq, layer, s0=None):
    has_s0 = s0 is not None
    rb0 = row0 // seq
    hpg = HG_HEADS // HG_HPS
    width = HG_HPS * HEAD_W
    col0 = 3 * DA_HEADS * HEAD_W // width
    once = pl.Buffered(1)

    def col_spec(j):
        return pl.BlockSpec((seq, width), lambda b, h: (rb0 + b, col0 + j * hpg + h), pipeline_mode=once)

    in_specs = [col_spec(0), col_spec(1), col_spec(2), col_spec(3), col_spec(4),
                pl.BlockSpec((2, DEPTH + 1, HG_HPS, 1, HG_DK), lambda b, h: (0, 0, h, 0, 0)),
                pl.BlockSpec((1, HEAD_W), lambda b, h: (0, 0))]
    args = [proj, proj, proj, proj, proj, hg_lb_l, hg_norm_g]
    state_spec = pl.BlockSpec((None, 2, HG_HPS, HG_DK, HEAD_W), lambda b, h: (b, 0, h, 0, 0))
    if has_s0:
        in_specs.append(state_spec)
        args.append(s0)
    in_specs.append(pl.BlockSpec(memory_space=pl.ANY))
    args.append(mix)
    return pl.pallas_call(
        functools.partial(_hgrn_kernel, layer=layer, has_s0=has_s0),
        out_shape=(jax.ShapeDtypeStruct(mix.shape, mix.dtype),
                   jax.ShapeDtypeStruct((n_batch, 2, HG_HEADS, HG_DK, HEAD_W), F32)),
        grid=(n_batch, hpg),
        in_specs=in_specs,
        out_specs=(pl.BlockSpec((seq, width), lambda b, h: (rb0 + b, DA_HEADS * HEAD_W // width + h)),
                   state_spec),
        scratch_shapes=[pltpu.VMEM((2 * HG_HPS, HEAD_W, HG_DK), F32), pltpu.VMEM((seq, width), F32)],
        input_output_aliases={len(args) - 1: 0},
        compiler_params=_params(("arbitrary", "arbitrary")),
        name="hgrn2_state" if has_s0 else "hgrn2",
    )(*args)


def _odd_kernel(x_ref, m_ref, g_ref, win_ref, ws_ref, sb_ref, o_ref):
    h = _norm_mod(x_ref[...], g_ref[...], m_ref[0], m_ref[1])
    z = jnp.dot(h.astype(BF16), win_ref[...], preferred_element_type=F32)
    z = 0.5 * z * (1.0 + lax.erf(z * (2.0 ** -0.5)))
    width = z.shape[1] // 2
    u, v = z[:, 0:width], z[:, width:2 * width]
    mu = jnp.mean(v, axis=-1, keepdims=True)
    vc = v - mu
    var = jnp.mean(vc * vc, axis=-1, keepdims=True)
    vn = (vc * lax.rsqrt(var + EPS)).astype(BF16)
    gd = width // CM_GROUPS
    for r in range(x_ref.shape[0] // CM_CHUNK):
        rows = slice(r * CM_CHUNK, (r + 1) * CM_CHUNK)
        for gi in range(CM_GROUPS):
            cols = slice(gi * gd, (gi + 1) * gd)
            sv = jnp.dot(ws_ref[gi], vn[rows, cols], preferred_element_type=F32) + sb_ref[:, gi:gi + 1]
            o_ref[rows, cols] = u[rows, cols] * sv


def _odd_mixer(x, m, g, w_in, sgu_w, sgu_bt, mod_spec):
    nt = x.shape[0]
    return pl.pallas_call(
        _odd_kernel,
        out_shape=jax.ShapeDtypeStruct((nt, D_MODEL), F32),
        grid=(nt // TM,),
        in_specs=[
            pl.BlockSpec((TM, D_MODEL), lambda i: (i, 0)),
            mod_spec,
            pl.BlockSpec((1, D_MODEL), lambda i: (0, 0)),
            pl.BlockSpec(w_in.shape, lambda i: (0, 0)),
            pl.BlockSpec(sgu_w.shape, lambda i: (0, 0, 0)),
            pl.BlockSpec(sgu_bt.shape, lambda i: (0, 0)),
        ],
        out_specs=pl.BlockSpec((TM, D_MODEL), lambda i: (i, 0)),
        compiler_params=_params(("arbitrary",)),
        name="odd_mixer",
    )(x, m, g, w_in, sgu_w, sgu_bt)


def _post_kernel(x_ref, mix_ref, m_ref, g_ref, w_ref, rwt_ref, rb_ref, xo_ref, h2_ref, lg_ref):
    mo = jnp.dot(mix_ref[...].astype(BF16), w_ref[...], preferred_element_type=F32)
    xn = x_ref[...] + m_ref[2] * mo
    xo_ref[...] = xn
    h2 = _norm_mod(xn, g_ref[...], m_ref[3], m_ref[4])
    h2_ref[...] = h2
    hh = h2.astype(BF16)
    hl = (h2 - hh.astype(F32)).astype(BF16)
    rw = rwt_ref[...]
    rh = rw.astype(BF16)
    rl = (rw - rh.astype(F32)).astype(BF16)
    lg = (lax.dot_general(rh, hh, NT_DIMS, preferred_element_type=F32)
          + lax.dot_general(rh, hl, NT_DIMS, preferred_element_type=F32)
          + lax.dot_general(rl, hh, NT_DIMS, preferred_element_type=F32))
    lg_ref[...] = lg + rb_ref[...]


def _post(x, mix, m, g2, w_out, router_wt, router_b, mod_spec):
    nt = x.shape[0]
    row = lambda i: (i, 0)
    return pl.pallas_call(
        _post_kernel,
        out_shape=(jax.ShapeDtypeStruct((nt, D_MODEL), F32),
                   jax.ShapeDtypeStruct((nt, D_MODEL), F32),
                   jax.ShapeDtypeStruct((N_EXPERTS, nt), F32)),
        grid=(nt // TM,),
        in_specs=[
            pl.BlockSpec((TM, D_MODEL), row),
            pl.BlockSpec((TM, D_MODEL), row),
            mod_spec,
            pl.BlockSpec((1, D_MODEL), lambda i: (0, 0)),
            pl.BlockSpec((D_MODEL, D_MODEL), lambda i: (0, 0)),
            pl.BlockSpec((N_EXPERTS, D_MODEL), lambda i: (0, 0)),
            pl.BlockSpec((N_EXPERTS, 1), lambda i: (0, 0)),
        ],
        out_specs=(pl.BlockSpec((TM, D_MODEL), row), pl.BlockSpec((TM, D_MODEL), row),
                   pl.BlockSpec((N_EXPERTS, TM), lambda i: (0, i))),
        compiler_params=_params(("arbitrary",)),
        name="mixer_out_router",
    )(x, mix, m, g2, w_out, router_wt, router_b)


def _route_a_kernel(lg_ref, idx_ref, gate_ref, rank_ref, cnt_ref, carry_ref):
    i = pl.program_id(0)

    @pl.when(i == 0)
    def _():
        carry_ref[...] = jnp.zeros(carry_ref.shape, F32)

    l = lg_ref[...]
    tn = l.shape[1]
    eio = lax.broadcasted_iota(I32, l.shape, 0)
    vals, idxs = [], []
    for _ in range(TOP_K):
        mk = jnp.max(l, axis=0, keepdims=True)
        ik = jnp.min(jnp.where(l == mk, eio, N_EXPERTS), axis=0, keepdims=True)
        vals.append(mk)
        idxs.append(ik)
        l = jnp.where(eio == ik, -jnp.inf, l)
    es = [jnp.exp(v - vals[0]) for v in vals]
    den = es[0] + es[1] + es[2] + es[3]
    sel = jnp.zeros(l.shape, F32)
    for ik in idxs:
        sel = sel + jnp.where(eio == ik, 1.0, 0.0)
    upper = jnp.where(lax.broadcasted_iota(I32, (tn, tn), 0) < lax.broadcasted_iota(I32, (tn, tn), 1),
                      1.0, 0.0).astype(BF16)
    base = jnp.dot(sel.astype(BF16), upper, preferred_element_type=F32) + carry_ref[...]
    for k in range(TOP_K):
        idx_ref[k:k + 1, :] = idxs[k]
        gate_ref[k:k + 1, :] = es[k] / den
        rk = jnp.sum(jnp.where(eio == idxs[k], base, 0.0), axis=0, keepdims=True)
        rank_ref[k:k + 1, :] = rk.astype(I32)
    carry_ref[...] = carry_ref[...] + jnp.sum(sel, axis=1, keepdims=True)
    cnt_ref[...] = carry_ref[...]


def _route_b_kernel(cnt_ref, idx_ref, rank_ref, gate_ref, dest_ref, gt_ref, be_ref, nv_ref, pad_ref, *,
                    n_blk_pad, n_rows):
    cnt = cnt_ref[...]
    padded = jnp.floor((cnt + (MOE_BM - 1)) / MOE_BM) * MOE_BM
    r = lax.broadcasted_iota(I32, (N_EXPERTS, N_EXPERTS), 0)
    c = lax.broadcasted_iota(I32, (N_EXPERTS, N_EXPERTS), 1)
    padded_row = jnp.sum(jnp.where(r == c, padded, 0.0), axis=0, keepdims=True)
    pend = jnp.sum(jnp.where(c <= r, padded_row, 0.0), axis=1, keepdims=True)
    pstart = pend - padded
    tn = idx_ref.shape[1]
    eio = lax.broadcasted_iota(I32, (N_EXPERTS, tn), 0)
    for k in range(TOP_K):
        off = jnp.sum(jnp.where(eio == idx_ref[k:k + 1, :], pstart, 0.0), axis=0, keepdims=True)
        dest_ref[k:k + 1, :] = off.astype(I32) + rank_ref[k:k + 1, :]
    dest_ref[TOP_K:DEST_ROWS, :] = jnp.zeros((DEST_ROWS - TOP_K, tn), I32)
    gp = jnp.concatenate([gate_ref[...], jnp.zeros((HEAD_W - TOP_K, tn), F32)], axis=0)
    gt_ref[...] = gp.T
    blk_start = (lax.broadcasted_iota(I32, (N_EXPERTS, n_blk_pad), 1) * MOE_BM).astype(F32)
    be = jnp.sum(jnp.where(pend <= blk_start, 1.0, 0.0), axis=0, keepdims=True)
    be_ref[...] = jnp.minimum(be, N_EXPERTS - 1.0).astype(I32)
    total = jnp.sum(padded, axis=0, keepdims=True)
    nv_ref[...] = jnp.broadcast_to(total / MOE_BM, nv_ref.shape).astype(I32)
    on_lane = (lax.broadcasted_iota(I32, (N_EXPERTS, HEAD_W), 0) == lax.broadcasted_iota(I32, (N_EXPERTS, HEAD_W), 1))
    lane = lax.broadcasted_iota(I32, (1, HEAD_W), 1)
    pad_lo = jnp.sum(jnp.where(on_lane, pstart + cnt, 0.0), axis=0, keepdims=True)
    pad_hi = jnp.sum(jnp.where(on_lane, pend, 0.0), axis=0, keepdims=True)
    pad_ref[0:1, :] = jnp.where(lane == N_EXPERTS, total, pad_lo).astype(I32)
    pad_ref[1:2, :] = jnp.where(lane == N_EXPERTS, float(n_rows), pad_hi).astype(I32)
    pad_ref[2:DEST_ROWS, :] = jnp.zeros((DEST_ROWS - 2, HEAD_W), I32)


def _route(logits_t, n_blk_pad, n_rows):
    nt = logits_t.shape[1]
    tn = TM
    blk = lambda i: (0, i)
    idx, gate, rank, cnt = pl.pallas_call(
        _route_a_kernel,
        out_shape=(jax.ShapeDtypeStruct((TOP_K, nt), I32), jax.ShapeDtypeStruct((TOP_K, nt), F32),
                   jax.ShapeDtypeStruct((TOP_K, nt), I32), jax.ShapeDtypeStruct((N_EXPERTS, 1), F32)),
        grid=(nt // tn,),
        in_specs=[pl.BlockSpec((N_EXPERTS, tn), blk)],
        out_specs=(pl.BlockSpec((TOP_K, tn), blk), pl.BlockSpec((TOP_K, tn), blk),
                   pl.BlockSpec((TOP_K, tn), blk), pl.BlockSpec((N_EXPERTS, 1), lambda i: (0, 0))),
        scratch_shapes=[pltpu.VMEM((N_EXPERTS, 1), F32)],
        compiler_params=_params(("arbitrary",)),
        name="route_topk_rank",
    )(logits_t)
    dest, gate_t, blk_exp, n_valid, pad = pl.pallas_call(
        functools.partial(_route_b_kernel, n_blk_pad=n_blk_pad, n_rows=n_rows),
        out_shape=(jax.ShapeDtypeStruct((nt // tn * DEST_ROWS, tn), I32), jax.ShapeDtypeStruct((nt, HEAD_W), F32),
                   jax.ShapeDtypeStruct((1, n_blk_pad), I32), jax.ShapeDtypeStruct((1, HEAD_W), I32),
                   jax.ShapeDtypeStruct((DEST_ROWS, HEAD_W), I32)),
        grid=(nt // tn,),
        in_specs=[pl.BlockSpec((N_EXPERTS, 1), lambda i: (0, 0)), pl.BlockSpec((TOP_K, tn), blk),
                  pl.BlockSpec((TOP_K, tn), blk), pl.BlockSpec((TOP_K, tn), blk)],
        out_specs=(pl.BlockSpec((DEST_ROWS, tn), lambda i: (i, 0)), pl.BlockSpec((tn, HEAD_W), lambda i: (i, 0)),
                   pl.BlockSpec((1, n_blk_pad), lambda i: (0, 0)), pl.BlockSpec((1, HEAD_W), lambda i: (0, 0)),
                   pl.BlockSpec((DEST_ROWS, HEAD_W), lambda i: (0, 0))),
        compiler_params=_params(("arbitrary",)),
        name="route_dest",
    )(cnt, idx, rank, gate)
    return dest, gate_t, blk_exp.reshape(n_blk_pad), n_valid[0, 0:1], pad


def _row_copy_out(h_ref, xs_ref, sem, r, d):
    return pltpu.make_async_copy(h_ref.at[pl.ds(r, 1)], xs_ref.at[pl.ds(d, 1)], sem)


def _dispatch_kernel(h_ref, dest_ref, pad_ref, xs_ref, dsm_ref, psm_ref, zero_ref, sem_idx, sem_rows):
    cp = pltpu.make_async_copy(dest_ref, dsm_ref, sem_idx)
    cp.start()
    cp.wait()
    tn = h_ref.shape[0]

    @pl.when(pl.program_id(0) == 0)
    def _zero_padding_rows():
        zero_ref[...] = jnp.zeros(zero_ref.shape, F32)
        cpp = pltpu.make_async_copy(pad_ref, psm_ref, sem_idx)
        cpp.start()
        cpp.wait()

        def zero_copy(r):
            return pltpu.make_async_copy(zero_ref.at[pl.ds(0, 1)], xs_ref.at[pl.ds(r, 1)], sem_rows)

        def per_expert(e, carry):
            lo, hi = psm_ref[0, e], psm_ref[1, e]
            lax.fori_loop(lo, hi, lambda r, c: (zero_copy(r).start(), c)[1], 0)
            lax.fori_loop(lo, hi, lambda r, c: (zero_copy(r).wait(), c)[1], 0)
            return carry

        lax.fori_loop(0, N_EXPERTS + 1, per_expert, 0)

    def issue(j, carry):
        r0 = pl.multiple_of(j * ROW_UNROLL, ROW_UNROLL)
        for s in range(ROW_UNROLL):
            for k in range(TOP_K):
                _row_copy_out(h_ref, xs_ref, sem_rows, r0 + s, dsm_ref[k, r0 + s]).start(priority=k % 2)
        return carry

    lax.fori_loop(0, tn // ROW_UNROLL, issue, 0)
    for k in range(TOP_K):
        pltpu.make_async_copy(h_ref, xs_ref.at[pl.ds(0, tn)], sem_rows).wait()


def _dispatch(h2, dest, pad, n_rows):
    nt = h2.shape[0]
    return pl.pallas_call(
        _dispatch_kernel,
        out_shape=jax.ShapeDtypeStruct((n_rows, D_MODEL), F32),
        grid=(nt // TM,),
        in_specs=[pl.BlockSpec((TM, D_MODEL), lambda i: (i, 0)),
                  pl.BlockSpec((DEST_ROWS, TM), lambda i: (i, 0)),
                  pl.BlockSpec((DEST_ROWS, HEAD_W), lambda i: (0, 0))],
        out_specs=pl.BlockSpec(memory_space=pl.ANY),
        scratch_shapes=[pltpu.SMEM((DEST_ROWS, TM), I32), pltpu.SMEM((DEST_ROWS, HEAD_W), I32),
                        pltpu.VMEM((DEST_ROWS, D_MODEL), F32), pltpu.SemaphoreType.DMA, pltpu.SemaphoreType.DMA],
        compiler_params=_params(("arbitrary",)),
        name="moe_dispatch",
    )(h2, dest, pad)


def _expert_kernel(be_ref, nv_ref, xs_ref, w1_ref, b1_ref, w2_ref, b2_ref, ys_ref, w1b_ref, w2b_ref):
    i = pl.program_id(0)
    valid = i < nv_ref[0]
    prev = be_ref[jnp.maximum(i - 1, 0)]
    changed = jnp.logical_or(i == 0, be_ref[i] != prev)

    @pl.when(jnp.logical_and(valid, changed))
    def _cast_weights():
        step = 128
        def cast(j, carry):
            rows = pl.ds(pl.multiple_of(j * step, step), step)
            w1b_ref[rows, :] = w1_ref[rows, :].astype(BF16)
            w2b_ref[rows, :] = w2_ref[rows, :].astype(BF16)
            return carry
        lax.fori_loop(0, w1_ref.shape[0] // step, cast, 0)

    @pl.when(jnp.logical_not(valid))
    def _unused_block():
        ys_ref[...] = jnp.zeros(ys_ref.shape, F32)

    @pl.when(valid)
    def _mlp():
        h = jnp.dot(xs_ref[...].astype(BF16), w1b_ref[...], preferred_element_type=F32) + b1_ref[...]
        half = h.shape[1] // 2
        a = jnp.minimum(h[:, 0:half], SWIGLU_LIMIT)
        lin = jnp.clip(h[:, half:2 * half], -SWIGLU_LIMIT, SWIGLU_LIMIT)
        act = a * _sigmoid(SWIGLU_ALPHA * a) * (lin + 1.0)
        ys_ref[...] = jnp.dot(act.astype(BF16), w2b_ref[...], preferred_element_type=F32) + b2_ref[...]


def _experts(xs, blk_exp, n_valid, w1, b1, w2, b2):
    n_rows = xs.shape[0]
    n_blk = n_rows // MOE_BM
    d_e2 = w1.shape[2]

    def blk(i, be, nv):
        return jnp.minimum(i, nv[0] - 1)

    grid_spec = pltpu.PrefetchScalarGridSpec(
        num_scalar_prefetch=2,
        grid=(n_blk,),
        in_specs=[
            pl.BlockSpec((MOE_BM, D_MODEL), lambda i, be, nv: (blk(i, be, nv), 0)),
            pl.BlockSpec((None, D_MODEL, d_e2), lambda i, be, nv: (be[blk(i, be, nv)], 0, 0)),
            pl.BlockSpec((None, 1, d_e2), lambda i, be, nv: (be[blk(i, be, nv)], 0, 0)),
            pl.BlockSpec((None, d_e2 // 2, D_MODEL), lambda i, be, nv: (be[blk(i, be, nv)], 0, 0)),
            pl.BlockSpec((None, 1, D_MODEL), lambda i, be, nv: (be[blk(i, be, nv)], 0, 0)),
        ],
        out_specs=pl.BlockSpec((MOE_BM, D_MODEL), lambda i, be, nv: (i, 0)),
        scratch_shapes=[pltpu.VMEM((D_MODEL, d_e2), BF16), pltpu.VMEM((d_e2 // 2, D_MODEL), BF16)],
    )
    return pl.pallas_call(
        _expert_kernel,
        out_shape=jax.ShapeDtypeStruct((n_rows, D_MODEL), F32),
        grid_spec=grid_spec,
        compiler_params=_params(("arbitrary",)),
        name="moe_experts",
    )(blk_exp, n_valid, xs, w1, b1.reshape(N_EXPERTS, 1, d_e2), w2, b2.reshape(N_EXPERTS, 1, D_MODEL))


def _row_copy_in(ys_ref, buf_ref, sem, k, r, d):
    return pltpu.make_async_copy(ys_ref.at[pl.ds(d, 1)], buf_ref.at[k, pl.ds(r, 1)], sem)


def _combine_kernel(x_ref, dest_ref, gt_ref, m_ref, fg_ref, ys_ref, o_ref, dsm_ref, buf_ref, sem_idx, sem_rows,
                    *, final):
    cp = pltpu.make_async_copy(dest_ref, dsm_ref, sem_idx)
    cp.start()
    cp.wait()
    tn = x_ref.shape[0]

    def issue(j, carry):
        r0 = pl.multiple_of(j * ROW_UNROLL, ROW_UNROLL)
        for s in range(ROW_UNROLL):
            for k in range(TOP_K):
                _row_copy_in(ys_ref, buf_ref, sem_rows, k, r0 + s, dsm_ref[k, r0 + s]).start(priority=k % 2)
        return carry

    lax.fori_loop(0, tn // ROW_UNROLL, issue, 0)
    for k in range(TOP_K):
        pltpu.make_async_copy(ys_ref.at[pl.ds(0, tn)], buf_ref.at[k], sem_rows).wait()

    gt = gt_ref[...]
    acc = gt[:, 0:1] * buf_ref[0]
    for k in range(1, TOP_K):
        acc = acc + gt[:, k:k + 1] * buf_ref[k]
    xn = x_ref[...] + m_ref[5] * acc
    if final:
        xn = _rms(xn, fg_ref[...])
    o_ref[...] = xn


def _combine(x, dest, gate_t, m, final_g, ys, mod_spec, final):
    nt = x.shape[0]
    return pl.pallas_call(
        functools.partial(_combine_kernel, final=final),
        out_shape=jax.ShapeDtypeStruct((nt, D_MODEL), F32),
        grid=(nt // TM,),
        in_specs=[pl.BlockSpec((TM, D_MODEL), lambda i: (i, 0)),
                  pl.BlockSpec((DEST_ROWS, TM), lambda i: (i, 0)),
                  pl.BlockSpec((TM, HEAD_W), lambda i: (i, 0)),
                  mod_spec,
                  pl.BlockSpec((1, D_MODEL), lambda i: (0, 0)),
                  pl.BlockSpec(memory_space=pl.ANY)],
        out_specs=pl.BlockSpec((TM, D_MODEL), lambda i: (i, 0)),
        scratch_shapes=[pltpu.SMEM((DEST_ROWS, TM), I32), pltpu.VMEM((TOP_K, TM, D_MODEL), F32),
                        pltpu.SemaphoreType.DMA, pltpu.SemaphoreType.DMA],
        compiler_params=_params(("arbitrary",)),
        name="moe_combine_final" if final else "moe_combine",
    )(x, dest, gate_t, m, final_g, ys)


def _moe(x, h2, logits_t, m, final_g, w1, b1, w2, b2, mod_spec, final):
    nt = x.shape[0]
    n_blk = nt * TOP_K // MOE_BM + N_EXPERTS
    n_blk_pad = -(-n_blk // HEAD_W) * HEAD_W
    dest, gate_t, blk_exp, n_valid, pad = _route(logits_t, n_blk_pad, n_blk * MOE_BM)
    xs = _dispatch(h2, dest, pad, n_blk * MOE_BM)
    ys = _experts(xs, blk_exp, n_valid, w1, b1, w2, b2)
    return _combine(x, dest, gate_t, m, final_g, ys, mod_spec, final)


def _rope_tables(seq):
    t = jnp.arange(seq)
    inv = 1.0 / (ROPE_THETA ** (jnp.arange(ROPE_HALF, dtype=F32) / ROPE_HALF))
    ang_r = (t // GRID_W).astype(F32)[:, None] * inv
    ang_c = (t % GRID_W).astype(F32)[:, None] * inv
    cr, sr, cc, sc = jnp.cos(ang_r), jnp.sin(ang_r), jnp.cos(ang_c), jnp.sin(ang_c)
    cos = jnp.concatenate([cr, cr, cc, cc] * 2, axis=1)
    sin = jnp.concatenate([-sr, sr, -sc, sc] * 2, axis=1)
    return cos, sin


def kernel(x_prompt, x_sample, c, cache_k, cache_v, state_hgrn, c_ctx, mod_w, mod_b, norm_g, final_norm_g,
           w_in_even, w_out_even, da_lambda, da_subln_g, hg_norm_g, hg_lb, w_in_odd, sgu_w, sgu_b, w_out_odd,
           router_w, router_b, ex_w1, ex_b1, ex_w2, ex_b2):
    bp, lp, _ = x_prompt.shape
    bs, ls, _ = x_sample.shape
    n_p, n_s = bp * lp, bs * ls
    assert lp % TM == 0 and ls % TM == 0 and n_p % ls == 0 and bs < MOD_ROWS
    depth = mod_w.shape[0]

    x = jnp.concatenate([x_prompt.reshape(n_p, D_MODEL), x_sample.reshape(n_s, D_MODEL)], axis=0)
    cond = jnp.zeros((MOD_ROWS, D_MODEL), F32).at[0:bs].set(c).at[bs].set(c_ctx)
    mod = _modulation(cond, mod_w, mod_b)
    mod_spec = _mod_spec(n_p, ls, bs)
    rope_tabs = _rope_tables(ls)
    hg_lb_h = hg_lb.reshape(2, depth + 1, HG_HEADS, 1, HG_DK)
    final_g = final_norm_g.reshape(1, D_MODEL)

    new_k, new_v, new_s = [], [], []
    for l in range(depth):
        m = mod[l]
        g1 = norm_g[l, 0].reshape(1, D_MODEL)
        g2 = norm_g[l, 1].reshape(1, D_MODEL)
        if l % 2 == 0:
            e = l // 2
            lam_init = 0.8 - 0.6 * math.exp(-0.3 * l)
            proj = _inproj(x, m, g1, w_in_even[e].astype(BF16), mod_spec)
            subln = da_subln_g[e].reshape(1, HEAD_W)
            hgn = hg_norm_g[e].reshape(1, HEAD_W)
            mix = jnp.zeros((n_p + n_s, D_MODEL), F32)
            mix, k_new, v_new = _diff_attention(proj, mix, da_lambda[e], subln, row0=0, n_batch=bp, seq=lp,
                                                lam_init=lam_init)
            mix = _diff_attention(proj, mix, da_lambda[e], subln, row0=n_p, n_batch=bs, seq=ls, lam_init=lam_init,
                                  rope_tabs=rope_tabs, ctx_k=cache_k, ctx_v=cache_v, layer_e=e)
            mix, s_p = _hgrn(proj, mix, hg_lb_h, hgn, row0=0, n_batch=bp, seq=lp, layer=l)
            mix, _ = _hgrn(proj, mix, hg_lb_h, hgn, row0=n_p, n_batch=bs, seq=ls, layer=l, s0=state_hgrn[:, e])
            w_out = w_out_even[e].astype(BF16)
            new_k.append(k_new)
            new_v.append(v_new)
            new_s.append(s_p)
        else:
            o = l // 2
            mix = _odd_mixer(x, m, g1, w_in_odd[o].astype(BF16), sgu_w[o].astype(BF16), sgu_b[o].T, mod_spec)
            w_out = w_out_odd[o].astype(BF16)
        x, h2, logits_t = _post(x, mix, m, g2, w_out, router_w[l].T, router_b[l].reshape(N_EXPERTS, 1), mod_spec)
        x = _moe(x, h2, logits_t, m, final_g, ex_w1[l], ex_b1[l], ex_w2[l], ex_b2[l], mod_spec,
                 final=(l == depth - 1))

    y_prompt = x[0:n_p].reshape(bp, lp, D_MODEL)
    y_sample = x[n_p:].reshape(bs, ls, D_MODEL)
    return (y_prompt, y_sample, jnp.stack(new_k, axis=1), jnp.stack(new_v, axis=1), jnp.stack(new_s, axis=1))
```

```python
import functools
import math

import jax
import jax.numpy as jnp
from jax import lax
from jax.experimental import pallas as pl
from jax.experimental.pallas import tpu as pltpu

F32 = jnp.float32
BF16 = jnp.bfloat16
I32 = jnp.int32

D_MODEL = 1024
DEPTH = 2
GRID_W = 64
DA_HEADS = 4
DA_HD = 64
HG_HEADS = 4
HG_DK = 128
ROPE_THETA = 10000.0
ROPE_HALF = DA_HD // 4
CM_CHUNK = 128
CM_GROUPS = 8
N_EXPERTS = 32
TOP_K = 4
SWIGLU_ALPHA = 1.702
SWIGLU_LIMIT = 7.0
EPS = 1e-6
EVEN_IN = 4096
HEAD_W = 128

TM = 256
HG_C = 64
HG_SB = 16
HG_HPS = 2
HG_FIN = 256
L_ROWS = -(-(TOP_K * TM + N_EXPERTS * 14) // 128) * 128
GROUP_BITS = (32, 16, 8, 4, 2, 1)
WAIT_BITS = (128, 64, 32, 16, 8, 4, 2, 1)
MOE_BM = 256
DEST_ROWS = 8
MOD_ROWS = 16
V7X_VMEM_LIMIT = 56 * 1024 * 1024

NT_DIMS = (((1,), (1,)), ((), ()))
TN_DIMS = (((0,), (0,)), ((), ()))


def _params(sem):
    return pltpu.CompilerParams(dimension_semantics=sem, vmem_limit_bytes=V7X_VMEM_LIMIT)


def _sigmoid(x):
    return 1.0 / (1.0 + jnp.exp(-x))


def _silu(x):
    return x * _sigmoid(x)


def _rms(x, g):
    ms = jnp.mean(x * x, axis=-1, keepdims=True)
    return x * lax.rsqrt(ms + EPS) * g


def _norm_mod(x, g, shift, scale):
    return _rms(x, g) * (1.0 + scale) + shift


def _mod_kernel(c_ref, w_ref, b_ref, o_ref):
    s = _silu(c_ref[...])
    o_ref[...] = jnp.dot(s.astype(BF16), w_ref[...].astype(BF16), preferred_element_type=F32) + b_ref[...]


def _modulation(cond, mod_w, mod_b):
    depth = mod_w.shape[0]
    m = pl.pallas_call(
        _mod_kernel,
        out_shape=jax.ShapeDtypeStruct((depth, 6, MOD_ROWS, D_MODEL), F32),
        grid=(depth, 6),
        in_specs=[
            pl.BlockSpec((MOD_ROWS, D_MODEL), lambda l, j: (0, 0)),
            pl.BlockSpec((None, D_MODEL, D_MODEL), lambda l, j: (l, 0, j)),
            pl.BlockSpec((None, 1, D_MODEL), lambda l, j: (l, 0, j)),
        ],
        out_specs=pl.BlockSpec((None, None, MOD_ROWS, D_MODEL), lambda l, j: (l, j, 0, 0)),
        compiler_params=_params(("arbitrary", "arbitrary")),
        name="modulation",
    )(cond, mod_w, mod_b.reshape(depth, 1, 6 * D_MODEL))
    return m.transpose(0, 2, 1, 3)[:, :, :, None, :]


def _mod_spec(n_prompt, l_sample, ctx_row):
    def index(i):
        t = i * TM
        return (jnp.where(t < n_prompt, ctx_row, (t - n_prompt) // l_sample), 0, 0, 0)
    return pl.BlockSpec((None, 6, 1, D_MODEL), index)


def _inproj_kernel(x_ref, m_ref, g_ref, w_ref, o_ref):
    h = _norm_mod(x_ref[...], g_ref[...], m_ref[0], m_ref[1])
    o_ref[...] = jnp.dot(h.astype(BF16), w_ref[...], preferred_element_type=F32)


def _inproj(x, m, g, w, mod_spec):
    nt, n_out = x.shape[0], w.shape[1]
    return pl.pallas_call(
        _inproj_kernel,
        out_shape=jax.ShapeDtypeStruct((nt, n_out), F32),
        grid=(nt // TM,),
        in_specs=[
            pl.BlockSpec((TM, D_MODEL), lambda i: (i, 0)),
            mod_spec,
            pl.BlockSpec((1, D_MODEL), lambda i: (0, 0)),
            pl.BlockSpec((D_MODEL, n_out), lambda i: (0, 0)),
        ],
        out_specs=pl.BlockSpec((TM, n_out), lambda i: (i, 0)),
        compiler_params=_params(("arbitrary",)),
        name="even_inproj",
    )(x, m, g, w)


def _rope(x, cos, sin_signed):
    lane = lax.broadcasted_iota(I32, x.shape, 1)
    first = ((lane // ROPE_HALF) % 2) == 0
    partner = jnp.where(first, pltpu.roll(x, HEAD_W - ROPE_HALF, 1), pltpu.roll(x, ROPE_HALF, 1))
    return x * cos + partner * sin_signed


def _attn_kernel(*refs, rope, ctx, bq, chunks, lam_init):
    it = iter(refs)
    lam_ref, g_ref, q_ref, k_ref, v_ref = (next(it) for _ in range(5))
    if rope:
        cq_ref, sq_ref, ck_ref, sk_ref = (next(it) for _ in range(4))
    if ctx:
        kc_ref, vc_ref = next(it), next(it)
    next(it)
    o_ref = next(it)
    if not ctx:
        ko_ref, vo_ref = next(it), next(it)
    kt_ref, vx_ref = next(it), next(it)
    l_new = k_ref.shape[0]
    l_ctx = kc_ref.shape[0] if ctx else 0

    @pl.when(pl.program_id(2) == 0)
    def _prepare_keys():
        if ctx:
            kt_ref[:, 0:l_ctx] = kc_ref[...].T.astype(BF16)
            vx_ref[0:l_ctx, 0:HEAD_W] = vc_ref[...].astype(BF16)
        else:
            ko_ref[...] = k_ref[...]
            vo_ref[...] = v_ref[...]
        step = min(512, l_new)
        for c0 in range(0, l_new, step):
            k = k_ref[c0:c0 + step, :]
            if rope:
                k = _rope(k, ck_ref[c0:c0 + step, :], sk_ref[c0:c0 + step, :])
            kt_ref[:, l_ctx + c0:l_ctx + c0 + step] = k.T.astype(BF16)
            vx_ref[l_ctx + c0:l_ctx + c0 + step, 0:HEAD_W] = v_ref[c0:c0 + step, :].astype(BF16)
        vx_ref[:, HEAD_W:2 * HEAD_W] = jnp.ones((l_ctx + l_new, HEAD_W), BF16)

    q = q_ref[...]
    if rope:
        q = _rope(q, cq_ref[...], sq_ref[...])
    q = q * (DA_HD ** -0.5)
    lane = lax.broadcasted_iota(I32, q.shape, 1)
    lo = lane < DA_HD
    zero = jnp.zeros_like(q)
    qs = jnp.concatenate([jnp.where(lo, q, zero), jnp.where(lo, zero, q)], axis=0).astype(BF16)

    m = jnp.full((2 * bq, 1), -jnp.inf, F32)
    l = jnp.zeros((2 * bq, 1), F32)
    acc = jnp.zeros((2 * bq, HEAD_W), F32)
    def scores(chunk):
        return jnp.dot(qs, kt_ref[:, chunk[0]:chunk[0] + chunk[1]], preferred_element_type=F32)

    s_next = scores(chunks[0])
    for ci, (c0, cs) in enumerate(chunks):
        s = s_next
        if ci + 1 < len(chunks):
            s_next = scores(chunks[ci + 1])
        mn = jnp.maximum(m, jnp.max(s, axis=-1, keepdims=True))
        alpha = jnp.exp(m - mn)
        p = jnp.exp(s - mn).astype(BF16)
        pv = jnp.dot(p, vx_ref[c0:c0 + cs, :], preferred_element_type=F32)
        acc = alpha * acc + pv[:, 0:HEAD_W]
        l = alpha * l + pv[:, HEAD_W:HEAD_W + 1]
        m = mn
    o2 = acc / l
    lp = lam_ref[...]
    lam = (jnp.exp(jnp.sum(lp[0:1] * lp[1:2], axis=-1, keepdims=True))
           - jnp.exp(jnp.sum(lp[2:3] * lp[3:4], axis=-1, keepdims=True)) + lam_init)
    o = o2[0:bq] - lam * o2[bq:2 * bq]
    o_ref[...] = _rms(o, g_ref[...]) * (1.0 - lam_init)


def _attn_chunks(l_ctx, l_new):
    total = l_ctx + l_new
    if total <= 1280:
        return ((0, total),)
    chunks, c0 = [], 0
    first = l_ctx + 1024
    chunks.append((0, first))
    c0 = first
    while c0 < total:
        cs = min(1024, total - c0)
        chunks.append((c0, cs))
        c0 += cs
    return tuple(chunks)


def _diff_attention(proj, mix, lam_p, subln_g, *, row0, n_batch, seq, lam_init, rope_tabs=None,
                    ctx_k=None, ctx_v=None, layer_e=0):
    rope, ctx = rope_tabs is not None, ctx_k is not None
    n_rows = proj.shape[0]
    bq = min(256, seq)
    nq = seq // bq
    l_ctx = ctx_k.shape[3] if ctx else 0
    qb0, kb0 = row0 // bq, row0 // seq
    in_specs = [
        pl.BlockSpec((4, DA_HD), lambda b, h, i: (0, 0)),
        pl.BlockSpec((1, HEAD_W), lambda b, h, i: (0, 0)),
        pl.BlockSpec((bq, HEAD_W), lambda b, h, i: (qb0 + b * nq + i, h)),
        pl.BlockSpec((seq, HEAD_W), lambda b, h, i: (kb0 + b, DA_HEADS + h)),
        pl.BlockSpec((seq, HEAD_W), lambda b, h, i: (kb0 + b, 2 * DA_HEADS + h)),
    ]
    args = [lam_p, subln_g, proj, proj, proj]
    if rope:
        cos, sin = rope_tabs
        in_specs += [
            pl.BlockSpec((bq, HEAD_W), lambda b, h, i: (i, 0)),
            pl.BlockSpec((bq, HEAD_W), lambda b, h, i: (i, 0)),
            pl.BlockSpec((seq, HEAD_W), lambda b, h, i: (0, 0)),
            pl.BlockSpec((seq, HEAD_W), lambda b, h, i: (0, 0)),
        ]
        args += [cos, sin, cos, sin]
    if ctx:
        in_specs += [
            pl.BlockSpec((None, None, None, l_ctx, HEAD_W), lambda b, h, i: (b, layer_e, h, 0, 0)),
            pl.BlockSpec((None, None, None, l_ctx, HEAD_W), lambda b, h, i: (b, layer_e, h, 0, 0)),
        ]
        args += [ctx_k, ctx_v]
    mix_sds = jax.ShapeDtypeStruct((n_rows, D_MODEL), F32)
    mix_spec = pl.BlockSpec((bq, HEAD_W), lambda b, h, i: (qb0 + b * nq + i, h))
    in_specs.append(pl.BlockSpec(memory_space=pl.ANY))
    args.append(mix)
    aliases = {len(args) - 1: 0}
    if ctx:
        out_shape, out_specs = mix_sds, mix_spec
    else:
        cache_sds = jax.ShapeDtypeStruct((n_batch, DA_HEADS, seq, HEAD_W), F32)
        cache_spec = pl.BlockSpec((None, None, seq, HEAD_W), lambda b, h, i: (b, h, 0, 0))
        out_shape, out_specs = (mix_sds, cache_sds, cache_sds), (mix_spec, cache_spec, cache_spec)
    kernel = functools.partial(_attn_kernel, rope=rope, ctx=ctx, bq=bq,
                               chunks=_attn_chunks(l_ctx, seq), lam_init=lam_init)
    return pl.pallas_call(
        kernel,
        out_shape=out_shape,
        grid=(n_batch, DA_HEADS, nq),
        in_specs=in_specs,
        out_specs=out_specs,
        scratch_shapes=[pltpu.VMEM((HEAD_W, l_ctx + seq), BF16), pltpu.VMEM((l_ctx + seq, 2 * HEAD_W), BF16)],
        input_output_aliases=aliases,
        compiler_params=_params(("arbitrary", "arbitrary", "arbitrary")),
        name="diff_attention_ctx" if ctx else "diff_attention",
    )(*args)


def _hgrn_chunks(streams):
    c, nsb = HG_C, HG_C // HG_SB
    row = lax.broadcasted_iota(I32, (c, c), 0)
    col = lax.broadcasted_iota(I32, (c, c), 1)
    tri = {r: jnp.where((col >= row) if r else (col <= row), 1.0, 0.0).astype(BF16) for r in (False, True)}
    ones = jnp.ones((HG_DK, HG_DK), BF16)
    t_idx = lax.broadcasted_iota(I32, (HG_SB, 1), 0)

    g3s = []
    for q, k, v, lf, st, rev in streams:
        hi = lf.astype(BF16)
        r1 = lf - hi.astype(F32)
        mid = r1.astype(BF16)
        lo = (r1 - mid.astype(F32)).astype(BF16)
        g3s.append(jnp.dot(tri[rev], jnp.concatenate([hi, mid, lo], axis=1), preferred_element_type=F32))
    gs = [g3[:, 0:HG_DK] + g3[:, HG_DK:2 * HG_DK] + g3[:, 2 * HG_DK:3 * HG_DK] for g3 in g3s]
    g_lasts = [g[0:1] if s[5] else g[c - 1:c] for g, s in zip(gs, streams)]

    o_inters = [lax.dot_general((s[0] * jnp.exp(g)).astype(BF16), s[4].astype(BF16), NT_DIMS,
                                preferred_element_type=F32) for g, s in zip(gs, streams)]

    def outside(i, rev):
        r0 = i * HG_SB
        if rev:
            return (r0 + HG_SB, c, r0 + HG_SB) if r0 + HG_SB < c else None
        return (0, r0, r0 - 1) if r0 > 0 else None

    a_offs = {}
    for i in range(nsb):
        for n, (g, s) in enumerate(zip(gs, streams)):
            span = outside(i, s[5])
            if span is None:
                continue
            s_lo, s_hi, rb = span
            r0 = i * HG_SB
            ref_g = g[rb:rb + 1]
            qd = (s[0][r0:r0 + HG_SB] * jnp.exp(g[r0:r0 + HG_SB] - ref_g)).astype(BF16)
            kd = (s[1][s_lo:s_hi] * jnp.exp(ref_g - g[s_lo:s_hi])).astype(BF16)
            a_offs[i, n] = lax.dot_general(qd, kd, NT_DIMS, preferred_element_type=F32)
    o_offs = {}
    for (i, n), a in a_offs.items():
        s_lo, s_hi, _ = outside(i, streams[n][5])
        o_offs[i, n] = jnp.dot(a.astype(BF16), streams[n][2][s_lo:s_hi].astype(BF16), preferred_element_type=F32)

    a_reps = {}
    for i in range(nsb):
        r0 = i * HG_SB
        for n, (g, s) in enumerate(zip(gs, streams)):
            gi, qi, ki = g[r0:r0 + HG_SB], s[0][r0:r0 + HG_SB], s[1][r0:r0 + HG_SB]
            ws = []
            for j in range(HG_SB):
                valid = (t_idx <= j) if s[5] else (t_idx >= j)
                e = jnp.exp(jnp.where(valid, gi - gi[j:j + 1], -1e30))
                ws.append(qi * (ki[j:j + 1] * e))
            a_reps[i, n] = jnp.dot(jnp.concatenate(ws, axis=0).astype(BF16), ones, preferred_element_type=F32)

    results = []
    for n, (g, s) in enumerate(zip(gs, streams)):
        q, k, v, lf, st, rev = s
        outs = []
        for i in range(nsb):
            r0 = i * HG_SB
            vi = v[r0:r0 + HG_SB]
            oi = o_offs.get((i, n), jnp.zeros((HG_SB, v.shape[1]), F32))
            for j in range(HG_SB):
                oi = oi + a_reps[i, n][j * HG_SB:(j + 1) * HG_SB] * vi[j:j + 1]
            outs.append(oi)
        o = o_inters[n] + jnp.concatenate(outs, axis=0)
        kd = (k * jnp.exp(g_lasts[n] - g)).astype(BF16)
        st_new = st * jnp.exp(g_lasts[n]) + lax.dot_general(v.astype(BF16), kd, TN_DIMS, preferred_element_type=F32)
        results.append((o, st_new))
    return results


def _hgrn_kernel(*refs, layer, has_s0):
    it = iter(refs)
    hq_ref, zf_ref, zb_ref, hi_ref, hg_ref, lbp_ref, g_ref = (next(it) for _ in range(7))
    s0_ref = next(it) if has_s0 else None
    next(it)
    o_ref, sout_ref, st_ref, ob_ref = next(it), next(it), next(it), next(it)
    seq = hq_ref.shape[0]
    n = seq // HG_C
    z_refs = (zf_ref, zb_ref)

    lbs = []
    for hh in range(HG_HPS):
        for d in range(2):
            p = lbp_ref[d, :, hh]
            e = jnp.exp(p - jnp.max(p, axis=0))
            lbs.append(jnp.sum(e[0:layer + 1], axis=0) / jnp.sum(e, axis=0))
            if has_s0:
                st_ref[2 * hh + d] = s0_ref[d, hh].T
            else:
                st_ref[2 * hh + d] = jnp.zeros((HEAD_W, HG_DK), F32)

    def body(ci, carry):
        streams, dests = [], []
        for hh in range(HG_HPS):
            lanes = slice(hh * HEAD_W, (hh + 1) * HEAD_W)
            for d in range(2):
                c = (n - 1 - ci) if d == 1 else ci
                rows = pl.ds(pl.multiple_of(c * HG_C, HG_C), HG_C)
                lb = lbs[2 * hh + d]
                f = lb + (1.0 - lb) * _sigmoid(z_refs[d][rows, lanes])
                streams.append((_silu(hq_ref[rows, lanes]), 1.0 - f, hi_ref[rows, lanes], jnp.log(f),
                                st_ref[2 * hh + d], d == 1))
                dests.append((o_ref if d == 0 else ob_ref, rows, lanes, 2 * hh + d))
        for (o, st_new), (dst_ref, rows, lanes, slot) in zip(_hgrn_chunks(streams), dests):
            st_ref[slot] = st_new
            dst_ref[rows, lanes] = o
        return carry

    lax.fori_loop(0, n, body, 0)
    for hh in range(HG_HPS):
        for d in range(2):
            sout_ref[d, hh] = st_ref[2 * hh + d].T

    def finish(ci, carry):
        rows = pl.ds(pl.multiple_of(ci * HG_FIN, HG_FIN), HG_FIN)
        for hh in range(HG_HPS):
            lanes = slice(hh * HEAD_W, (hh + 1) * HEAD_W)
            tot = o_ref[rows, lanes] + ob_ref[rows, lanes]
            o_ref[rows, lanes] = _rms(tot, g_ref[...]) * _silu(hg_ref[rows, lanes])
        return carry

    lax.fori_loop(0, seq // HG_FIN, finish, 0)


def _hgrn(proj, mix, hg_lb_l, hg_norm_g, *, row0, n_batch, seq, layer, s0=None):
    has_s0 = s0 is not None
    rb0 = row0 // seq
    hpg = HG_HEADS // HG_HPS
    width = HG_HPS * HEAD_W
    col0 = 3 * DA_HEADS * HEAD_W // width
    once = pl.Buffered(1)

    def col_spec(j):
        return pl.BlockSpec((seq, width), lambda b, h: (rb0 + b, col0 + j * hpg + h), pipeline_mode=once)

    in_specs = [col_spec(0), col_spec(1), col_spec(2), col_spec(3), col_spec(4),
                pl.BlockSpec((2, DEPTH + 1, HG_HPS, 1, HG_DK), lambda b, h: (0, 0, h, 0, 0)),
                pl.BlockSpec((1, HEAD_W), lambda b, h: (0, 0))]
    args = [proj, proj, proj, proj, proj, hg_lb_l, hg_norm_g]
    state_spec = pl.BlockSpec((None, 2, HG_HPS, HG_DK, HEAD_W), lambda b, h: (b, 0, h, 0, 0))
    if has_s0:
        in_specs.append(state_spec)
        args.append(s0)
    in_specs.append(pl.BlockSpec(memory_space=pl.ANY))
    args.append(mix)
    return pl.pallas_call(
        functools.partial(_hgrn_kernel, layer=layer, has_s0=has_s0),
        out_shape=(jax.ShapeDtypeStruct(mix.shape, mix.dtype),
                   jax.ShapeDtypeStruct((n_batch, 2, HG_HEADS, HG_DK, HEAD_W), F32)),
        grid=(n_batch, hpg),
        in_specs=in_specs,
        out_specs=(pl.BlockSpec((seq, width), lambda b, h: (rb0 + b, DA_HEADS * HEAD_W // width + h)),
                   state_spec),
        scratch_shapes=[pltpu.VMEM((2 * HG_HPS, HEAD_W, HG_DK), F32), pltpu.VMEM((seq, width), F32)],
        input_output_aliases={len(args) - 1: 0},
        compiler_params=_params(("arbitrary", "arbitrary")),
        name="hgrn2_state" if has_s0 else "hgrn2",
    )(*args)


def _odd_kernel(x_ref, m_ref, g_ref, win_ref, ws_ref, sb_ref, o_ref):
    h = _norm_mod(x_ref[...], g_ref[...], m_ref[0], m_ref[1])
    z = jnp.dot(h.astype(BF16), win_ref[...], preferred_element_type=F32)
    z = 0.5 * z * (1.0 + lax.erf(z * (2.0 ** -0.5)))
    width = z.shape[1] // 2
    u, v = z[:, 0:width], z[:, width:2 * width]
    mu = jnp.mean(v, axis=-1, keepdims=True)
    vc = v - mu
    var = jnp.mean(vc * vc, axis=-1, keepdims=True)
    vn = (vc * lax.rsqrt(var + EPS)).astype(BF16)
    gd = width // CM_GROUPS
    for r in range(x_ref.shape[0] // CM_CHUNK):
        rows = slice(r * CM_CHUNK, (r + 1) * CM_CHUNK)
        for gi in range(CM_GROUPS):
            cols = slice(gi * gd, (gi + 1) * gd)
            sv = jnp.dot(ws_ref[gi], vn[rows, cols], preferred_element_type=F32) + sb_ref[:, gi:gi + 1]
            o_ref[rows, cols] = u[rows, cols] * sv


def _odd_mixer(x, m, g, w_in, sgu_w, sgu_bt, mod_spec):
    nt = x.shape[0]
    return pl.pallas_call(
        _odd_kernel,
        out_shape=jax.ShapeDtypeStruct((nt, D_MODEL), F32),
        grid=(nt // TM,),
        in_specs=[
            pl.BlockSpec((TM, D_MODEL), lambda i: (i, 0)),
            mod_spec,
            pl.BlockSpec((1, D_MODEL), lambda i: (0, 0)),
            pl.BlockSpec(w_in.shape, lambda i: (0, 0)),
            pl.BlockSpec(sgu_w.shape, lambda i: (0, 0, 0)),
            pl.BlockSpec(sgu_bt.shape, lambda i: (0, 0)),
        ],
        out_specs=pl.BlockSpec((TM, D_MODEL), lambda i: (i, 0)),
        compiler_params=_params(("arbitrary",)),
        name="odd_mixer",
    )(x, m, g, w_in, sgu_w, sgu_bt)


def _post_kernel(x_ref, mix_ref, m_ref, g_ref, w_ref, rwt_ref, rb_ref, xo_ref, h2_ref, lg_ref):
    mo = jnp.dot(mix_ref[...].astype(BF16), w_ref[...], preferred_element_type=F32)
    xn = x_ref[...] + m_ref[2] * mo
    xo_ref[...] = xn
    h2 = _norm_mod(xn, g_ref[...], m_ref[3], m_ref[4])
    h2_ref[...] = h2
    hh = h2.astype(BF16)
    hl = (h2 - hh.astype(F32)).astype(BF16)
    rw = rwt_ref[...]
    rh = rw.astype(BF16)
    rl = (rw - rh.astype(F32)).astype(BF16)
    lg = (lax.dot_general(rh, hh, NT_DIMS, preferred_element_type=F32)
          + lax.dot_general(rh, hl, NT_DIMS, preferred_element_type=F32)
          + lax.dot_general(rl, hh, NT_DIMS, preferred_element_type=F32))
    lg_ref[...] = lg + rb_ref[...]


def _post(x, mix, m, g2, w_out, router_wt, router_b, mod_spec):
    nt = x.shape[0]
    row = lambda i: (i, 0)
    return pl.pallas_call(
        _post_kernel,
        out_shape=(jax.ShapeDtypeStruct((nt, D_MODEL), F32),
                   jax.ShapeDtypeStruct((nt, D_MODEL), F32),
                   jax.ShapeDtypeStruct((N_EXPERTS, nt), F32)),
        grid=(nt // TM,),
        in_specs=[
            pl.BlockSpec((TM, D_MODEL), row),
            pl.BlockSpec((TM, D_MODEL), row),
            mod_spec,
            pl.BlockSpec((1, D_MODEL), lambda i: (0, 0)),
            pl.BlockSpec((D_MODEL, D_MODEL), lambda i: (0, 0)),
            pl.BlockSpec((N_EXPERTS, D_MODEL), lambda i: (0, 0)),
            pl.BlockSpec((N_EXPERTS, 1), lambda i: (0, 0)),
        ],
        out_specs=(pl.BlockSpec((TM, D_MODEL), row), pl.BlockSpec((TM, D_MODEL), row),
                   pl.BlockSpec((N_EXPERTS, TM), lambda i: (0, i))),
        compiler_params=_params(("arbitrary",)),
        name="mixer_out_router",
    )(x, mix, m, g2, w_out, router_wt, router_b)


def _experts_on_lanes(col):
    on_lane = (lax.broadcasted_iota(I32, (N_EXPERTS, HEAD_W), 0) == lax.broadcasted_iota(I32, (N_EXPERTS, HEAD_W), 1))
    return jnp.sum(jnp.where(on_lane, col, 0.0), axis=0, keepdims=True)


def _route_a_kernel(lg_ref, idx_ref, gate_ref, lrank_ref, ttab_ref, cnt_ref, carry_ref):
    i = pl.program_id(0)

    @pl.when(i == 0)
    def _():
        carry_ref[...] = jnp.zeros(carry_ref.shape, F32)

    l = lg_ref[...]
    tn = l.shape[1]
    eio = lax.broadcasted_iota(I32, l.shape, 0)
    vals, idxs = [], []
    for _ in range(TOP_K):
        mk = jnp.max(l, axis=0, keepdims=True)
        ik = jnp.min(jnp.where(l == mk, eio, N_EXPERTS), axis=0, keepdims=True)
        vals.append(mk)
        idxs.append(ik)
        l = jnp.where(eio == ik, -jnp.inf, l)
    es = [jnp.exp(v - vals[0]) for v in vals]
    den = es[0] + es[1] + es[2] + es[3]
    sel = jnp.zeros(l.shape, F32)
    for ik in idxs:
        sel = sel + jnp.where(eio == ik, 1.0, 0.0)
    upper = jnp.where(lax.broadcasted_iota(I32, (tn, tn), 0) < lax.broadcasted_iota(I32, (tn, tn), 1),
                      1.0, 0.0).astype(BF16)
    before = jnp.dot(sel.astype(BF16), upper, preferred_element_type=F32)
    for k in range(TOP_K):
        idx_ref[k:k + 1, :] = idxs[k]
        gate_ref[k:k + 1, :] = es[k] / den
        rk = jnp.sum(jnp.where(eio == idxs[k], before, 0.0), axis=0, keepdims=True)
        lrank_ref[k:k + 1, :] = rk.astype(I32)
    tile_cnt = jnp.sum(sel, axis=1, keepdims=True)
    ttab_ref[0:1, :] = _experts_on_lanes(tile_cnt).astype(I32)
    ttab_ref[1:2, :] = _experts_on_lanes(carry_ref[...]).astype(I32)
    ttab_ref[2:DEST_ROWS, :] = jnp.zeros((DEST_ROWS - 2, HEAD_W), I32)
    carry_ref[...] = carry_ref[...] + tile_cnt
    cnt_ref[...] = carry_ref[...]


def _route_b_kernel(cnt_ref, idx_ref, lrank_ref, gate_ref, ttab_ref, slot_ref, gt_ref, tab_ref, be_ref, nv_ref,
                    pad_ref, *, n_blk_pad, n_rows):
    cnt = cnt_ref[...]
    padded = jnp.floor((cnt + (MOE_BM - 1)) / MOE_BM) * MOE_BM
    r = lax.broadcasted_iota(I32, (N_EXPERTS, N_EXPERTS), 0)
    c = lax.broadcasted_iota(I32, (N_EXPERTS, N_EXPERTS), 1)
    padded_row = jnp.sum(jnp.where(r == c, padded, 0.0), axis=0, keepdims=True)
    pend = jnp.sum(jnp.where(c <= r, padded_row, 0.0), axis=1, keepdims=True)
    pstart = pend - padded
    tn = idx_ref.shape[1]
    eio = lax.broadcasted_iota(I32, (N_EXPERTS, tn), 0)

    sub = lax.broadcasted_iota(I32, (N_EXPERTS, HEAD_W), 0)
    lane_e = lax.broadcasted_iota(I32, (N_EXPERTS, HEAD_W), 1)
    tt = ttab_ref[...].astype(F32)
    c_row, base_row = tt[0:1], tt[1:2]
    g0_row = _experts_on_lanes(pstart) + base_row
    r_row = g0_row - 8.0 * jnp.floor(g0_row / 8.0)
    seg_row = 8.0 * jnp.floor((r_row + c_row + 7.0) / 8.0)
    seg_col = jnp.sum(jnp.where(sub == lane_e, seg_row, 0.0), axis=1, keepdims=True)
    r_col = jnp.sum(jnp.where(sub == lane_e, r_row, 0.0), axis=1, keepdims=True)
    loff_row = jnp.sum(jnp.where(sub < lane_e, seg_col, 0.0), axis=0, keepdims=True)
    loff_col = jnp.sum(jnp.where(lane_e < sub, seg_row, 0.0), axis=1, keepdims=True)
    first_slot = loff_col + r_col
    slots = []
    for k in range(TOP_K):
        off = jnp.sum(jnp.where(eio == idx_ref[k:k + 1, :], first_slot, 0.0), axis=0, keepdims=True)
        slots.append(off + lrank_ref[k:k + 1, :].astype(F32))
        slot_ref[k:k + 1, :] = slots[k].astype(I32)
    slot_ref[TOP_K:DEST_ROWS, :] = jnp.zeros((DEST_ROWS - TOP_K, tn), I32)
    gp = jnp.concatenate([gate_ref[...]] + slots + [jnp.zeros((HEAD_W - 2 * TOP_K, tn), F32)], axis=0)
    gt_ref[...] = gp.T
    tab_ref[0:1, :] = c_row.astype(I32)
    tab_ref[1:2, :] = loff_row.astype(I32)
    tab_ref[2:3, :] = g0_row.astype(I32)
    tab_ref[3:DEST_ROWS, :] = jnp.zeros((DEST_ROWS - 3, HEAD_W), I32)
    blk_start = (lax.broadcasted_iota(I32, (N_EXPERTS, n_blk_pad), 1) * MOE_BM).astype(F32)
    be = jnp.sum(jnp.where(pend <= blk_start, 1.0, 0.0), axis=0, keepdims=True)
    be_ref[...] = jnp.minimum(be, N_EXPERTS - 1.0).astype(I32)
    total = jnp.sum(padded, axis=0, keepdims=True)
    nv_ref[...] = jnp.broadcast_to(total / MOE_BM, nv_ref.shape).astype(I32)
    on_lane = (lax.broadcasted_iota(I32, (N_EXPERTS, HEAD_W), 0) == lax.broadcasted_iota(I32, (N_EXPERTS, HEAD_W), 1))
    lane = lax.broadcasted_iota(I32, (1, HEAD_W), 1)
    pad_lo = jnp.sum(jnp.where(on_lane, pstart + cnt, 0.0), axis=0, keepdims=True)
    pad_hi = jnp.sum(jnp.where(on_lane, pend, 0.0), axis=0, keepdims=True)
    pad_ref[0:1, :] = jnp.where(lane == N_EXPERTS, total, pad_lo).astype(I32)
    pad_ref[1:2, :] = jnp.where(lane == N_EXPERTS, float(n_rows), pad_hi).astype(I32)
    pad_ref[2:DEST_ROWS, :] = jnp.zeros((DEST_ROWS - 2, HEAD_W), I32)


def _route(logits_t, n_blk_pad, n_rows):
    nt = logits_t.shape[1]
    tn = TM
    blk = lambda i: (0, i)
    tile = lambda i: (i, 0)
    whole = lambda i: (0, 0)
    n_tiles = nt // tn
    idx, gate, lrank, ttab, cnt = pl.pallas_call(
        _route_a_kernel,
        out_shape=(jax.ShapeDtypeStruct((TOP_K, nt), I32), jax.ShapeDtypeStruct((TOP_K, nt), F32),
                   jax.ShapeDtypeStruct((TOP_K, nt), I32), jax.ShapeDtypeStruct((n_tiles * DEST_ROWS, HEAD_W), I32),
                   jax.ShapeDtypeStruct((N_EXPERTS, 1), F32)),
        grid=(n_tiles,),
        in_specs=[pl.BlockSpec((N_EXPERTS, tn), blk)],
        out_specs=(pl.BlockSpec((TOP_K, tn), blk), pl.BlockSpec((TOP_K, tn), blk), pl.BlockSpec((TOP_K, tn), blk),
                   pl.BlockSpec((DEST_ROWS, HEAD_W), tile), pl.BlockSpec((N_EXPERTS, 1), whole)),
        scratch_shapes=[pltpu.VMEM((N_EXPERTS, 1), F32)],
        compiler_params=_params(("arbitrary",)),
        name="route_topk_rank",
    )(logits_t)
    slot, gate_t, tab, blk_exp, n_valid, pad = pl.pallas_call(
        functools.partial(_route_b_kernel, n_blk_pad=n_blk_pad, n_rows=n_rows),
        out_shape=(jax.ShapeDtypeStruct((n_tiles * DEST_ROWS, tn), I32), jax.ShapeDtypeStruct((nt, HEAD_W), F32),
                   jax.ShapeDtypeStruct((n_tiles * DEST_ROWS, HEAD_W), I32),
                   jax.ShapeDtypeStruct((1, n_blk_pad), I32), jax.ShapeDtypeStruct((1, HEAD_W), I32),
                   jax.ShapeDtypeStruct((DEST_ROWS, HEAD_W), I32)),
        grid=(n_tiles,),
        in_specs=[pl.BlockSpec((N_EXPERTS, 1), whole), pl.BlockSpec((TOP_K, tn), blk),
                  pl.BlockSpec((TOP_K, tn), blk), pl.BlockSpec((TOP_K, tn), blk),
                  pl.BlockSpec((DEST_ROWS, HEAD_W), tile)],
        out_specs=(pl.BlockSpec((DEST_ROWS, tn), tile), pl.BlockSpec((tn, HEAD_W), tile),
                   pl.BlockSpec((DEST_ROWS, HEAD_W), tile),
                   pl.BlockSpec((1, n_blk_pad), whole), pl.BlockSpec((1, HEAD_W), whole),
                   pl.BlockSpec((DEST_ROWS, HEAD_W), whole)),
        compiler_params=_params(("arbitrary",)),
        name="route_dest",
    )(cnt, idx, lrank, gate, ttab)
    return slot, gate_t, tab, blk_exp.reshape(n_blk_pad), n_valid[0, 0:1], pad


def _for_each_chunk(n8, fn):
    for bi, b in enumerate(GROUP_BITS):
        @pl.when((n8 & b) != 0)
        def _(bi=bi, b=b):
            first = pl.multiple_of((n8 - (n8 & (2 * b - 1))) * 8, 8)
            fn(first, 8 * b, bi % 2)


def _wait_rows(n8, make_copy):
    for b in WAIT_BITS:
        @pl.when((n8 & b) != 0)
        def _(b=b):
            make_copy(8 * b).wait()


def _dispatch_kernel(h_ref, slot_ref, tab_ref, pad_ref, xs_ref, tsm_ref, psm_ref, xl_ref, part_ref, zero_ref,
                     sem_tab, sem_pad, sem_rows):
    i = pl.program_id(0)
    tn = h_ref.shape[0]
    cpt = pltpu.make_async_copy(tab_ref, tsm_ref, sem_tab)
    cpt.start()

    def out_copy(src_ref, src0, dst0, rows):
        return pltpu.make_async_copy(src_ref.at[pl.ds(src0, rows)], xs_ref.at[pl.ds(dst0, rows)], sem_rows)

    @pl.when(i == 0)
    def _first_step():
        part_ref[...] = jnp.zeros(part_ref.shape, F32)
        zero_ref[...] = jnp.zeros(zero_ref.shape, F32)
        cpp = pltpu.make_async_copy(pad_ref, psm_ref, sem_pad)
        cpp.start()
        cpp.wait()

        def per_span(e, carry):
            lo8, hi8 = (psm_ref[0, e] + 7) >> 3, psm_ref[1, e] >> 3

            def zero_copy(u):
                return out_copy(zero_ref, 0, pl.multiple_of(u * 8, 8), 8)

            lax.fori_loop(lo8, hi8, lambda u, c: (zero_copy(u).start(), c)[1], 0)
            lax.fori_loop(lo8, hi8, lambda u, c: (zero_copy(u).wait(), c)[1], 0)
            return carry

        lax.fori_loop(0, N_EXPERTS + 1, per_span, 0)

    slot = lax.broadcasted_iota(I32, (L_ROWS, tn), 0)
    sl = slot_ref[...]
    hit = jnp.where(slot == sl[0:1], 1.0, jnp.where(slot == sl[1:2], 1.0, jnp.where(
        slot == sl[2:3], 1.0, jnp.where(slot == sl[3:4], 1.0, 0.0))))
    xl_ref[...] = jnp.dot(hit.astype(BF16), h_ref[...].astype(BF16), preferred_element_type=F32)
    cpt.wait()
    sub8 = lax.broadcasted_iota(I32, (8, 1), 0)

    def per_expert(e, sent8):
        cnt, lo, g0 = tsm_ref[0, e], pl.multiple_of(tsm_ref[1, e], 8), tsm_ref[2, e]
        r = g0 & 7
        g_tile = pl.multiple_of(g0 - r, 8)
        head = pl.ds(lo, 8)
        xl_ref[head, :] = xl_ref[head, :] + part_ref[e]
        full8, rem = (r + cnt) >> 3, (r + cnt) & 7
        _for_each_chunk(full8, lambda first, rows, prio: out_copy(
            xl_ref, pl.multiple_of(lo + first, 8), pl.multiple_of(g_tile + first, 8), rows).start(priority=prio))
        tail = xl_ref[pl.ds(pl.multiple_of(lo + full8 * 8, 8), 8), :]
        part_ref[e] = jnp.where(sub8 < rem, tail, 0.0)
        return sent8 + full8

    sent8 = lax.fori_loop(0, N_EXPERTS, per_expert, 0)

    def flush(e, sent8):
        end = psm_ref[0, e]
        due = jnp.logical_and(i == pl.num_programs(0) - 1, (end & 7) != 0)

        @pl.when(due)
        def _():
            pltpu.make_async_copy(part_ref.at[e], xs_ref.at[pl.ds(pl.multiple_of(end - (end & 7), 8), 8)],
                                  sem_rows).start()

        return sent8 + jnp.where(due, 1, 0)

    sent8 = lax.fori_loop(0, N_EXPERTS, flush, sent8)
    _wait_rows(sent8, lambda rows: out_copy(xl_ref, 0, 0, rows))


def _dispatch(h2, slot, tab, pad, n_rows):
    nt = h2.shape[0]
    tile = lambda i: (i, 0)
    return pl.pallas_call(
        _dispatch_kernel,
        out_shape=jax.ShapeDtypeStruct((n_rows, D_MODEL), F32),
        grid=(nt // TM,),
        in_specs=[pl.BlockSpec((TM, D_MODEL), tile),
                  pl.BlockSpec((DEST_ROWS, TM), tile),
                  pl.BlockSpec((DEST_ROWS, HEAD_W), tile),
                  pl.BlockSpec((DEST_ROWS, HEAD_W), lambda i: (0, 0))],
        out_specs=pl.BlockSpec(memory_space=pl.ANY),
        scratch_shapes=[pltpu.SMEM((DEST_ROWS, HEAD_W), I32), pltpu.SMEM((DEST_ROWS, HEAD_W), I32),
                        pltpu.VMEM((L_ROWS, D_MODEL), F32), pltpu.VMEM((N_EXPERTS, 8, D_MODEL), F32),
                        pltpu.VMEM((8, D_MODEL), F32),
                        pltpu.SemaphoreType.DMA, pltpu.SemaphoreType.DMA, pltpu.SemaphoreType.DMA],
        compiler_params=_params(("arbitrary",)),
        name="moe_dispatch",
    )(h2, slot, tab, pad)


def _expert_kernel(be_ref, nv_ref, xs_ref, w1_ref, b1_ref, w2_ref, b2_ref, ys_ref, w1b_ref, w2b_ref):
    i = pl.program_id(0)
    valid = i < nv_ref[0]
    prev = be_ref[jnp.maximum(i - 1, 0)]
    changed = jnp.logical_or(i == 0, be_ref[i] != prev)

    @pl.when(jnp.logical_and(valid, changed))
    def _cast_weights():
        step = 128
        def cast(j, carry):
            rows = pl.ds(pl.multiple_of(j * step, step), step)
            w1b_ref[rows, :] = w1_ref[rows, :].astype(BF16)
            w2b_ref[rows, :] = w2_ref[rows, :].astype(BF16)
            return carry
        lax.fori_loop(0, w1_ref.shape[0] // step, cast, 0)

    @pl.when(jnp.logical_not(valid))
    def _unused_block():
        ys_ref[...] = jnp.zeros(ys_ref.shape, F32)

    @pl.when(valid)
    def _mlp():
        h = jnp.dot(xs_ref[...].astype(BF16), w1b_ref[...], preferred_element_type=F32) + b1_ref[...]
        half = h.shape[1] // 2
        a = jnp.minimum(h[:, 0:half], SWIGLU_LIMIT)
        lin = jnp.clip(h[:, half:2 * half], -SWIGLU_LIMIT, SWIGLU_LIMIT)
        act = a * _sigmoid(SWIGLU_ALPHA * a) * (lin + 1.0)
        ys_ref[...] = jnp.dot(act.astype(BF16), w2b_ref[...], preferred_element_type=F32) + b2_ref[...]


def _experts(xs, blk_exp, n_valid, w1, b1, w2, b2, layer):
    n_rows = xs.shape[0]
    n_blk = n_rows // MOE_BM
    depth, d_e2 = w1.shape[0], w1.shape[3]

    def blk(i, be, nv):
        return jnp.minimum(i, nv[0] - 1)

    def expert(i, be, nv):
        return (layer, be[blk(i, be, nv)], 0, 0)

    grid_spec = pltpu.PrefetchScalarGridSpec(
        num_scalar_prefetch=2,
        grid=(n_blk,),
        in_specs=[
            pl.BlockSpec((MOE_BM, D_MODEL), lambda i, be, nv: (blk(i, be, nv), 0)),
            pl.BlockSpec((None, None, D_MODEL, d_e2), expert),
            pl.BlockSpec((None, None, 1, d_e2), expert),
            pl.BlockSpec((None, None, d_e2 // 2, D_MODEL), expert),
            pl.BlockSpec((None, None, 1, D_MODEL), expert),
        ],
        out_specs=pl.BlockSpec((MOE_BM, D_MODEL), lambda i, be, nv: (i, 0)),
        scratch_shapes=[pltpu.VMEM((D_MODEL, d_e2), BF16), pltpu.VMEM((d_e2 // 2, D_MODEL), BF16)],
    )
    return pl.pallas_call(
        _expert_kernel,
        out_shape=jax.ShapeDtypeStruct((n_rows, D_MODEL), F32),
        grid_spec=grid_spec,
        compiler_params=_params(("arbitrary",)),
        name="moe_experts",
    )(blk_exp, n_valid, xs, w1, b1.reshape(depth, N_EXPERTS, 1, d_e2), w2, b2.reshape(depth, N_EXPERTS, 1, D_MODEL))


def _combine_kernel(x_ref, gt_ref, tab_ref, m_ref, fg_ref, ys_ref, o_ref, tsm_ref, yl_ref, sem_tab, sem_rows,
                    *, final):
    cpt = pltpu.make_async_copy(tab_ref, tsm_ref, sem_tab)
    cpt.start()
    cpt.wait()
    tn = x_ref.shape[0]

    @pl.when(pl.program_id(0) == 0)
    def _first_step():
        yl_ref[...] = jnp.zeros(yl_ref.shape, F32)

    def in_copy(src0, dst0, rows):
        return pltpu.make_async_copy(ys_ref.at[pl.ds(src0, rows)], yl_ref.at[pl.ds(dst0, rows)], sem_rows)

    def per_expert(e, got8):
        cnt, lo, g0 = tsm_ref[0, e], pl.multiple_of(tsm_ref[1, e], 8), tsm_ref[2, e]
        r = g0 & 7
        g_tile = pl.multiple_of(g0 - r, 8)
        cover8 = jnp.where(cnt > 0, (r + cnt + 7) >> 3, 0)
        _for_each_chunk(cover8, lambda first, rows, prio: in_copy(
            pl.multiple_of(g_tile + first, 8), pl.multiple_of(lo + first, 8), rows).start(priority=prio))
        return got8 + cover8

    got8 = lax.fori_loop(0, N_EXPERTS, per_expert, 0)
    gt = gt_ref[...]
    slot = lax.broadcasted_iota(I32, (tn, L_ROWS), 1).astype(F32)
    pick = jnp.where(slot == gt[:, TOP_K:TOP_K + 1], gt[:, 0:1], 0.0)
    for k in range(1, TOP_K):
        pick = pick + jnp.where(slot == gt[:, TOP_K + k:TOP_K + k + 1], gt[:, k:k + 1], 0.0)
    _wait_rows(got8, lambda rows: in_copy(0, 0, rows))
    ph = pick.astype(BF16)
    plo = (pick - ph.astype(F32)).astype(BF16)
    y = yl_ref[...]
    yh = y.astype(BF16)
    ylo = (y - yh.astype(F32)).astype(BF16)
    acc = (jnp.dot(ph, yh, preferred_element_type=F32) + jnp.dot(plo, yh, preferred_element_type=F32)
           + jnp.dot(ph, ylo, preferred_element_type=F32))
    xn = x_ref[...] + m_ref[5] * acc
    if final:
        xn = _rms(xn, fg_ref[...])
    o_ref[...] = xn


def _combine(x, gate_t, tab, m, final_g, ys, mod_spec, final):
    nt = x.shape[0]
    tile = lambda i: (i, 0)
    return pl.pallas_call(
        functools.partial(_combine_kernel, final=final),
        out_shape=jax.ShapeDtypeStruct((nt, D_MODEL), F32),
        grid=(nt // TM,),
        in_specs=[pl.BlockSpec((TM, D_MODEL), tile),
                  pl.BlockSpec((TM, HEAD_W), tile),
                  pl.BlockSpec((DEST_ROWS, HEAD_W), tile),
                  mod_spec,
                  pl.BlockSpec((1, D_MODEL), lambda i: (0, 0)),
                  pl.BlockSpec(memory_space=pl.ANY)],
        out_specs=pl.BlockSpec((TM, D_MODEL), tile),
        scratch_shapes=[pltpu.SMEM((DEST_ROWS, HEAD_W), I32), pltpu.VMEM((L_ROWS, D_MODEL), F32),
                        pltpu.SemaphoreType.DMA, pltpu.SemaphoreType.DMA],
        compiler_params=_params(("arbitrary",)),
        name="moe_combine_final" if final else "moe_combine",
    )(x, gate_t, tab, m, final_g, ys)


def _moe(x, h2, logits_t, m, final_g, w1, b1, w2, b2, mod_spec, layer, final):
    nt = x.shape[0]
    n_blk = nt * TOP_K // MOE_BM + N_EXPERTS
    n_blk_pad = -(-n_blk // HEAD_W) * HEAD_W
    slot, gate_t, tab, blk_exp, n_valid, pad = _route(logits_t, n_blk_pad, n_blk * MOE_BM)
    xs = _dispatch(h2, slot, tab, pad, n_blk * MOE_BM)
    ys = _experts(xs, blk_exp, n_valid, w1, b1, w2, b2, layer)
    return _combine(x, gate_t, tab, m, final_g, ys, mod_spec, final)


def _rope_tables(seq):
    t = jnp.arange(seq)
    inv = 1.0 / (ROPE_THETA ** (jnp.arange(ROPE_HALF, dtype=F32) / ROPE_HALF))
    ang_r = (t // GRID_W).astype(F32)[:, None] * inv
    ang_c = (t % GRID_W).astype(F32)[:, None] * inv
    cr, sr, cc, sc = jnp.cos(ang_r), jnp.sin(ang_r), jnp.cos(ang_c), jnp.sin(ang_c)
    cos = jnp.concatenate([cr, cr, cc, cc] * 2, axis=1)
    sin = jnp.concatenate([-sr, sr, -sc, sc] * 2, axis=1)
    return cos, sin


def kernel(x_prompt, x_sample, c, cache_k, cache_v, state_hgrn, c_ctx, mod_w, mod_b, norm_g, final_norm_g,
           w_in_even, w_out_even, da_lambda, da_subln_g, hg_norm_g, hg_lb, w_in_odd, sgu_w, sgu_b, w_out_odd,
           router_w, router_b, ex_w1, ex_b1, ex_w2, ex_b2):
    bp, lp, _ = x_prompt.shape
    bs, ls, _ = x_sample.shape
    n_p, n_s = bp * lp, bs * ls
    assert lp % TM == 0 and ls % TM == 0 and n_p % ls == 0 and bs < MOD_ROWS
    depth = mod_w.shape[0]

    x = jnp.concatenate([x_prompt.reshape(n_p, D_MODEL), x_sample.reshape(n_s, D_MODEL)], axis=0)
    cond = jnp.zeros((MOD_ROWS, D_MODEL), F32).at[0:bs].set(c).at[bs].set(c_ctx)
    mod = _modulation(cond, mod_w, mod_b)
    mod_spec = _mod_spec(n_p, ls, bs)
    rope_tabs = _rope_tables(ls)
    hg_lb_h = hg_lb.reshape(2, depth + 1, HG_HEADS, 1, HG_DK)
    final_g = final_norm_g.reshape(1, D_MODEL)

    new_k, new_v, new_s = [], [], []
    for l in range(depth):
        m = mod[l]
        g1 = norm_g[l, 0].reshape(1, D_MODEL)
        g2 = norm_g[l, 1].reshape(1, D_MODEL)
        if l % 2 == 0:
            e = l // 2
            lam_init = 0.8 - 0.6 * math.exp(-0.3 * l)
            proj = _inproj(x, m, g1, w_in_even[e].astype(BF16), mod_spec)
            subln = da_subln_g[e].reshape(1, HEAD_W)
            hgn = hg_norm_g[e].reshape(1, HEAD_W)
            mix = jnp.zeros((n_p + n_s, D_MODEL), F32)
            mix, k_new, v_new = _diff_attention(proj, mix, da_lambda[e], subln, row0=0, n_batch=bp, seq=lp,
                                                lam_init=lam_init)
            mix = _diff_attention(proj, mix, da_lambda[e], subln, row0=n_p, n_batch=bs, seq=ls, lam_init=lam_init,
                                  rope_tabs=rope_tabs, ctx_k=cache_k, ctx_v=cache_v, layer_e=e)
            mix, s_p = _hgrn(proj, mix, hg_lb_h, hgn, row0=0, n_batch=bp, seq=lp, layer=l)
            mix, _ = _hgrn(proj, mix, hg_lb_h, hgn, row0=n_p, n_batch=bs, seq=ls, layer=l, s0=state_hgrn[:, e])
            w_out = w_out_even[e].astype(BF16)
            new_k.append(k_new)
            new_v.append(v_new)
            new_s.append(s_p)
        else:
            o = l // 2
            mix = _odd_mixer(x, m, g1, w_in_odd[o].astype(BF16), sgu_w[o].astype(BF16), sgu_b[o].T, mod_spec)
            w_out = w_out_odd[o].astype(BF16)
        x, h2, logits_t = _post(x, mix, m, g2, w_out, router_w[l].T, router_b[l].reshape(N_EXPERTS, 1), mod_spec)
        x = _moe(x, h2, logits_t, m, final_g, ex_w1, ex_b1, ex_w2, ex_b2, mod_spec, layer=l,
                 final=(l == depth - 1))

    y_prompt = x[0:n_p].reshape(bp, lp, D_MODEL)
    y_sample = x[n_p:].reshape(bs, ls, D_MODEL)
    return (y_prompt, y_sample, jnp.stack(new_k, axis=1), jnp.stack(new_v, axis=1), jnp.stack(new_s, axis=1))
```

```python
import functools
import math

import jax
import jax.numpy as jnp
from jax import lax
from jax.experimental import pallas as pl
from jax.experimental.pallas import tpu as pltpu

F32 = jnp.float32
BF16 = jnp.bfloat16
I32 = jnp.int32

D_MODEL = 1024
DEPTH = 2
GRID_W = 64
DA_HEADS = 4
DA_HD = 64
HG_HEADS = 4
HG_DK = 128
ROPE_THETA = 10000.0
ROPE_HALF = DA_HD // 4
CM_CHUNK = 128
CM_GROUPS = 8
N_EXPERTS = 32
TOP_K = 4
SWIGLU_ALPHA = 1.702
SWIGLU_LIMIT = 7.0
EPS = 1e-6
EVEN_IN = 4096
HEAD_W = 128

TM = 256
ATTN_SUB = 2
HG_C = 64
HG_SB = 16
HG_HPS = 2
HG_FIN = 256
L_ROWS = -(-(TOP_K * TM + N_EXPERTS * 14) // 128) * 128
GROUP_BITS = (32, 16, 8, 4, 2, 1)
WAIT_BITS = (128, 64, 32, 16, 8, 4, 2, 1)
MOE_BM = 512
DEST_ROWS = 8
MOD_ROWS = 16
V7X_VMEM_LIMIT = 56 * 1024 * 1024

NT_DIMS = (((1,), (1,)), ((), ()))
TN_DIMS = (((0,), (0,)), ((), ()))


def _params(sem):
    return pltpu.CompilerParams(dimension_semantics=sem, vmem_limit_bytes=V7X_VMEM_LIMIT)


def _sigmoid(x):
    return 1.0 / (1.0 + jnp.exp(-x))


def _silu(x):
    return x * _sigmoid(x)


def _rms(x, g):
    ms = jnp.mean(x * x, axis=-1, keepdims=True)
    return x * lax.rsqrt(ms + EPS) * g


def _norm_mod(x, g, shift, scale):
    return _rms(x, g) * (1.0 + scale) + shift


def _mod_kernel(c_ref, w_ref, b_ref, o_ref):
    s = _silu(c_ref[...])
    o_ref[...] = jnp.dot(s.astype(BF16), w_ref[...].astype(BF16), preferred_element_type=F32) + b_ref[...]


def _modulation(cond, mod_w, mod_b):
    depth = mod_w.shape[0]
    m = pl.pallas_call(
        _mod_kernel,
        out_shape=jax.ShapeDtypeStruct((depth, 6, MOD_ROWS, D_MODEL), F32),
        grid=(depth, 6),
        in_specs=[
            pl.BlockSpec((MOD_ROWS, D_MODEL), lambda l, j: (0, 0)),
            pl.BlockSpec((None, D_MODEL, D_MODEL), lambda l, j: (l, 0, j)),
            pl.BlockSpec((None, 1, D_MODEL), lambda l, j: (l, 0, j)),
        ],
        out_specs=pl.BlockSpec((None, None, MOD_ROWS, D_MODEL), lambda l, j: (l, j, 0, 0)),
        compiler_params=_params(("arbitrary", "arbitrary")),
        name="modulation",
    )(cond, mod_w, mod_b.reshape(depth, 1, 6 * D_MODEL))
    return m.transpose(0, 2, 1, 3)[:, :, :, None, :]


def _mod_spec(n_prompt, l_sample, ctx_row):
    def index(i):
        t = i * TM
        return (jnp.where(t < n_prompt, ctx_row, (t - n_prompt) // l_sample), 0, 0, 0)
    return pl.BlockSpec((None, 6, 1, D_MODEL), index)


def _inproj_kernel(x_ref, m_ref, g_ref, w_ref, o_ref):
    h = _norm_mod(x_ref[...], g_ref[...], m_ref[0], m_ref[1])
    o_ref[...] = jnp.dot(h.astype(BF16), w_ref[...], preferred_element_type=F32)


def _inproj(x, m, g, w, mod_spec):
    nt, n_out = x.shape[0], w.shape[1]
    return pl.pallas_call(
        _inproj_kernel,
        out_shape=jax.ShapeDtypeStruct((nt, n_out), F32),
        grid=(nt // TM,),
        in_specs=[
            pl.BlockSpec((TM, D_MODEL), lambda i: (i, 0)),
            mod_spec,
            pl.BlockSpec((1, D_MODEL), lambda i: (0, 0)),
            pl.BlockSpec((D_MODEL, n_out), lambda i: (0, 0)),
        ],
        out_specs=pl.BlockSpec((TM, n_out), lambda i: (i, 0)),
        compiler_params=_params(("arbitrary",)),
        name="even_inproj",
    )(x, m, g, w)


def _rope(x, cos, sin_signed):
    lane = lax.broadcasted_iota(I32, x.shape, 1)
    first = ((lane // ROPE_HALF) % 2) == 0
    partner = jnp.where(first, pltpu.roll(x, HEAD_W - ROPE_HALF, 1), pltpu.roll(x, ROPE_HALF, 1))
    return x * cos + partner * sin_signed


def _attn_kernel(*refs, rope, ctx, bq, chunks, lam_init):
    it = iter(refs)
    lam_ref, g_ref, q_ref, k_ref, v_ref = (next(it) for _ in range(5))
    if rope:
        cq_ref, sq_ref, ck_ref, sk_ref = (next(it) for _ in range(4))
    if ctx:
        kc_ref, vc_ref = next(it), next(it)
    next(it)
    o_ref = next(it)
    if not ctx:
        ko_ref, vo_ref = next(it), next(it)
    kt_ref, vx_ref = next(it), next(it)
    l_new = k_ref.shape[0]
    l_ctx = kc_ref.shape[0] if ctx else 0

    @pl.when(pl.program_id(2) == 0)
    def _prepare_keys():
        if ctx:
            kt_ref[:, 0:l_ctx] = kc_ref[...].T.astype(BF16)
            vx_ref[0:l_ctx, 0:HEAD_W] = vc_ref[...].astype(BF16)
        else:
            ko_ref[...] = k_ref[...]
            vo_ref[...] = v_ref[...]
        step = min(512, l_new)
        for c0 in range(0, l_new, step):
            k = k_ref[c0:c0 + step, :]
            if rope:
                k = _rope(k, ck_ref[c0:c0 + step, :], sk_ref[c0:c0 + step, :])
            kt_ref[:, l_ctx + c0:l_ctx + c0 + step] = k.T.astype(BF16)
            vx_ref[l_ctx + c0:l_ctx + c0 + step, 0:HEAD_W] = v_ref[c0:c0 + step, :].astype(BF16)
        vx_ref[:, HEAD_W:2 * HEAD_W] = jnp.ones((l_ctx + l_new, HEAD_W), BF16)

    n_sub = q_ref.shape[0] // bq
    lane = lax.broadcasted_iota(I32, (bq, HEAD_W), 1)
    lo = lane < DA_HD
    zero = jnp.zeros((bq, HEAD_W), F32)
    qss = []
    for u in range(n_sub):
        rows = slice(u * bq, (u + 1) * bq)
        q = q_ref[rows, :]
        if rope:
            q = _rope(q, cq_ref[rows, :], sq_ref[rows, :])
        q = q * (DA_HD ** -0.5)
        qss.append(jnp.concatenate([jnp.where(lo, q, zero), jnp.where(lo, zero, q)], axis=0).astype(BF16))

    def scores(qs, chunk):
        return jnp.dot(qs, kt_ref[:, chunk[0]:chunk[0] + chunk[1]], preferred_element_type=F32)

    m = [jnp.full((2 * bq, 1), -jnp.inf, F32)] * n_sub
    l = [jnp.zeros((2 * bq, 1), F32)] * n_sub
    acc = [jnp.zeros((2 * bq, HEAD_W), F32)] * n_sub
    s_next = [scores(qs, chunks[0]) for qs in qss]
    for ci, (c0, cs) in enumerate(chunks):
        s_cur = s_next
        if ci + 1 < len(chunks):
            s_next = [scores(qs, chunks[ci + 1]) for qs in qss]
        for u in range(n_sub):
            s = s_cur[u]
            mn = jnp.maximum(m[u], jnp.max(s, axis=-1, keepdims=True))
            alpha = jnp.exp(m[u] - mn)
            p = jnp.exp(s - mn).astype(BF16)
            pv = jnp.dot(p, vx_ref[c0:c0 + cs, :], preferred_element_type=F32)
            acc[u] = alpha * acc[u] + pv[:, 0:HEAD_W]
            l[u] = alpha * l[u] + pv[:, HEAD_W:HEAD_W + 1]
            m[u] = mn
    lp = lam_ref[...]
    lam = (jnp.exp(jnp.sum(lp[0:1] * lp[1:2], axis=-1, keepdims=True))
           - jnp.exp(jnp.sum(lp[2:3] * lp[3:4], axis=-1, keepdims=True)) + lam_init)
    for u in range(n_sub):
        o2 = acc[u] / l[u]
        o = o2[0:bq] - lam * o2[bq:2 * bq]
        o_ref[u * bq:(u + 1) * bq, :] = _rms(o, g_ref[...]) * (1.0 - lam_init)


def _attn_chunks(l_ctx, l_new):
    total = l_ctx + l_new
    if total <= 1280:
        return ((0, total),)
    chunks, c0 = [], 0
    first = l_ctx + 1024
    chunks.append((0, first))
    c0 = first
    while c0 < total:
        cs = min(1024, total - c0)
        chunks.append((c0, cs))
        c0 += cs
    return tuple(chunks)


def _diff_attention(proj, mix, lam_p, subln_g, *, row0, n_batch, seq, lam_init, rope_tabs=None,
                    ctx_k=None, ctx_v=None, layer_e=0):
    rope, ctx = rope_tabs is not None, ctx_k is not None
    n_rows = proj.shape[0]
    bq = min(ATTN_SUB * 256, seq)
    nq = seq // bq
    l_ctx = ctx_k.shape[3] if ctx else 0
    qb0, kb0 = row0 // bq, row0 // seq
    in_specs = [
        pl.BlockSpec((4, DA_HD), lambda b, h, i: (0, 0)),
        pl.BlockSpec((1, HEAD_W), lambda b, h, i: (0, 0)),
        pl.BlockSpec((bq, HEAD_W), lambda b, h, i: (qb0 + b * nq + i, h)),
        pl.BlockSpec((seq, HEAD_W), lambda b, h, i: (kb0 + b, DA_HEADS + h)),
        pl.BlockSpec((seq, HEAD_W), lambda b, h, i: (kb0 + b, 2 * DA_HEADS + h)),
    ]
    args = [lam_p, subln_g, proj, proj, proj]
    if rope:
        cos, sin = rope_tabs
        in_specs += [
            pl.BlockSpec((bq, HEAD_W), lambda b, h, i: (i, 0)),
            pl.BlockSpec((bq, HEAD_W), lambda b, h, i: (i, 0)),
            pl.BlockSpec((seq, HEAD_W), lambda b, h, i: (0, 0)),
            pl.BlockSpec((seq, HEAD_W), lambda b, h, i: (0, 0)),
        ]
        args += [cos, sin, cos, sin]
    if ctx:
        in_specs += [
            pl.BlockSpec((None, None, None, l_ctx, HEAD_W), lambda b, h, i: (b, layer_e, h, 0, 0)),
            pl.BlockSpec((None, None, None, l_ctx, HEAD_W), lambda b, h, i: (b, layer_e, h, 0, 0)),
        ]
        args += [ctx_k, ctx_v]
    mix_sds = jax.ShapeDtypeStruct((n_rows, D_MODEL), F32)
    mix_spec = pl.BlockSpec((bq, HEAD_W), lambda b, h, i: (qb0 + b * nq + i, h))
    in_specs.append(pl.BlockSpec(memory_space=pl.ANY))
    args.append(mix)
    aliases = {len(args) - 1: 0}
    if ctx:
        out_shape, out_specs = mix_sds, mix_spec
    else:
        cache_sds = jax.ShapeDtypeStruct((n_batch, DA_HEADS, seq, HEAD_W), F32)
        cache_spec = pl.BlockSpec((None, None, seq, HEAD_W), lambda b, h, i: (b, h, 0, 0))
        out_shape, out_specs = (mix_sds, cache_sds, cache_sds), (mix_spec, cache_spec, cache_spec)
    kernel = functools.partial(_attn_kernel, rope=rope, ctx=ctx, bq=bq // ATTN_SUB,
                               chunks=_attn_chunks(l_ctx, seq), lam_init=lam_init)
    return pl.pallas_call(
        kernel,
        out_shape=out_shape,
        grid=(n_batch, DA_HEADS, nq),
        in_specs=in_specs,
        out_specs=out_specs,
        scratch_shapes=[pltpu.VMEM((HEAD_W, l_ctx + seq), BF16), pltpu.VMEM((l_ctx + seq, 2 * HEAD_W), BF16)],
        input_output_aliases=aliases,
        compiler_params=_params(("arbitrary", "arbitrary", "arbitrary")),
        name="diff_attention_ctx" if ctx else "diff_attention",
    )(*args)


def _hgrn_chunks(streams):
    c, nsb = HG_C, HG_C // HG_SB
    row = lax.broadcasted_iota(I32, (c, c), 0)
    col = lax.broadcasted_iota(I32, (c, c), 1)
    tri = {r: jnp.where((col >= row) if r else (col <= row), 1.0, 0.0).astype(BF16) for r in (False, True)}
    ones = jnp.ones((HG_DK, HG_DK), BF16)
    t_idx = lax.broadcasted_iota(I32, (HG_SB, 1), 0)

    g3s = []
    for q, k, v, lf, st, rev in streams:
        hi = lf.astype(BF16)
        r1 = lf - hi.astype(F32)
        mid = r1.astype(BF16)
        lo = (r1 - mid.astype(F32)).astype(BF16)
        g3s.append(jnp.dot(tri[rev], jnp.concatenate([hi, mid, lo], axis=1), preferred_element_type=F32))
    gs = [g3[:, 0:HG_DK] + g3[:, HG_DK:2 * HG_DK] + g3[:, 2 * HG_DK:3 * HG_DK] for g3 in g3s]
    g_lasts = [g[0:1] if s[5] else g[c - 1:c] for g, s in zip(gs, streams)]

    o_inters = [lax.dot_general((s[0] * jnp.exp(g)).astype(BF16), s[4].astype(BF16), NT_DIMS,
                                preferred_element_type=F32) for g, s in zip(gs, streams)]

    def outside(i, rev):
        r0 = i * HG_SB
        if rev:
            return (r0 + HG_SB, c, r0 + HG_SB) if r0 + HG_SB < c else None
        return (0, r0, r0 - 1) if r0 > 0 else None

    a_offs = {}
    for i in range(nsb):
        for n, (g, s) in enumerate(zip(gs, streams)):
            span = outside(i, s[5])
            if span is None:
                continue
            s_lo, s_hi, rb = span
            r0 = i * HG_SB
            ref_g = g[rb:rb + 1]
            qd = (s[0][r0:r0 + HG_SB] * jnp.exp(g[r0:r0 + HG_SB] - ref_g)).astype(BF16)
            kd = (s[1][s_lo:s_hi] * jnp.exp(ref_g - g[s_lo:s_hi])).astype(BF16)
            a_offs[i, n] = lax.dot_general(qd, kd, NT_DIMS, preferred_element_type=F32)
    o_offs = {}
    for (i, n), a in a_offs.items():
        s_lo, s_hi, _ = outside(i, streams[n][5])
        o_offs[i, n] = jnp.dot(a.astype(BF16), streams[n][2][s_lo:s_hi].astype(BF16), preferred_element_type=F32)

    def rows_of(j, rev):
        half = HG_SB // 2
        if rev:
            return (0, half) if j < half else (0, HG_SB)
        return (half, HG_SB) if j >= half else (0, HG_SB)

    a_reps = {}
    for i in range(nsb):
        r0 = i * HG_SB
        for n, (g, s) in enumerate(zip(gs, streams)):
            gi, qi, ki = g[r0:r0 + HG_SB], s[0][r0:r0 + HG_SB], s[1][r0:r0 + HG_SB]
            ws = []
            for j in range(HG_SB):
                t0, t1 = rows_of(j, s[5])
                valid = (t_idx[t0:t1] <= j) if s[5] else (t_idx[t0:t1] >= j)
                e = jnp.exp(jnp.where(valid, gi[t0:t1] - gi[j:j + 1], -1e30))
                ws.append(qi[t0:t1] * (ki[j:j + 1] * e))
            a_reps[i, n] = jnp.dot(jnp.concatenate(ws, axis=0).astype(BF16), ones, preferred_element_type=F32)

    results = []
    for n, (g, s) in enumerate(zip(gs, streams)):
        q, k, v, lf, st, rev = s
        outs = []
        for i in range(nsb):
            r0 = i * HG_SB
            vi = v[r0:r0 + HG_SB]
            zero8 = jnp.zeros((HG_SB // 2, v.shape[1]), F32)
            lo_acc, hi_acc, at = zero8, zero8, 0
            for j in range(HG_SB):
                t0, t1 = rows_of(j, rev)
                term = a_reps[i, n][at:at + (t1 - t0)] * vi[j:j + 1]
                at += t1 - t0
                if t1 - t0 == HG_SB:
                    lo_acc, hi_acc = lo_acc + term[0:HG_SB // 2], hi_acc + term[HG_SB // 2:HG_SB]
                elif t0 == 0:
                    lo_acc = lo_acc + term
                else:
                    hi_acc = hi_acc + term
            oi = jnp.concatenate([lo_acc, hi_acc], axis=0)
            if (i, n) in o_offs:
                oi = oi + o_offs[i, n]
            outs.append(oi)
        o = o_inters[n] + jnp.concatenate(outs, axis=0)
        kd = (k * jnp.exp(g_lasts[n] - g)).astype(BF16)
        st_new = st * jnp.exp(g_lasts[n]) + lax.dot_general(v.astype(BF16), kd, TN_DIMS, preferred_element_type=F32)
        results.append((o, st_new))
    return results


def _hgrn_kernel(*refs, layer, has_s0):
    it = iter(refs)
    hq_ref, zf_ref, zb_ref, hi_ref, hg_ref, lbp_ref, g_ref = (next(it) for _ in range(7))
    s0_ref = next(it) if has_s0 else None
    next(it)
    o_ref, sout_ref, st_ref, ob_ref = next(it), next(it), next(it), next(it)
    seq = hq_ref.shape[0]
    n = seq // HG_C
    z_refs = (zf_ref, zb_ref)

    lbs = []
    for hh in range(HG_HPS):
        for d in range(2):
            p = lbp_ref[d, :, hh]
            e = jnp.exp(p - jnp.max(p, axis=0))
            lbs.append(jnp.sum(e[0:layer + 1], axis=0) / jnp.sum(e, axis=0))
            if has_s0:
                st_ref[2 * hh + d] = s0_ref[d, hh].T
            else:
                st_ref[2 * hh + d] = jnp.zeros((HEAD_W, HG_DK), F32)

    def body(ci, carry):
        streams, dests = [], []
        for hh in range(HG_HPS):
            lanes = slice(hh * HEAD_W, (hh + 1) * HEAD_W)
            for d in range(2):
                c = (n - 1 - ci) if d == 1 else ci
                rows = pl.ds(pl.multiple_of(c * HG_C, HG_C), HG_C)
                lb = lbs[2 * hh + d]
                f = lb + (1.0 - lb) * _sigmoid(z_refs[d][rows, lanes])
                streams.append((_silu(hq_ref[rows, lanes]), 1.0 - f, hi_ref[rows, lanes], jnp.log(f),
                                st_ref[2 * hh + d], d == 1))
                dests.append((o_ref if d == 0 else ob_ref, rows, lanes, 2 * hh + d))
        for (o, st_new), (dst_ref, rows, lanes, slot) in zip(_hgrn_chunks(streams), dests):
            st_ref[slot] = st_new
            dst_ref[rows, lanes] = o
        return carry

    lax.fori_loop(0, n, body, 0)
    for hh in range(HG_HPS):
        for d in range(2):
            sout_ref[d, hh] = st_ref[2 * hh + d].T

    def finish(ci, carry):
        rows = pl.ds(pl.multiple_of(ci * HG_FIN, HG_FIN), HG_FIN)
        for hh in range(HG_HPS):
            lanes = slice(hh * HEAD_W, (hh + 1) * HEAD_W)
            tot = o_ref[rows, lanes] + ob_ref[rows, lanes]
            o_ref[rows, lanes] = _rms(tot, g_ref[...]) * _silu(hg_ref[rows, lanes])
        return carry

    lax.fori_loop(0, seq // HG_FIN, finish, 0)


def _hgrn(proj, mix, hg_lb_l, hg_norm_g, *, row0, n_batch, seq, layer, s0=None):
    has_s0 = s0 is not None
    rb0 = row0 // seq
    hpg = HG_HEADS // HG_HPS
    width = HG_HPS * HEAD_W
    col0 = 3 * DA_HEADS * HEAD_W // width
    once = pl.Buffered(1)

    def col_spec(j):
        return pl.BlockSpec((seq, width), lambda b, h: (rb0 + b, col0 + j * hpg + h), pipeline_mode=once)

    in_specs = [col_spec(0), col_spec(1), col_spec(2), col_spec(3), col_spec(4),
                pl.BlockSpec((2, DEPTH + 1, HG_HPS, 1, HG_DK), lambda b, h: (0, 0, h, 0, 0)),
                pl.BlockSpec((1, HEAD_W), lambda b, h: (0, 0))]
    args = [proj, proj, proj, proj, proj, hg_lb_l, hg_norm_g]
    state_spec = pl.BlockSpec((None, 2, HG_HPS, HG_DK, HEAD_W), lambda b, h: (b, 0, h, 0, 0))
    if has_s0:
        in_specs.append(state_spec)
        args.append(s0)
    in_specs.append(pl.BlockSpec(memory_space=pl.ANY))
    args.append(mix)
    return pl.pallas_call(
        functools.partial(_hgrn_kernel, layer=layer, has_s0=has_s0),
        out_shape=(jax.ShapeDtypeStruct(mix.shape, mix.dtype),
                   jax.ShapeDtypeStruct((n_batch, 2, HG_HEADS, HG_DK, HEAD_W), F32)),
        grid=(n_batch, hpg),
        in_specs=in_specs,
        out_specs=(pl.BlockSpec((seq, width), lambda b, h: (rb0 + b, DA_HEADS * HEAD_W // width + h)),
                   state_spec),
        scratch_shapes=[pltpu.VMEM((2 * HG_HPS, HEAD_W, HG_DK), F32), pltpu.VMEM((seq, width), F32)],
        input_output_aliases={len(args) - 1: 0},
        compiler_params=_params(("arbitrary", "arbitrary")),
        name="hgrn2_state" if has_s0 else "hgrn2",
    )(*args)


def _odd_kernel(x_ref, m_ref, g_ref, win_ref, ws_ref, sb_ref, o_ref):
    h = _norm_mod(x_ref[...], g_ref[...], m_ref[0], m_ref[1])
    z = jnp.dot(h.astype(BF16), win_ref[...], preferred_element_type=F32)
    z = 0.5 * z * (1.0 + lax.erf(z * (2.0 ** -0.5)))
    width = z.shape[1] // 2
    u, v = z[:, 0:width], z[:, width:2 * width]
    mu = jnp.mean(v, axis=-1, keepdims=True)
    vc = v - mu
    var = jnp.mean(vc * vc, axis=-1, keepdims=True)
    vn = (vc * lax.rsqrt(var + EPS)).astype(BF16)
    gd = width // CM_GROUPS
    for r in range(x_ref.shape[0] // CM_CHUNK):
        rows = slice(r * CM_CHUNK, (r + 1) * CM_CHUNK)
        for gi in range(CM_GROUPS):
            cols = slice(gi * gd, (gi + 1) * gd)
            sv = jnp.dot(ws_ref[gi], vn[rows, cols], preferred_element_type=F32) + sb_ref[:, gi:gi + 1]
            o_ref[rows, cols] = u[rows, cols] * sv


def _odd_mixer(x, m, g, w_in, sgu_w, sgu_bt, mod_spec):
    nt = x.shape[0]
    return pl.pallas_call(
        _odd_kernel,
        out_shape=jax.ShapeDtypeStruct((nt, D_MODEL), F32),
        grid=(nt // TM,),
        in_specs=[
            pl.BlockSpec((TM, D_MODEL), lambda i: (i, 0)),
            mod_spec,
            pl.BlockSpec((1, D_MODEL), lambda i: (0, 0)),
            pl.BlockSpec(w_in.shape, lambda i: (0, 0)),
            pl.BlockSpec(sgu_w.shape, lambda i: (0, 0, 0)),
            pl.BlockSpec(sgu_bt.shape, lambda i: (0, 0)),
        ],
        out_specs=pl.BlockSpec((TM, D_MODEL), lambda i: (i, 0)),
        compiler_params=_params(("arbitrary",)),
        name="odd_mixer",
    )(x, m, g, w_in, sgu_w, sgu_bt)


def _post_kernel(x_ref, mix_ref, m_ref, g_ref, w_ref, rwt_ref, rb_ref, xo_ref, h2_ref, lg_ref):
    mo = jnp.dot(mix_ref[...].astype(BF16), w_ref[...], preferred_element_type=F32)
    xn = x_ref[...] + m_ref[2] * mo
    xo_ref[...] = xn
    h2 = _norm_mod(xn, g_ref[...], m_ref[3], m_ref[4])
    h2_ref[...] = h2
    hh = h2.astype(BF16)
    hl = (h2 - hh.astype(F32)).astype(BF16)
    rw = rwt_ref[...]
    rh = rw.astype(BF16)
    rl = (rw - rh.astype(F32)).astype(BF16)
    lg = (lax.dot_general(rh, hh, NT_DIMS, preferred_element_type=F32)
          + lax.dot_general(rh, hl, NT_DIMS, preferred_element_type=F32)
          + lax.dot_general(rl, hh, NT_DIMS, preferred_element_type=F32))
    lg_ref[...] = lg + rb_ref[...]


def _post(x, mix, m, g2, w_out, router_wt, router_b, mod_spec):
    nt = x.shape[0]
    row = lambda i: (i, 0)
    return pl.pallas_call(
        _post_kernel,
        out_shape=(jax.ShapeDtypeStruct((nt, D_MODEL), F32),
                   jax.ShapeDtypeStruct((nt, D_MODEL), F32),
                   jax.ShapeDtypeStruct((N_EXPERTS, nt), F32)),
        grid=(nt // TM,),
        in_specs=[
            pl.BlockSpec((TM, D_MODEL), row),
            pl.BlockSpec((TM, D_MODEL), row),
            mod_spec,
            pl.BlockSpec((1, D_MODEL), lambda i: (0, 0)),
            pl.BlockSpec((D_MODEL, D_MODEL), lambda i: (0, 0)),
            pl.BlockSpec((N_EXPERTS, D_MODEL), lambda i: (0, 0)),
            pl.BlockSpec((N_EXPERTS, 1), lambda i: (0, 0)),
        ],
        out_specs=(pl.BlockSpec((TM, D_MODEL), row), pl.BlockSpec((TM, D_MODEL), row),
                   pl.BlockSpec((N_EXPERTS, TM), lambda i: (0, i))),
        compiler_params=_params(("arbitrary",)),
        name="mixer_out_router",
    )(x, mix, m, g2, w_out, router_wt, router_b)


def _experts_on_lanes(col):
    on_lane = (lax.broadcasted_iota(I32, (N_EXPERTS, HEAD_W), 0) == lax.broadcasted_iota(I32, (N_EXPERTS, HEAD_W), 1))
    return jnp.sum(jnp.where(on_lane, col, 0.0), axis=0, keepdims=True)


def _route_a_kernel(lg_ref, idx_ref, gate_ref, lrank_ref, ttab_ref, cnt_ref, carry_ref):
    i = pl.program_id(0)

    @pl.when(i == 0)
    def _():
        carry_ref[...] = jnp.zeros(carry_ref.shape, F32)

    l = lg_ref[...]
    tn = l.shape[1]
    eio = lax.broadcasted_iota(I32, l.shape, 0)
    vals, idxs = [], []
    for _ in range(TOP_K):
        mk = jnp.max(l, axis=0, keepdims=True)
        ik = jnp.min(jnp.where(l == mk, eio, N_EXPERTS), axis=0, keepdims=True)
        vals.append(mk)
        idxs.append(ik)
        l = jnp.where(eio == ik, -jnp.inf, l)
    es = [jnp.exp(v - vals[0]) for v in vals]
    den = es[0] + es[1] + es[2] + es[3]
    sel = jnp.zeros(l.shape, F32)
    for ik in idxs:
        sel = sel + jnp.where(eio == ik, 1.0, 0.0)
    upper = jnp.where(lax.broadcasted_iota(I32, (tn, tn), 0) < lax.broadcasted_iota(I32, (tn, tn), 1),
                      1.0, 0.0).astype(BF16)
    before = jnp.dot(sel.astype(BF16), upper, preferred_element_type=F32)
    for k in range(TOP_K):
        idx_ref[k:k + 1, :] = idxs[k]
        gate_ref[k:k + 1, :] = es[k] / den
        rk = jnp.sum(jnp.where(eio == idxs[k], before, 0.0), axis=0, keepdims=True)
        lrank_ref[k:k + 1, :] = rk.astype(I32)
    tile_cnt = jnp.sum(sel, axis=1, keepdims=True)
    ttab_ref[0:1, :] = _experts_on_lanes(tile_cnt).astype(I32)
    ttab_ref[1:2, :] = _experts_on_lanes(carry_ref[...]).astype(I32)
    ttab_ref[2:DEST_ROWS, :] = jnp.zeros((DEST_ROWS - 2, HEAD_W), I32)
    carry_ref[...] = carry_ref[...] + tile_cnt
    cnt_ref[...] = carry_ref[...]


def _route_b_kernel(cnt_ref, idx_ref, lrank_ref, gate_ref, ttab_ref, slot_ref, gt_ref, tab_ref, be_ref, nv_ref,
                    pad_ref, *, n_blk_pad, n_rows):
    cnt = cnt_ref[...]
    padded = jnp.floor((cnt + (MOE_BM - 1)) / MOE_BM) * MOE_BM
    r = lax.broadcasted_iota(I32, (N_EXPERTS, N_EXPERTS), 0)
    c = lax.broadcasted_iota(I32, (N_EXPERTS, N_EXPERTS), 1)
    padded_row = jnp.sum(jnp.where(r == c, padded, 0.0), axis=0, keepdims=True)
    pend = jnp.sum(jnp.where(c <= r, padded_row, 0.0), axis=1, keepdims=True)
    pstart = pend - padded
    tn = idx_ref.shape[1]
    eio = lax.broadcasted_iota(I32, (N_EXPERTS, tn), 0)

    sub = lax.broadcasted_iota(I32, (N_EXPERTS, HEAD_W), 0)
    lane_e = lax.broadcasted_iota(I32, (N_EXPERTS, HEAD_W), 1)
    tt = ttab_ref[...].astype(F32)
    c_row, base_row = tt[0:1], tt[1:2]
    g0_row = _experts_on_lanes(pstart) + base_row
    r_row = g0_row - 8.0 * jnp.floor(g0_row / 8.0)
    seg_row = 8.0 * jnp.floor((r_row + c_row + 7.0) / 8.0)
    seg_col = jnp.sum(jnp.where(sub == lane_e, seg_row, 0.0), axis=1, keepdims=True)
    r_col = jnp.sum(jnp.where(sub == lane_e, r_row, 0.0), axis=1, keepdims=True)
    loff_row = jnp.sum(jnp.where(sub < lane_e, seg_col, 0.0), axis=0, keepdims=True)
    loff_col = jnp.sum(jnp.where(lane_e < sub, seg_row, 0.0), axis=1, keepdims=True)
    first_slot = loff_col + r_col
    slots = []
    for k in range(TOP_K):
        off = jnp.sum(jnp.where(eio == idx_ref[k:k + 1, :], first_slot, 0.0), axis=0, keepdims=True)
        slots.append(off + lrank_ref[k:k + 1, :].astype(F32))
        slot_ref[k:k + 1, :] = slots[k].astype(I32)
    slot_ref[TOP_K:DEST_ROWS, :] = jnp.zeros((DEST_ROWS - TOP_K, tn), I32)
    gp = jnp.concatenate([gate_ref[...]] + slots + [jnp.zeros((HEAD_W - 2 * TOP_K, tn), F32)], axis=0)
    gt_ref[...] = gp.T
    tab_ref[0:1, :] = c_row.astype(I32)
    tab_ref[1:2, :] = loff_row.astype(I32)
    tab_ref[2:3, :] = g0_row.astype(I32)
    tab_ref[3:DEST_ROWS, :] = jnp.zeros((DEST_ROWS - 3, HEAD_W), I32)
    blk_start = (lax.broadcasted_iota(I32, (N_EXPERTS, n_blk_pad), 1) * MOE_BM).astype(F32)
    be = jnp.sum(jnp.where(pend <= blk_start, 1.0, 0.0), axis=0, keepdims=True)
    be_ref[...] = jnp.minimum(be, N_EXPERTS - 1.0).astype(I32)
    total = jnp.sum(padded, axis=0, keepdims=True)
    nv_ref[...] = jnp.broadcast_to(total / MOE_BM, nv_ref.shape).astype(I32)
    on_lane = (lax.broadcasted_iota(I32, (N_EXPERTS, HEAD_W), 0) == lax.broadcasted_iota(I32, (N_EXPERTS, HEAD_W), 1))
    lane = lax.broadcasted_iota(I32, (1, HEAD_W), 1)
    pad_lo = jnp.sum(jnp.where(on_lane, pstart + cnt, 0.0), axis=0, keepdims=True)
    pad_hi = jnp.sum(jnp.where(on_lane, pend, 0.0), axis=0, keepdims=True)
    pad_ref[0:1, :] = jnp.where(lane == N_EXPERTS, total, pad_lo).astype(I32)
    pad_ref[1:2, :] = jnp.where(lane == N_EXPERTS, float(n_rows), pad_hi).astype(I32)
    pad_ref[2:DEST_ROWS, :] = jnp.zeros((DEST_ROWS - 2, HEAD_W), I32)


def _route(logits_t, n_blk_pad, n_rows):
    nt = logits_t.shape[1]
    tn = TM
    blk = lambda i: (0, i)
    tile = lambda i: (i, 0)
    whole = lambda i: (0, 0)
    n_tiles = nt // tn
    idx, gate, lrank, ttab, cnt = pl.pallas_call(
        _route_a_kernel,
        out_shape=(jax.ShapeDtypeStruct((TOP_K, nt), I32), jax.ShapeDtypeStruct((TOP_K, nt), F32),
                   jax.ShapeDtypeStruct((TOP_K, nt), I32), jax.ShapeDtypeStruct((n_tiles * DEST_ROWS, HEAD_W), I32),
                   jax.ShapeDtypeStruct((N_EXPERTS, 1), F32)),
        grid=(n_tiles,),
        in_specs=[pl.BlockSpec((N_EXPERTS, tn), blk)],
        out_specs=(pl.BlockSpec((TOP_K, tn), blk), pl.BlockSpec((TOP_K, tn), blk), pl.BlockSpec((TOP_K, tn), blk),
                   pl.BlockSpec((DEST_ROWS, HEAD_W), tile), pl.BlockSpec((N_EXPERTS, 1), whole)),
        scratch_shapes=[pltpu.VMEM((N_EXPERTS, 1), F32)],
        compiler_params=_params(("arbitrary",)),
        name="route_topk_rank",
    )(logits_t)
    slot, gate_t, tab, blk_exp, n_valid, pad = pl.pallas_call(
        functools.partial(_route_b_kernel, n_blk_pad=n_blk_pad, n_rows=n_rows),
        out_shape=(jax.ShapeDtypeStruct((n_tiles * DEST_ROWS, tn), I32), jax.ShapeDtypeStruct((nt, HEAD_W), F32),
                   jax.ShapeDtypeStruct((n_tiles * DEST_ROWS, HEAD_W), I32),
                   jax.ShapeDtypeStruct((1, n_blk_pad), I32), jax.ShapeDtypeStruct((1, HEAD_W), I32),
                   jax.ShapeDtypeStruct((DEST_ROWS, HEAD_W), I32)),
        grid=(n_tiles,),
        in_specs=[pl.BlockSpec((N_EXPERTS, 1), whole), pl.BlockSpec((TOP_K, tn), blk),
                  pl.BlockSpec((TOP_K, tn), blk), pl.BlockSpec((TOP_K, tn), blk),
                  pl.BlockSpec((DEST_ROWS, HEAD_W), tile)],
        out_specs=(pl.BlockSpec((DEST_ROWS, tn), tile), pl.BlockSpec((tn, HEAD_W), tile),
                   pl.BlockSpec((DEST_ROWS, HEAD_W), tile),
                   pl.BlockSpec((1, n_blk_pad), whole), pl.BlockSpec((1, HEAD_W), whole),
                   pl.BlockSpec((DEST_ROWS, HEAD_W), whole)),
        compiler_params=_params(("arbitrary",)),
        name="route_dest",
    )(cnt, idx, lrank, gate, ttab)
    return slot, gate_t, tab, blk_exp.reshape(n_blk_pad), n_valid[0, 0:1], pad


def _for_each_chunk(n8, fn):
    for bi, b in enumerate(GROUP_BITS):
        @pl.when((n8 & b) != 0)
        def _(bi=bi, b=b):
            first = pl.multiple_of((n8 - (n8 & (2 * b - 1))) * 8, 8)
            fn(first, 8 * b, bi % 2)


def _wait_rows(n8, make_copy):
    for b in WAIT_BITS:
        @pl.when((n8 & b) != 0)
        def _(b=b):
            make_copy(8 * b).wait()


def _dispatch_kernel(h_ref, slot_ref, tab_ref, pad_ref, xs_ref, tsm_ref, psm_ref, sent_ref, xl2_ref, part_ref,
                     zero_ref, sem_tab, sem_pad, sem_rows2):
    i = pl.program_id(0)
    last = pl.num_programs(0) - 1
    tn = h_ref.shape[0]
    buf = i % 2
    xl_ref, sem_rows = xl2_ref.at[buf], sem_rows2.at[buf]
    cpt = pltpu.make_async_copy(tab_ref, tsm_ref, sem_tab)
    cpt.start()

    def out_copy(src_ref, src0, dst0, rows, sem=sem_rows):
        return pltpu.make_async_copy(src_ref.at[pl.ds(src0, rows)], xs_ref.at[pl.ds(dst0, rows)], sem)

    @pl.when(i == 0)
    def _first_step():
        part_ref[...] = jnp.zeros(part_ref.shape, F32)
        zero_ref[...] = jnp.zeros(zero_ref.shape, F32)
        cpp = pltpu.make_async_copy(pad_ref, psm_ref, sem_pad)
        cpp.start()
        cpp.wait()

        def per_span(e, carry):
            lo8, hi8 = (psm_ref[0, e] + 7) >> 3, psm_ref[1, e] >> 3

            def zero_copy(u):
                return out_copy(zero_ref, 0, pl.multiple_of(u * 8, 8), 8)

            lax.fori_loop(lo8, hi8, lambda u, c: (zero_copy(u).start(), c)[1], 0)
            lax.fori_loop(lo8, hi8, lambda u, c: (zero_copy(u).wait(), c)[1], 0)
            return carry

        lax.fori_loop(0, N_EXPERTS + 1, per_span, 0)

    slot = lax.broadcasted_iota(I32, (L_ROWS, tn), 0)
    sl = slot_ref[...]
    hit = jnp.where(slot == sl[0:1], 1.0, jnp.where(slot == sl[1:2], 1.0, jnp.where(
        slot == sl[2:3], 1.0, jnp.where(slot == sl[3:4], 1.0, 0.0))))
    xl_ref[...] = jnp.dot(hit.astype(BF16), h_ref[...].astype(BF16), preferred_element_type=F32)
    cpt.wait()
    sub8 = lax.broadcasted_iota(I32, (8, 1), 0)

    def per_expert(e, sent8):
        cnt, lo, g0 = tsm_ref[0, e], pl.multiple_of(tsm_ref[1, e], 8), tsm_ref[2, e]
        r = g0 & 7
        g_tile = pl.multiple_of(g0 - r, 8)
        head = pl.ds(lo, 8)
        xl_ref[head, :] = xl_ref[head, :] + part_ref[e]
        full8, rem = (r + cnt) >> 3, (r + cnt) & 7
        _for_each_chunk(full8, lambda first, rows, prio: out_copy(
            xl_ref, pl.multiple_of(lo + first, 8), pl.multiple_of(g_tile + first, 8), rows).start(priority=prio))
        tail = xl_ref[pl.ds(pl.multiple_of(lo + full8 * 8, 8), 8), :]
        part_ref[e] = jnp.where(sub8 < rem, tail, 0.0)
        return sent8 + full8

    sent8 = lax.fori_loop(0, N_EXPERTS, per_expert, 0)

    def flush(e, sent8):
        end = psm_ref[0, e]
        due = jnp.logical_and(i == pl.num_programs(0) - 1, (end & 7) != 0)

        @pl.when(due)
        def _():
            pltpu.make_async_copy(part_ref.at[e], xs_ref.at[pl.ds(pl.multiple_of(end - (end & 7), 8), 8)],
                                  sem_rows).start()

        return sent8 + jnp.where(due, 1, 0)

    sent8 = lax.fori_loop(0, N_EXPERTS, flush, sent8)
    sent_ref[buf] = sent8

    @pl.when(i > 0)
    def _previous_step_copies():
        _wait_rows(sent_ref[1 - buf], lambda rows: out_copy(xl_ref, 0, 0, rows, sem_rows2.at[1 - buf]))

    @pl.when(i == last)
    def _own_copies():
        _wait_rows(sent8, lambda rows: out_copy(xl_ref, 0, 0, rows))


def _dispatch(h2, slot, tab, pad, n_rows):
    nt = h2.shape[0]
    tile = lambda i: (i, 0)
    return pl.pallas_call(
        _dispatch_kernel,
        out_shape=jax.ShapeDtypeStruct((n_rows, D_MODEL), F32),
        grid=(nt // TM,),
        in_specs=[pl.BlockSpec((TM, D_MODEL), tile),
                  pl.BlockSpec((DEST_ROWS, TM), tile),
                  pl.BlockSpec((DEST_ROWS, HEAD_W), tile),
                  pl.BlockSpec((DEST_ROWS, HEAD_W), lambda i: (0, 0))],
        out_specs=pl.BlockSpec(memory_space=pl.ANY),
        scratch_shapes=[pltpu.SMEM((DEST_ROWS, HEAD_W), I32), pltpu.SMEM((DEST_ROWS, HEAD_W), I32),
                        pltpu.SMEM((2,), I32),
                        pltpu.VMEM((2, L_ROWS, D_MODEL), F32), pltpu.VMEM((N_EXPERTS, 8, D_MODEL), F32),
                        pltpu.VMEM((8, D_MODEL), F32),
                        pltpu.SemaphoreType.DMA, pltpu.SemaphoreType.DMA, pltpu.SemaphoreType.DMA((2,))],
        compiler_params=_params(("arbitrary",)),
        name="moe_dispatch",
    )(h2, slot, tab, pad)


def _expert_kernel(be_ref, nv_ref, xs_ref, w1_ref, b1_ref, w2_ref, b2_ref, ys_ref, w1b_ref, w2b_ref):
    i = pl.program_id(0)
    valid = i < nv_ref[0]
    prev = be_ref[jnp.maximum(i - 1, 0)]
    changed = jnp.logical_or(i == 0, be_ref[i] != prev)

    @pl.when(jnp.logical_and(valid, changed))
    def _cast_weights():
        step = 128
        def cast(j, carry):
            rows = pl.ds(pl.multiple_of(j * step, step), step)
            w1b_ref[rows, :] = w1_ref[rows, :].astype(BF16)
            w2b_ref[rows, :] = w2_ref[rows, :].astype(BF16)
            return carry
        lax.fori_loop(0, w1_ref.shape[0] // step, cast, 0)

    @pl.when(jnp.logical_not(valid))
    def _unused_block():
        ys_ref[...] = jnp.zeros(ys_ref.shape, F32)

    @pl.when(valid)
    def _mlp():
        h = jnp.dot(xs_ref[...].astype(BF16), w1b_ref[...], preferred_element_type=F32) + b1_ref[...]
        half = h.shape[1] // 2
        a = jnp.minimum(h[:, 0:half], SWIGLU_LIMIT)
        lin = jnp.clip(h[:, half:2 * half], -SWIGLU_LIMIT, SWIGLU_LIMIT)
        act = a * _sigmoid(SWIGLU_ALPHA * a) * (lin + 1.0)
        ys_ref[...] = jnp.dot(act.astype(BF16), w2b_ref[...], preferred_element_type=F32) + b2_ref[...]


def _experts(xs, blk_exp, n_valid, w1, b1, w2, b2, layer):
    n_rows = xs.shape[0]
    n_blk = n_rows // MOE_BM
    depth, d_e2 = w1.shape[0], w1.shape[3]

    def blk(i, be, nv):
        return jnp.minimum(i, nv[0] - 1)

    def expert(i, be, nv):
        return (layer, be[blk(i, be, nv)], 0, 0)

    grid_spec = pltpu.PrefetchScalarGridSpec(
        num_scalar_prefetch=2,
        grid=(n_blk,),
        in_specs=[
            pl.BlockSpec((MOE_BM, D_MODEL), lambda i, be, nv: (blk(i, be, nv), 0)),
            pl.BlockSpec((None, None, D_MODEL, d_e2), expert),
            pl.BlockSpec((None, None, 1, d_e2), expert),
            pl.BlockSpec((None, None, d_e2 // 2, D_MODEL), expert),
            pl.BlockSpec((None, None, 1, D_MODEL), expert),
        ],
        out_specs=pl.BlockSpec((MOE_BM, D_MODEL), lambda i, be, nv: (i, 0)),
        scratch_shapes=[pltpu.VMEM((D_MODEL, d_e2), BF16), pltpu.VMEM((d_e2 // 2, D_MODEL), BF16)],
    )
    return pl.pallas_call(
        _expert_kernel,
        out_shape=jax.ShapeDtypeStruct((n_rows, D_MODEL), F32),
        grid_spec=grid_spec,
        compiler_params=_params(("arbitrary",)),
        name="moe_experts",
    )(blk_exp, n_valid, xs, w1, b1.reshape(depth, N_EXPERTS, 1, d_e2), w2, b2.reshape(depth, N_EXPERTS, 1, D_MODEL))


def _combine_kernel(x_ref, gt_ref, tab_ref, tab_next_ref, m_ref, fg_ref, ys_ref, o_ref, tsm_ref, got_ref, yl2_ref,
                    sem_tab, sem_rows2, *, final):
    i = pl.program_id(0)
    tn = x_ref.shape[0]
    buf = i % 2

    def in_copy(b, src0, dst0, rows):
        return pltpu.make_async_copy(ys_ref.at[pl.ds(src0, rows)], yl2_ref.at[b, pl.ds(dst0, rows)],
                                     sem_rows2.at[b])

    def fetch(table_ref, b):
        cpt = pltpu.make_async_copy(table_ref, tsm_ref, sem_tab)
        cpt.start()
        cpt.wait()

        def per_expert(e, got8):
            cnt, lo, g0 = tsm_ref[0, e], pl.multiple_of(tsm_ref[1, e], 8), tsm_ref[2, e]
            r = g0 & 7
            g_tile = pl.multiple_of(g0 - r, 8)
            cover8 = jnp.where(cnt > 0, (r + cnt + 7) >> 3, 0)
            _for_each_chunk(cover8, lambda first, rows, prio: in_copy(
                b, pl.multiple_of(g_tile + first, 8), pl.multiple_of(lo + first, 8), rows).start(priority=prio))
            return got8 + cover8

        got_ref[b] = lax.fori_loop(0, N_EXPERTS, per_expert, 0)

    @pl.when(i == 0)
    def _first_step():
        yl2_ref[...] = jnp.zeros(yl2_ref.shape, F32)
        fetch(tab_ref, 0)

    @pl.when(i + 1 < pl.num_programs(0))
    def _next_tile():
        fetch(tab_next_ref, 1 - buf)

    gt = gt_ref[...]
    slot = lax.broadcasted_iota(I32, (tn, L_ROWS), 1).astype(F32)
    pick = jnp.where(slot == gt[:, TOP_K:TOP_K + 1], gt[:, 0:1], 0.0)
    for k in range(1, TOP_K):
        pick = pick + jnp.where(slot == gt[:, TOP_K + k:TOP_K + k + 1], gt[:, k:k + 1], 0.0)
    _wait_rows(got_ref[buf], lambda rows: in_copy(buf, 0, 0, rows))
    ph = pick.astype(BF16)
    plo = (pick - ph.astype(F32)).astype(BF16)
    y = yl2_ref[buf]
    yh = y.astype(BF16)
    ylo = (y - yh.astype(F32)).astype(BF16)
    acc = (jnp.dot(ph, yh, preferred_element_type=F32) + jnp.dot(plo, yh, preferred_element_type=F32)
           + jnp.dot(ph, ylo, preferred_element_type=F32))
    xn = x_ref[...] + m_ref[5] * acc
    if final:
        xn = _rms(xn, fg_ref[...])
    o_ref[...] = xn


def _combine(x, gate_t, tab, m, final_g, ys, mod_spec, final):
    nt = x.shape[0]
    tile = lambda i: (i, 0)
    return pl.pallas_call(
        functools.partial(_combine_kernel, final=final),
        out_shape=jax.ShapeDtypeStruct((nt, D_MODEL), F32),
        grid=(nt // TM,),
        in_specs=[pl.BlockSpec((TM, D_MODEL), tile),
                  pl.BlockSpec((TM, HEAD_W), tile),
                  pl.BlockSpec((DEST_ROWS, HEAD_W), tile),
                  pl.BlockSpec((DEST_ROWS, HEAD_W), lambda i: (jnp.minimum(i + 1, nt // TM - 1), 0)),
                  mod_spec,
                  pl.BlockSpec((1, D_MODEL), lambda i: (0, 0)),
                  pl.BlockSpec(memory_space=pl.ANY)],
        out_specs=pl.BlockSpec((TM, D_MODEL), tile),
        scratch_shapes=[pltpu.SMEM((DEST_ROWS, HEAD_W), I32), pltpu.SMEM((2,), I32),
                        pltpu.VMEM((2, L_ROWS, D_MODEL), F32),
                        pltpu.SemaphoreType.DMA, pltpu.SemaphoreType.DMA((2,))],
        compiler_params=_params(("arbitrary",)),
        name="moe_combine_final" if final else "moe_combine",
    )(x, gate_t, tab, tab, m, final_g, ys)


def _moe(x, h2, logits_t, m, final_g, w1, b1, w2, b2, mod_spec, layer, final):
    nt = x.shape[0]
    n_blk = nt * TOP_K // MOE_BM + N_EXPERTS
    n_blk_pad = -(-n_blk // HEAD_W) * HEAD_W
    slot, gate_t, tab, blk_exp, n_valid, pad = _route(logits_t, n_blk_pad, n_blk * MOE_BM)
    xs = _dispatch(h2, slot, tab, pad, n_blk * MOE_BM)
    ys = _experts(xs, blk_exp, n_valid, w1, b1, w2, b2, layer)
    return _combine(x, gate_t, tab, m, final_g, ys, mod_spec, final)


def _rope_tables(seq):
    t = jnp.arange(seq)
    inv = 1.0 / (ROPE_THETA ** (jnp.arange(ROPE_HALF, dtype=F32) / ROPE_HALF))
    ang_r = (t // GRID_W).astype(F32)[:, None] * inv
    ang_c = (t % GRID_W).astype(F32)[:, None] * inv
    cr, sr, cc, sc = jnp.cos(ang_r), jnp.sin(ang_r), jnp.cos(ang_c), jnp.sin(ang_c)
    cos = jnp.concatenate([cr, cr, cc, cc] * 2, axis=1)
    sin = jnp.concatenate([-sr, sr, -sc, sc] * 2, axis=1)
    return cos, sin


def kernel(x_prompt, x_sample, c, cache_k, cache_v, state_hgrn, c_ctx, mod_w, mod_b, norm_g, final_norm_g,
           w_in_even, w_out_even, da_lambda, da_subln_g, hg_norm_g, hg_lb, w_in_odd, sgu_w, sgu_b, w_out_odd,
           router_w, router_b, ex_w1, ex_b1, ex_w2, ex_b2):
    bp, lp, _ = x_prompt.shape
    bs, ls, _ = x_sample.shape
    n_p, n_s = bp * lp, bs * ls
    assert lp % TM == 0 and ls % TM == 0 and n_p % ls == 0 and bs < MOD_ROWS
    depth = mod_w.shape[0]

    x = jnp.concatenate([x_prompt.reshape(n_p, D_MODEL), x_sample.reshape(n_s, D_MODEL)], axis=0)
    cond = jnp.zeros((MOD_ROWS, D_MODEL), F32).at[0:bs].set(c).at[bs].set(c_ctx)
    mod = _modulation(cond, mod_w, mod_b)
    mod_spec = _mod_spec(n_p, ls, bs)
    rope_tabs = _rope_tables(ls)
    hg_lb_h = hg_lb.reshape(2, depth + 1, HG_HEADS, 1, HG_DK)
    final_g = final_norm_g.reshape(1, D_MODEL)

    new_k, new_v, new_s = [], [], []
    for l in range(depth):
        m = mod[l]
        g1 = norm_g[l, 0].reshape(1, D_MODEL)
        g2 = norm_g[l, 1].reshape(1, D_MODEL)
        if l % 2 == 0:
            e = l // 2
            lam_init = 0.8 - 0.6 * math.exp(-0.3 * l)
            proj = _inproj(x, m, g1, w_in_even[e].astype(BF16), mod_spec)
            subln = da_subln_g[e].reshape(1, HEAD_W)
            hgn = hg_norm_g[e].reshape(1, HEAD_W)
            mix = jnp.zeros((n_p + n_s, D_MODEL), F32)
            mix, k_new, v_new = _diff_attention(proj, mix, da_lambda[e], subln, row0=0, n_batch=bp, seq=lp,
                                                lam_init=lam_init)
            mix = _diff_attention(proj, mix, da_lambda[e], subln, row0=n_p, n_batch=bs, seq=ls, lam_init=lam_init,
                                  rope_tabs=rope_tabs, ctx_k=cache_k, ctx_v=cache_v, layer_e=e)
            mix, s_p = _hgrn(proj, mix, hg_lb_h, hgn, row0=0, n_batch=bp, seq=lp, layer=l)
            mix, _ = _hgrn(proj, mix, hg_lb_h, hgn, row0=n_p, n_batch=bs, seq=ls, layer=l, s0=state_hgrn[:, e])
            w_out = w_out_even[e].astype(BF16)
            new_k.append(k_new)
            new_v.append(v_new)
            new_s.append(s_p)
        else:
            o = l // 2
            mix = _odd_mixer(x, m, g1, w_in_odd[o].astype(BF16), sgu_w[o].astype(BF16), sgu_b[o].T, mod_spec)
            w_out = w_out_odd[o].astype(BF16)
        x, h2, logits_t = _post(x, mix, m, g2, w_out, router_w[l].T, router_b[l].reshape(N_EXPERTS, 1), mod_spec)
        x = _moe(x, h2, logits_t, m, final_g, ex_w1, ex_b1, ex_w2, ex_b2, mod_spec, layer=l,
                 final=(l == depth - 1))

    y_prompt = x[0:n_p].reshape(bp, lp, D_MODEL)
    y_sample = x[n_p:].reshape(bs, ls, D_MODEL)
    return (y_prompt, y_sample, jnp.stack(new_k, axis=1), jnp.stack(new_v, axis=1), jnp.stack(new_s, axis=1))
```

```python
import functools
import math

import jax
import jax.numpy as jnp
from jax import lax
from jax.experimental import pallas as pl
from jax.experimental.pallas import tpu as pltpu

F32 = jnp.float32
BF16 = jnp.bfloat16
I32 = jnp.int32

D_MODEL = 1024
DEPTH = 2
GRID_W = 64
DA_HEADS = 4
DA_HD = 64
HG_HEADS = 4
HG_DK = 128
ROPE_THETA = 10000.0
ROPE_HALF = DA_HD // 4
CM_CHUNK = 128
CM_GROUPS = 8
N_EXPERTS = 32
TOP_K = 4
SWIGLU_ALPHA = 1.702
SWIGLU_LIMIT = 7.0
EPS = 1e-6
EVEN_IN = 4096
HEAD_W = 128

TM = 256
TD = 512
ATTN_SUB = 2
HG_C = 64
HG_SB = 16
HG_HPS = 2
HG_FIN = 256
L_ROWS = -(-(TOP_K * TM + N_EXPERTS * 14) // 128) * 128
GROUP_BITS = (32, 16, 8, 4, 2, 1)
WAIT_BITS = (128, 64, 32, 16, 8, 4, 2, 1)
MOE_BM = 512
DEST_ROWS = 8
MOD_ROWS = 16
V7X_VMEM_LIMIT = 56 * 1024 * 1024

NT_DIMS = (((1,), (1,)), ((), ()))
TN_DIMS = (((0,), (0,)), ((), ()))


def _params(sem):
    return pltpu.CompilerParams(dimension_semantics=sem, vmem_limit_bytes=V7X_VMEM_LIMIT)


def _sigmoid(x):
    return 1.0 / (1.0 + jnp.exp(-x))


def _silu(x):
    return x * _sigmoid(x)


def _rms(x, g):
    ms = jnp.mean(x * x, axis=-1, keepdims=True)
    return x * lax.rsqrt(ms + EPS) * g


def _norm_mod(x, g, shift, scale):
    return _rms(x, g) * (1.0 + scale) + shift


def _mod_kernel(c_ref, w_ref, b_ref, o_ref):
    s = _silu(c_ref[...])
    o_ref[...] = jnp.dot(s.astype(BF16), w_ref[...].astype(BF16), preferred_element_type=F32) + b_ref[...]


def _modulation(cond, mod_w, mod_b):
    depth = mod_w.shape[0]
    m = pl.pallas_call(
        _mod_kernel,
        out_shape=jax.ShapeDtypeStruct((depth, 6, MOD_ROWS, D_MODEL), F32),
        grid=(depth, 6),
        in_specs=[
            pl.BlockSpec((MOD_ROWS, D_MODEL), lambda l, j: (0, 0)),
            pl.BlockSpec((None, D_MODEL, D_MODEL), lambda l, j: (l, 0, j)),
            pl.BlockSpec((None, 1, D_MODEL), lambda l, j: (l, 0, j)),
        ],
        out_specs=pl.BlockSpec((None, None, MOD_ROWS, D_MODEL), lambda l, j: (l, j, 0, 0)),
        compiler_params=_params(("arbitrary", "arbitrary")),
        name="modulation",
    )(cond, mod_w, mod_b.reshape(depth, 1, 6 * D_MODEL))
    return m.transpose(0, 2, 1, 3)[:, :, :, None, :]


def _mod_spec(n_prompt, l_sample, ctx_row, tile):
    def index(i):
        t = i * tile
        return (jnp.where(t < n_prompt, ctx_row, (t - n_prompt) // l_sample), 0, 0, 0)
    return pl.BlockSpec((None, 6, 1, D_MODEL), index)


def _x_parts(x, n_prompt):
    npt = n_prompt // TD
    if isinstance(x, tuple):
        a, b = x
        spec_b = pl.BlockSpec((TD, D_MODEL), lambda i: (jnp.maximum(i - npt, 0), 0))
    else:
        a = b = x
        spec_b = pl.BlockSpec((TD, D_MODEL), lambda i: (jnp.maximum(i, npt), 0))
    spec_a = pl.BlockSpec((TD, D_MODEL), lambda i: (jnp.minimum(i, npt - 1), 0))
    n_rows = a.shape[0] + b.shape[0] if isinstance(x, tuple) else x.shape[0]
    return [spec_a, spec_b], [a, b], npt, n_rows


def _pick_x(xa_ref, xb_ref, npt):
    return jnp.where(pl.program_id(0) < npt, xa_ref[...], xb_ref[...])


def _inproj_kernel(xa_ref, xb_ref, m_ref, g_ref, w_ref, o_ref, *, npt):
    h = _norm_mod(_pick_x(xa_ref, xb_ref, npt), g_ref[...], m_ref[0], m_ref[1])
    o_ref[...] = jnp.dot(h.astype(BF16), w_ref[...], preferred_element_type=F32)


def _inproj(x, n_prompt, m, g, w, mod_spec):
    x_specs, x_args, npt, nt = _x_parts(x, n_prompt)
    n_out = w.shape[1]
    return pl.pallas_call(
        functools.partial(_inproj_kernel, npt=npt),
        out_shape=jax.ShapeDtypeStruct((nt, n_out), F32),
        grid=(nt // TD,),
        in_specs=x_specs + [
            mod_spec,
            pl.BlockSpec((1, D_MODEL), lambda i: (0, 0)),
            pl.BlockSpec((D_MODEL, n_out), lambda i: (0, 0)),
        ],
        out_specs=pl.BlockSpec((TD, n_out), lambda i: (i, 0)),
        compiler_params=_params(("arbitrary",)),
        name="even_inproj",
    )(*x_args, m, g, w)


def _rope(x, cos, sin_signed):
    lane = lax.broadcasted_iota(I32, x.shape, 1)
    first = ((lane // ROPE_HALF) % 2) == 0
    partner = jnp.where(first, pltpu.roll(x, HEAD_W - ROPE_HALF, 1), pltpu.roll(x, ROPE_HALF, 1))
    return x * cos + partner * sin_signed


def _attn_kernel(*refs, rope, ctx, bq, chunks, lam_init):
    it = iter(refs)
    lam_ref, g_ref, q_ref, k_ref, v_ref = (next(it) for _ in range(5))
    if rope:
        cq_ref, sq_ref, ck_ref, sk_ref = (next(it) for _ in range(4))
    if ctx:
        kc_ref, vc_ref = next(it), next(it)
    next(it)
    o_ref = next(it)
    if not ctx:
        ko_ref, vo_ref = next(it), next(it)
    kt_ref, vx_ref = next(it), next(it)
    l_new = k_ref.shape[0]
    l_ctx = kc_ref.shape[0] if ctx else 0

    @pl.when(pl.program_id(2) == 0)
    def _prepare_keys():
        if ctx:
            kt_ref[:, 0:l_ctx] = kc_ref[...].T.astype(BF16)
            vx_ref[0:l_ctx, 0:HEAD_W] = vc_ref[...].astype(BF16)
        else:
            ko_ref[...] = k_ref[...]
            vo_ref[...] = v_ref[...]
        step = min(512, l_new)
        for c0 in range(0, l_new, step):
            k = k_ref[c0:c0 + step, :]
            if rope:
                k = _rope(k, ck_ref[c0:c0 + step, :], sk_ref[c0:c0 + step, :])
            kt_ref[:, l_ctx + c0:l_ctx + c0 + step] = k.T.astype(BF16)
            vx_ref[l_ctx + c0:l_ctx + c0 + step, 0:HEAD_W] = v_ref[c0:c0 + step, :].astype(BF16)
        vx_ref[:, HEAD_W:2 * HEAD_W] = jnp.ones((l_ctx + l_new, HEAD_W), BF16)

    n_sub = q_ref.shape[0] // bq
    lane = lax.broadcasted_iota(I32, (bq, HEAD_W), 1)
    lo = lane < DA_HD
    zero = jnp.zeros((bq, HEAD_W), F32)
    qss = []
    for u in range(n_sub):
        rows = slice(u * bq, (u + 1) * bq)
        q = q_ref[rows, :]
        if rope:
            q = _rope(q, cq_ref[rows, :], sq_ref[rows, :])
        q = q * (DA_HD ** -0.5)
        qss.append(jnp.concatenate([jnp.where(lo, q, zero), jnp.where(lo, zero, q)], axis=0).astype(BF16))

    def scores(qs, chunk):
        return jnp.dot(qs, kt_ref[:, chunk[0]:chunk[0] + chunk[1]], preferred_element_type=F32)

    m = [jnp.full((2 * bq, 1), -jnp.inf, F32)] * n_sub
    l = [jnp.zeros((2 * bq, 1), F32)] * n_sub
    acc = [jnp.zeros((2 * bq, HEAD_W), F32)] * n_sub
    s_next = [scores(qs, chunks[0]) for qs in qss]
    for ci, (c0, cs) in enumerate(chunks):
        s_cur = s_next
        if ci + 1 < len(chunks):
            s_next = [scores(qs, chunks[ci + 1]) for qs in qss]
        for u in range(n_sub):
            s = s_cur[u]
            mn = jnp.maximum(m[u], jnp.max(s, axis=-1, keepdims=True))
            alpha = jnp.exp(m[u] - mn)
            p = jnp.exp(s - mn).astype(BF16)
            pv = jnp.dot(p, vx_ref[c0:c0 + cs, :], preferred_element_type=F32)
            acc[u] = alpha * acc[u] + pv[:, 0:HEAD_W]
            l[u] = alpha * l[u] + pv[:, HEAD_W:HEAD_W + 1]
            m[u] = mn
    lp = lam_ref[...]
    lam = (jnp.exp(jnp.sum(lp[0:1] * lp[1:2], axis=-1, keepdims=True))
           - jnp.exp(jnp.sum(lp[2:3] * lp[3:4], axis=-1, keepdims=True)) + lam_init)
    for u in range(n_sub):
        o2 = acc[u] / l[u]
        o = o2[0:bq] - lam * o2[bq:2 * bq]
        o_ref[u * bq:(u + 1) * bq, :] = _rms(o, g_ref[...]) * (1.0 - lam_init)


def _attn_chunks(l_ctx, l_new):
    total = l_ctx + l_new
    if total <= 1280:
        return ((0, total),)
    chunks, c0 = [], 0
    first = l_ctx + 1024
    chunks.append((0, first))
    c0 = first
    while c0 < total:
        cs = min(1024, total - c0)
        chunks.append((c0, cs))
        c0 += cs
    return tuple(chunks)


def _diff_attention(proj, mix, lam_p, subln_g, *, row0, n_batch, seq, lam_init, rope_tabs=None,
                    ctx_k=None, ctx_v=None, layer_e=0):
    rope, ctx = rope_tabs is not None, ctx_k is not None
    n_rows = proj.shape[0]
    bq = min(ATTN_SUB * 256, seq)
    nq = seq // bq
    l_ctx = ctx_k.shape[3] if ctx else 0
    qb0, kb0 = row0 // bq, row0 // seq
    in_specs = [
        pl.BlockSpec((4, DA_HD), lambda b, h, i: (0, 0)),
        pl.BlockSpec((1, HEAD_W), lambda b, h, i: (0, 0)),
        pl.BlockSpec((bq, HEAD_W), lambda b, h, i: (qb0 + b * nq + i, h)),
        pl.BlockSpec((seq, HEAD_W), lambda b, h, i: (kb0 + b, DA_HEADS + h)),
        pl.BlockSpec((seq, HEAD_W), lambda b, h, i: (kb0 + b, 2 * DA_HEADS + h)),
    ]
    args = [lam_p, subln_g, proj, proj, proj]
    if rope:
        cos, sin = rope_tabs
        in_specs += [
            pl.BlockSpec((bq, HEAD_W), lambda b, h, i: (i, 0)),
            pl.BlockSpec((bq, HEAD_W), lambda b, h, i: (i, 0)),
            pl.BlockSpec((seq, HEAD_W), lambda b, h, i: (0, 0)),
            pl.BlockSpec((seq, HEAD_W), lambda b, h, i: (0, 0)),
        ]
        args += [cos, sin, cos, sin]
    if ctx:
        in_specs += [
            pl.BlockSpec((None, None, None, l_ctx, HEAD_W), lambda b, h, i: (b, layer_e, h, 0, 0)),
            pl.BlockSpec((None, None, None, l_ctx, HEAD_W), lambda b, h, i: (b, layer_e, h, 0, 0)),
        ]
        args += [ctx_k, ctx_v]
    mix_sds = jax.ShapeDtypeStruct((n_rows, D_MODEL), F32)
    mix_spec = pl.BlockSpec((bq, HEAD_W), lambda b, h, i: (qb0 + b * nq + i, h))
    in_specs.append(pl.BlockSpec(memory_space=pl.ANY))
    args.append(mix)
    aliases = {len(args) - 1: 0}
    if ctx:
        out_shape, out_specs = mix_sds, mix_spec
    else:
        cache_sds = jax.ShapeDtypeStruct((n_batch, DA_HEADS, seq, HEAD_W), F32)
        cache_spec = pl.BlockSpec((None, None, seq, HEAD_W), lambda b, h, i: (b, h, 0, 0))
        out_shape, out_specs = (mix_sds, cache_sds, cache_sds), (mix_spec, cache_spec, cache_spec)
    kernel = functools.partial(_attn_kernel, rope=rope, ctx=ctx, bq=bq // ATTN_SUB,
                               chunks=_attn_chunks(l_ctx, seq), lam_init=lam_init)
    return pl.pallas_call(
        kernel,
        out_shape=out_shape,
        grid=(n_batch, DA_HEADS, nq),
        in_specs=in_specs,
        out_specs=out_specs,
        scratch_shapes=[pltpu.VMEM((HEAD_W, l_ctx + seq), BF16), pltpu.VMEM((l_ctx + seq, 2 * HEAD_W), BF16)],
        input_output_aliases=aliases,
        compiler_params=_params(("arbitrary", "arbitrary", "arbitrary")),
        name="diff_attention_ctx" if ctx else "diff_attention",
    )(*args)


def _hgrn_chunks(streams):
    c, nsb = HG_C, HG_C // HG_SB
    row = lax.broadcasted_iota(I32, (c, c), 0)
    col = lax.broadcasted_iota(I32, (c, c), 1)
    tri = {r: jnp.where((col >= row) if r else (col <= row), 1.0, 0.0).astype(BF16) for r in (False, True)}
    ones = jnp.ones((HG_DK, HG_DK), BF16)
    t_idx = lax.broadcasted_iota(I32, (HG_SB, 1), 0)

    g3s = []
    for q, k, v, lf, st, rev, f in streams:
        hi = lf.astype(BF16)
        r1 = lf - hi.astype(F32)
        mid = r1.astype(BF16)
        lo = (r1 - mid.astype(F32)).astype(BF16)
        g3s.append(jnp.dot(tri[rev], jnp.concatenate([hi, mid, lo], axis=1), preferred_element_type=F32))
    gs = [g3[:, 0:HG_DK] + g3[:, HG_DK:2 * HG_DK] + g3[:, 2 * HG_DK:3 * HG_DK] for g3 in g3s]
    g_lasts = [g[0:1] if s[5] else g[c - 1:c] for g, s in zip(gs, streams)]

    o_inters = [lax.dot_general((s[0] * jnp.exp(g)).astype(BF16), s[4].astype(BF16), NT_DIMS,
                                preferred_element_type=F32) for g, s in zip(gs, streams)]

    def outside(i, rev):
        r0 = i * HG_SB
        if rev:
            return (r0 + HG_SB, c, r0 + HG_SB) if r0 + HG_SB < c else None
        return (0, r0, r0 - 1) if r0 > 0 else None

    a_offs = {}
    for i in range(nsb):
        for n, (g, s) in enumerate(zip(gs, streams)):
            span = outside(i, s[5])
            if span is None:
                continue
            s_lo, s_hi, rb = span
            r0 = i * HG_SB
            ref_g = g[rb:rb + 1]
            qd = (s[0][r0:r0 + HG_SB] * jnp.exp(g[r0:r0 + HG_SB] - ref_g)).astype(BF16)
            kd = (s[1][s_lo:s_hi] * jnp.exp(ref_g - g[s_lo:s_hi])).astype(BF16)
            a_offs[i, n] = lax.dot_general(qd, kd, NT_DIMS, preferred_element_type=F32)
    o_offs = {}
    for (i, n), a in a_offs.items():
        s_lo, s_hi, _ = outside(i, streams[n][5])
        o_offs[i, n] = jnp.dot(a.astype(BF16), streams[n][2][s_lo:s_hi].astype(BF16), preferred_element_type=F32)

    def rows_of(j, rev):
        half = HG_SB // 2
        if rev:
            return (0, half) if j < half else (0, HG_SB)
        return (half, HG_SB) if j >= half else (0, HG_SB)

    a_reps = {}
    for i in range(nsb):
        r0 = i * HG_SB
        for n, (g, s) in enumerate(zip(gs, streams)):
            qi, ki, fi = s[0][r0:r0 + HG_SB], s[1][r0:r0 + HG_SB], s[6][r0:r0 + HG_SB]
            ws, qe = {}, None
            order = range(HG_SB) if s[5] else range(HG_SB - 1, -1, -1)
            for j in order:
                t0, t1 = rows_of(j, s[5])
                here = t_idx[t0:t1] == j
                if qe is None:
                    qe = jnp.where(here, qi[t0:t1], 0.0)
                else:
                    if qe.shape[0] != t1 - t0:
                        pad8 = jnp.zeros_like(qe)
                        qe = jnp.concatenate([qe, pad8] if s[5] else [pad8, qe], axis=0)
                    step = fi[j - 1:j] if s[5] else fi[j + 1:j + 2]
                    qe = jnp.where(here, qi[t0:t1], qe * step)
                ws[j] = qe * ki[j:j + 1]
            a_reps[i, n] = jnp.dot(jnp.concatenate([ws[j] for j in range(HG_SB)], axis=0).astype(BF16), ones,
                                   preferred_element_type=F32)

    results = []
    for n, (g, s) in enumerate(zip(gs, streams)):
        q, k, v, lf, st, rev, f = s
        outs = []
        for i in range(nsb):
            r0 = i * HG_SB
            vi = v[r0:r0 + HG_SB]
            zero8 = jnp.zeros((HG_SB // 2, v.shape[1]), F32)
            lo_acc, hi_acc, at = zero8, zero8, 0
            for j in range(HG_SB):
                t0, t1 = rows_of(j, rev)
                term = a_reps[i, n][at:at + (t1 - t0)] * vi[j:j + 1]
                at += t1 - t0
                if t1 - t0 == HG_SB:
                    lo_acc, hi_acc = lo_acc + term[0:HG_SB // 2], hi_acc + term[HG_SB // 2:HG_SB]
                elif t0 == 0:
                    lo_acc = lo_acc + term
                else:
                    hi_acc = hi_acc + term
            oi = jnp.concatenate([lo_acc, hi_acc], axis=0)
            if (i, n) in o_offs:
                oi = oi + o_offs[i, n]
            outs.append(oi)
        o = o_inters[n] + jnp.concatenate(outs, axis=0)
        kd = (k * jnp.exp(g_lasts[n] - g)).astype(BF16)
        st_new = st * jnp.exp(g_lasts[n]) + lax.dot_general(v.astype(BF16), kd, TN_DIMS, preferred_element_type=F32)
        results.append((o, st_new))
    return results


def _hgrn_kernel(*refs, layer, has_s0):
    it = iter(refs)
    hq_ref, zf_ref, zb_ref, hi_ref, hg_ref, lbp_ref, g_ref = (next(it) for _ in range(7))
    s0_ref = next(it) if has_s0 else None
    next(it)
    o_ref, sout_ref, st_ref, ob_ref = next(it), next(it), next(it), next(it)
    seq = hq_ref.shape[0]
    n = seq // HG_C
    z_refs = (zf_ref, zb_ref)

    lbs = []
    for hh in range(HG_HPS):
        for d in range(2):
            p = lbp_ref[d, :, hh]
            e = jnp.exp(p - jnp.max(p, axis=0))
            lbs.append(jnp.sum(e[0:layer + 1], axis=0) / jnp.sum(e, axis=0))
            if has_s0:
                st_ref[2 * hh + d] = s0_ref[d, hh].T
            else:
                st_ref[2 * hh + d] = jnp.zeros((HEAD_W, HG_DK), F32)

    def body(ci, carry):
        streams, dests = [], []
        for hh in range(HG_HPS):
            lanes = slice(hh * HEAD_W, (hh + 1) * HEAD_W)
            for d in range(2):
                c = (n - 1 - ci) if d == 1 else ci
                rows = pl.ds(pl.multiple_of(c * HG_C, HG_C), HG_C)
                lb = lbs[2 * hh + d]
                f = lb + (1.0 - lb) * _sigmoid(z_refs[d][rows, lanes])
                streams.append((_silu(hq_ref[rows, lanes]), 1.0 - f, hi_ref[rows, lanes], jnp.log(f),
                                st_ref[2 * hh + d], d == 1, f))
                dests.append((o_ref if d == 0 else ob_ref, rows, lanes, 2 * hh + d))
        for (o, st_new), (dst_ref, rows, lanes, slot) in zip(_hgrn_chunks(streams), dests):
            st_ref[slot] = st_new
            dst_ref[rows, lanes] = o
        return carry

    lax.fori_loop(0, n, body, 0)
    for hh in range(HG_HPS):
        for d in range(2):
            sout_ref[d, hh] = st_ref[2 * hh + d].T

    def finish(ci, carry):
        rows = pl.ds(pl.multiple_of(ci * HG_FIN, HG_FIN), HG_FIN)
        for hh in range(HG_HPS):
            lanes = slice(hh * HEAD_W, (hh + 1) * HEAD_W)
            tot = o_ref[rows, lanes] + ob_ref[rows, lanes]
            o_ref[rows, lanes] = _rms(tot, g_ref[...]) * _silu(hg_ref[rows, lanes])
        return carry

    lax.fori_loop(0, seq // HG_FIN, finish, 0)


def _hgrn(proj, mix, hg_lb_l, hg_norm_g, *, row0, n_batch, seq, layer, s0=None):
    has_s0 = s0 is not None
    rb0 = row0 // seq
    hpg = HG_HEADS // HG_HPS
    width = HG_HPS * HEAD_W
    col0 = 3 * DA_HEADS * HEAD_W // width
    once = pl.Buffered(1)

    def col_spec(j):
        return pl.BlockSpec((seq, width), lambda b, h: (rb0 + b, col0 + j * hpg + h), pipeline_mode=once)

    in_specs = [col_spec(0), col_spec(1), col_spec(2), col_spec(3), col_spec(4),
                pl.BlockSpec((2, DEPTH + 1, HG_HPS, 1, HG_DK), lambda b, h: (0, 0, h, 0, 0)),
                pl.BlockSpec((1, HEAD_W), lambda b, h: (0, 0))]
    args = [proj, proj, proj, proj, proj, hg_lb_l, hg_norm_g]
    state_spec = pl.BlockSpec((None, 2, HG_HPS, HG_DK, HEAD_W), lambda b, h: (b, 0, h, 0, 0))
    if has_s0:
        in_specs.append(state_spec)
        args.append(s0)
    in_specs.append(pl.BlockSpec(memory_space=pl.ANY))
    args.append(mix)
    return pl.pallas_call(
        functools.partial(_hgrn_kernel, layer=layer, has_s0=has_s0),
        out_shape=(jax.ShapeDtypeStruct(mix.shape, mix.dtype),
                   jax.ShapeDtypeStruct((n_batch, 2, HG_HEADS, HG_DK, HEAD_W), F32)),
        grid=(n_batch, hpg),
        in_specs=in_specs,
        out_specs=(pl.BlockSpec((seq, width), lambda b, h: (rb0 + b, DA_HEADS * HEAD_W // width + h)),
                   state_spec),
        scratch_shapes=[pltpu.VMEM((2 * HG_HPS, HEAD_W, HG_DK), F32), pltpu.VMEM((seq, width), F32)],
        input_output_aliases={len(args) - 1: 0},
        compiler_params=_params(("arbitrary", "arbitrary")),
        name="hgrn2_state" if has_s0 else "hgrn2",
    )(*args)


def _odd_kernel(x_ref, m_ref, g_ref, win_ref, ws_ref, sb_ref, o_ref):
    h = _norm_mod(x_ref[...], g_ref[...], m_ref[0], m_ref[1])
    z = jnp.dot(h.astype(BF16), win_ref[...], preferred_element_type=F32)
    z = 0.5 * z * (1.0 + lax.erf(z * (2.0 ** -0.5)))
    width = z.shape[1] // 2
    u, v = z[:, 0:width], z[:, width:2 * width]
    mu = jnp.mean(v, axis=-1, keepdims=True)
    vc = v - mu
    var = jnp.mean(vc * vc, axis=-1, keepdims=True)
    vn = (vc * lax.rsqrt(var + EPS)).astype(BF16)
    gd = width // CM_GROUPS
    for r in range(x_ref.shape[0] // CM_CHUNK):
        rows = slice(r * CM_CHUNK, (r + 1) * CM_CHUNK)
        for gi in range(CM_GROUPS):
            cols = slice(gi * gd, (gi + 1) * gd)
            sv = jnp.dot(ws_ref[gi], vn[rows, cols], preferred_element_type=F32) + sb_ref[:, gi:gi + 1]
            o_ref[rows, cols] = u[rows, cols] * sv


def _odd_mixer(x, m, g, w_in, sgu_w, sgu_bt, mod_spec):
    nt = x.shape[0]
    return pl.pallas_call(
        _odd_kernel,
        out_shape=jax.ShapeDtypeStruct((nt, D_MODEL), F32),
        grid=(nt // TD,),
        in_specs=[
            pl.BlockSpec((TD, D_MODEL), lambda i: (i, 0)),
            mod_spec,
            pl.BlockSpec((1, D_MODEL), lambda i: (0, 0)),
            pl.BlockSpec(w_in.shape, lambda i: (0, 0)),
            pl.BlockSpec(sgu_w.shape, lambda i: (0, 0, 0)),
            pl.BlockSpec(sgu_bt.shape, lambda i: (0, 0)),
        ],
        out_specs=pl.BlockSpec((TD, D_MODEL), lambda i: (i, 0)),
        compiler_params=_params(("arbitrary",)),
        name="odd_mixer",
    )(x, m, g, w_in, sgu_w, sgu_bt)


def _post_kernel(xa_ref, xb_ref, mix_ref, m_ref, g_ref, w_ref, rwt_ref, rb_ref, xo_ref, h2_ref, lg_ref, *, npt):
    mo = jnp.dot(mix_ref[...].astype(BF16), w_ref[...], preferred_element_type=F32)
    xn = _pick_x(xa_ref, xb_ref, npt) + m_ref[2] * mo
    xo_ref[...] = xn
    h2 = _norm_mod(xn, g_ref[...], m_ref[3], m_ref[4])
    h2_ref[...] = h2
    hh = h2.astype(BF16)
    hl = (h2 - hh.astype(F32)).astype(BF16)
    rw = rwt_ref[...]
    rh = rw.astype(BF16)
    rl = (rw - rh.astype(F32)).astype(BF16)
    lg = (lax.dot_general(rh, hh, NT_DIMS, preferred_element_type=F32)
          + lax.dot_general(rh, hl, NT_DIMS, preferred_element_type=F32)
          + lax.dot_general(rl, hh, NT_DIMS, preferred_element_type=F32))
    lg_ref[...] = lg + rb_ref[...]


def _post(x, n_prompt, mix, m, g2, w_out, router_wt, router_b, mod_spec):
    x_specs, x_args, npt, nt = _x_parts(x, n_prompt)
    row = lambda i: (i, 0)
    return pl.pallas_call(
        functools.partial(_post_kernel, npt=npt),
        out_shape=(jax.ShapeDtypeStruct((nt, D_MODEL), F32),
                   jax.ShapeDtypeStruct((nt, D_MODEL), F32),
                   jax.ShapeDtypeStruct((N_EXPERTS, nt), F32)),
        grid=(nt // TD,),
        in_specs=x_specs + [
            pl.BlockSpec((TD, D_MODEL), row),
            mod_spec,
            pl.BlockSpec((1, D_MODEL), lambda i: (0, 0)),
            pl.BlockSpec((D_MODEL, D_MODEL), lambda i: (0, 0)),
            pl.BlockSpec((N_EXPERTS, D_MODEL), lambda i: (0, 0)),
            pl.BlockSpec((N_EXPERTS, 1), lambda i: (0, 0)),
        ],
        out_specs=(pl.BlockSpec((TD, D_MODEL), row), pl.BlockSpec((TD, D_MODEL), row),
                   pl.BlockSpec((N_EXPERTS, TD), lambda i: (0, i))),
        compiler_params=_params(("arbitrary",)),
        name="mixer_out_router",
    )(*x_args, mix, m, g2, w_out, router_wt, router_b)


def _experts_on_lanes(col):
    on_lane = (lax.broadcasted_iota(I32, (N_EXPERTS, HEAD_W), 0) == lax.broadcasted_iota(I32, (N_EXPERTS, HEAD_W), 1))
    return jnp.sum(jnp.where(on_lane, col, 0.0), axis=0, keepdims=True)


def _route_a_kernel(lg_ref, idx_ref, gate_ref, lrank_ref, ttab_ref, cnt_ref, carry_ref):
    i = pl.program_id(0)

    @pl.when(i == 0)
    def _():
        carry_ref[...] = jnp.zeros(carry_ref.shape, F32)

    l = lg_ref[...]
    tn = l.shape[1]
    eio = lax.broadcasted_iota(I32, l.shape, 0)
    vals, idxs = [], []
    for _ in range(TOP_K):
        mk = jnp.max(l, axis=0, keepdims=True)
        ik = jnp.min(jnp.where(l == mk, eio, N_EXPERTS), axis=0, keepdims=True)
        vals.append(mk)
        idxs.append(ik)
        l = jnp.where(eio == ik, -jnp.inf, l)
    es = [jnp.exp(v - vals[0]) for v in vals]
    den = es[0] + es[1] + es[2] + es[3]
    sel = jnp.zeros(l.shape, F32)
    for ik in idxs:
        sel = sel + jnp.where(eio == ik, 1.0, 0.0)
    upper = jnp.where(lax.broadcasted_iota(I32, (tn, tn), 0) < lax.broadcasted_iota(I32, (tn, tn), 1),
                      1.0, 0.0).astype(BF16)
    before = jnp.dot(sel.astype(BF16), upper, preferred_element_type=F32)
    for k in range(TOP_K):
        idx_ref[k:k + 1, :] = idxs[k]
        gate_ref[k:k + 1, :] = es[k] / den
        rk = jnp.sum(jnp.where(eio == idxs[k], before, 0.0), axis=0, keepdims=True)
        lrank_ref[k:k + 1, :] = rk.astype(I32)
    tile_cnt = jnp.sum(sel, axis=1, keepdims=True)
    ttab_ref[0:1, :] = _experts_on_lanes(tile_cnt).astype(I32)
    ttab_ref[1:2, :] = _experts_on_lanes(carry_ref[...]).astype(I32)
    ttab_ref[2:DEST_ROWS, :] = jnp.zeros((DEST_ROWS - 2, HEAD_W), I32)
    carry_ref[...] = carry_ref[...] + tile_cnt
    cnt_ref[...] = carry_ref[...]


def _route_b_kernel(cnt_ref, idx_ref, lrank_ref, gate_ref, ttab_ref, slot_ref, gt_ref, tab_ref, be_ref, nv_ref,
                    pad_ref, *, n_blk_pad, n_rows):
    cnt = cnt_ref[...]
    padded = jnp.floor((cnt + (MOE_BM - 1)) / MOE_BM) * MOE_BM
    r = lax.broadcasted_iota(I32, (N_EXPERTS, N_EXPERTS), 0)
    c = lax.broadcasted_iota(I32, (N_EXPERTS, N_EXPERTS), 1)
    padded_row = jnp.sum(jnp.where(r == c, padded, 0.0), axis=0, keepdims=True)
    pend = jnp.sum(jnp.where(c <= r, padded_row, 0.0), axis=1, keepdims=True)
    pstart = pend - padded
    tn = idx_ref.shape[1]
    eio = lax.broadcasted_iota(I32, (N_EXPERTS, tn), 0)

    sub = lax.broadcasted_iota(I32, (N_EXPERTS, HEAD_W), 0)
    lane_e = lax.broadcasted_iota(I32, (N_EXPERTS, HEAD_W), 1)
    tt = ttab_ref[...].astype(F32)
    c_row, base_row = tt[0:1], tt[1:2]
    g0_row = _experts_on_lanes(pstart) + base_row
    r_row = g0_row - 8.0 * jnp.floor(g0_row / 8.0)
    seg_row = 8.0 * jnp.floor((r_row + c_row + 7.0) / 8.0)
    seg_col = jnp.sum(jnp.where(sub == lane_e, seg_row, 0.0), axis=1, keepdims=True)
    r_col = jnp.sum(jnp.where(sub == lane_e, r_row, 0.0), axis=1, keepdims=True)
    loff_row = jnp.sum(jnp.where(sub < lane_e, seg_col, 0.0), axis=0, keepdims=True)
    loff_col = jnp.sum(jnp.where(lane_e < sub, seg_row, 0.0), axis=1, keepdims=True)
    first_slot = loff_col + r_col
    slots = []
    for k in range(TOP_K):
        off = jnp.sum(jnp.where(eio == idx_ref[k:k + 1, :], first_slot, 0.0), axis=0, keepdims=True)
        slots.append(off + lrank_ref[k:k + 1, :].astype(F32))
        slot_ref[k:k + 1, :] = slots[k].astype(I32)
    slot_ref[TOP_K:DEST_ROWS, :] = jnp.zeros((DEST_ROWS - TOP_K, tn), I32)
    gp = jnp.concatenate([gate_ref[...]] + slots + [jnp.zeros((HEAD_W - 2 * TOP_K, tn), F32)], axis=0)
    gt_ref[...] = gp.T
    tab_ref[0:1, :] = c_row.astype(I32)
    tab_ref[1:2, :] = loff_row.astype(I32)
    tab_ref[2:3, :] = g0_row.astype(I32)
    tab_ref[3:DEST_ROWS, :] = jnp.zeros((DEST_ROWS - 3, HEAD_W), I32)
    blk_start = (lax.broadcasted_iota(I32, (N_EXPERTS, n_blk_pad), 1) * MOE_BM).astype(F32)
    be = jnp.sum(jnp.where(pend <= blk_start, 1.0, 0.0), axis=0, keepdims=True)
    be_ref[...] = jnp.minimum(be, N_EXPERTS - 1.0).astype(I32)
    total = jnp.sum(padded, axis=0, keepdims=True)
    nv_ref[...] = jnp.broadcast_to(total / MOE_BM, nv_ref.shape).astype(I32)
    on_lane = (lax.broadcasted_iota(I32, (N_EXPERTS, HEAD_W), 0) == lax.broadcasted_iota(I32, (N_EXPERTS, HEAD_W), 1))
    lane = lax.broadcasted_iota(I32, (1, HEAD_W), 1)
    pad_lo = jnp.sum(jnp.where(on_lane, pstart + cnt, 0.0), axis=0, keepdims=True)
    pad_hi = jnp.sum(jnp.where(on_lane, pend, 0.0), axis=0, keepdims=True)
    pad_ref[0:1, :] = jnp.where(lane == N_EXPERTS, total, pad_lo).astype(I32)
    pad_ref[1:2, :] = jnp.where(lane == N_EXPERTS, float(n_rows), pad_hi).astype(I32)
    pad_ref[2:DEST_ROWS, :] = jnp.zeros((DEST_ROWS - 2, HEAD_W), I32)


def _route(logits_t, n_blk_pad, n_rows):
    nt = logits_t.shape[1]
    tn = TM
    blk = lambda i: (0, i)
    tile = lambda i: (i, 0)
    whole = lambda i: (0, 0)
    n_tiles = nt // tn
    idx, gate, lrank, ttab, cnt = pl.pallas_call(
        _route_a_kernel,
        out_shape=(jax.ShapeDtypeStruct((TOP_K, nt), I32), jax.ShapeDtypeStruct((TOP_K, nt), F32),
                   jax.ShapeDtypeStruct((TOP_K, nt), I32), jax.ShapeDtypeStruct((n_tiles * DEST_ROWS, HEAD_W), I32),
                   jax.ShapeDtypeStruct((N_EXPERTS, 1), F32)),
        grid=(n_tiles,),
        in_specs=[pl.BlockSpec((N_EXPERTS, tn), blk)],
        out_specs=(pl.BlockSpec((TOP_K, tn), blk), pl.BlockSpec((TOP_K, tn), blk), pl.BlockSpec((TOP_K, tn), blk),
                   pl.BlockSpec((DEST_ROWS, HEAD_W), tile), pl.BlockSpec((N_EXPERTS, 1), whole)),
        scratch_shapes=[pltpu.VMEM((N_EXPERTS, 1), F32)],
        compiler_params=_params(("arbitrary",)),
        name="route_topk_rank",
    )(logits_t)
    slot, gate_t, tab, blk_exp, n_valid, pad = pl.pallas_call(
        functools.partial(_route_b_kernel, n_blk_pad=n_blk_pad, n_rows=n_rows),
        out_shape=(jax.ShapeDtypeStruct((n_tiles * DEST_ROWS, tn), I32), jax.ShapeDtypeStruct((nt, HEAD_W), F32),
                   jax.ShapeDtypeStruct((n_tiles * DEST_ROWS, HEAD_W), I32),
                   jax.ShapeDtypeStruct((1, n_blk_pad), I32), jax.ShapeDtypeStruct((1, HEAD_W), I32),
                   jax.ShapeDtypeStruct((DEST_ROWS, HEAD_W), I32)),
        grid=(n_tiles,),
        in_specs=[pl.BlockSpec((N_EXPERTS, 1), whole), pl.BlockSpec((TOP_K, tn), blk),
                  pl.BlockSpec((TOP_K, tn), blk), pl.BlockSpec((TOP_K, tn), blk),
                  pl.BlockSpec((DEST_ROWS, HEAD_W), tile)],
        out_specs=(pl.BlockSpec((DEST_ROWS, tn), tile), pl.BlockSpec((tn, HEAD_W), tile),
                   pl.BlockSpec((DEST_ROWS, HEAD_W), tile),
                   pl.BlockSpec((1, n_blk_pad), whole), pl.BlockSpec((1, HEAD_W), whole),
                   pl.BlockSpec((DEST_ROWS, HEAD_W), whole)),
        compiler_params=_params(("arbitrary",)),
        name="route_dest",
    )(cnt, idx, lrank, gate, ttab)
    return slot, gate_t, tab, blk_exp.reshape(n_blk_pad), n_valid[0, 0:1], pad


def _for_each_chunk(n8, fn):
    for bi, b in enumerate(GROUP_BITS):
        @pl.when((n8 & b) != 0)
        def _(bi=bi, b=b):
            first = pl.multiple_of((n8 - (n8 & (2 * b - 1))) * 8, 8)
            fn(first, 8 * b, bi % 2)


def _wait_rows(n8, make_copy):
    for b in WAIT_BITS:
        @pl.when((n8 & b) != 0)
        def _(b=b):
            make_copy(8 * b).wait()


def _dispatch_kernel(h_ref, slot_ref, tab_ref, pad_ref, xs_ref, tsm_ref, psm_ref, sent_ref, xl2_ref, part_ref,
                     zero_ref, sem_tab, sem_pad, sem_rows2):
    i = pl.program_id(0)
    last = pl.num_programs(0) - 1
    tn = h_ref.shape[0]
    buf = i % 2
    xl_ref, sem_rows = xl2_ref.at[buf], sem_rows2.at[buf]
    cpt = pltpu.make_async_copy(tab_ref, tsm_ref, sem_tab)
    cpt.start()

    def out_copy(src_ref, src0, dst0, rows, sem=sem_rows):
        return pltpu.make_async_copy(src_ref.at[pl.ds(src0, rows)], xs_ref.at[pl.ds(dst0, rows)], sem)

    @pl.when(i == 0)
    def _first_step():
        part_ref[...] = jnp.zeros(part_ref.shape, F32)
        zero_ref[...] = jnp.zeros(zero_ref.shape, F32)
        cpp = pltpu.make_async_copy(pad_ref, psm_ref, sem_pad)
        cpp.start()
        cpp.wait()

        def per_span(e, carry):
            lo8, hi8 = (psm_ref[0, e] + 7) >> 3, psm_ref[1, e] >> 3

            def zero_copy(u):
                return out_copy(zero_ref, 0, pl.multiple_of(u * 8, 8), 8)

            lax.fori_loop(lo8, hi8, lambda u, c: (zero_copy(u).start(), c)[1], 0)
            lax.fori_loop(lo8, hi8, lambda u, c: (zero_copy(u).wait(), c)[1], 0)
            return carry

        lax.fori_loop(0, N_EXPERTS + 1, per_span, 0)

    slot = lax.broadcasted_iota(I32, (L_ROWS, tn), 0)
    sl = slot_ref[...]
    hit = jnp.where(slot == sl[0:1], 1.0, jnp.where(slot == sl[1:2], 1.0, jnp.where(
        slot == sl[2:3], 1.0, jnp.where(slot == sl[3:4], 1.0, 0.0))))
    xl_ref[...] = jnp.dot(hit.astype(BF16), h_ref[...].astype(BF16), preferred_element_type=F32)
    cpt.wait()
    sub8 = lax.broadcasted_iota(I32, (8, 1), 0)

    def per_expert(e, sent8):
        cnt, lo, g0 = tsm_ref[0, e], pl.multiple_of(tsm_ref[1, e], 8), tsm_ref[2, e]
        r = g0 & 7
        g_tile = pl.multiple_of(g0 - r, 8)
        head = pl.ds(lo, 8)
        xl_ref[head, :] = xl_ref[head, :] + part_ref[e]
        full8, rem = (r + cnt) >> 3, (r + cnt) & 7
        _for_each_chunk(full8, lambda first, rows, prio: out_copy(
            xl_ref, pl.multiple_of(lo + first, 8), pl.multiple_of(g_tile + first, 8), rows).start(priority=prio))
        tail = xl_ref[pl.ds(pl.multiple_of(lo + full8 * 8, 8), 8), :]
        part_ref[e] = jnp.where(sub8 < rem, tail, 0.0)
        return sent8 + full8

    sent8 = lax.fori_loop(0, N_EXPERTS, per_expert, 0)

    sent_ref[buf] = sent8

    @pl.when(i > 0)
    def _previous_step_copies():
        _wait_rows(sent_ref[1 - buf], lambda rows: out_copy(xl_ref, 0, 0, rows, sem_rows2.at[1 - buf]))

    @pl.when(i == last)
    def _last_step():
        def flush(e, n8):
            end = psm_ref[0, e]
            due = (end & 7) != 0

            @pl.when(due)
            def _():
                pltpu.make_async_copy(part_ref.at[e], xs_ref.at[pl.ds(pl.multiple_of(end - (end & 7), 8), 8)],
                                      sem_rows).start()

            return n8 + jnp.where(due, 1, 0)

        _wait_rows(lax.fori_loop(0, N_EXPERTS, flush, sent8), lambda rows: out_copy(xl_ref, 0, 0, rows))


def _dispatch(h2, slot, tab, pad, n_rows):
    nt = h2.shape[0]
    tile = lambda i: (i, 0)
    return pl.pallas_call(
        _dispatch_kernel,
        out_shape=jax.ShapeDtypeStruct((n_rows, D_MODEL), F32),
        grid=(nt // TM,),
        in_specs=[pl.BlockSpec((TM, D_MODEL), tile),
                  pl.BlockSpec((DEST_ROWS, TM), tile),
                  pl.BlockSpec((DEST_ROWS, HEAD_W), tile),
                  pl.BlockSpec((DEST_ROWS, HEAD_W), lambda i: (0, 0))],
        out_specs=pl.BlockSpec(memory_space=pl.ANY),
        scratch_shapes=[pltpu.SMEM((DEST_ROWS, HEAD_W), I32), pltpu.SMEM((DEST_ROWS, HEAD_W), I32),
                        pltpu.SMEM((2,), I32),
                        pltpu.VMEM((2, L_ROWS, D_MODEL), F32), pltpu.VMEM((N_EXPERTS, 8, D_MODEL), F32),
                        pltpu.VMEM((8, D_MODEL), F32),
                        pltpu.SemaphoreType.DMA, pltpu.SemaphoreType.DMA, pltpu.SemaphoreType.DMA((2,))],
        compiler_params=_params(("arbitrary",)),
        name="moe_dispatch",
    )(h2, slot, tab, pad)


def _expert_kernel(be_ref, nv_ref, xs_ref, w1_ref, b1_ref, w2_ref, b2_ref, ys_ref, w1b_ref, w2b_ref):
    i = pl.program_id(0)
    valid = i < nv_ref[0]
    prev = be_ref[jnp.maximum(i - 1, 0)]
    changed = jnp.logical_or(i == 0, be_ref[i] != prev)

    @pl.when(jnp.logical_and(valid, changed))
    def _cast_weights():
        step = 128
        def cast(j, carry):
            rows = pl.ds(pl.multiple_of(j * step, step), step)
            w1b_ref[rows, :] = w1_ref[rows, :].astype(BF16)
            w2b_ref[rows, :] = w2_ref[rows, :].astype(BF16)
            return carry
        lax.fori_loop(0, w1_ref.shape[0] // step, cast, 0)

    @pl.when(jnp.logical_not(valid))
    def _unused_block():
        ys_ref[...] = jnp.zeros(ys_ref.shape, F32)

    @pl.when(valid)
    def _mlp():
        h = jnp.dot(xs_ref[...].astype(BF16), w1b_ref[...], preferred_element_type=F32) + b1_ref[...]
        half = h.shape[1] // 2
        a = jnp.minimum(h[:, 0:half], SWIGLU_LIMIT)
        lin = jnp.clip(h[:, half:2 * half], -SWIGLU_LIMIT, SWIGLU_LIMIT)
        act = a * _sigmoid(SWIGLU_ALPHA * a) * (lin + 1.0)
        ys_ref[...] = jnp.dot(act.astype(BF16), w2b_ref[...], preferred_element_type=F32) + b2_ref[...]


def _experts(xs, blk_exp, n_valid, w1, b1, w2, b2, layer):
    n_rows = xs.shape[0]
    n_blk = n_rows // MOE_BM
    depth, d_e2 = w1.shape[0], w1.shape[3]

    def blk(i, be, nv):
        return jnp.minimum(i, nv[0] - 1)

    def expert(i, be, nv):
        return (layer, be[blk(i, be, nv)], 0, 0)

    grid_spec = pltpu.PrefetchScalarGridSpec(
        num_scalar_prefetch=2,
        grid=(n_blk,),
        in_specs=[
            pl.BlockSpec((MOE_BM, D_MODEL), lambda i, be, nv: (blk(i, be, nv), 0)),
            pl.BlockSpec((None, None, D_MODEL, d_e2), expert),
            pl.BlockSpec((None, None, 1, d_e2), expert),
            pl.BlockSpec((None, None, d_e2 // 2, D_MODEL), expert),
            pl.BlockSpec((None, None, 1, D_MODEL), expert),
        ],
        out_specs=pl.BlockSpec((MOE_BM, D_MODEL), lambda i, be, nv: (i, 0)),
        scratch_shapes=[pltpu.VMEM((D_MODEL, d_e2), BF16), pltpu.VMEM((d_e2 // 2, D_MODEL), BF16)],
    )
    return pl.pallas_call(
        _expert_kernel,
        out_shape=jax.ShapeDtypeStruct((n_rows, D_MODEL), F32),
        grid_spec=grid_spec,
        compiler_params=_params(("arbitrary",)),
        name="moe_experts",
    )(blk_exp, n_valid, xs, w1, b1.reshape(depth, N_EXPERTS, 1, d_e2), w2, b2.reshape(depth, N_EXPERTS, 1, D_MODEL))


def _combine_kernel(x_ref, gt_ref, tab_ref, tab_next_ref, m_ref, fg_ref, ys_ref, *out_and_scratch, npt):
    *o_ref, tsm_ref, got_ref, yl2_ref, sem_tab, sem_rows2 = out_and_scratch
    o_ref = o_ref[0] if npt is None else o_ref
    i = pl.program_id(0)
    tn = x_ref.shape[0]
    buf = i % 2

    def in_copy(b, src0, dst0, rows):
        return pltpu.make_async_copy(ys_ref.at[pl.ds(src0, rows)], yl2_ref.at[b, pl.ds(dst0, rows)],
                                     sem_rows2.at[b])

    def fetch(table_ref, b):
        cpt = pltpu.make_async_copy(table_ref, tsm_ref, sem_tab)
        cpt.start()
        cpt.wait()

        def per_expert(e, got8):
            cnt, lo, g0 = tsm_ref[0, e], pl.multiple_of(tsm_ref[1, e], 8), tsm_ref[2, e]
            r = g0 & 7
            g_tile = pl.multiple_of(g0 - r, 8)
            cover8 = jnp.where(cnt > 0, (r + cnt + 7) >> 3, 0)
            _for_each_chunk(cover8, lambda first, rows, prio: in_copy(
                b, pl.multiple_of(g_tile + first, 8), pl.multiple_of(lo + first, 8), rows).start(priority=prio))
            return got8 + cover8

        got_ref[b] = lax.fori_loop(0, N_EXPERTS, per_expert, 0)

    @pl.when(i == 0)
    def _first_step():
        yl2_ref[...] = jnp.zeros(yl2_ref.shape, F32)
        fetch(tab_ref, 0)

    @pl.when(i + 1 < pl.num_programs(0))
    def _next_tile():
        fetch(tab_next_ref, 1 - buf)

    gt = gt_ref[...]
    slot = lax.broadcasted_iota(I32, (tn, L_ROWS), 1).astype(F32)
    pick = jnp.where(slot == gt[:, TOP_K:TOP_K + 1], gt[:, 0:1], 0.0)
    for k in range(1, TOP_K):
        pick = pick + jnp.where(slot == gt[:, TOP_K + k:TOP_K + k + 1], gt[:, k:k + 1], 0.0)
    _wait_rows(got_ref[buf], lambda rows: in_copy(buf, 0, 0, rows))
    ph = pick.astype(BF16)
    plo = (pick - ph.astype(F32)).astype(BF16)
    both = jnp.dot(jnp.concatenate([ph, plo], axis=0), yl2_ref[buf].astype(BF16), preferred_element_type=F32)
    acc = both[0:tn] + both[tn:2 * tn]
    xn = x_ref[...] + m_ref[5] * acc
    if npt is None:
        o_ref[...] = xn
    else:
        xn = _rms(xn, fg_ref[...])
        op_ref, os_ref = o_ref

        @pl.when(i < npt)
        def _():
            op_ref[...] = xn

        @pl.when(i >= npt)
        def _():
            os_ref[...] = xn


def _combine(x, gate_t, tab, m, final_g, ys, mod_spec, n_prompt, final):
    nt = x.shape[0]
    tile = lambda i: (i, 0)
    npt = n_prompt // TM
    if final:
        out_shape = (jax.ShapeDtypeStruct((n_prompt, D_MODEL), F32), jax.ShapeDtypeStruct((nt - n_prompt, D_MODEL), F32))
        out_specs = (pl.BlockSpec((TM, D_MODEL), lambda i: (jnp.minimum(i, npt - 1), 0)),
                     pl.BlockSpec((TM, D_MODEL), lambda i: (jnp.maximum(i - npt, 0), 0)))
    else:
        out_shape, out_specs = jax.ShapeDtypeStruct((nt, D_MODEL), F32), pl.BlockSpec((TM, D_MODEL), tile)
    return pl.pallas_call(
        functools.partial(_combine_kernel, npt=npt if final else None),
        out_shape=out_shape,
        grid=(nt // TM,),
        in_specs=[pl.BlockSpec((TM, D_MODEL), tile),
                  pl.BlockSpec((TM, HEAD_W), tile),
                  pl.BlockSpec((DEST_ROWS, HEAD_W), tile),
                  pl.BlockSpec((DEST_ROWS, HEAD_W), lambda i: (jnp.minimum(i + 1, nt // TM - 1), 0)),
                  mod_spec,
                  pl.BlockSpec((1, D_MODEL), lambda i: (0, 0)),
                  pl.BlockSpec(memory_space=pl.ANY)],
        out_specs=out_specs,
        scratch_shapes=[pltpu.SMEM((DEST_ROWS, HEAD_W), I32), pltpu.SMEM((2,), I32),
                        pltpu.VMEM((2, L_ROWS, D_MODEL), F32),
                        pltpu.SemaphoreType.DMA, pltpu.SemaphoreType.DMA((2,))],
        compiler_params=_params(("arbitrary",)),
        name="moe_combine_final" if final else "moe_combine",
    )(x, gate_t, tab, tab, m, final_g, ys)


def _moe(x, h2, logits_t, m, final_g, w1, b1, w2, b2, mod_spec, n_prompt, layer, final):
    nt = x.shape[0]
    n_blk = nt * TOP_K // MOE_BM + N_EXPERTS
    n_blk_pad = -(-n_blk // HEAD_W) * HEAD_W
    slot, gate_t, tab, blk_exp, n_valid, pad = _route(logits_t, n_blk_pad, n_blk * MOE_BM)
    xs = _dispatch(h2, slot, tab, pad, n_blk * MOE_BM)
    ys = _experts(xs, blk_exp, n_valid, w1, b1, w2, b2, layer)
    return _combine(x, gate_t, tab, m, final_g, ys, mod_spec, n_prompt, final)


def _rope_tables(seq):
    t = jnp.arange(seq)
    inv = 1.0 / (ROPE_THETA ** (jnp.arange(ROPE_HALF, dtype=F32) / ROPE_HALF))
    ang_r = (t // GRID_W).astype(F32)[:, None] * inv
    ang_c = (t % GRID_W).astype(F32)[:, None] * inv
    cr, sr, cc, sc = jnp.cos(ang_r), jnp.sin(ang_r), jnp.cos(ang_c), jnp.sin(ang_c)
    cos = jnp.concatenate([cr, cr, cc, cc] * 2, axis=1)
    sin = jnp.concatenate([-sr, sr, -sc, sc] * 2, axis=1)
    return cos, sin


def kernel(x_prompt, x_sample, c, cache_k, cache_v, state_hgrn, c_ctx, mod_w, mod_b, norm_g, final_norm_g,
           w_in_even, w_out_even, da_lambda, da_subln_g, hg_norm_g, hg_lb, w_in_odd, sgu_w, sgu_b, w_out_odd,
           router_w, router_b, ex_w1, ex_b1, ex_w2, ex_b2):
    bp, lp, _ = x_prompt.shape
    bs, ls, _ = x_sample.shape
    n_p, n_s = bp * lp, bs * ls
    assert lp % TM == 0 and ls % TD == 0 and n_p % TD == 0 and n_p % ls == 0 and bs < MOD_ROWS
    depth = mod_w.shape[0]

    x = (x_prompt.reshape(n_p, D_MODEL), x_sample.reshape(n_s, D_MODEL))
    cond = jnp.zeros((MOD_ROWS, D_MODEL), F32).at[0:bs].set(c).at[bs].set(c_ctx)
    mod = _modulation(cond, mod_w, mod_b)
    mod_spec = _mod_spec(n_p, ls, bs, TM)
    mod_spec_d = _mod_spec(n_p, ls, bs, TD)
    rope_tabs = _rope_tables(ls)
    hg_lb_h = hg_lb.reshape(2, depth + 1, HG_HEADS, 1, HG_DK)
    final_g = final_norm_g.reshape(1, D_MODEL)

    new_k, new_v, new_s = [], [], []
    for l in range(depth):
        m = mod[l]
        g1 = norm_g[l, 0].reshape(1, D_MODEL)
        g2 = norm_g[l, 1].reshape(1, D_MODEL)
        if l % 2 == 0:
            e = l // 2
            lam_init = 0.8 - 0.6 * math.exp(-0.3 * l)
            proj = _inproj(x, n_p, m, g1, w_in_even[e].astype(BF16), mod_spec_d)
            subln = da_subln_g[e].reshape(1, HEAD_W)
            hgn = hg_norm_g[e].reshape(1, HEAD_W)
            mix = jnp.zeros((n_p + n_s, D_MODEL), F32)
            mix, k_new, v_new = _diff_attention(proj, mix, da_lambda[e], subln, row0=0, n_batch=bp, seq=lp,
                                                lam_init=lam_init)
            mix = _diff_attention(proj, mix, da_lambda[e], subln, row0=n_p, n_batch=bs, seq=ls, lam_init=lam_init,
                                  rope_tabs=rope_tabs, ctx_k=cache_k, ctx_v=cache_v, layer_e=e)
            mix, s_p = _hgrn(proj, mix, hg_lb_h, hgn, row0=0, n_batch=bp, seq=lp, layer=l)
            mix, _ = _hgrn(proj, mix, hg_lb_h, hgn, row0=n_p, n_batch=bs, seq=ls, layer=l, s0=state_hgrn[:, e])
            w_out = w_out_even[e].astype(BF16)
            new_k.append(k_new)
            new_v.append(v_new)
            new_s.append(s_p)
        else:
            o = l // 2
            mix = _odd_mixer(x, m, g1, w_in_odd[o].astype(BF16), sgu_w[o].astype(BF16), sgu_b[o].T, mod_spec_d)
            w_out = w_out_odd[o].astype(BF16)
        x, h2, logits_t = _post(x, n_p, mix, m, g2, w_out, router_w[l].T, router_b[l].reshape(N_EXPERTS, 1),
                                mod_spec_d)
        x = _moe(x, h2, logits_t, m, final_g, ex_w1, ex_b1, ex_w2, ex_b2, mod_spec, n_p, layer=l,
                 final=(l == depth - 1))

    y_prompt, y_sample = x
    return (y_prompt.reshape(bp, lp, D_MODEL), y_sample.reshape(bs, ls, D_MODEL),
            jnp.stack(new_k, axis=1), jnp.stack(new_v, axis=1), jnp.stack(new_s, axis=1))
```

```python
import functools
import math

import jax
import jax.numpy as jnp
from jax import lax
from jax.experimental import pallas as pl
from jax.experimental.pallas import tpu as pltpu

F32 = jnp.float32
BF16 = jnp.bfloat16
I32 = jnp.int32

D_MODEL = 1024
DEPTH = 2
GRID_W = 64
DA_HEADS = 4
DA_HD = 64
HG_HEADS = 4
HG_DK = 128
ROPE_THETA = 10000.0
ROPE_HALF = DA_HD // 4
CM_CHUNK = 128
CM_GROUPS = 8
N_EXPERTS = 32
TOP_K = 4
SWIGLU_ALPHA = 1.702
SWIGLU_LIMIT = 7.0
EPS = 1e-6
EVEN_IN = 4096
HEAD_W = 128

TM = 256
TD = 512
ATTN_SUB = 2
HG_C = 64
HG_SB = 16
HG_HPS = 2
HG_UNROLL = 4
HG_FIN = 256
HG_DOUBLE_BUFFER_BYTES = 2 * 1024 * 1024
L_ROWS = -(-(TOP_K * TM + N_EXPERTS * 14) // 128) * 128
GROUP_BITS = (32, 16, 8, 4, 2, 1)
WAIT_BITS = (128, 64, 32, 16, 8, 4, 2, 1)
MOE_BM = 512
DEST_ROWS = 8
MOD_ROWS = 16
V7X_VMEM_LIMIT = 56 * 1024 * 1024

NT_DIMS = (((1,), (1,)), ((), ()))
TN_DIMS = (((0,), (0,)), ((), ()))


def _params(sem):
    return pltpu.CompilerParams(dimension_semantics=sem, vmem_limit_bytes=V7X_VMEM_LIMIT)


def _sigmoid(x):
    return 1.0 / (1.0 + jnp.exp(-x))


def _silu(x):
    return x * _sigmoid(x)


def _rms(x, g):
    ms = jnp.mean(x * x, axis=-1, keepdims=True)
    return x * lax.rsqrt(ms + EPS) * g


def _norm_mod(x, g, shift, scale):
    return _rms(x, g) * (1.0 + scale) + shift


def _mod_kernel(c_ref, w_ref, b_ref, o_ref):
    s = _silu(c_ref[...])
    o_ref[...] = jnp.dot(s.astype(BF16), w_ref[...].astype(BF16), preferred_element_type=F32) + b_ref[...]


def _modulation(cond, mod_w, mod_b):
    depth = mod_w.shape[0]
    m = pl.pallas_call(
        _mod_kernel,
        out_shape=jax.ShapeDtypeStruct((depth, 6, MOD_ROWS, D_MODEL), F32),
        grid=(depth, 6),
        in_specs=[
            pl.BlockSpec((MOD_ROWS, D_MODEL), lambda l, j: (0, 0)),
            pl.BlockSpec((None, D_MODEL, D_MODEL), lambda l, j: (l, 0, j)),
            pl.BlockSpec((None, 1, D_MODEL), lambda l, j: (l, 0, j)),
        ],
        out_specs=pl.BlockSpec((None, None, MOD_ROWS, D_MODEL), lambda l, j: (l, j, 0, 0)),
        compiler_params=_params(("arbitrary", "arbitrary")),
        name="modulation",
    )(cond, mod_w, mod_b.reshape(depth, 1, 6 * D_MODEL))
    return m.transpose(0, 2, 1, 3)[:, :, :, None, :]


def _mod_spec(n_prompt, l_sample, ctx_row, tile):
    def index(i):
        t = i * tile
        return (jnp.where(t < n_prompt, ctx_row, (t - n_prompt) // l_sample), 0, 0, 0)
    return pl.BlockSpec((None, 6, 1, D_MODEL), index)


def _x_parts(x, n_prompt):
    npt = n_prompt // TD
    if isinstance(x, tuple):
        a, b = x
        spec_b = pl.BlockSpec((TD, D_MODEL), lambda i: (jnp.maximum(i - npt, 0), 0))
    else:
        a = b = x
        spec_b = pl.BlockSpec((TD, D_MODEL), lambda i: (jnp.maximum(i, npt), 0))
    spec_a = pl.BlockSpec((TD, D_MODEL), lambda i: (jnp.minimum(i, npt - 1), 0))
    n_rows = a.shape[0] + b.shape[0] if isinstance(x, tuple) else x.shape[0]
    return [spec_a, spec_b], [a, b], npt, n_rows


def _pick_x(xa_ref, xb_ref, npt):
    return jnp.where(pl.program_id(0) < npt, xa_ref[...], xb_ref[...])


def _inproj_kernel(xa_ref, xb_ref, m_ref, g_ref, w_ref, o_ref, *, npt):
    h = _norm_mod(_pick_x(xa_ref, xb_ref, npt), g_ref[...], m_ref[0], m_ref[1])
    o_ref[...] = jnp.dot(h.astype(BF16), w_ref[...], preferred_element_type=F32)


def _inproj(x, n_prompt, m, g, w, mod_spec):
    x_specs, x_args, npt, nt = _x_parts(x, n_prompt)
    n_out = w.shape[1]
    return pl.pallas_call(
        functools.partial(_inproj_kernel, npt=npt),
        out_shape=jax.ShapeDtypeStruct((nt, n_out), F32),
        grid=(nt // TD,),
        in_specs=x_specs + [
            mod_spec,
            pl.BlockSpec((1, D_MODEL), lambda i: (0, 0)),
            pl.BlockSpec((D_MODEL, n_out), lambda i: (0, 0)),
        ],
        out_specs=pl.BlockSpec((TD, n_out), lambda i: (i, 0)),
        compiler_params=_params(("arbitrary",)),
        name="even_inproj",
    )(*x_args, m, g, w)


def _rope(x, cos, sin_signed):
    lane = lax.broadcasted_iota(I32, x.shape, 1)
    first = ((lane // ROPE_HALF) % 2) == 0
    partner = jnp.where(first, pltpu.roll(x, HEAD_W - ROPE_HALF, 1), pltpu.roll(x, ROPE_HALF, 1))
    return x * cos + partner * sin_signed


def _attn_kernel(*refs, rope, ctx, bq, chunks, lam_init):
    it = iter(refs)
    lam_ref, g_ref, q_ref, k_ref, v_ref = (next(it) for _ in range(5))
    if rope:
        cq_ref, sq_ref, ck_ref, sk_ref = (next(it) for _ in range(4))
    if ctx:
        kc_ref, vc_ref = next(it), next(it)
    next(it)
    o_ref = next(it)
    if not ctx:
        ko_ref, vo_ref = next(it), next(it)
    kt_ref, vx_ref = next(it), next(it)
    l_new = k_ref.shape[0]
    l_ctx = kc_ref.shape[0] if ctx else 0

    @pl.when(pl.program_id(2) == 0)
    def _prepare_keys():
        if ctx:
            kt_ref[:, 0:l_ctx] = kc_ref[...].T.astype(BF16)
            vx_ref[0:l_ctx, 0:HEAD_W] = vc_ref[...].astype(BF16)
        else:
            ko_ref[...] = k_ref[...]
            vo_ref[...] = v_ref[...]
        step = min(512, l_new)
        for c0 in range(0, l_new, step):
            k = k_ref[c0:c0 + step, :]
            if rope:
                k = _rope(k, ck_ref[c0:c0 + step, :], sk_ref[c0:c0 + step, :])
            kt_ref[:, l_ctx + c0:l_ctx + c0 + step] = k.T.astype(BF16)
            vx_ref[l_ctx + c0:l_ctx + c0 + step, 0:HEAD_W] = v_ref[c0:c0 + step, :].astype(BF16)
        vx_ref[:, HEAD_W:2 * HEAD_W] = jnp.ones((l_ctx + l_new, HEAD_W), BF16)

    n_sub = q_ref.shape[0] // bq
    lane = lax.broadcasted_iota(I32, (bq, HEAD_W), 1)
    lo = lane < DA_HD
    zero = jnp.zeros((bq, HEAD_W), F32)
    qss = []
    for u in range(n_sub):
        rows = slice(u * bq, (u + 1) * bq)
        q = q_ref[rows, :]
        if rope:
            q = _rope(q, cq_ref[rows, :], sq_ref[rows, :])
        q = q * (DA_HD ** -0.5)
        qss.append(jnp.concatenate([jnp.where(lo, q, zero), jnp.where(lo, zero, q)], axis=0).astype(BF16))

    def scores(qs, chunk):
        return jnp.dot(qs, kt_ref[:, chunk[0]:chunk[0] + chunk[1]], preferred_element_type=F32)

    m = [jnp.full((2 * bq, 1), -jnp.inf, F32)] * n_sub
    l = [jnp.zeros((2 * bq, 1), F32)] * n_sub
    acc = [jnp.zeros((2 * bq, HEAD_W), F32)] * n_sub
    s_next = [scores(qs, chunks[0]) for qs in qss]
    for ci, (c0, cs) in enumerate(chunks):
        s_cur = s_next
        if ci + 1 < len(chunks):
            s_next = [scores(qs, chunks[ci + 1]) for qs in qss]
        for u in range(n_sub):
            s = s_cur[u]
            mn = jnp.maximum(m[u], jnp.max(s, axis=-1, keepdims=True))
            alpha = jnp.exp(m[u] - mn)
            p = jnp.exp(s - mn).astype(BF16)
            pv = jnp.dot(p, vx_ref[c0:c0 + cs, :], preferred_element_type=F32)
            acc[u] = alpha * acc[u] + pv[:, 0:HEAD_W]
            l[u] = alpha * l[u] + pv[:, HEAD_W:HEAD_W + 1]
            m[u] = mn
    lp = lam_ref[...]
    lam = (jnp.exp(jnp.sum(lp[0:1] * lp[1:2], axis=-1, keepdims=True))
           - jnp.exp(jnp.sum(lp[2:3] * lp[3:4], axis=-1, keepdims=True)) + lam_init)
    for u in range(n_sub):
        o2 = acc[u] / l[u]
        o = o2[0:bq] - lam * o2[bq:2 * bq]
        o_ref[u * bq:(u + 1) * bq, :] = _rms(o, g_ref[...]) * (1.0 - lam_init)


def _attn_chunks(l_ctx, l_new):
    total = l_ctx + l_new
    if total <= 1280:
        return ((0, total),)
    chunks, c0 = [], 0
    first = l_ctx + 1024
    chunks.append((0, first))
    c0 = first
    while c0 < total:
        cs = min(1024, total - c0)
        chunks.append((c0, cs))
        c0 += cs
    return tuple(chunks)


def _diff_attention(proj, mix, lam_p, subln_g, *, row0, n_batch, seq, lam_init, rope_tabs=None,
                    ctx_k=None, ctx_v=None, layer_e=0):
    rope, ctx = rope_tabs is not None, ctx_k is not None
    n_rows = proj.shape[0]
    bq = min(ATTN_SUB * 256, seq)
    nq = seq // bq
    l_ctx = ctx_k.shape[3] if ctx else 0
    qb0, kb0 = row0 // bq, row0 // seq
    in_specs = [
        pl.BlockSpec((4, DA_HD), lambda b, h, i: (0, 0)),
        pl.BlockSpec((1, HEAD_W), lambda b, h, i: (0, 0)),
        pl.BlockSpec((bq, HEAD_W), lambda b, h, i: (qb0 + b * nq + i, h)),
        pl.BlockSpec((seq, HEAD_W), lambda b, h, i: (kb0 + b, DA_HEADS + h)),
        pl.BlockSpec((seq, HEAD_W), lambda b, h, i: (kb0 + b, 2 * DA_HEADS + h)),
    ]
    args = [lam_p, subln_g, proj, proj, proj]
    if rope:
        cos, sin = rope_tabs
        in_specs += [
            pl.BlockSpec((bq, HEAD_W), lambda b, h, i: (i, 0)),
            pl.BlockSpec((bq, HEAD_W), lambda b, h, i: (i, 0)),
            pl.BlockSpec((seq, HEAD_W), lambda b, h, i: (0, 0)),
            pl.BlockSpec((seq, HEAD_W), lambda b, h, i: (0, 0)),
        ]
        args += [cos, sin, cos, sin]
    if ctx:
        in_specs += [
            pl.BlockSpec((None, None, None, l_ctx, HEAD_W), lambda b, h, i: (b, layer_e, h, 0, 0)),
            pl.BlockSpec((None, None, None, l_ctx, HEAD_W), lambda b, h, i: (b, layer_e, h, 0, 0)),
        ]
        args += [ctx_k, ctx_v]
    mix_sds = jax.ShapeDtypeStruct((n_rows, D_MODEL), F32)
    mix_spec = pl.BlockSpec((bq, HEAD_W), lambda b, h, i: (qb0 + b * nq + i, h))
    in_specs.append(pl.BlockSpec(memory_space=pl.ANY))
    args.append(mix)
    aliases = {len(args) - 1: 0}
    if ctx:
        out_shape, out_specs = mix_sds, mix_spec
    else:
        cache_sds = jax.ShapeDtypeStruct((n_batch, DA_HEADS, seq, HEAD_W), F32)
        cache_spec = pl.BlockSpec((None, None, seq, HEAD_W), lambda b, h, i: (b, h, 0, 0))
        out_shape, out_specs = (mix_sds, cache_sds, cache_sds), (mix_spec, cache_spec, cache_spec)
    kernel = functools.partial(_attn_kernel, rope=rope, ctx=ctx, bq=bq // ATTN_SUB,
                               chunks=_attn_chunks(l_ctx, seq), lam_init=lam_init)
    return pl.pallas_call(
        kernel,
        out_shape=out_shape,
        grid=(n_batch, DA_HEADS, nq),
        in_specs=in_specs,
        out_specs=out_specs,
        scratch_shapes=[pltpu.VMEM((HEAD_W, l_ctx + seq), BF16), pltpu.VMEM((l_ctx + seq, 2 * HEAD_W), BF16)],
        input_output_aliases=aliases,
        compiler_params=_params(("arbitrary", "arbitrary", "arbitrary")),
        name="diff_attention_ctx" if ctx else "diff_attention",
    )(*args)


def _hgrn_chunks(streams):
    c, nsb = HG_C, HG_C // HG_SB
    row = lax.broadcasted_iota(I32, (c, c), 0)
    col = lax.broadcasted_iota(I32, (c, c), 1)
    tri = {r: jnp.where((col >= row) if r else (col <= row), 1.0, 0.0).astype(BF16) for r in (False, True)}
    ones = jnp.ones((HG_DK, HG_DK), BF16)
    t_idx = lax.broadcasted_iota(I32, (HG_SB, 1), 0)

    g3s = []
    for q, k, v, lf, rev, f in streams:
        hi = lf.astype(BF16)
        r1 = lf - hi.astype(F32)
        mid = r1.astype(BF16)
        lo = (r1 - mid.astype(F32)).astype(BF16)
        g3s.append(jnp.dot(tri[rev], jnp.concatenate([hi, mid, lo], axis=1), preferred_element_type=F32))
    gs = [g3[:, 0:HG_DK] + g3[:, HG_DK:2 * HG_DK] + g3[:, 2 * HG_DK:3 * HG_DK] for g3 in g3s]
    g_lasts = [g[0:1] if s[4] else g[c - 1:c] for g, s in zip(gs, streams)]

    qgs = [(s[0] * jnp.exp(g)).astype(BF16) for g, s in zip(gs, streams)]

    def outside(i, rev):
        r0 = i * HG_SB
        if rev:
            return (r0 + HG_SB, c, r0 + HG_SB) if r0 + HG_SB < c else None
        return (0, r0, r0 - 1) if r0 > 0 else None

    a_offs = {}
    for i in range(nsb):
        for n, (g, s) in enumerate(zip(gs, streams)):
            span = outside(i, s[4])
            if span is None:
                continue
            s_lo, s_hi, rb = span
            r0 = i * HG_SB
            ref_g = g[rb:rb + 1]
            qd = (s[0][r0:r0 + HG_SB] * jnp.exp(g[r0:r0 + HG_SB] - ref_g)).astype(BF16)
            kd = (s[1][s_lo:s_hi] * jnp.exp(ref_g - g[s_lo:s_hi])).astype(BF16)
            a_offs[i, n] = lax.dot_general(qd, kd, NT_DIMS, preferred_element_type=F32)
    o_offs = {}
    for (i, n), a in a_offs.items():
        s_lo, s_hi, _ = outside(i, streams[n][4])
        o_offs[i, n] = jnp.dot(a.astype(BF16), streams[n][2][s_lo:s_hi].astype(BF16), preferred_element_type=F32)

    def rows_of(j, rev):
        half = HG_SB // 2
        if rev:
            return (0, half) if j < half else (0, HG_SB)
        return (half, HG_SB) if j >= half else (0, HG_SB)

    a_reps = {}
    for i in range(nsb):
        r0 = i * HG_SB
        for n, (g, s) in enumerate(zip(gs, streams)):
            qi, ki, fi = s[0][r0:r0 + HG_SB], s[1][r0:r0 + HG_SB], s[5][r0:r0 + HG_SB]
            ws, qe = {}, None
            order = range(HG_SB) if s[4] else range(HG_SB - 1, -1, -1)
            for j in order:
                t0, t1 = rows_of(j, s[4])
                here = t_idx[t0:t1] == j
                if qe is None:
                    qe = jnp.where(here, qi[t0:t1], 0.0)
                else:
                    if qe.shape[0] != t1 - t0:
                        pad8 = jnp.zeros_like(qe)
                        qe = jnp.concatenate([qe, pad8] if s[4] else [pad8, qe], axis=0)
                    step = fi[j - 1:j] if s[4] else fi[j + 1:j + 2]
                    qe = jnp.where(here, qi[t0:t1], qe * step)
                ws[j] = qe * ki[j:j + 1]
            a_reps[i, n] = jnp.dot(jnp.concatenate([ws[j] for j in range(HG_SB)], axis=0).astype(BF16), ones,
                                   preferred_element_type=F32)

    results = []
    for n, (g, s) in enumerate(zip(gs, streams)):
        q, k, v, lf, rev, f = s
        outs = []
        for i in range(nsb):
            r0 = i * HG_SB
            vi = v[r0:r0 + HG_SB]
            zero8 = jnp.zeros((HG_SB // 2, v.shape[1]), F32)
            lo_acc, hi_acc, at = zero8, zero8, 0
            for j in range(HG_SB):
                t0, t1 = rows_of(j, rev)
                term = a_reps[i, n][at:at + (t1 - t0)] * vi[j:j + 1]
                at += t1 - t0
                if t1 - t0 == HG_SB:
                    lo_acc, hi_acc = lo_acc + term[0:HG_SB // 2], hi_acc + term[HG_SB // 2:HG_SB]
                elif t0 == 0:
                    lo_acc = lo_acc + term
                else:
                    hi_acc = hi_acc + term
            oi = jnp.concatenate([lo_acc, hi_acc], axis=0)
            if (i, n) in o_offs:
                oi = oi + o_offs[i, n]
            outs.append(oi)
        kd = (k * jnp.exp(g_lasts[n] - g)).astype(BF16)
        gain = lax.dot_general(v.astype(BF16), kd, TN_DIMS, preferred_element_type=F32)
        results.append((jnp.concatenate(outs, axis=0), qgs[n], jnp.exp(g_lasts[n]), gain))
    return results


def _hgrn_advance(part, st):
    o_local, qg, decay, gain = part
    o = o_local + lax.dot_general(qg, st.astype(BF16), NT_DIMS, preferred_element_type=F32)
    return o, st * decay + gain


def _hgrn_kernel(*refs, layer, has_s0):
    it = iter(refs)
    hq_ref, zf_ref, zb_ref, hi_ref, hg_ref, lbp_ref, g_ref = (next(it) for _ in range(7))
    s0_ref = next(it) if has_s0 else None
    next(it)
    o_ref, sout_ref, st_ref, ob_ref = next(it), next(it), next(it), next(it)
    seq = hq_ref.shape[0]
    n = seq // HG_C
    z_refs = (zf_ref, zb_ref)

    lbs = []
    for hh in range(HG_HPS):
        for d in range(2):
            p = lbp_ref[d, :, hh]
            e = jnp.exp(p - jnp.max(p, axis=0))
            lbs.append(jnp.sum(e[0:layer + 1], axis=0) / jnp.sum(e, axis=0))
            if has_s0:
                st_ref[2 * hh + d] = s0_ref[d, hh].T
            else:
                st_ref[2 * hh + d] = jnp.zeros((HEAD_W, HG_DK), F32)

    def body(ci, carry):
        streams, dests = [], []
        for hh in range(HG_HPS):
            lanes = slice(hh * HEAD_W, (hh + 1) * HEAD_W)
            for d in range(2):
                for u in range(HG_UNROLL):
                    c = ci * HG_UNROLL + u
                    c = (n - 1 - c) if d == 1 else c
                    rows = pl.ds(pl.multiple_of(c * HG_C, HG_C), HG_C)
                    lb = lbs[2 * hh + d]
                    f = lb + (1.0 - lb) * _sigmoid(z_refs[d][rows, lanes])
                    streams.append((_silu(hq_ref[rows, lanes]), 1.0 - f, hi_ref[rows, lanes], jnp.log(f), d == 1, f))
                    dests.append((o_ref if d == 0 else ob_ref, rows, lanes))
        parts = _hgrn_chunks(streams)
        for slot in range(2 * HG_HPS):
            st = st_ref[slot]
            for u in range(HG_UNROLL):
                o, st = _hgrn_advance(parts[slot * HG_UNROLL + u], st)
                dst_ref, rows, lanes = dests[slot * HG_UNROLL + u]
                dst_ref[rows, lanes] = o
            st_ref[slot] = st
        return carry

    lax.fori_loop(0, n // HG_UNROLL, body, 0)
    for hh in range(HG_HPS):
        for d in range(2):
            sout_ref[d, hh] = st_ref[2 * hh + d].T

    def finish(ci, carry):
        rows = pl.ds(pl.multiple_of(ci * HG_FIN, HG_FIN), HG_FIN)
        for hh in range(HG_HPS):
            lanes = slice(hh * HEAD_W, (hh + 1) * HEAD_W)
            tot = o_ref[rows, lanes] + ob_ref[rows, lanes]
            o_ref[rows, lanes] = _rms(tot, g_ref[...]) * _silu(hg_ref[rows, lanes])
        return carry

    lax.fori_loop(0, seq // HG_FIN, finish, 0)


def _hgrn(proj, mix, hg_lb_l, hg_norm_g, *, row0, n_batch, seq, layer, s0=None):
    has_s0 = s0 is not None
    rb0 = row0 // seq
    hpg = HG_HEADS // HG_HPS
    width = HG_HPS * HEAD_W
    col0 = 3 * DA_HEADS * HEAD_W // width
    once = pl.Buffered(1) if seq * width * 4 > HG_DOUBLE_BUFFER_BYTES else None

    def col_spec(j):
        return pl.BlockSpec((seq, width), lambda b, h: (rb0 + b, col0 + j * hpg + h), pipeline_mode=once)

    in_specs = [col_spec(0), col_spec(1), col_spec(2), col_spec(3), col_spec(4),
                pl.BlockSpec((2, DEPTH + 1, HG_HPS, 1, HG_DK), lambda b, h: (0, 0, h, 0, 0)),
                pl.BlockSpec((1, HEAD_W), lambda b, h: (0, 0))]
    args = [proj, proj, proj, proj, proj, hg_lb_l, hg_norm_g]
    state_spec = pl.BlockSpec((None, 2, HG_HPS, HG_DK, HEAD_W), lambda b, h: (b, 0, h, 0, 0))
    if has_s0:
        in_specs.append(state_spec)
        args.append(s0)
    in_specs.append(pl.BlockSpec(memory_space=pl.ANY))
    args.append(mix)
    return pl.pallas_call(
        functools.partial(_hgrn_kernel, layer=layer, has_s0=has_s0),
        out_shape=(jax.ShapeDtypeStruct(mix.shape, mix.dtype),
                   jax.ShapeDtypeStruct((n_batch, 2, HG_HEADS, HG_DK, HEAD_W), F32)),
        grid=(n_batch, hpg),
        in_specs=in_specs,
        out_specs=(pl.BlockSpec((seq, width), lambda b, h: (rb0 + b, DA_HEADS * HEAD_W // width + h)),
                   state_spec),
        scratch_shapes=[pltpu.VMEM((2 * HG_HPS, HEAD_W, HG_DK), F32), pltpu.VMEM((seq, width), F32)],
        input_output_aliases={len(args) - 1: 0},
        compiler_params=_params(("arbitrary", "arbitrary")),
        name="hgrn2_state" if has_s0 else "hgrn2",
    )(*args)


def _odd_kernel(x_ref, m_ref, g_ref, win_ref, ws_ref, sb_ref, o_ref):
    h = _norm_mod(x_ref[...], g_ref[...], m_ref[0], m_ref[1])
    z = jnp.dot(h.astype(BF16), win_ref[...], preferred_element_type=F32)
    z = 0.5 * z * (1.0 + lax.erf(z * (2.0 ** -0.5)))
    width = z.shape[1] // 2
    u, v = z[:, 0:width], z[:, width:2 * width]
    mu = jnp.mean(v, axis=-1, keepdims=True)
    vc = v - mu
    var = jnp.mean(vc * vc, axis=-1, keepdims=True)
    vn = (vc * lax.rsqrt(var + EPS)).astype(BF16)
    gd = width // CM_GROUPS
    for r in range(x_ref.shape[0] // CM_CHUNK):
        rows = slice(r * CM_CHUNK, (r + 1) * CM_CHUNK)
        for gi in range(CM_GROUPS):
            cols = slice(gi * gd, (gi + 1) * gd)
            sv = jnp.dot(ws_ref[gi], vn[rows, cols], preferred_element_type=F32) + sb_ref[:, gi:gi + 1]
            o_ref[rows, cols] = u[rows, cols] * sv


def _odd_mixer(x, m, g, w_in, sgu_w, sgu_bt, mod_spec):
    nt = x.shape[0]
    return pl.pallas_call(
        _odd_kernel,
        out_shape=jax.ShapeDtypeStruct((nt, D_MODEL), F32),
        grid=(nt // TD,),
        in_specs=[
            pl.BlockSpec((TD, D_MODEL), lambda i: (i, 0)),
            mod_spec,
            pl.BlockSpec((1, D_MODEL), lambda i: (0, 0)),
            pl.BlockSpec(w_in.shape, lambda i: (0, 0)),
            pl.BlockSpec(sgu_w.shape, lambda i: (0, 0, 0)),
            pl.BlockSpec(sgu_bt.shape, lambda i: (0, 0)),
        ],
        out_specs=pl.BlockSpec((TD, D_MODEL), lambda i: (i, 0)),
        compiler_params=_params(("arbitrary",)),
        name="odd_mixer",
    )(x, m, g, w_in, sgu_w, sgu_bt)


def _post_kernel(xa_ref, xb_ref, mix_ref, m_ref, g_ref, w_ref, rwt_ref, rb_ref, xo_ref, h2_ref, lg_ref, *, npt):
    mo = jnp.dot(mix_ref[...].astype(BF16), w_ref[...], preferred_element_type=F32)
    xn = _pick_x(xa_ref, xb_ref, npt) + m_ref[2] * mo
    xo_ref[...] = xn
    h2 = _norm_mod(xn, g_ref[...], m_ref[3], m_ref[4])
    h2_ref[...] = h2
    hh = h2.astype(BF16)
    hl = (h2 - hh.astype(F32)).astype(BF16)
    rw = rwt_ref[...]
    rh = rw.astype(BF16)
    rl = (rw - rh.astype(F32)).astype(BF16)
    lg = (lax.dot_general(rh, hh, NT_DIMS, preferred_element_type=F32)
          + lax.dot_general(rh, hl, NT_DIMS, preferred_element_type=F32)
          + lax.dot_general(rl, hh, NT_DIMS, preferred_element_type=F32))
    lg_ref[...] = lg + rb_ref[...]


def _post(x, n_prompt, mix, m, g2, w_out, router_wt, router_b, mod_spec):
    x_specs, x_args, npt, nt = _x_parts(x, n_prompt)
    row = lambda i: (i, 0)
    return pl.pallas_call(
        functools.partial(_post_kernel, npt=npt),
        out_shape=(jax.ShapeDtypeStruct((nt, D_MODEL), F32),
                   jax.ShapeDtypeStruct((nt, D_MODEL), F32),
                   jax.ShapeDtypeStruct((N_EXPERTS, nt), F32)),
        grid=(nt // TD,),
        in_specs=x_specs + [
            pl.BlockSpec((TD, D_MODEL), row),
            mod_spec,
            pl.BlockSpec((1, D_MODEL), lambda i: (0, 0)),
            pl.BlockSpec((D_MODEL, D_MODEL), lambda i: (0, 0)),
            pl.BlockSpec((N_EXPERTS, D_MODEL), lambda i: (0, 0)),
            pl.BlockSpec((N_EXPERTS, 1), lambda i: (0, 0)),
        ],
        out_specs=(pl.BlockSpec((TD, D_MODEL), row), pl.BlockSpec((TD, D_MODEL), row),
                   pl.BlockSpec((N_EXPERTS, TD), lambda i: (0, i))),
        compiler_params=_params(("arbitrary",)),
        name="mixer_out_router",
    )(*x_args, mix, m, g2, w_out, router_wt, router_b)


def _experts_on_lanes(col):
    on_lane = (lax.broadcasted_iota(I32, (N_EXPERTS, HEAD_W), 0) == lax.broadcasted_iota(I32, (N_EXPERTS, HEAD_W), 1))
    return jnp.sum(jnp.where(on_lane, col, 0.0), axis=0, keepdims=True)


def _route_a_kernel(lg_ref, slot_ref, gt_ref, tab_ref, cnt_ref, carry_ref):
    i = pl.program_id(0)

    @pl.when(i == 0)
    def _():
        carry_ref[...] = jnp.zeros(carry_ref.shape, F32)

    l = lg_ref[...]
    tn = l.shape[1]
    eio = lax.broadcasted_iota(I32, l.shape, 0)
    vals, idxs = [], []
    for _ in range(TOP_K):
        mk = jnp.max(l, axis=0, keepdims=True)
        ik = jnp.min(jnp.where(l == mk, eio, N_EXPERTS), axis=0, keepdims=True)
        vals.append(mk)
        idxs.append(ik)
        l = jnp.where(eio == ik, -jnp.inf, l)
    es = [jnp.exp(v - vals[0]) for v in vals]
    den = es[0] + es[1] + es[2] + es[3]
    sel = jnp.zeros(l.shape, F32)
    for ik in idxs:
        sel = sel + jnp.where(eio == ik, 1.0, 0.0)
    upper = jnp.where(lax.broadcasted_iota(I32, (tn, tn), 0) < lax.broadcasted_iota(I32, (tn, tn), 1),
                      1.0, 0.0).astype(BF16)
    before = jnp.dot(sel.astype(BF16), upper, preferred_element_type=F32)

    tile_cnt = jnp.sum(sel, axis=1, keepdims=True)
    c_row, base_row = _experts_on_lanes(tile_cnt), _experts_on_lanes(carry_ref[...])
    sub = lax.broadcasted_iota(I32, (N_EXPERTS, HEAD_W), 0)
    lane_e = lax.broadcasted_iota(I32, (N_EXPERTS, HEAD_W), 1)
    r_row = base_row - 8.0 * jnp.floor(base_row / 8.0)
    seg_row = 8.0 * jnp.floor((r_row + c_row + 7.0) / 8.0)
    seg_col = jnp.sum(jnp.where(sub == lane_e, seg_row, 0.0), axis=1, keepdims=True)
    r_col = jnp.sum(jnp.where(sub == lane_e, r_row, 0.0), axis=1, keepdims=True)
    loff_row = jnp.sum(jnp.where(sub < lane_e, seg_col, 0.0), axis=0, keepdims=True)
    loff_col = jnp.sum(jnp.where(lane_e < sub, seg_row, 0.0), axis=1, keepdims=True)
    first_slot = loff_col + r_col
    slots = []
    for k in range(TOP_K):
        here = eio == idxs[k]
        slots.append(jnp.sum(jnp.where(here, first_slot + before, 0.0), axis=0, keepdims=True))
        slot_ref[k:k + 1, :] = slots[k].astype(I32)
    slot_ref[TOP_K:DEST_ROWS, :] = jnp.zeros((DEST_ROWS - TOP_K, tn), I32)
    gp = jnp.concatenate([e / den for e in es] + slots + [jnp.zeros((HEAD_W - 2 * TOP_K, tn), F32)], axis=0)
    gt_ref[...] = gp.T
    tab_ref[0:1, :] = c_row.astype(I32)
    tab_ref[1:2, :] = loff_row.astype(I32)
    tab_ref[2:3, :] = base_row.astype(I32)
    tab_ref[3:DEST_ROWS, :] = jnp.zeros((DEST_ROWS - 3, HEAD_W), I32)
    carry_ref[...] = carry_ref[...] + tile_cnt
    cnt_ref[...] = carry_ref[...]


def _route_plan_kernel(cnt_ref, be_ref, nv_ref, pad_ref, *, n_blk_pad, n_rows):
    cnt = cnt_ref[...]
    padded = jnp.floor((cnt + (MOE_BM - 1)) / MOE_BM) * MOE_BM
    r = lax.broadcasted_iota(I32, (N_EXPERTS, N_EXPERTS), 0)
    c = lax.broadcasted_iota(I32, (N_EXPERTS, N_EXPERTS), 1)
    padded_row = jnp.sum(jnp.where(r == c, padded, 0.0), axis=0, keepdims=True)
    pend = jnp.sum(jnp.where(c <= r, padded_row, 0.0), axis=1, keepdims=True)
    pstart = pend - padded
    blk_start = (lax.broadcasted_iota(I32, (N_EXPERTS, n_blk_pad), 1) * MOE_BM).astype(F32)
    be = jnp.sum(jnp.where(pend <= blk_start, 1.0, 0.0), axis=0, keepdims=True)
    be_ref[...] = jnp.minimum(be, N_EXPERTS - 1.0).astype(I32)
    total = jnp.sum(padded, axis=0, keepdims=True)
    nv_ref[...] = jnp.broadcast_to(total / MOE_BM, nv_ref.shape).astype(I32)
    on_lane = (lax.broadcasted_iota(I32, (N_EXPERTS, HEAD_W), 0) == lax.broadcasted_iota(I32, (N_EXPERTS, HEAD_W), 1))
    lane = lax.broadcasted_iota(I32, (1, HEAD_W), 1)
    pad_lo = jnp.sum(jnp.where(on_lane, pstart + cnt, 0.0), axis=0, keepdims=True)
    pad_hi = jnp.sum(jnp.where(on_lane, pend, 0.0), axis=0, keepdims=True)
    pad_ref[0:1, :] = jnp.where(lane == N_EXPERTS, total, pad_lo).astype(I32)
    pad_ref[1:2, :] = jnp.where(lane == N_EXPERTS, float(n_rows), pad_hi).astype(I32)
    pad_ref[2:3, :] = jnp.sum(jnp.where(on_lane, pstart, 0.0), axis=0, keepdims=True).astype(I32)
    pad_ref[3:DEST_ROWS, :] = jnp.zeros((DEST_ROWS - 3, HEAD_W), I32)


def _route(logits_t, n_blk_pad, n_rows):
    nt = logits_t.shape[1]
    tn = TM
    blk = lambda i: (0, i)
    tile = lambda i: (i, 0)
    whole = lambda i: (0, 0)
    n_tiles = nt // tn
    slot, gate_t, tab, cnt = pl.pallas_call(
        _route_a_kernel,
        out_shape=(jax.ShapeDtypeStruct((n_tiles * DEST_ROWS, tn), I32), jax.ShapeDtypeStruct((nt, HEAD_W), F32),
                   jax.ShapeDtypeStruct((n_tiles * DEST_ROWS, HEAD_W), I32),
                   jax.ShapeDtypeStruct((N_EXPERTS, 1), F32)),
        grid=(n_tiles,),
        in_specs=[pl.BlockSpec((N_EXPERTS, tn), blk)],
        out_specs=(pl.BlockSpec((DEST_ROWS, tn), tile), pl.BlockSpec((tn, HEAD_W), tile),
                   pl.BlockSpec((DEST_ROWS, HEAD_W), tile), pl.BlockSpec((N_EXPERTS, 1), whole)),
        scratch_shapes=[pltpu.VMEM((N_EXPERTS, 1), F32)],
        compiler_params=_params(("arbitrary",)),
        name="route_topk_slots",
    )(logits_t)
    blk_exp, n_valid, pad = pl.pallas_call(
        functools.partial(_route_plan_kernel, n_blk_pad=n_blk_pad, n_rows=n_rows),
        out_shape=(jax.ShapeDtypeStruct((1, n_blk_pad), I32), jax.ShapeDtypeStruct((1, HEAD_W), I32),
                   jax.ShapeDtypeStruct((DEST_ROWS, HEAD_W), I32)),
        grid=(1,),
        in_specs=[pl.BlockSpec((N_EXPERTS, 1), whole)],
        out_specs=(pl.BlockSpec((1, n_blk_pad), whole), pl.BlockSpec((1, HEAD_W), whole),
                   pl.BlockSpec((DEST_ROWS, HEAD_W), whole)),
        compiler_params=_params(("arbitrary",)),
        name="route_plan",
    )(cnt)
    return slot, gate_t, tab, blk_exp.reshape(n_blk_pad), n_valid[0, 0:1], pad


def _for_each_chunk(n8, fn):
    for bi, b in enumerate(GROUP_BITS):
        @pl.when((n8 & b) != 0)
        def _(bi=bi, b=b):
            first = pl.multiple_of((n8 - (n8 & (2 * b - 1))) * 8, 8)
            fn(first, 8 * b, bi % 2)


def _wait_rows(n8, make_copy):
    for b in WAIT_BITS:
        @pl.when((n8 & b) != 0)
        def _(b=b):
            make_copy(8 * b).wait()


def _dispatch_kernel(h_ref, slot_ref, tab_ref, pad_ref, xs_ref, tsm_ref, psm_ref, sent_ref, xl2_ref, part_ref,
                     zero_ref, sem_tab, sem_pad, sem_rows2):
    i = pl.program_id(0)
    last = pl.num_programs(0) - 1
    tn = h_ref.shape[0]
    buf = i % 2
    xl_ref, sem_rows = xl2_ref.at[buf], sem_rows2.at[buf]
    cpt = pltpu.make_async_copy(tab_ref, tsm_ref, sem_tab)
    cpt.start()

    def out_copy(src_ref, src0, dst0, rows, sem=sem_rows):
        return pltpu.make_async_copy(src_ref.at[pl.ds(src0, rows)], xs_ref.at[pl.ds(dst0, rows)], sem)

    @pl.when(i == 0)
    def _first_step():
        part_ref[...] = jnp.zeros(part_ref.shape, F32)
        zero_ref[...] = jnp.zeros(zero_ref.shape, F32)
        cpp = pltpu.make_async_copy(pad_ref, psm_ref, sem_pad)
        cpp.start()
        cpp.wait()

        def per_span(e, carry):
            lo8, hi8 = (psm_ref[0, e] + 7) >> 3, psm_ref[1, e] >> 3

            def zero_copy(u):
                return out_copy(zero_ref, 0, pl.multiple_of(u * 8, 8), 8)

            lax.fori_loop(lo8, hi8, lambda u, c: (zero_copy(u).start(), c)[1], 0)
            lax.fori_loop(lo8, hi8, lambda u, c: (zero_copy(u).wait(), c)[1], 0)
            return carry

        lax.fori_loop(0, N_EXPERTS + 1, per_span, 0)

    slot = lax.broadcasted_iota(I32, (L_ROWS, tn), 0)
    sl = slot_ref[...]
    hit = jnp.where(slot == sl[0:1], 1.0, jnp.where(slot == sl[1:2], 1.0, jnp.where(
        slot == sl[2:3], 1.0, jnp.where(slot == sl[3:4], 1.0, 0.0))))
    xl_ref[...] = jnp.dot(hit.astype(BF16), h_ref[...].astype(BF16), preferred_element_type=F32)
    cpt.wait()
    sub8 = lax.broadcasted_iota(I32, (8, 1), 0)

    def per_expert(e, sent8):
        cnt, lo, g0 = tsm_ref[0, e], pl.multiple_of(tsm_ref[1, e], 8), psm_ref[2, e] + tsm_ref[2, e]
        r = g0 & 7
        g_tile = pl.multiple_of(g0 - r, 8)
        head = pl.ds(lo, 8)
        xl_ref[head, :] = xl_ref[head, :] + part_ref[e]
        full8, rem = (r + cnt) >> 3, (r + cnt) & 7
        _for_each_chunk(full8, lambda first, rows, prio: out_copy(
            xl_ref, pl.multiple_of(lo + first, 8), pl.multiple_of(g_tile + first, 8), rows).start(priority=prio))
        tail = xl_ref[pl.ds(pl.multiple_of(lo + full8 * 8, 8), 8), :]
        part_ref[e] = jnp.where(sub8 < rem, tail, 0.0)
        return sent8 + full8

    sent8 = lax.fori_loop(0, N_EXPERTS, per_expert, 0)

    sent_ref[buf] = sent8

    @pl.when(i > 0)
    def _previous_step_copies():
        _wait_rows(sent_ref[1 - buf], lambda rows: out_copy(xl_ref, 0, 0, rows, sem_rows2.at[1 - buf]))

    @pl.when(i == last)
    def _last_step():
        def flush(e, n8):
            end = psm_ref[0, e]
            due = (end & 7) != 0

            @pl.when(due)
            def _():
                pltpu.make_async_copy(part_ref.at[e], xs_ref.at[pl.ds(pl.multiple_of(end - (end & 7), 8), 8)],
                                      sem_rows).start()

            return n8 + jnp.where(due, 1, 0)

        _wait_rows(lax.fori_loop(0, N_EXPERTS, flush, sent8), lambda rows: out_copy(xl_ref, 0, 0, rows))


def _dispatch(h2, slot, tab, pad, n_rows):
    nt = h2.shape[0]
    tile = lambda i: (i, 0)
    return pl.pallas_call(
        _dispatch_kernel,
        out_shape=jax.ShapeDtypeStruct((n_rows, D_MODEL), F32),
        grid=(nt // TM,),
        in_specs=[pl.BlockSpec((TM, D_MODEL), tile),
                  pl.BlockSpec((DEST_ROWS, TM), tile),
                  pl.BlockSpec((DEST_ROWS, HEAD_W), tile),
                  pl.BlockSpec((DEST_ROWS, HEAD_W), lambda i: (0, 0))],
        out_specs=pl.BlockSpec(memory_space=pl.ANY),
        scratch_shapes=[pltpu.SMEM((DEST_ROWS, HEAD_W), I32), pltpu.SMEM((DEST_ROWS, HEAD_W), I32),
                        pltpu.SMEM((2,), I32),
                        pltpu.VMEM((2, L_ROWS, D_MODEL), F32), pltpu.VMEM((N_EXPERTS, 8, D_MODEL), F32),
                        pltpu.VMEM((8, D_MODEL), F32),
                        pltpu.SemaphoreType.DMA, pltpu.SemaphoreType.DMA, pltpu.SemaphoreType.DMA((2,))],
        compiler_params=_params(("arbitrary",)),
        name="moe_dispatch",
    )(h2, slot, tab, pad)


def _expert_kernel(be_ref, nv_ref, xs_ref, w1_ref, b1_ref, w2_ref, b2_ref, ys_ref, w1b_ref, w2b_ref):
    i = pl.program_id(0)
    valid = i < nv_ref[0]
    prev = be_ref[jnp.maximum(i - 1, 0)]
    changed = jnp.logical_or(i == 0, be_ref[i] != prev)

    @pl.when(jnp.logical_and(valid, changed))
    def _cast_weights():
        step = 128
        def cast(j, carry):
            rows = pl.ds(pl.multiple_of(j * step, step), step)
            w1b_ref[rows, :] = w1_ref[rows, :].astype(BF16)
            w2b_ref[rows, :] = w2_ref[rows, :].astype(BF16)
            return carry
        lax.fori_loop(0, w1_ref.shape[0] // step, cast, 0)

    @pl.when(jnp.logical_not(valid))
    def _unused_block():
        ys_ref[...] = jnp.zeros(ys_ref.shape, F32)

    @pl.when(valid)
    def _mlp():
        h = jnp.dot(xs_ref[...].astype(BF16), w1b_ref[...], preferred_element_type=F32) + b1_ref[...]
        half = h.shape[1] // 2
        a = jnp.minimum(h[:, 0:half], SWIGLU_LIMIT)
        lin = jnp.clip(h[:, half:2 * half], -SWIGLU_LIMIT, SWIGLU_LIMIT)
        act = a * _sigmoid(SWIGLU_ALPHA * a) * (lin + 1.0)
        ys_ref[...] = jnp.dot(act.astype(BF16), w2b_ref[...], preferred_element_type=F32) + b2_ref[...]


def _experts(xs, blk_exp, n_valid, w1, b1, w2, b2, layer):
    n_rows = xs.shape[0]
    n_blk = n_rows // MOE_BM
    depth, d_e2 = w1.shape[0], w1.shape[3]

    def blk(i, be, nv):
        return jnp.minimum(i, nv[0] - 1)

    def expert(i, be, nv):
        return (layer, be[blk(i, be, nv)], 0, 0)

    grid_spec = pltpu.PrefetchScalarGridSpec(
        num_scalar_prefetch=2,
        grid=(n_blk,),
        in_specs=[
            pl.BlockSpec((MOE_BM, D_MODEL), lambda i, be, nv: (blk(i, be, nv), 0)),
            pl.BlockSpec((None, None, D_MODEL, d_e2), expert),
            pl.BlockSpec((None, None, 1, d_e2), expert),
            pl.BlockSpec((None, None, d_e2 // 2, D_MODEL), expert),
            pl.BlockSpec((None, None, 1, D_MODEL), expert),
        ],
        out_specs=pl.BlockSpec((MOE_BM, D_MODEL), lambda i, be, nv: (i, 0)),
        scratch_shapes=[pltpu.VMEM((D_MODEL, d_e2), BF16), pltpu.VMEM((d_e2 // 2, D_MODEL), BF16)],
    )
    return pl.pallas_call(
        _expert_kernel,
        out_shape=jax.ShapeDtypeStruct((n_rows, D_MODEL), F32),
        grid_spec=grid_spec,
        compiler_params=_params(("arbitrary",)),
        name="moe_experts",
    )(blk_exp, n_valid, xs, w1, b1.reshape(depth, N_EXPERTS, 1, d_e2), w2, b2.reshape(depth, N_EXPERTS, 1, D_MODEL))


def _combine_kernel(x_ref, gt_ref, tab_ref, tab_next_ref, pad_ref, m_ref, fg_ref, ys_ref, *out_and_scratch, npt):
    *o_ref, tsm_ref, psm_ref, got_ref, yl2_ref, sem_tab, sem_rows2 = out_and_scratch
    o_ref = o_ref[0] if npt is None else o_ref
    i = pl.program_id(0)
    tn = x_ref.shape[0]
    buf = i % 2

    def in_copy(b, src0, dst0, rows):
        return pltpu.make_async_copy(ys_ref.at[pl.ds(src0, rows)], yl2_ref.at[b, pl.ds(dst0, rows)],
                                     sem_rows2.at[b])

    def fetch(table_ref, b):
        cpt = pltpu.make_async_copy(table_ref, tsm_ref, sem_tab)
        cpt.start()
        cpt.wait()

        def per_expert(e, got8):
            cnt, lo, g0 = tsm_ref[0, e], pl.multiple_of(tsm_ref[1, e], 8), psm_ref[2, e] + tsm_ref[2, e]
            r = g0 & 7
            g_tile = pl.multiple_of(g0 - r, 8)
            cover8 = jnp.where(cnt > 0, (r + cnt + 7) >> 3, 0)
            _for_each_chunk(cover8, lambda first, rows, prio: in_copy(
                b, pl.multiple_of(g_tile + first, 8), pl.multiple_of(lo + first, 8), rows).start(priority=prio))
            return got8 + cover8

        got_ref[b] = lax.fori_loop(0, N_EXPERTS, per_expert, 0)

    @pl.when(i == 0)
    def _first_step():
        yl2_ref[...] = jnp.zeros(yl2_ref.shape, F32)
        cpp = pltpu.make_async_copy(pad_ref, psm_ref, sem_tab)
        cpp.start()
        cpp.wait()
        fetch(tab_ref, 0)

    @pl.when(i + 1 < pl.num_programs(0))
    def _next_tile():
        fetch(tab_next_ref, 1 - buf)

    gt = gt_ref[...]
    slot = lax.broadcasted_iota(I32, (tn, L_ROWS), 1).astype(F32)
    pick = jnp.where(slot == gt[:, TOP_K:TOP_K + 1], gt[:, 0:1], 0.0)
    for k in range(1, TOP_K):
        pick = pick + jnp.where(slot == gt[:, TOP_K + k:TOP_K + k + 1], gt[:, k:k + 1], 0.0)
    _wait_rows(got_ref[buf], lambda rows: in_copy(buf, 0, 0, rows))
    ph = pick.astype(BF16)
    plo = (pick - ph.astype(F32)).astype(BF16)
    both = jnp.dot(jnp.concatenate([ph, plo], axis=0), yl2_ref[buf].astype(BF16), preferred_element_type=F32)
    acc = both[0:tn] + both[tn:2 * tn]
    xn = x_ref[...] + m_ref[5] * acc
    if npt is None:
        o_ref[...] = xn
    else:
        xn = _rms(xn, fg_ref[...])
        op_ref, os_ref = o_ref

        @pl.when(i < npt)
        def _():
            op_ref[...] = xn

        @pl.when(i >= npt)
        def _():
            os_ref[...] = xn


def _combine(x, gate_t, tab, pad, m, final_g, ys, mod_spec, n_prompt, final):
    nt = x.shape[0]
    tile = lambda i: (i, 0)
    npt = n_prompt // TM
    if final:
        out_shape = (jax.ShapeDtypeStruct((n_prompt, D_MODEL), F32), jax.ShapeDtypeStruct((nt - n_prompt, D_MODEL), F32))
        out_specs = (pl.BlockSpec((TM, D_MODEL), lambda i: (jnp.minimum(i, npt - 1), 0)),
                     pl.BlockSpec((TM, D_MODEL), lambda i: (jnp.maximum(i - npt, 0), 0)))
    else:
        out_shape, out_specs = jax.ShapeDtypeStruct((nt, D_MODEL), F32), pl.BlockSpec((TM, D_MODEL), tile)
    return pl.pallas_call(
        functools.partial(_combine_kernel, npt=npt if final else None),
        out_shape=out_shape,
        grid=(nt // TM,),
        in_specs=[pl.BlockSpec((TM, D_MODEL), tile),
                  pl.BlockSpec((TM, HEAD_W), tile),
                  pl.BlockSpec((DEST_ROWS, HEAD_W), tile),
                  pl.BlockSpec((DEST_ROWS, HEAD_W), lambda i: (jnp.minimum(i + 1, nt // TM - 1), 0)),
                  pl.BlockSpec((DEST_ROWS, HEAD_W), lambda i: (0, 0)),
                  mod_spec,
                  pl.BlockSpec((1, D_MODEL), lambda i: (0, 0)),
                  pl.BlockSpec(memory_space=pl.ANY)],
        out_specs=out_specs,
        scratch_shapes=[pltpu.SMEM((DEST_ROWS, HEAD_W), I32), pltpu.SMEM((DEST_ROWS, HEAD_W), I32),
                        pltpu.SMEM((2,), I32), pltpu.VMEM((2, L_ROWS, D_MODEL), F32),
                        pltpu.SemaphoreType.DMA, pltpu.SemaphoreType.DMA((2,))],
        compiler_params=_params(("arbitrary",)),
        name="moe_combine_final" if final else "moe_combine",
    )(x, gate_t, tab, tab, pad, m, final_g, ys)


def _moe(x, h2, logits_t, m, final_g, w1, b1, w2, b2, mod_spec, n_prompt, layer, final):
    nt = x.shape[0]
    n_blk = nt * TOP_K // MOE_BM + N_EXPERTS
    n_blk_pad = -(-n_blk // HEAD_W) * HEAD_W
    slot, gate_t, tab, blk_exp, n_valid, pad = _route(logits_t, n_blk_pad, n_blk * MOE_BM)
    xs = _dispatch(h2, slot, tab, pad, n_blk * MOE_BM)
    ys = _experts(xs, blk_exp, n_valid, w1, b1, w2, b2, layer)
    return _combine(x, gate_t, tab, pad, m, final_g, ys, mod_spec, n_prompt, final)


def _rope_tables(seq):
    t = jnp.arange(seq)
    inv = 1.0 / (ROPE_THETA ** (jnp.arange(ROPE_HALF, dtype=F32) / ROPE_HALF))
    ang_r = (t // GRID_W).astype(F32)[:, None] * inv
    ang_c = (t % GRID_W).astype(F32)[:, None] * inv
    cr, sr, cc, sc = jnp.cos(ang_r), jnp.sin(ang_r), jnp.cos(ang_c), jnp.sin(ang_c)
    cos = jnp.concatenate([cr, cr, cc, cc] * 2, axis=1)
    sin = jnp.concatenate([-sr, sr, -sc, sc] * 2, axis=1)
    return cos, sin


def kernel(x_prompt, x_sample, c, cache_k, cache_v, state_hgrn, c_ctx, mod_w, mod_b, norm_g, final_norm_g,
           w_in_even, w_out_even, da_lambda, da_subln_g, hg_norm_g, hg_lb, w_in_odd, sgu_w, sgu_b, w_out_odd,
           router_w, router_b, ex_w1, ex_b1, ex_w2, ex_b2):
    bp, lp, _ = x_prompt.shape
    bs, ls, _ = x_sample.shape
    n_p, n_s = bp * lp, bs * ls
    assert lp % TM == 0 and ls % TD == 0 and n_p % TD == 0 and n_p % ls == 0 and bs < MOD_ROWS
    depth = mod_w.shape[0]

    x = (x_prompt.reshape(n_p, D_MODEL), x_sample.reshape(n_s, D_MODEL))
    cond = jnp.zeros((MOD_ROWS, D_MODEL), F32).at[0:bs].set(c).at[bs].set(c_ctx)
    mod = _modulation(cond, mod_w, mod_b)
    mod_spec = _mod_spec(n_p, ls, bs, TM)
    mod_spec_d = _mod_spec(n_p, ls, bs, TD)
    rope_tabs = _rope_tables(ls)
    hg_lb_h = hg_lb.reshape(2, depth + 1, HG_HEADS, 1, HG_DK)
    final_g = final_norm_g.reshape(1, D_MODEL)

    new_k, new_v, new_s = [], [], []
    for l in range(depth):
        m = mod[l]
        g1 = norm_g[l, 0].reshape(1, D_MODEL)
        g2 = norm_g[l, 1].reshape(1, D_MODEL)
        if l % 2 == 0:
            e = l // 2
            lam_init = 0.8 - 0.6 * math.exp(-0.3 * l)
            proj = _inproj(x, n_p, m, g1, w_in_even[e].astype(BF16), mod_spec_d)
            subln = da_subln_g[e].reshape(1, HEAD_W)
            hgn = hg_norm_g[e].reshape(1, HEAD_W)
            mix = jnp.zeros((n_p + n_s, D_MODEL), F32)
            mix, k_new, v_new = _diff_attention(proj, mix, da_lambda[e], subln, row0=0, n_batch=bp, seq=lp,
                                                lam_init=lam_init)
            mix = _diff_attention(proj, mix, da_lambda[e], subln, row0=n_p, n_batch=bs, seq=ls, lam_init=lam_init,
                                  rope_tabs=rope_tabs, ctx_k=cache_k, ctx_v=cache_v, layer_e=e)
            mix, s_p = _hgrn(proj, mix, hg_lb_h, hgn, row0=0, n_batch=bp, seq=lp, layer=l)
            mix, _ = _hgrn(proj, mix, hg_lb_h, hgn, row0=n_p, n_batch=bs, seq=ls, layer=l, s0=state_hgrn[:, e])
            w_out = w_out_even[e].astype(BF16)
            new_k.append(k_new)
            new_v.append(v_new)
            new_s.append(s_p)
        else:
            o = l // 2
            mix = _odd_mixer(x, m, g1, w_in_odd[o].astype(BF16), sgu_w[o].astype(BF16), sgu_b[o].T, mod_spec_d)
            w_out = w_out_odd[o].astype(BF16)
        x, h2, logits_t = _post(x, n_p, mix, m, g2, w_out, router_w[l].T, router_b[l].reshape(N_EXPERTS, 1),
                                mod_spec_d)
        x = _moe(x, h2, logits_t, m, final_g, ex_w1, ex_b1, ex_w2, ex_b2, mod_spec, n_p, layer=l,
                 final=(l == depth - 1))

    y_prompt, y_sample = x
    return (y_prompt.reshape(bp, lp, D_MODEL), y_sample.reshape(bs, ls, D_MODEL),
            jnp.stack(new_k, axis=1), jnp.stack(new_v, axis=1), jnp.stack(new_s, axis=1))
```

```python
import functools
import math

import jax
import jax.numpy as jnp
from jax import lax
from jax.experimental import pallas as pl
from jax.experimental.pallas import tpu as pltpu

F32 = jnp.float32
BF16 = jnp.bfloat16
I32 = jnp.int32

D_MODEL = 1024
DEPTH = 2
GRID_W = 64
DA_HEADS = 4
DA_HD = 64
HG_HEADS = 4
HG_DK = 128
ROPE_THETA = 10000.0
ROPE_HALF = DA_HD // 4
CM_CHUNK = 128
CM_GROUPS = 8
N_EXPERTS = 32
TOP_K = 4
SWIGLU_ALPHA = 1.702
SWIGLU_LIMIT = 7.0
EPS = 1e-6
EVEN_IN = 4096
HEAD_W = 128

TM = 256
TD = 512
ATTN_SUB = 2
HG_C = 64
HG_SB = 16
HG_HPS = 2
HG_UNROLL = 4
HG_FIN = 256
HG_DOUBLE_BUFFER_BYTES = 2 * 1024 * 1024
L_ROWS = -(-(TOP_K * TM + N_EXPERTS * 14) // 128) * 128
GROUP_BITS = (32, 16, 8, 4, 2, 1)
WAIT_BITS = (128, 64, 32, 16, 8, 4, 2, 1)
MOE_BM = 512
DEST_ROWS = 8
MOD_ROWS = 16
V7X_VMEM_LIMIT = 56 * 1024 * 1024

NT_DIMS = (((1,), (1,)), ((), ()))
TN_DIMS = (((0,), (0,)), ((), ()))


def _params(sem):
    return pltpu.CompilerParams(dimension_semantics=sem, vmem_limit_bytes=V7X_VMEM_LIMIT)


def _sigmoid(x):
    return 1.0 / (1.0 + jnp.exp(-x))


def _silu(x):
    return x * _sigmoid(x)


def _rms(x, g):
    ms = jnp.mean(x * x, axis=-1, keepdims=True)
    return x * lax.rsqrt(ms + EPS) * g


def _norm_mod(x, g, shift, scale):
    return _rms(x, g) * (1.0 + scale) + shift


def _mod_kernel(c_ref, w_ref, b_ref, o_ref):
    s = _silu(c_ref[...])
    o_ref[...] = jnp.dot(s.astype(BF16), w_ref[...].astype(BF16), preferred_element_type=F32) + b_ref[...]


def _modulation(cond, mod_w, mod_b):
    depth = mod_w.shape[0]
    m = pl.pallas_call(
        _mod_kernel,
        out_shape=jax.ShapeDtypeStruct((depth, 6, MOD_ROWS, D_MODEL), F32),
        grid=(depth, 6),
        in_specs=[
            pl.BlockSpec((MOD_ROWS, D_MODEL), lambda l, j: (0, 0)),
            pl.BlockSpec((None, D_MODEL, D_MODEL), lambda l, j: (l, 0, j)),
            pl.BlockSpec((None, 1, D_MODEL), lambda l, j: (l, 0, j)),
        ],
        out_specs=pl.BlockSpec((None, None, MOD_ROWS, D_MODEL), lambda l, j: (l, j, 0, 0)),
        compiler_params=_params(("arbitrary", "arbitrary")),
        name="modulation",
    )(cond, mod_w, mod_b.reshape(depth, 1, 6 * D_MODEL))
    return m.transpose(0, 2, 1, 3)[:, :, :, None, :]


def _mod_spec(n_prompt, l_sample, ctx_row, tile):
    def index(i):
        t = i * tile
        return (jnp.where(t < n_prompt, ctx_row, (t - n_prompt) // l_sample), 0, 0, 0)
    return pl.BlockSpec((None, 6, 1, D_MODEL), index)


def _x_parts(x, n_prompt):
    npt = n_prompt // TD
    if isinstance(x, tuple):
        a, b = x
        spec_b = pl.BlockSpec((TD, D_MODEL), lambda i: (jnp.maximum(i - npt, 0), 0))
    else:
        a = b = x
        spec_b = pl.BlockSpec((TD, D_MODEL), lambda i: (jnp.maximum(i, npt), 0))
    spec_a = pl.BlockSpec((TD, D_MODEL), lambda i: (jnp.minimum(i, npt - 1), 0))
    n_rows = a.shape[0] + b.shape[0] if isinstance(x, tuple) else x.shape[0]
    return [spec_a, spec_b], [a, b], npt, n_rows


def _pick_x(xa_ref, xb_ref, npt):
    return jnp.where(pl.program_id(0) < npt, xa_ref[...], xb_ref[...])


def _inproj_kernel(xa_ref, xb_ref, m_ref, g_ref, w_ref, o_ref, *, npt):
    h = _norm_mod(_pick_x(xa_ref, xb_ref, npt), g_ref[...], m_ref[0], m_ref[1])
    o_ref[...] = jnp.dot(h.astype(BF16), w_ref[...], preferred_element_type=F32)


def _inproj(x, n_prompt, m, g, w, mod_spec):
    x_specs, x_args, npt, nt = _x_parts(x, n_prompt)
    n_out = w.shape[1]
    return pl.pallas_call(
        functools.partial(_inproj_kernel, npt=npt),
        out_shape=jax.ShapeDtypeStruct((nt, n_out), F32),
        grid=(nt // TD,),
        in_specs=x_specs + [
            mod_spec,
            pl.BlockSpec((1, D_MODEL), lambda i: (0, 0)),
            pl.BlockSpec((D_MODEL, n_out), lambda i: (0, 0)),
        ],
        out_specs=pl.BlockSpec((TD, n_out), lambda i: (i, 0)),
        compiler_params=_params(("arbitrary",)),
        name="even_inproj",
    )(*x_args, m, g, w)


def _rope(x, cos, sin_signed):
    lane = lax.broadcasted_iota(I32, x.shape, 1)
    first = ((lane // ROPE_HALF) % 2) == 0
    partner = jnp.where(first, pltpu.roll(x, HEAD_W - ROPE_HALF, 1), pltpu.roll(x, ROPE_HALF, 1))
    return x * cos + partner * sin_signed


def _attn_kernel(*refs, rope, ctx, bq, chunks, lam_init):
    it = iter(refs)
    lam_ref, g_ref, q_ref, k_ref, v_ref = (next(it) for _ in range(5))
    if rope:
        cq_ref, sq_ref, ck_ref, sk_ref = (next(it) for _ in range(4))
    if ctx:
        kc_ref, vc_ref = next(it), next(it)
    next(it)
    o_ref = next(it)
    if not ctx:
        ko_ref, vo_ref = next(it), next(it)
    kt_ref, vx_ref = next(it), next(it)
    l_new = k_ref.shape[0]
    l_ctx = kc_ref.shape[0] if ctx else 0

    @pl.when(pl.program_id(2) == 0)
    def _prepare_keys():
        if ctx:
            kt_ref[:, 0:l_ctx] = kc_ref[...].T.astype(BF16)
            vx_ref[0:l_ctx, 0:HEAD_W] = vc_ref[...].astype(BF16)
        else:
            ko_ref[...] = k_ref[...]
            vo_ref[...] = v_ref[...]
        step = min(512, l_new)
        for c0 in range(0, l_new, step):
            k = k_ref[c0:c0 + step, :]
            if rope:
                k = _rope(k, ck_ref[c0:c0 + step, :], sk_ref[c0:c0 + step, :])
            kt_ref[:, l_ctx + c0:l_ctx + c0 + step] = k.T.astype(BF16)
            vx_ref[l_ctx + c0:l_ctx + c0 + step, 0:HEAD_W] = v_ref[c0:c0 + step, :].astype(BF16)
        vx_ref[:, HEAD_W:2 * HEAD_W] = jnp.ones((l_ctx + l_new, HEAD_W), BF16)

    n_sub = q_ref.shape[0] // bq
    lane = lax.broadcasted_iota(I32, (bq, HEAD_W), 1)
    lo = lane < DA_HD
    zero = jnp.zeros((bq, HEAD_W), F32)
    qss = []
    for u in range(n_sub):
        rows = slice(u * bq, (u + 1) * bq)
        q = q_ref[rows, :]
        if rope:
            q = _rope(q, cq_ref[rows, :], sq_ref[rows, :])
        q = q * (DA_HD ** -0.5)
        qss.append(jnp.concatenate([jnp.where(lo, q, zero), jnp.where(lo, zero, q)], axis=0).astype(BF16))

    def scores(qs, chunk):
        return jnp.dot(qs, kt_ref[:, chunk[0]:chunk[0] + chunk[1]], preferred_element_type=F32)

    m = [jnp.full((2 * bq, 1), -jnp.inf, F32)] * n_sub
    l = [jnp.zeros((2 * bq, 1), F32)] * n_sub
    acc = [jnp.zeros((2 * bq, HEAD_W), F32)] * n_sub
    s_next = [scores(qs, chunks[0]) for qs in qss]
    for ci, (c0, cs) in enumerate(chunks):
        s_cur = s_next
        if ci + 1 < len(chunks):
            s_next = [scores(qs, chunks[ci + 1]) for qs in qss]
        for u in range(n_sub):
            s = s_cur[u]
            mn = jnp.maximum(m[u], jnp.max(s, axis=-1, keepdims=True))
            alpha = jnp.exp(m[u] - mn)
            p = jnp.exp(s - mn).astype(BF16)
            pv = jnp.dot(p, vx_ref[c0:c0 + cs, :], preferred_element_type=F32)
            acc[u] = alpha * acc[u] + pv[:, 0:HEAD_W]
            l[u] = alpha * l[u] + pv[:, HEAD_W:HEAD_W + 1]
            m[u] = mn
    lp = lam_ref[...]
    lam = (jnp.exp(jnp.sum(lp[0:1] * lp[1:2], axis=-1, keepdims=True))
           - jnp.exp(jnp.sum(lp[2:3] * lp[3:4], axis=-1, keepdims=True)) + lam_init)
    for u in range(n_sub):
        o2 = acc[u] / l[u]
        o = o2[0:bq] - lam * o2[bq:2 * bq]
        o_ref[u * bq:(u + 1) * bq, :] = _rms(o, g_ref[...]) * (1.0 - lam_init)


def _attn_chunks(l_ctx, l_new):
    total = l_ctx + l_new
    if total <= 1280:
        return ((0, total),)
    chunks, c0 = [], 0
    first = l_ctx + 1024
    chunks.append((0, first))
    c0 = first
    while c0 < total:
        cs = min(1024, total - c0)
        chunks.append((c0, cs))
        c0 += cs
    return tuple(chunks)


def _diff_attention(proj, mix, lam_p, subln_g, *, row0, n_batch, seq, lam_init, rope_tabs=None,
                    ctx_k=None, ctx_v=None, layer_e=0):
    rope, ctx = rope_tabs is not None, ctx_k is not None
    n_rows = proj.shape[0]
    bq = min(ATTN_SUB * 256, seq)
    nq = seq // bq
    l_ctx = ctx_k.shape[3] if ctx else 0
    qb0, kb0 = row0 // bq, row0 // seq
    in_specs = [
        pl.BlockSpec((4, DA_HD), lambda b, h, i: (0, 0)),
        pl.BlockSpec((1, HEAD_W), lambda b, h, i: (0, 0)),
        pl.BlockSpec((bq, HEAD_W), lambda b, h, i: (qb0 + b * nq + i, h)),
        pl.BlockSpec((seq, HEAD_W), lambda b, h, i: (kb0 + b, DA_HEADS + h)),
        pl.BlockSpec((seq, HEAD_W), lambda b, h, i: (kb0 + b, 2 * DA_HEADS + h)),
    ]
    args = [lam_p, subln_g, proj, proj, proj]
    if rope:
        cos, sin = rope_tabs
        in_specs += [
            pl.BlockSpec((bq, HEAD_W), lambda b, h, i: (i, 0)),
            pl.BlockSpec((bq, HEAD_W), lambda b, h, i: (i, 0)),
            pl.BlockSpec((seq, HEAD_W), lambda b, h, i: (0, 0)),
            pl.BlockSpec((seq, HEAD_W), lambda b, h, i: (0, 0)),
        ]
        args += [cos, sin, cos, sin]
    if ctx:
        in_specs += [
            pl.BlockSpec((None, None, None, l_ctx, HEAD_W), lambda b, h, i: (b, layer_e, h, 0, 0)),
            pl.BlockSpec((None, None, None, l_ctx, HEAD_W), lambda b, h, i: (b, layer_e, h, 0, 0)),
        ]
        args += [ctx_k, ctx_v]
    mix_sds = jax.ShapeDtypeStruct((n_rows, D_MODEL), F32)
    mix_spec = pl.BlockSpec((bq, HEAD_W), lambda b, h, i: (qb0 + b * nq + i, h))
    in_specs.append(pl.BlockSpec(memory_space=pl.ANY))
    args.append(mix)
    aliases = {len(args) - 1: 0}
    if ctx:
        out_shape, out_specs = mix_sds, mix_spec
    else:
        cache_sds = jax.ShapeDtypeStruct((n_batch, DA_HEADS, seq, HEAD_W), F32)
        cache_spec = pl.BlockSpec((None, None, seq, HEAD_W), lambda b, h, i: (b, h, 0, 0))
        out_shape, out_specs = (mix_sds, cache_sds, cache_sds), (mix_spec, cache_spec, cache_spec)
    kernel = functools.partial(_attn_kernel, rope=rope, ctx=ctx, bq=bq // ATTN_SUB,
                               chunks=_attn_chunks(l_ctx, seq), lam_init=lam_init)
    return pl.pallas_call(
        kernel,
        out_shape=out_shape,
        grid=(n_batch, DA_HEADS, nq),
        in_specs=in_specs,
        out_specs=out_specs,
        scratch_shapes=[pltpu.VMEM((HEAD_W, l_ctx + seq), BF16), pltpu.VMEM((l_ctx + seq, 2 * HEAD_W), BF16)],
        input_output_aliases=aliases,
        compiler_params=_params(("arbitrary", "arbitrary", "arbitrary")),
        name="diff_attention_ctx" if ctx else "diff_attention",
    )(*args)


def _hgrn_chunks(streams):
    c, nsb = HG_C, HG_C // HG_SB
    row = lax.broadcasted_iota(I32, (c, c), 0)
    col = lax.broadcasted_iota(I32, (c, c), 1)
    tri = {r: jnp.where((col >= row) if r else (col <= row), 1.0, 0.0).astype(BF16) for r in (False, True)}
    ones = jnp.ones((HG_DK, HG_DK), BF16)
    t_idx = lax.broadcasted_iota(I32, (HG_SB, 1), 0)

    g3s = []
    for q, k, v, lf, rev, f in streams:
        hi = lf.astype(BF16)
        r1 = lf - hi.astype(F32)
        mid = r1.astype(BF16)
        lo = (r1 - mid.astype(F32)).astype(BF16)
        g3s.append(jnp.dot(tri[rev], jnp.concatenate([hi, mid, lo], axis=1), preferred_element_type=F32))
    gs = [g3[:, 0:HG_DK] + g3[:, HG_DK:2 * HG_DK] + g3[:, 2 * HG_DK:3 * HG_DK] for g3 in g3s]
    g_lasts = [g[0:1] if s[4] else g[c - 1:c] for g, s in zip(gs, streams)]

    qgs = [(s[0] * jnp.exp(g)).astype(BF16) for g, s in zip(gs, streams)]

    def outside(i, rev):
        r0 = i * HG_SB
        if rev:
            return (r0 + HG_SB, c, r0 + HG_SB) if r0 + HG_SB < c else None
        return (0, r0, r0 - 1) if r0 > 0 else None

    a_offs = {}
    for i in range(nsb):
        for n, (g, s) in enumerate(zip(gs, streams)):
            span = outside(i, s[4])
            if span is None:
                continue
            s_lo, s_hi, rb = span
            r0 = i * HG_SB
            ref_g = g[rb:rb + 1]
            qd = (s[0][r0:r0 + HG_SB] * jnp.exp(g[r0:r0 + HG_SB] - ref_g)).astype(BF16)
            kd = (s[1][s_lo:s_hi] * jnp.exp(ref_g - g[s_lo:s_hi])).astype(BF16)
            a_offs[i, n] = lax.dot_general(qd, kd, NT_DIMS, preferred_element_type=F32)
    o_offs = {}
    for (i, n), a in a_offs.items():
        s_lo, s_hi, _ = outside(i, streams[n][4])
        o_offs[i, n] = jnp.dot(a.astype(BF16), streams[n][2][s_lo:s_hi].astype(BF16), preferred_element_type=F32)

    def rows_of(j, rev):
        half = HG_SB // 2
        if rev:
            return (0, half) if j < half else (0, HG_SB)
        return (half, HG_SB) if j >= half else (0, HG_SB)

    a_reps = {}
    for i in range(nsb):
        r0 = i * HG_SB
        for n, (g, s) in enumerate(zip(gs, streams)):
            qi, ki, fi = s[0][r0:r0 + HG_SB], s[1][r0:r0 + HG_SB], s[5][r0:r0 + HG_SB]
            ws, qe = {}, None
            order = range(HG_SB) if s[4] else range(HG_SB - 1, -1, -1)
            for j in order:
                t0, t1 = rows_of(j, s[4])
                here = t_idx[t0:t1] == j
                if qe is None:
                    qe = jnp.where(here, qi[t0:t1], 0.0)
                else:
                    if qe.shape[0] != t1 - t0:
                        pad8 = jnp.zeros_like(qe)
                        qe = jnp.concatenate([qe, pad8] if s[4] else [pad8, qe], axis=0)
                    step = fi[j - 1:j] if s[4] else fi[j + 1:j + 2]
                    qe = jnp.where(here, qi[t0:t1], qe * step)
                ws[j] = qe * ki[j:j + 1]
            a_reps[i, n] = jnp.dot(jnp.concatenate([ws[j] for j in range(HG_SB)], axis=0).astype(BF16), ones,
                                   preferred_element_type=F32)

    results = []
    for n, (g, s) in enumerate(zip(gs, streams)):
        q, k, v, lf, rev, f = s
        outs = []
        for i in range(nsb):
            r0 = i * HG_SB
            vi = v[r0:r0 + HG_SB]
            zero8 = jnp.zeros((HG_SB // 2, v.shape[1]), F32)
            lo_acc, hi_acc, at = zero8, zero8, 0
            for j in range(HG_SB):
                t0, t1 = rows_of(j, rev)
                term = a_reps[i, n][at:at + (t1 - t0)] * vi[j:j + 1]
                at += t1 - t0
                if t1 - t0 == HG_SB:
                    lo_acc, hi_acc = lo_acc + term[0:HG_SB // 2], hi_acc + term[HG_SB // 2:HG_SB]
                elif t0 == 0:
                    lo_acc = lo_acc + term
                else:
                    hi_acc = hi_acc + term
            oi = jnp.concatenate([lo_acc, hi_acc], axis=0)
            if (i, n) in o_offs:
                oi = oi + o_offs[i, n]
            outs.append(oi)
        kd = (k * jnp.exp(g_lasts[n] - g)).astype(BF16)
        gain = lax.dot_general(v.astype(BF16), kd, TN_DIMS, preferred_element_type=F32)
        results.append((jnp.concatenate(outs, axis=0), qgs[n], jnp.exp(g_lasts[n]), gain))
    return results


def _hgrn_advance(part, st):
    o_local, qg, decay, gain = part
    o = o_local + lax.dot_general(qg, st.astype(BF16), NT_DIMS, preferred_element_type=F32)
    return o, st * decay + gain


def _hgrn_kernel(*refs, layer, has_s0):
    it = iter(refs)
    hq_ref, zf_ref, zb_ref, hi_ref, hg_ref, lbp_ref, g_ref = (next(it) for _ in range(7))
    s0_ref = next(it) if has_s0 else None
    next(it)
    o_ref, sout_ref, st_ref, ob_ref = next(it), next(it), next(it), next(it)
    seq = hq_ref.shape[0]
    n = seq // HG_C
    z_refs = (zf_ref, zb_ref)

    lbs = []
    for hh in range(HG_HPS):
        for d in range(2):
            p = lbp_ref[d, :, hh]
            e = jnp.exp(p - jnp.max(p, axis=0))
            lbs.append(jnp.sum(e[0:layer + 1], axis=0) / jnp.sum(e, axis=0))
            if has_s0:
                st_ref[2 * hh + d] = s0_ref[d, hh].T
            else:
                st_ref[2 * hh + d] = jnp.zeros((HEAD_W, HG_DK), F32)

    def body(ci, carry):
        streams, dests = [], []
        for hh in range(HG_HPS):
            lanes = slice(hh * HEAD_W, (hh + 1) * HEAD_W)
            for d in range(2):
                for u in range(HG_UNROLL):
                    c = ci * HG_UNROLL + u
                    c = (n - 1 - c) if d == 1 else c
                    rows = pl.ds(pl.multiple_of(c * HG_C, HG_C), HG_C)
                    lb = lbs[2 * hh + d]
                    f = lb + (1.0 - lb) * _sigmoid(z_refs[d][rows, lanes])
                    streams.append((_silu(hq_ref[rows, lanes]), 1.0 - f, hi_ref[rows, lanes], jnp.log(f), d == 1, f))
                    dests.append((o_ref if d == 0 else ob_ref, rows, lanes))
        parts = _hgrn_chunks(streams)
        for slot in range(2 * HG_HPS):
            st = st_ref[slot]
            for u in range(HG_UNROLL):
                o, st = _hgrn_advance(parts[slot * HG_UNROLL + u], st)
                dst_ref, rows, lanes = dests[slot * HG_UNROLL + u]
                dst_ref[rows, lanes] = o
            st_ref[slot] = st
        return carry

    lax.fori_loop(0, n // HG_UNROLL, body, 0)
    for hh in range(HG_HPS):
        for d in range(2):
            sout_ref[d, hh] = st_ref[2 * hh + d].T

    def finish(ci, carry):
        rows = pl.ds(pl.multiple_of(ci * HG_FIN, HG_FIN), HG_FIN)
        for hh in range(HG_HPS):
            lanes = slice(hh * HEAD_W, (hh + 1) * HEAD_W)
            tot = o_ref[rows, lanes] + ob_ref[rows, lanes]
            o_ref[rows, lanes] = _rms(tot, g_ref[...]) * _silu(hg_ref[rows, lanes])
        return carry

    lax.fori_loop(0, seq // HG_FIN, finish, 0)


def _hgrn(proj, mix, hg_lb_l, hg_norm_g, *, row0, n_batch, seq, layer, s0=None):
    has_s0 = s0 is not None
    rb0 = row0 // seq
    hpg = HG_HEADS // HG_HPS
    width = HG_HPS * HEAD_W
    col0 = 3 * DA_HEADS * HEAD_W // width
    once = pl.Buffered(1) if seq * width * 4 > HG_DOUBLE_BUFFER_BYTES else None

    def col_spec(j):
        return pl.BlockSpec((seq, width), lambda b, h: (rb0 + b, col0 + j * hpg + h), pipeline_mode=once)

    in_specs = [col_spec(0), col_spec(1), col_spec(2), col_spec(3), col_spec(4),
                pl.BlockSpec((2, DEPTH + 1, HG_HPS, 1, HG_DK), lambda b, h: (0, 0, h, 0, 0)),
                pl.BlockSpec((1, HEAD_W), lambda b, h: (0, 0))]
    args = [proj, proj, proj, proj, proj, hg_lb_l, hg_norm_g]
    state_spec = pl.BlockSpec((None, 2, HG_HPS, HG_DK, HEAD_W), lambda b, h: (b, 0, h, 0, 0))
    if has_s0:
        in_specs.append(state_spec)
        args.append(s0)
    in_specs.append(pl.BlockSpec(memory_space=pl.ANY))
    args.append(mix)
    return pl.pallas_call(
        functools.partial(_hgrn_kernel, layer=layer, has_s0=has_s0),
        out_shape=(jax.ShapeDtypeStruct(mix.shape, mix.dtype),
                   jax.ShapeDtypeStruct((n_batch, 2, HG_HEADS, HG_DK, HEAD_W), F32)),
        grid=(n_batch, hpg),
        in_specs=in_specs,
        out_specs=(pl.BlockSpec((seq, width), lambda b, h: (rb0 + b, DA_HEADS * HEAD_W // width + h)),
                   state_spec),
        scratch_shapes=[pltpu.VMEM((2 * HG_HPS, HEAD_W, HG_DK), F32), pltpu.VMEM((seq, width), F32)],
        input_output_aliases={len(args) - 1: 0},
        compiler_params=_params(("arbitrary", "arbitrary")),
        name="hgrn2_state" if has_s0 else "hgrn2",
    )(*args)


def _odd_kernel(x_ref, m_ref, g_ref, win_ref, ws_ref, sb_ref, o_ref):
    h = _norm_mod(x_ref[...], g_ref[...], m_ref[0], m_ref[1])
    z = jnp.dot(h.astype(BF16), win_ref[...], preferred_element_type=F32)
    z = 0.5 * z * (1.0 + lax.erf(z * (2.0 ** -0.5)))
    width = z.shape[1] // 2
    u, v = z[:, 0:width], z[:, width:2 * width]
    mu = jnp.mean(v, axis=-1, keepdims=True)
    vc = v - mu
    var = jnp.mean(vc * vc, axis=-1, keepdims=True)
    vn = (vc * lax.rsqrt(var + EPS)).astype(BF16)
    gd = width // CM_GROUPS
    for r in range(x_ref.shape[0] // CM_CHUNK):
        rows = slice(r * CM_CHUNK, (r + 1) * CM_CHUNK)
        for gi in range(CM_GROUPS):
            cols = slice(gi * gd, (gi + 1) * gd)
            sv = jnp.dot(ws_ref[gi], vn[rows, cols], preferred_element_type=F32) + sb_ref[:, gi:gi + 1]
            o_ref[rows, cols] = u[rows, cols] * sv


def _odd_mixer(x, m, g, w_in, sgu_w, sgu_bt, mod_spec):
    nt = x.shape[0]
    return pl.pallas_call(
        _odd_kernel,
        out_shape=jax.ShapeDtypeStruct((nt, D_MODEL), F32),
        grid=(nt // TD,),
        in_specs=[
            pl.BlockSpec((TD, D_MODEL), lambda i: (i, 0)),
            mod_spec,
            pl.BlockSpec((1, D_MODEL), lambda i: (0, 0)),
            pl.BlockSpec(w_in.shape, lambda i: (0, 0)),
            pl.BlockSpec(sgu_w.shape, lambda i: (0, 0, 0)),
            pl.BlockSpec(sgu_bt.shape, lambda i: (0, 0)),
        ],
        out_specs=pl.BlockSpec((TD, D_MODEL), lambda i: (i, 0)),
        compiler_params=_params(("arbitrary",)),
        name="odd_mixer",
    )(x, m, g, w_in, sgu_w, sgu_bt)


def _post_kernel(xa_ref, xb_ref, mix_ref, m_ref, g_ref, w_ref, rwt_ref, rb_ref, xo_ref, h2_ref, lg_ref, *, npt):
    mo = jnp.dot(mix_ref[...].astype(BF16), w_ref[...], preferred_element_type=F32)
    xn = _pick_x(xa_ref, xb_ref, npt) + m_ref[2] * mo
    xo_ref[...] = xn
    h2 = _norm_mod(xn, g_ref[...], m_ref[3], m_ref[4])
    h2_ref[...] = h2
    hh = h2.astype(BF16)
    hl = (h2 - hh.astype(F32)).astype(BF16)
    rw = rwt_ref[...]
    rh = rw.astype(BF16)
    rl = (rw - rh.astype(F32)).astype(BF16)
    lg = (lax.dot_general(rh, hh, NT_DIMS, preferred_element_type=F32)
          + lax.dot_general(rh, hl, NT_DIMS, preferred_element_type=F32)
          + lax.dot_general(rl, hh, NT_DIMS, preferred_element_type=F32))
    lg_ref[...] = lg + rb_ref[...]


def _post(x, n_prompt, mix, m, g2, w_out, router_wt, router_b, mod_spec):
    x_specs, x_args, npt, nt = _x_parts(x, n_prompt)
    row = lambda i: (i, 0)
    return pl.pallas_call(
        functools.partial(_post_kernel, npt=npt),
        out_shape=(jax.ShapeDtypeStruct((nt, D_MODEL), F32),
                   jax.ShapeDtypeStruct((nt, D_MODEL), F32),
                   jax.ShapeDtypeStruct((N_EXPERTS, nt), F32)),
        grid=(nt // TD,),
        in_specs=x_specs + [
            pl.BlockSpec((TD, D_MODEL), row),
            mod_spec,
            pl.BlockSpec((1, D_MODEL), lambda i: (0, 0)),
            pl.BlockSpec((D_MODEL, D_MODEL), lambda i: (0, 0)),
            pl.BlockSpec((N_EXPERTS, D_MODEL), lambda i: (0, 0)),
            pl.BlockSpec((N_EXPERTS, 1), lambda i: (0, 0)),
        ],
        out_specs=(pl.BlockSpec((TD, D_MODEL), row), pl.BlockSpec((TD, D_MODEL), row),
                   pl.BlockSpec((N_EXPERTS, TD), lambda i: (0, i))),
        compiler_params=_params(("arbitrary",)),
        name="mixer_out_router",
    )(*x_args, mix, m, g2, w_out, router_wt, router_b)


def _experts_on_lanes(col):
    on_lane = (lax.broadcasted_iota(I32, (N_EXPERTS, HEAD_W), 0) == lax.broadcasted_iota(I32, (N_EXPERTS, HEAD_W), 1))
    return jnp.sum(jnp.where(on_lane, col, 0.0), axis=0, keepdims=True)


def _route_a_kernel(lg_ref, slot_ref, gt_ref, tab_ref, cnt_ref, carry_ref):
    i = pl.program_id(0)

    @pl.when(i == 0)
    def _():
        carry_ref[...] = jnp.zeros(carry_ref.shape, F32)

    l = lg_ref[...]
    tn = l.shape[1]
    eio = lax.broadcasted_iota(I32, l.shape, 0)
    vals, idxs = [], []
    for _ in range(TOP_K):
        mk = jnp.max(l, axis=0, keepdims=True)
        ik = jnp.min(jnp.where(l == mk, eio, N_EXPERTS), axis=0, keepdims=True)
        vals.append(mk)
        idxs.append(ik)
        l = jnp.where(eio == ik, -jnp.inf, l)
    es = [jnp.exp(v - vals[0]) for v in vals]
    den = es[0] + es[1] + es[2] + es[3]
    sel = jnp.zeros(l.shape, F32)
    for ik in idxs:
        sel = sel + jnp.where(eio == ik, 1.0, 0.0)
    upper = jnp.where(lax.broadcasted_iota(I32, (tn, tn), 0) < lax.broadcasted_iota(I32, (tn, tn), 1),
                      1.0, 0.0).astype(BF16)
    before = jnp.dot(sel.astype(BF16), upper, preferred_element_type=F32)

    tile_cnt = jnp.sum(sel, axis=1, keepdims=True)
    c_row, base_row = _experts_on_lanes(tile_cnt), _experts_on_lanes(carry_ref[...])
    sub = lax.broadcasted_iota(I32, (N_EXPERTS, HEAD_W), 0)
    lane_e = lax.broadcasted_iota(I32, (N_EXPERTS, HEAD_W), 1)
    r_row = base_row - 8.0 * jnp.floor(base_row / 8.0)
    seg_row = 8.0 * jnp.floor((r_row + c_row + 7.0) / 8.0)
    seg_col = jnp.sum(jnp.where(sub == lane_e, seg_row, 0.0), axis=1, keepdims=True)
    r_col = jnp.sum(jnp.where(sub == lane_e, r_row, 0.0), axis=1, keepdims=True)
    loff_row = jnp.sum(jnp.where(sub < lane_e, seg_col, 0.0), axis=0, keepdims=True)
    loff_col = jnp.sum(jnp.where(lane_e < sub, seg_row, 0.0), axis=1, keepdims=True)
    first_slot = loff_col + r_col
    slots = []
    for k in range(TOP_K):
        here = eio == idxs[k]
        slots.append(jnp.sum(jnp.where(here, first_slot + before, 0.0), axis=0, keepdims=True))
        slot_ref[k:k + 1, :] = slots[k].astype(I32)
    slot_ref[TOP_K:DEST_ROWS, :] = jnp.zeros((DEST_ROWS - TOP_K, tn), I32)
    gp = jnp.concatenate([e / den for e in es] + slots + [jnp.zeros((HEAD_W - 2 * TOP_K, tn), F32)], axis=0)
    gt_ref[...] = gp.T
    tab_ref[0:1, :] = c_row.astype(I32)
    tab_ref[1:2, :] = loff_row.astype(I32)
    tab_ref[2:3, :] = base_row.astype(I32)
    tab_ref[3:DEST_ROWS, :] = jnp.zeros((DEST_ROWS - 3, HEAD_W), I32)
    carry_ref[...] = carry_ref[...] + tile_cnt
    cnt_ref[...] = carry_ref[...]


def _route_plan_kernel(cnt_ref, be_ref, nv_ref, pad_ref, *, n_blk_pad, n_rows):
    cnt = cnt_ref[...]
    padded = jnp.floor((cnt + (MOE_BM - 1)) / MOE_BM) * MOE_BM
    r = lax.broadcasted_iota(I32, (N_EXPERTS, N_EXPERTS), 0)
    c = lax.broadcasted_iota(I32, (N_EXPERTS, N_EXPERTS), 1)
    padded_row = jnp.sum(jnp.where(r == c, padded, 0.0), axis=0, keepdims=True)
    pend = jnp.sum(jnp.where(c <= r, padded_row, 0.0), axis=1, keepdims=True)
    pstart = pend - padded
    blk_start = (lax.broadcasted_iota(I32, (N_EXPERTS, n_blk_pad), 1) * MOE_BM).astype(F32)
    be = jnp.sum(jnp.where(pend <= blk_start, 1.0, 0.0), axis=0, keepdims=True)
    be_ref[...] = jnp.minimum(be, N_EXPERTS - 1.0).astype(I32)
    total = jnp.sum(padded, axis=0, keepdims=True)
    nv_ref[...] = jnp.broadcast_to(total / MOE_BM, nv_ref.shape).astype(I32)
    on_lane = (lax.broadcasted_iota(I32, (N_EXPERTS, HEAD_W), 0) == lax.broadcasted_iota(I32, (N_EXPERTS, HEAD_W), 1))
    lane = lax.broadcasted_iota(I32, (1, HEAD_W), 1)
    pad_lo = jnp.sum(jnp.where(on_lane, pstart + cnt, 0.0), axis=0, keepdims=True)
    pad_hi = jnp.sum(jnp.where(on_lane, pend, 0.0), axis=0, keepdims=True)
    pad_ref[0:1, :] = jnp.where(lane == N_EXPERTS, total, pad_lo).astype(I32)
    pad_ref[1:2, :] = jnp.where(lane == N_EXPERTS, float(n_rows), pad_hi).astype(I32)
    pad_ref[2:3, :] = jnp.sum(jnp.where(on_lane, pstart, 0.0), axis=0, keepdims=True).astype(I32)
    pad_ref[3:DEST_ROWS, :] = jnp.zeros((DEST_ROWS - 3, HEAD_W), I32)


def _route(logits_t, n_blk_pad, n_rows):
    nt = logits_t.shape[1]
    tn = TM
    blk = lambda i: (0, i)
    tile = lambda i: (i, 0)
    whole = lambda i: (0, 0)
    n_tiles = nt // tn
    slot, gate_t, tab, cnt = pl.pallas_call(
        _route_a_kernel,
        out_shape=(jax.ShapeDtypeStruct((n_tiles * DEST_ROWS, tn), I32), jax.ShapeDtypeStruct((nt, HEAD_W), F32),
                   jax.ShapeDtypeStruct((n_tiles * DEST_ROWS, HEAD_W), I32),
                   jax.ShapeDtypeStruct((N_EXPERTS, 1), F32)),
        grid=(n_tiles,),
        in_specs=[pl.BlockSpec((N_EXPERTS, tn), blk)],
        out_specs=(pl.BlockSpec((DEST_ROWS, tn), tile), pl.BlockSpec((tn, HEAD_W), tile),
                   pl.BlockSpec((DEST_ROWS, HEAD_W), tile), pl.BlockSpec((N_EXPERTS, 1), whole)),
        scratch_shapes=[pltpu.VMEM((N_EXPERTS, 1), F32)],
        compiler_params=_params(("arbitrary",)),
        name="route_topk_slots",
    )(logits_t)
    blk_exp, n_valid, pad = pl.pallas_call(
        functools.partial(_route_plan_kernel, n_blk_pad=n_blk_pad, n_rows=n_rows),
        out_shape=(jax.ShapeDtypeStruct((1, n_blk_pad), I32), jax.ShapeDtypeStruct((1, HEAD_W), I32),
                   jax.ShapeDtypeStruct((DEST_ROWS, HEAD_W), I32)),
        grid=(1,),
        in_specs=[pl.BlockSpec((N_EXPERTS, 1), whole)],
        out_specs=(pl.BlockSpec((1, n_blk_pad), whole), pl.BlockSpec((1, HEAD_W), whole),
                   pl.BlockSpec((DEST_ROWS, HEAD_W), whole)),
        compiler_params=_params(("arbitrary",)),
        name="route_plan",
    )(cnt)
    return slot, gate_t, tab, blk_exp.reshape(n_blk_pad), n_valid[0, 0:1], pad


def _for_each_chunk(n8, fn):
    for bi, b in enumerate(GROUP_BITS):
        @pl.when((n8 & b) != 0)
        def _(bi=bi, b=b):
            first = pl.multiple_of((n8 - (n8 & (2 * b - 1))) * 8, 8)
            fn(first, 8 * b, bi % 2)


def _wait_rows(n8, make_copy):
    for b in WAIT_BITS:
        @pl.when((n8 & b) != 0)
        def _(b=b):
            make_copy(8 * b).wait()


def _dispatch_kernel(h_ref, slot_ref, tab_ref, pad_ref, xs_ref, tsm_ref, psm_ref, sent_ref, xl2_ref, part_ref,
                     zero_ref, sem_tab, sem_pad, sem_rows2):
    i = pl.program_id(0)
    last = pl.num_programs(0) - 1
    tn = h_ref.shape[0]
    buf = i % 2
    xl_ref, sem_rows = xl2_ref.at[buf], sem_rows2.at[buf]
    cpt = pltpu.make_async_copy(tab_ref, tsm_ref, sem_tab)
    cpt.start()

    def out_copy(src_ref, src0, dst0, rows, sem=sem_rows):
        return pltpu.make_async_copy(src_ref.at[pl.ds(src0, rows)], xs_ref.at[pl.ds(dst0, rows)], sem)

    @pl.when(i == 0)
    def _first_step():
        part_ref[...] = jnp.zeros(part_ref.shape, F32)
        zero_ref[...] = jnp.zeros(zero_ref.shape, F32)
        cpp = pltpu.make_async_copy(pad_ref, psm_ref, sem_pad)
        cpp.start()
        cpp.wait()

        def per_span(e, carry):
            lo8, hi8 = (psm_ref[0, e] + 7) >> 3, psm_ref[1, e] >> 3

            def zero_copy(u):
                return out_copy(zero_ref, 0, pl.multiple_of(u * 8, 8), 8)

            lax.fori_loop(lo8, hi8, lambda u, c: (zero_copy(u).start(), c)[1], 0)
            lax.fori_loop(lo8, hi8, lambda u, c: (zero_copy(u).wait(), c)[1], 0)
            return carry

        lax.fori_loop(0, N_EXPERTS + 1, per_span, 0)

    slot = lax.broadcasted_iota(I32, (L_ROWS, tn), 0)
    sl = slot_ref[...]
    hit = jnp.where(slot == sl[0:1], 1.0, jnp.where(slot == sl[1:2], 1.0, jnp.where(
        slot == sl[2:3], 1.0, jnp.where(slot == sl[3:4], 1.0, 0.0))))
    xl_ref[...] = jnp.dot(hit.astype(BF16), h_ref[...].astype(BF16), preferred_element_type=F32)
    cpt.wait()
    sub8 = lax.broadcasted_iota(I32, (8, 1), 0)

    def per_expert(e, sent8):
        cnt, lo, g0 = tsm_ref[0, e], pl.multiple_of(tsm_ref[1, e], 8), psm_ref[2, e] + tsm_ref[2, e]
        r = g0 & 7
        g_tile = pl.multiple_of(g0 - r, 8)
        head = pl.ds(lo, 8)
        xl_ref[head, :] = xl_ref[head, :] + part_ref[e]
        full8, rem = (r + cnt) >> 3, (r + cnt) & 7
        _for_each_chunk(full8, lambda first, rows, prio: out_copy(
            xl_ref, pl.multiple_of(lo + first, 8), pl.multiple_of(g_tile + first, 8), rows).start(priority=prio))
        tail = xl_ref[pl.ds(pl.multiple_of(lo + full8 * 8, 8), 8), :]
        part_ref[e] = jnp.where(sub8 < rem, tail, 0.0)
        return sent8 + full8

    sent8 = lax.fori_loop(0, N_EXPERTS, per_expert, 0)

    sent_ref[buf] = sent8

    @pl.when(i > 0)
    def _previous_step_copies():
        _wait_rows(sent_ref[1 - buf], lambda rows: out_copy(xl_ref, 0, 0, rows, sem_rows2.at[1 - buf]))

    @pl.when(i == last)
    def _last_step():
        def flush(e, n8):
            end = psm_ref[0, e]
            due = (end & 7) != 0

            @pl.when(due)
            def _():
                pltpu.make_async_copy(part_ref.at[e], xs_ref.at[pl.ds(pl.multiple_of(end - (end & 7), 8), 8)],
                                      sem_rows).start()

            return n8 + jnp.where(due, 1, 0)

        _wait_rows(lax.fori_loop(0, N_EXPERTS, flush, sent8), lambda rows: out_copy(xl_ref, 0, 0, rows))


def _dispatch(h2, slot, tab, pad, n_rows):
    nt = h2.shape[0]
    tile = lambda i: (i, 0)
    return pl.pallas_call(
        _dispatch_kernel,
        out_shape=jax.ShapeDtypeStruct((n_rows, D_MODEL), F32),
        grid=(nt // TM,),
        in_specs=[pl.BlockSpec((TM, D_MODEL), tile),
                  pl.BlockSpec((DEST_ROWS, TM), tile),
                  pl.BlockSpec((DEST_ROWS, HEAD_W), tile),
                  pl.BlockSpec((DEST_ROWS, HEAD_W), lambda i: (0, 0))],
        out_specs=pl.BlockSpec(memory_space=pl.ANY),
        scratch_shapes=[pltpu.SMEM((DEST_ROWS, HEAD_W), I32), pltpu.SMEM((DEST_ROWS, HEAD_W), I32),
                        pltpu.SMEM((2,), I32),
                        pltpu.VMEM((2, L_ROWS, D_MODEL), F32), pltpu.VMEM((N_EXPERTS, 8, D_MODEL), F32),
                        pltpu.VMEM((8, D_MODEL), F32),
                        pltpu.SemaphoreType.DMA, pltpu.SemaphoreType.DMA, pltpu.SemaphoreType.DMA((2,))],
        compiler_params=_params(("arbitrary",)),
        name="moe_dispatch",
    )(h2, slot, tab, pad)


def _expert_kernel(be_ref, nv_ref, xs_ref, w1_ref, b1_ref, w2_ref, b2_ref, ys_ref, w1b_ref, w2b_ref):
    i = pl.program_id(0)
    valid = i < nv_ref[0]
    prev = be_ref[jnp.maximum(i - 1, 0)]
    changed = jnp.logical_or(i == 0, be_ref[i] != prev)

    @pl.when(jnp.logical_and(valid, changed))
    def _cast_weights():
        step = 128
        def cast(j, carry):
            rows = pl.ds(pl.multiple_of(j * step, step), step)
            w1b_ref[rows, :] = w1_ref[rows, :].astype(BF16)
            w2b_ref[rows, :] = w2_ref[rows, :].astype(BF16)
            return carry
        lax.fori_loop(0, w1_ref.shape[0] // step, cast, 0)

    @pl.when(jnp.logical_not(valid))
    def _unused_block():
        ys_ref[...] = jnp.zeros(ys_ref.shape, F32)

    @pl.when(valid)
    def _mlp():
        h = jnp.dot(xs_ref[...].astype(BF16), w1b_ref[...], preferred_element_type=F32) + b1_ref[...]
        half = h.shape[1] // 2
        a = jnp.minimum(h[:, 0:half], SWIGLU_LIMIT)
        lin = jnp.clip(h[:, half:2 * half], -SWIGLU_LIMIT, SWIGLU_LIMIT)
        act = a * _sigmoid(SWIGLU_ALPHA * a) * (lin + 1.0)
        ys_ref[...] = jnp.dot(act.astype(BF16), w2b_ref[...], preferred_element_type=F32) + b2_ref[...]


def _experts(xs, blk_exp, n_valid, w1, b1, w2, b2, layer):
    n_rows = xs.shape[0]
    n_blk = n_rows // MOE_BM
    depth, d_e2 = w1.shape[0], w1.shape[3]

    def blk(i, be, nv):
        return jnp.minimum(i, nv[0] - 1)

    def expert(i, be, nv):
        return (layer, be[blk(i, be, nv)], 0, 0)

    grid_spec = pltpu.PrefetchScalarGridSpec(
        num_scalar_prefetch=2,
        grid=(n_blk,),
        in_specs=[
            pl.BlockSpec((MOE_BM, D_MODEL), lambda i, be, nv: (blk(i, be, nv), 0)),
            pl.BlockSpec((None, None, D_MODEL, d_e2), expert),
            pl.BlockSpec((None, None, 1, d_e2), expert),
            pl.BlockSpec((None, None, d_e2 // 2, D_MODEL), expert),
            pl.BlockSpec((None, None, 1, D_MODEL), expert),
        ],
        out_specs=pl.BlockSpec((MOE_BM, D_MODEL), lambda i, be, nv: (i, 0)),
        scratch_shapes=[pltpu.VMEM((D_MODEL, d_e2), BF16), pltpu.VMEM((d_e2 // 2, D_MODEL), BF16)],
    )
    return pl.pallas_call(
        _expert_kernel,
        out_shape=jax.ShapeDtypeStruct((n_rows, D_MODEL), F32),
        grid_spec=grid_spec,
        compiler_params=_params(("arbitrary",)),
        name="moe_experts",
    )(blk_exp, n_valid, xs, w1, b1.reshape(depth, N_EXPERTS, 1, d_e2), w2, b2.reshape(depth, N_EXPERTS, 1, D_MODEL))


def _combine_kernel(x_ref, gt_ref, tab_ref, tab_next_ref, pad_ref, m_ref, fg_ref, ys_ref, *out_and_scratch, npt):
    *o_ref, tsm_ref, psm_ref, got_ref, yl2_ref, sem_tab, sem_rows2 = out_and_scratch
    o_ref = o_ref[0] if npt is None else o_ref
    i = pl.program_id(0)
    tn = x_ref.shape[0]
    buf = i % 2

    def in_copy(b, src0, dst0, rows):
        return pltpu.make_async_copy(ys_ref.at[pl.ds(src0, rows)], yl2_ref.at[b, pl.ds(dst0, rows)],
                                     sem_rows2.at[b])

    def fetch(table_ref, b):
        cpt = pltpu.make_async_copy(table_ref, tsm_ref, sem_tab)
        cpt.start()
        cpt.wait()

        def per_expert(e, got8):
            cnt, lo, g0 = tsm_ref[0, e], pl.multiple_of(tsm_ref[1, e], 8), psm_ref[2, e] + tsm_ref[2, e]
            r = g0 & 7
            g_tile = pl.multiple_of(g0 - r, 8)
            cover8 = jnp.where(cnt > 0, (r + cnt + 7) >> 3, 0)
            _for_each_chunk(cover8, lambda first, rows, prio: in_copy(
                b, pl.multiple_of(g_tile + first, 8), pl.multiple_of(lo + first, 8), rows).start(priority=prio))
            return got8 + cover8

        got_ref[b] = lax.fori_loop(0, N_EXPERTS, per_expert, 0)

    @pl.when(i == 0)
    def _first_step():
        yl2_ref[...] = jnp.zeros(yl2_ref.shape, F32)
        cpp = pltpu.make_async_copy(pad_ref, psm_ref, sem_tab)
        cpp.start()
        cpp.wait()
        fetch(tab_ref, 0)

    @pl.when(i + 1 < pl.num_programs(0))
    def _next_tile():
        fetch(tab_next_ref, 1 - buf)

    gt = gt_ref[...]
    slot = lax.broadcasted_iota(I32, (tn, L_ROWS), 1).astype(F32)
    pick = jnp.where(slot == gt[:, TOP_K:TOP_K + 1], gt[:, 0:1], 0.0)
    for k in range(1, TOP_K):
        pick = pick + jnp.where(slot == gt[:, TOP_K + k:TOP_K + k + 1], gt[:, k:k + 1], 0.0)
    _wait_rows(got_ref[buf], lambda rows: in_copy(buf, 0, 0, rows))
    acc = jnp.dot(pick.astype(BF16), yl2_ref[buf].astype(BF16), preferred_element_type=F32)
    xn = x_ref[...] + m_ref[5] * acc
    if npt is None:
        o_ref[...] = xn
    else:
        xn = _rms(xn, fg_ref[...])
        op_ref, os_ref = o_ref

        @pl.when(i < npt)
        def _():
            op_ref[...] = xn

        @pl.when(i >= npt)
        def _():
            os_ref[...] = xn


def _combine(x, gate_t, tab, pad, m, final_g, ys, mod_spec, n_prompt, final):
    nt = x.shape[0]
    tile = lambda i: (i, 0)
    npt = n_prompt // TM
    if final:
        out_shape = (jax.ShapeDtypeStruct((n_prompt, D_MODEL), F32), jax.ShapeDtypeStruct((nt - n_prompt, D_MODEL), F32))
        out_specs = (pl.BlockSpec((TM, D_MODEL), lambda i: (jnp.minimum(i, npt - 1), 0)),
                     pl.BlockSpec((TM, D_MODEL), lambda i: (jnp.maximum(i - npt, 0), 0)))
    else:
        out_shape, out_specs = jax.ShapeDtypeStruct((nt, D_MODEL), F32), pl.BlockSpec((TM, D_MODEL), tile)
    return pl.pallas_call(
        functools.partial(_combine_kernel, npt=npt if final else None),
        out_shape=out_shape,
        grid=(nt // TM,),
        in_specs=[pl.BlockSpec((TM, D_MODEL), tile),
                  pl.BlockSpec((TM, HEAD_W), tile),
                  pl.BlockSpec((DEST_ROWS, HEAD_W), tile),
                  pl.BlockSpec((DEST_ROWS, HEAD_W), lambda i: (jnp.minimum(i + 1, nt // TM - 1), 0)),
                  pl.BlockSpec((DEST_ROWS, HEAD_W), lambda i: (0, 0)),
                  mod_spec,
                  pl.BlockSpec((1, D_MODEL), lambda i: (0, 0)),
                  pl.BlockSpec(memory_space=pl.ANY)],
        out_specs=out_specs,
        scratch_shapes=[pltpu.SMEM((DEST_ROWS, HEAD_W), I32), pltpu.SMEM((DEST_ROWS, HEAD_W), I32),
                        pltpu.SMEM((2,), I32), pltpu.VMEM((2, L_ROWS, D_MODEL), F32),
                        pltpu.SemaphoreType.DMA, pltpu.SemaphoreType.DMA((2,))],
        compiler_params=_params(("arbitrary",)),
        name="moe_combine_final" if final else "moe_combine",
    )(x, gate_t, tab, tab, pad, m, final_g, ys)


def _moe(x, h2, logits_t, m, final_g, w1, b1, w2, b2, mod_spec, n_prompt, layer, final):
    nt = x.shape[0]
    n_blk = nt * TOP_K // MOE_BM + N_EXPERTS
    n_blk_pad = -(-n_blk // HEAD_W) * HEAD_W
    slot, gate_t, tab, blk_exp, n_valid, pad = _route(logits_t, n_blk_pad, n_blk * MOE_BM)
    xs = _dispatch(h2, slot, tab, pad, n_blk * MOE_BM)
    ys = _experts(xs, blk_exp, n_valid, w1, b1, w2, b2, layer)
    return _combine(x, gate_t, tab, pad, m, final_g, ys, mod_spec, n_prompt, final)


def _rope_tables(seq):
    t = jnp.arange(seq)
    inv = 1.0 / (ROPE_THETA ** (jnp.arange(ROPE_HALF, dtype=F32) / ROPE_HALF))
    ang_r = (t // GRID_W).astype(F32)[:, None] * inv
    ang_c = (t % GRID_W).astype(F32)[:, None] * inv
    cr, sr, cc, sc = jnp.cos(ang_r), jnp.sin(ang_r), jnp.cos(ang_c), jnp.sin(ang_c)
    cos = jnp.concatenate([cr, cr, cc, cc] * 2, axis=1)
    sin = jnp.concatenate([-sr, sr, -sc, sc] * 2, axis=1)
    return cos, sin


def kernel(x_prompt, x_sample, c, cache_k, cache_v, state_hgrn, c_ctx, mod_w, mod_b, norm_g, final_norm_g,
           w_in_even, w_out_even, da_lambda, da_subln_g, hg_norm_g, hg_lb, w_in_odd, sgu_w, sgu_b, w_out_odd,
           router_w, router_b, ex_w1, ex_b1, ex_w2, ex_b2):
    bp, lp, _ = x_prompt.shape
    bs, ls, _ = x_sample.shape
    n_p, n_s = bp * lp, bs * ls
    assert lp % TM == 0 and ls % TD == 0 and n_p % TD == 0 and n_p % ls == 0 and bs < MOD_ROWS
    depth = mod_w.shape[0]

    x = (x_prompt.reshape(n_p, D_MODEL), x_sample.reshape(n_s, D_MODEL))
    cond = jnp.zeros((MOD_ROWS, D_MODEL), F32).at[0:bs].set(c).at[bs].set(c_ctx)
    mod = _modulation(cond, mod_w, mod_b)
    mod_spec = _mod_spec(n_p, ls, bs, TM)
    mod_spec_d = _mod_spec(n_p, ls, bs, TD)
    rope_tabs = _rope_tables(ls)
    hg_lb_h = hg_lb.reshape(2, depth + 1, HG_HEADS, 1, HG_DK)
    final_g = final_norm_g.reshape(1, D_MODEL)

    new_k, new_v, new_s = [], [], []
    for l in range(depth):
        m = mod[l]
        g1 = norm_g[l, 0].reshape(1, D_MODEL)
        g2 = norm_g[l, 1].reshape(1, D_MODEL)
        if l % 2 == 0:
            e = l // 2
            lam_init = 0.8 - 0.6 * math.exp(-0.3 * l)
            proj = _inproj(x, n_p, m, g1, w_in_even[e].astype(BF16), mod_spec_d)
            subln = da_subln_g[e].reshape(1, HEAD_W)
            hgn = hg_norm_g[e].reshape(1, HEAD_W)
            mix = jnp.zeros((n_p + n_s, D_MODEL), F32)
            mix, k_new, v_new = _diff_attention(proj, mix, da_lambda[e], subln, row0=0, n_batch=bp, seq=lp,
                                                lam_init=lam_init)
            mix = _diff_attention(proj, mix, da_lambda[e], subln, row0=n_p, n_batch=bs, seq=ls, lam_init=lam_init,
                                  rope_tabs=rope_tabs, ctx_k=cache_k, ctx_v=cache_v, layer_e=e)
            mix, s_p = _hgrn(proj, mix, hg_lb_h, hgn, row0=0, n_batch=bp, seq=lp, layer=l)
            mix, _ = _hgrn(proj, mix, hg_lb_h, hgn, row0=n_p, n_batch=bs, seq=ls, layer=l, s0=state_hgrn[:, e])
            w_out = w_out_even[e].astype(BF16)
            new_k.append(k_new)
            new_v.append(v_new)
            new_s.append(s_p)
        else:
            o = l // 2
            mix = _odd_mixer(x, m, g1, w_in_odd[o].astype(BF16), sgu_w[o].astype(BF16), sgu_b[o].T, mod_spec_d)
            w_out = w_out_odd[o].astype(BF16)
        x, h2, logits_t = _post(x, n_p, mix, m, g2, w_out, router_w[l].T, router_b[l].reshape(N_EXPERTS, 1),
                                mod_spec_d)
        x = _moe(x, h2, logits_t, m, final_g, ex_w1, ex_b1, ex_w2, ex_b2, mod_spec, n_p, layer=l,
                 final=(l == depth - 1))

    y_prompt, y_sample = x
    return (y_prompt.reshape(bp, lp, D_MODEL), y_sample.reshape(bs, ls, D_MODEL),
            jnp.stack(new_k, axis=1), jnp.stack(new_v, axis=1), jnp.stack(new_s, axis=1))
```

```python
import functools
import math

import jax
import jax.numpy as jnp
from jax import lax
from jax.experimental import pallas as pl
from jax.experimental.pallas import tpu as pltpu

F32 = jnp.float32
BF16 = jnp.bfloat16
I32 = jnp.int32

D_MODEL = 1024
DEPTH = 2
GRID_W = 64
DA_HEADS = 4
DA_HD = 64
HG_HEADS = 4
HG_DK = 128
ROPE_THETA = 10000.0
ROPE_HALF = DA_HD // 4
CM_CHUNK = 128
CM_GROUPS = 8
N_EXPERTS = 32
TOP_K = 4
SWIGLU_ALPHA = 1.702
SWIGLU_LIMIT = 7.0
EPS = 1e-6
EVEN_IN = 4096
HEAD_W = 128

TM = 256
TD = 512
ATTN_SUB = 2
HG_C = 64
HG_SB = 16
HG_HPS = 2
HG_UNROLL = 4
HG_FIN = 256
HG_DOUBLE_BUFFER_BYTES = 2 * 1024 * 1024
L_ROWS = -(-(TOP_K * TM + N_EXPERTS * 14) // 128) * 128
GROUP_BITS = (32, 16, 8, 4, 2, 1)
WAIT_BITS = (128, 64, 32, 16, 8, 4, 2, 1)
MOE_BM = 512
DEST_ROWS = 8
MOD_ROWS = 16
V7X_VMEM_LIMIT = 56 * 1024 * 1024

NT_DIMS = (((1,), (1,)), ((), ()))
TN_DIMS = (((0,), (0,)), ((), ()))


def _params(sem):
    return pltpu.CompilerParams(dimension_semantics=sem, vmem_limit_bytes=V7X_VMEM_LIMIT)


def _sigmoid(x):
    return 1.0 / (1.0 + jnp.exp(-x))


def _silu(x):
    return x * _sigmoid(x)


def _rms(x, g):
    ms = jnp.mean(x * x, axis=-1, keepdims=True)
    return x * lax.rsqrt(ms + EPS) * g


def _norm_mod(x, g, shift, scale):
    return _rms(x, g) * (1.0 + scale) + shift


def _mod_kernel(c_ref, w_ref, b_ref, o_ref):
    s = _silu(c_ref[...])
    o_ref[...] = jnp.dot(s.astype(BF16), w_ref[...].astype(BF16), preferred_element_type=F32) + b_ref[...]


def _modulation(cond, mod_w, mod_b):
    depth = mod_w.shape[0]
    m = pl.pallas_call(
        _mod_kernel,
        out_shape=jax.ShapeDtypeStruct((depth, 6, MOD_ROWS, D_MODEL), F32),
        grid=(depth, 6),
        in_specs=[
            pl.BlockSpec((MOD_ROWS, D_MODEL), lambda l, j: (0, 0)),
            pl.BlockSpec((None, D_MODEL, D_MODEL), lambda l, j: (l, 0, j)),
            pl.BlockSpec((None, 1, D_MODEL), lambda l, j: (l, 0, j)),
        ],
        out_specs=pl.BlockSpec((None, None, MOD_ROWS, D_MODEL), lambda l, j: (l, j, 0, 0)),
        compiler_params=_params(("arbitrary", "arbitrary")),
        name="modulation",
    )(cond, mod_w, mod_b.reshape(depth, 1, 6 * D_MODEL))
    return m.transpose(0, 2, 1, 3)[:, :, :, None, :]


def _mod_spec(n_prompt, l_sample, ctx_row, tile):
    def index(i):
        t = i * tile
        return (jnp.where(t < n_prompt, ctx_row, (t - n_prompt) // l_sample), 0, 0, 0)
    return pl.BlockSpec((None, 6, 1, D_MODEL), index)


def _x_parts(x, n_prompt):
    npt = n_prompt // TD
    if isinstance(x, tuple):
        a, b = x
        spec_b = pl.BlockSpec((TD, D_MODEL), lambda i: (jnp.maximum(i - npt, 0), 0))
    else:
        a = b = x
        spec_b = pl.BlockSpec((TD, D_MODEL), lambda i: (jnp.maximum(i, npt), 0))
    spec_a = pl.BlockSpec((TD, D_MODEL), lambda i: (jnp.minimum(i, npt - 1), 0))
    n_rows = a.shape[0] + b.shape[0] if isinstance(x, tuple) else x.shape[0]
    return [spec_a, spec_b], [a, b], npt, n_rows


def _pick_x(xa_ref, xb_ref, npt):
    return jnp.where(pl.program_id(0) < npt, xa_ref[...], xb_ref[...])


def _inproj_kernel(xa_ref, xb_ref, m_ref, g_ref, w_ref, o_ref, mix_ref, *, npt):
    h = _norm_mod(_pick_x(xa_ref, xb_ref, npt), g_ref[...], m_ref[0], m_ref[1])
    o_ref[...] = jnp.dot(h.astype(BF16), w_ref[...], preferred_element_type=F32)
    mix_ref[...] = jnp.zeros(mix_ref.shape, F32)


def _inproj(x, n_prompt, m, g, w, mod_spec):
    x_specs, x_args, npt, nt = _x_parts(x, n_prompt)
    n_out = w.shape[1]
    return pl.pallas_call(
        functools.partial(_inproj_kernel, npt=npt),
        out_shape=(jax.ShapeDtypeStruct((nt, n_out), F32), jax.ShapeDtypeStruct((nt, D_MODEL), F32)),
        grid=(nt // TD,),
        in_specs=x_specs + [
            mod_spec,
            pl.BlockSpec((1, D_MODEL), lambda i: (0, 0)),
            pl.BlockSpec((D_MODEL, n_out), lambda i: (0, 0)),
        ],
        out_specs=(pl.BlockSpec((TD, n_out), lambda i: (i, 0)), pl.BlockSpec((TD, D_MODEL), lambda i: (i, 0))),
        compiler_params=_params(("arbitrary",)),
        name="even_inproj",
    )(*x_args, m, g, w)


def _rope(x, cos, sin_signed):
    lane = lax.broadcasted_iota(I32, x.shape, 1)
    first = ((lane // ROPE_HALF) % 2) == 0
    partner = jnp.where(first, pltpu.roll(x, HEAD_W - ROPE_HALF, 1), pltpu.roll(x, ROPE_HALF, 1))
    return x * cos + partner * sin_signed


def _attn_kernel(*refs, rope, ctx, bq, chunks, lam_init):
    it = iter(refs)
    lam_ref, g_ref, q_ref, k_ref, v_ref = (next(it) for _ in range(5))
    if rope:
        cq_ref, sq_ref, ck_ref, sk_ref = (next(it) for _ in range(4))
    if ctx:
        kc_ref, vc_ref = next(it), next(it)
    next(it)
    o_ref = next(it)
    if not ctx:
        ko_ref, vo_ref = next(it), next(it)
    kt_ref, vx_ref = next(it), next(it)
    l_new = k_ref.shape[0]
    l_ctx = kc_ref.shape[0] if ctx else 0

    @pl.when(pl.program_id(2) == 0)
    def _prepare_keys():
        if ctx:
            kt_ref[:, 0:l_ctx] = kc_ref[...].T.astype(BF16)
            vx_ref[0:l_ctx, 0:HEAD_W] = vc_ref[...].astype(BF16)
        else:
            ko_ref[...] = k_ref[...]
            vo_ref[...] = v_ref[...]
        step = min(512, l_new)
        for c0 in range(0, l_new, step):
            k = k_ref[c0:c0 + step, :]
            if rope:
                k = _rope(k, ck_ref[c0:c0 + step, :], sk_ref[c0:c0 + step, :])
            kt_ref[:, l_ctx + c0:l_ctx + c0 + step] = k.T.astype(BF16)
            vx_ref[l_ctx + c0:l_ctx + c0 + step, 0:HEAD_W] = v_ref[c0:c0 + step, :].astype(BF16)
        vx_ref[:, HEAD_W:2 * HEAD_W] = jnp.ones((l_ctx + l_new, HEAD_W), BF16)

    n_sub = q_ref.shape[0] // bq
    lane = lax.broadcasted_iota(I32, (bq, HEAD_W), 1)
    lo = lane < DA_HD
    zero = jnp.zeros((bq, HEAD_W), F32)
    qss = []
    for u in range(n_sub):
        rows = slice(u * bq, (u + 1) * bq)
        q = q_ref[rows, :]
        if rope:
            q = _rope(q, cq_ref[rows, :], sq_ref[rows, :])
        q = q * (DA_HD ** -0.5)
        qss.append(jnp.concatenate([jnp.where(lo, q, zero), jnp.where(lo, zero, q)], axis=0).astype(BF16))

    def scores(qs, chunk):
        return jnp.dot(qs, kt_ref[:, chunk[0]:chunk[0] + chunk[1]], preferred_element_type=F32)

    m = [jnp.full((2 * bq, 1), -jnp.inf, F32)] * n_sub
    l = [jnp.zeros((2 * bq, 1), F32)] * n_sub
    acc = [jnp.zeros((2 * bq, HEAD_W), F32)] * n_sub
    s_next = [scores(qs, chunks[0]) for qs in qss]
    for ci, (c0, cs) in enumerate(chunks):
        s_cur = s_next
        if ci + 1 < len(chunks):
            s_next = [scores(qs, chunks[ci + 1]) for qs in qss]
        for u in range(n_sub):
            s = s_cur[u]
            mn = jnp.maximum(m[u], jnp.max(s, axis=-1, keepdims=True))
            alpha = jnp.exp(m[u] - mn)
            p = jnp.exp(s - mn).astype(BF16)
            pv = jnp.dot(p, vx_ref[c0:c0 + cs, :], preferred_element_type=F32)
            acc[u] = alpha * acc[u] + pv[:, 0:HEAD_W]
            l[u] = alpha * l[u] + pv[:, HEAD_W:HEAD_W + 1]
            m[u] = mn
    lp = lam_ref[...]
    lam = (jnp.exp(jnp.sum(lp[0:1] * lp[1:2], axis=-1, keepdims=True))
           - jnp.exp(jnp.sum(lp[2:3] * lp[3:4], axis=-1, keepdims=True)) + lam_init)
    for u in range(n_sub):
        o2 = acc[u] / l[u]
        o = o2[0:bq] - lam * o2[bq:2 * bq]
        o_ref[u * bq:(u + 1) * bq, :] = _rms(o, g_ref[...]) * (1.0 - lam_init)


def _attn_chunks(l_ctx, l_new):
    total = l_ctx + l_new
    if total <= 1280:
        return ((0, total),)
    chunks, c0 = [], 0
    first = l_ctx + 1024
    chunks.append((0, first))
    c0 = first
    while c0 < total:
        cs = min(1024, total - c0)
        chunks.append((c0, cs))
        c0 += cs
    return tuple(chunks)


def _diff_attention(proj, mix, lam_p, subln_g, *, row0, n_batch, seq, lam_init, rope_tabs=None,
                    ctx_k=None, ctx_v=None, layer_e=0):
    rope, ctx = rope_tabs is not None, ctx_k is not None
    n_rows = proj.shape[0]
    bq = min(ATTN_SUB * 256, seq)
    nq = seq // bq
    l_ctx = ctx_k.shape[3] if ctx else 0
    qb0, kb0 = row0 // bq, row0 // seq
    in_specs = [
        pl.BlockSpec((4, DA_HD), lambda b, h, i: (0, 0)),
        pl.BlockSpec((1, HEAD_W), lambda b, h, i: (0, 0)),
        pl.BlockSpec((bq, HEAD_W), lambda b, h, i: (qb0 + b * nq + i, h)),
        pl.BlockSpec((seq, HEAD_W), lambda b, h, i: (kb0 + b, DA_HEADS + h)),
        pl.BlockSpec((seq, HEAD_W), lambda b, h, i: (kb0 + b, 2 * DA_HEADS + h)),
    ]
    args = [lam_p, subln_g, proj, proj, proj]
    if rope:
        cos, sin = rope_tabs
        in_specs += [
            pl.BlockSpec((bq, HEAD_W), lambda b, h, i: (i, 0)),
            pl.BlockSpec((bq, HEAD_W), lambda b, h, i: (i, 0)),
            pl.BlockSpec((seq, HEAD_W), lambda b, h, i: (0, 0)),
            pl.BlockSpec((seq, HEAD_W), lambda b, h, i: (0, 0)),
        ]
        args += [cos, sin, cos, sin]
    if ctx:
        in_specs += [
            pl.BlockSpec((None, None, None, l_ctx, HEAD_W), lambda b, h, i: (b, layer_e, h, 0, 0)),
            pl.BlockSpec((None, None, None, l_ctx, HEAD_W), lambda b, h, i: (b, layer_e, h, 0, 0)),
        ]
        args += [ctx_k, ctx_v]
    mix_sds = jax.ShapeDtypeStruct((n_rows, D_MODEL), F32)
    mix_spec = pl.BlockSpec((bq, HEAD_W), lambda b, h, i: (qb0 + b * nq + i, h))
    in_specs.append(pl.BlockSpec(memory_space=pl.ANY))
    args.append(mix)
    aliases = {len(args) - 1: 0}
    if ctx:
        out_shape, out_specs = mix_sds, mix_spec
    else:
        cache_sds = jax.ShapeDtypeStruct((n_batch, DA_HEADS, seq, HEAD_W), F32)
        cache_spec = pl.BlockSpec((None, None, seq, HEAD_W), lambda b, h, i: (b, h, 0, 0))
        out_shape, out_specs = (mix_sds, cache_sds, cache_sds), (mix_spec, cache_spec, cache_spec)
    kernel = functools.partial(_attn_kernel, rope=rope, ctx=ctx, bq=bq // ATTN_SUB,
                               chunks=_attn_chunks(l_ctx, seq), lam_init=lam_init)
    return pl.pallas_call(
        kernel,
        out_shape=out_shape,
        grid=(n_batch, DA_HEADS, nq),
        in_specs=in_specs,
        out_specs=out_specs,
        scratch_shapes=[pltpu.VMEM((HEAD_W, l_ctx + seq), BF16), pltpu.VMEM((l_ctx + seq, 2 * HEAD_W), BF16)],
        input_output_aliases=aliases,
        compiler_params=_params(("arbitrary", "arbitrary", "arbitrary")),
        name="diff_attention_ctx" if ctx else "diff_attention",
    )(*args)


def _hgrn_chunks(streams):
    c, nsb = HG_C, HG_C // HG_SB
    row = lax.broadcasted_iota(I32, (c, c), 0)
    col = lax.broadcasted_iota(I32, (c, c), 1)
    tri = {r: jnp.where((col >= row) if r else (col <= row), 1.0, 0.0).astype(BF16) for r in (False, True)}
    ones = jnp.ones((HG_DK, HG_DK), BF16)
    t_idx = lax.broadcasted_iota(I32, (HG_SB, 1), 0)

    g3s = []
    for q, k, v, lf, rev, f in streams:
        hi = lf.astype(BF16)
        r1 = lf - hi.astype(F32)
        mid = r1.astype(BF16)
        lo = (r1 - mid.astype(F32)).astype(BF16)
        g3s.append(jnp.dot(tri[rev], jnp.concatenate([hi, mid, lo], axis=1), preferred_element_type=F32))
    gs = [g3[:, 0:HG_DK] + g3[:, HG_DK:2 * HG_DK] + g3[:, 2 * HG_DK:3 * HG_DK] for g3 in g3s]
    g_lasts = [g[0:1] if s[4] else g[c - 1:c] for g, s in zip(gs, streams)]

    qgs = [(s[0] * jnp.exp(g)).astype(BF16) for g, s in zip(gs, streams)]

    def outside(i, rev):
        r0 = i * HG_SB
        if rev:
            return (r0 + HG_SB, c, r0 + HG_SB) if r0 + HG_SB < c else None
        return (0, r0, r0 - 1) if r0 > 0 else None

    a_offs = {}
    for i in range(nsb):
        for n, (g, s) in enumerate(zip(gs, streams)):
            span = outside(i, s[4])
            if span is None:
                continue
            s_lo, s_hi, rb = span
            r0 = i * HG_SB
            ref_g = g[rb:rb + 1]
            qd = (s[0][r0:r0 + HG_SB] * jnp.exp(g[r0:r0 + HG_SB] - ref_g)).astype(BF16)
            kd = (s[1][s_lo:s_hi] * jnp.exp(ref_g - g[s_lo:s_hi])).astype(BF16)
            a_offs[i, n] = lax.dot_general(qd, kd, NT_DIMS, preferred_element_type=F32)
    o_offs = {}
    for (i, n), a in a_offs.items():
        s_lo, s_hi, _ = outside(i, streams[n][4])
        o_offs[i, n] = jnp.dot(a.astype(BF16), streams[n][2][s_lo:s_hi].astype(BF16), preferred_element_type=F32)

    def rows_of(j, rev):
        half = HG_SB // 2
        if rev:
            return (0, half) if j < half else (0, HG_SB)
        return (half, HG_SB) if j >= half else (0, HG_SB)

    a_reps = {}
    for i in range(nsb):
        r0 = i * HG_SB
        for n, (g, s) in enumerate(zip(gs, streams)):
            qi, ki, fi = s[0][r0:r0 + HG_SB], s[1][r0:r0 + HG_SB], s[5][r0:r0 + HG_SB]
            ws, qe = {}, None
            order = range(HG_SB) if s[4] else range(HG_SB - 1, -1, -1)
            for j in order:
                t0, t1 = rows_of(j, s[4])
                here = t_idx[t0:t1] == j
                if qe is None:
                    qe = jnp.where(here, qi[t0:t1], 0.0)
                else:
                    if qe.shape[0] != t1 - t0:
                        pad8 = jnp.zeros_like(qe)
                        qe = jnp.concatenate([qe, pad8] if s[4] else [pad8, qe], axis=0)
                    step = fi[j - 1:j] if s[4] else fi[j + 1:j + 2]
                    qe = jnp.where(here, qi[t0:t1], qe * step)
                ws[j] = qe * ki[j:j + 1]
            a_reps[i, n] = jnp.dot(jnp.concatenate([ws[j] for j in range(HG_SB)], axis=0).astype(BF16), ones,
                                   preferred_element_type=F32)

    results = []
    for n, (g, s) in enumerate(zip(gs, streams)):
        q, k, v, lf, rev, f = s
        outs = []
        for i in range(nsb):
            r0 = i * HG_SB
            vi = v[r0:r0 + HG_SB]
            zero8 = jnp.zeros((HG_SB // 2, v.shape[1]), F32)
            lo_acc, hi_acc, at = zero8, zero8, 0
            for j in range(HG_SB):
                t0, t1 = rows_of(j, rev)
                term = a_reps[i, n][at:at + (t1 - t0)] * vi[j:j + 1]
                at += t1 - t0
                if t1 - t0 == HG_SB:
                    lo_acc, hi_acc = lo_acc + term[0:HG_SB // 2], hi_acc + term[HG_SB // 2:HG_SB]
                elif t0 == 0:
                    lo_acc = lo_acc + term
                else:
                    hi_acc = hi_acc + term
            oi = jnp.concatenate([lo_acc, hi_acc], axis=0)
            if (i, n) in o_offs:
                oi = oi + o_offs[i, n]
            outs.append(oi)
        kd = (k * jnp.exp(g_lasts[n] - g)).astype(BF16)
        gain = lax.dot_general(v.astype(BF16), kd, TN_DIMS, preferred_element_type=F32)
        results.append((jnp.concatenate(outs, axis=0), qgs[n], jnp.exp(g_lasts[n]), gain))
    return results


def _hgrn_advance(part, st):
    o_local, qg, decay, gain = part
    o = o_local + lax.dot_general(qg, st.astype(BF16), NT_DIMS, preferred_element_type=F32)
    return o, st * decay + gain


def _hgrn_kernel(*refs, layer, has_s0):
    it = iter(refs)
    hq_ref, zf_ref, zb_ref, hi_ref, hg_ref, lbp_ref, g_ref = (next(it) for _ in range(7))
    s0_ref = next(it) if has_s0 else None
    next(it)
    o_ref, sout_ref, st_ref, ob_ref = next(it), next(it), next(it), next(it)
    seq = hq_ref.shape[0]
    n = seq // HG_C
    z_refs = (zf_ref, zb_ref)

    lbs = []
    for hh in range(HG_HPS):
        for d in range(2):
            p = lbp_ref[d, :, hh]
            e = jnp.exp(p - jnp.max(p, axis=0))
            lbs.append(jnp.sum(e[0:layer + 1], axis=0) / jnp.sum(e, axis=0))
            if has_s0:
                st_ref[2 * hh + d] = s0_ref[d, hh].T
            else:
                st_ref[2 * hh + d] = jnp.zeros((HEAD_W, HG_DK), F32)

    def body(ci, carry):
        streams, dests = [], []
        for hh in range(HG_HPS):
            lanes = slice(hh * HEAD_W, (hh + 1) * HEAD_W)
            for d in range(2):
                for u in range(HG_UNROLL):
                    c = ci * HG_UNROLL + u
                    c = (n - 1 - c) if d == 1 else c
                    rows = pl.ds(pl.multiple_of(c * HG_C, HG_C), HG_C)
                    lb = lbs[2 * hh + d]
                    f = lb + (1.0 - lb) * _sigmoid(z_refs[d][rows, lanes])
                    streams.append((_silu(hq_ref[rows, lanes]), 1.0 - f, hi_ref[rows, lanes], jnp.log(f), d == 1, f))
                    dests.append((o_ref if d == 0 else ob_ref, rows, lanes))
        parts = _hgrn_chunks(streams)
        for slot in range(2 * HG_HPS):
            st = st_ref[slot]
            for u in range(HG_UNROLL):
                o, st = _hgrn_advance(parts[slot * HG_UNROLL + u], st)
                dst_ref, rows, lanes = dests[slot * HG_UNROLL + u]
                dst_ref[rows, lanes] = o
            st_ref[slot] = st
        return carry

    lax.fori_loop(0, n // HG_UNROLL, body, 0)
    for hh in range(HG_HPS):
        for d in range(2):
            sout_ref[d, hh] = st_ref[2 * hh + d].T

    def finish(ci, carry):
        rows = pl.ds(pl.multiple_of(ci * HG_FIN, HG_FIN), HG_FIN)
        for hh in range(HG_HPS):
            lanes = slice(hh * HEAD_W, (hh + 1) * HEAD_W)
            tot = o_ref[rows, lanes] + ob_ref[rows, lanes]
            o_ref[rows, lanes] = _rms(tot, g_ref[...]) * _silu(hg_ref[rows, lanes])
        return carry

    lax.fori_loop(0, seq // HG_FIN, finish, 0)


def _hgrn(proj, mix, hg_lb_l, hg_norm_g, *, row0, n_batch, seq, layer, s0=None):
    has_s0 = s0 is not None
    rb0 = row0 // seq
    hpg = HG_HEADS // HG_HPS
    width = HG_HPS * HEAD_W
    col0 = 3 * DA_HEADS * HEAD_W // width
    once = pl.Buffered(1) if seq * width * 4 > HG_DOUBLE_BUFFER_BYTES else None

    def col_spec(j, mode=None):
        return pl.BlockSpec((seq, width), lambda b, h: (rb0 + b, col0 + j * hpg + h), pipeline_mode=mode)

    in_specs = [col_spec(0), col_spec(1), col_spec(2), col_spec(3), col_spec(4, once),
                pl.BlockSpec((2, DEPTH + 1, HG_HPS, 1, HG_DK), lambda b, h: (0, 0, h, 0, 0)),
                pl.BlockSpec((1, HEAD_W), lambda b, h: (0, 0))]
    args = [proj, proj, proj, proj, proj, hg_lb_l, hg_norm_g]
    state_spec = pl.BlockSpec((None, 2, HG_HPS, HG_DK, HEAD_W), lambda b, h: (b, 0, h, 0, 0))
    if has_s0:
        in_specs.append(state_spec)
        args.append(s0)
    in_specs.append(pl.BlockSpec(memory_space=pl.ANY))
    args.append(mix)
    return pl.pallas_call(
        functools.partial(_hgrn_kernel, layer=layer, has_s0=has_s0),
        out_shape=(jax.ShapeDtypeStruct(mix.shape, mix.dtype),
                   jax.ShapeDtypeStruct((n_batch, 2, HG_HEADS, HG_DK, HEAD_W), F32)),
        grid=(n_batch, hpg),
        in_specs=in_specs,
        out_specs=(pl.BlockSpec((seq, width), lambda b, h: (rb0 + b, DA_HEADS * HEAD_W // width + h)),
                   state_spec),
        scratch_shapes=[pltpu.VMEM((2 * HG_HPS, HEAD_W, HG_DK), F32), pltpu.VMEM((seq, width), F32)],
        input_output_aliases={len(args) - 1: 0},
        compiler_params=_params(("arbitrary", "arbitrary")),
        name="hgrn2_state" if has_s0 else "hgrn2",
    )(*args)


def _odd_kernel(x_ref, m_ref, g_ref, win_ref, ws_ref, sb_ref, o_ref):
    h = _norm_mod(x_ref[...], g_ref[...], m_ref[0], m_ref[1])
    z = jnp.dot(h.astype(BF16), win_ref[...], preferred_element_type=F32)
    z = 0.5 * z * (1.0 + lax.erf(z * (2.0 ** -0.5)))
    width = z.shape[1] // 2
    u, v = z[:, 0:width], z[:, width:2 * width]
    mu = jnp.mean(v, axis=-1, keepdims=True)
    vc = v - mu
    var = jnp.mean(vc * vc, axis=-1, keepdims=True)
    vn = (vc * lax.rsqrt(var + EPS)).astype(BF16)
    gd = width // CM_GROUPS
    for r in range(x_ref.shape[0] // CM_CHUNK):
        rows = slice(r * CM_CHUNK, (r + 1) * CM_CHUNK)
        for gi in range(CM_GROUPS):
            cols = slice(gi * gd, (gi + 1) * gd)
            sv = jnp.dot(ws_ref[gi], vn[rows, cols], preferred_element_type=F32) + sb_ref[:, gi:gi + 1]
            o_ref[rows, cols] = u[rows, cols] * sv


def _odd_mixer(x, m, g, w_in, sgu_w, sgu_bt, mod_spec):
    nt = x.shape[0]
    return pl.pallas_call(
        _odd_kernel,
        out_shape=jax.ShapeDtypeStruct((nt, D_MODEL), F32),
        grid=(nt // TD,),
        in_specs=[
            pl.BlockSpec((TD, D_MODEL), lambda i: (i, 0)),
            mod_spec,
            pl.BlockSpec((1, D_MODEL), lambda i: (0, 0)),
            pl.BlockSpec(w_in.shape, lambda i: (0, 0)),
            pl.BlockSpec(sgu_w.shape, lambda i: (0, 0, 0)),
            pl.BlockSpec(sgu_bt.shape, lambda i: (0, 0)),
        ],
        out_specs=pl.BlockSpec((TD, D_MODEL), lambda i: (i, 0)),
        compiler_params=_params(("arbitrary",)),
        name="odd_mixer",
    )(x, m, g, w_in, sgu_w, sgu_bt)


def _post_kernel(xa_ref, xb_ref, mix_ref, m_ref, g_ref, w_ref, rwt_ref, rb_ref, xo_ref, h2_ref, lg_ref, *, npt):
    mo = jnp.dot(mix_ref[...].astype(BF16), w_ref[...], preferred_element_type=F32)
    xn = _pick_x(xa_ref, xb_ref, npt) + m_ref[2] * mo
    xo_ref[...] = xn
    h2 = _norm_mod(xn, g_ref[...], m_ref[3], m_ref[4])
    h2_ref[...] = h2
    hh = h2.astype(BF16)
    hl = (h2 - hh.astype(F32)).astype(BF16)
    rw = rwt_ref[...]
    rh = rw.astype(BF16)
    rl = (rw - rh.astype(F32)).astype(BF16)
    lg = (lax.dot_general(rh, hh, NT_DIMS, preferred_element_type=F32)
          + lax.dot_general(rh, hl, NT_DIMS, preferred_element_type=F32)
          + lax.dot_general(rl, hh, NT_DIMS, preferred_element_type=F32))
    lg_ref[...] = lg + rb_ref[...]


def _post(x, n_prompt, mix, m, g2, w_out, router_wt, router_b, mod_spec):
    x_specs, x_args, npt, nt = _x_parts(x, n_prompt)
    row = lambda i: (i, 0)
    return pl.pallas_call(
        functools.partial(_post_kernel, npt=npt),
        out_shape=(jax.ShapeDtypeStruct((nt, D_MODEL), F32),
                   jax.ShapeDtypeStruct((nt, D_MODEL), F32),
                   jax.ShapeDtypeStruct((N_EXPERTS, nt), F32)),
        grid=(nt // TD,),
        in_specs=x_specs + [
            pl.BlockSpec((TD, D_MODEL), row),
            mod_spec,
            pl.BlockSpec((1, D_MODEL), lambda i: (0, 0)),
            pl.BlockSpec((D_MODEL, D_MODEL), lambda i: (0, 0)),
            pl.BlockSpec((N_EXPERTS, D_MODEL), lambda i: (0, 0)),
            pl.BlockSpec((N_EXPERTS, 1), lambda i: (0, 0)),
        ],
        out_specs=(pl.BlockSpec((TD, D_MODEL), row), pl.BlockSpec((TD, D_MODEL), row),
                   pl.BlockSpec((N_EXPERTS, TD), lambda i: (0, i))),
        compiler_params=_params(("arbitrary",)),
        name="mixer_out_router",
    )(*x_args, mix, m, g2, w_out, router_wt, router_b)


def _experts_on_lanes(col):
    on_lane = (lax.broadcasted_iota(I32, (N_EXPERTS, HEAD_W), 0) == lax.broadcasted_iota(I32, (N_EXPERTS, HEAD_W), 1))
    return jnp.sum(jnp.where(on_lane, col, 0.0), axis=0, keepdims=True)


def _route_a_kernel(lg_ref, slot_ref, gt_ref, tab_ref, cnt_ref, carry_ref):
    i = pl.program_id(0)

    @pl.when(i == 0)
    def _():
        carry_ref[...] = jnp.zeros(carry_ref.shape, F32)

    l = lg_ref[...]
    tn = l.shape[1]
    eio = lax.broadcasted_iota(I32, l.shape, 0)
    vals, idxs = [], []
    for _ in range(TOP_K):
        mk = jnp.max(l, axis=0, keepdims=True)
        ik = jnp.min(jnp.where(l == mk, eio, N_EXPERTS), axis=0, keepdims=True)
        vals.append(mk)
        idxs.append(ik)
        l = jnp.where(eio == ik, -jnp.inf, l)
    es = [jnp.exp(v - vals[0]) for v in vals]
    den = es[0] + es[1] + es[2] + es[3]
    sel = jnp.zeros(l.shape, F32)
    for ik in idxs:
        sel = sel + jnp.where(eio == ik, 1.0, 0.0)
    upper = jnp.where(lax.broadcasted_iota(I32, (tn, tn), 0) < lax.broadcasted_iota(I32, (tn, tn), 1),
                      1.0, 0.0).astype(BF16)
    before = jnp.dot(sel.astype(BF16), upper, preferred_element_type=F32)

    tile_cnt = jnp.sum(sel, axis=1, keepdims=True)
    c_row, base_row = _experts_on_lanes(tile_cnt), _experts_on_lanes(carry_ref[...])
    sub = lax.broadcasted_iota(I32, (N_EXPERTS, HEAD_W), 0)
    lane_e = lax.broadcasted_iota(I32, (N_EXPERTS, HEAD_W), 1)
    r_row = base_row - 8.0 * jnp.floor(base_row / 8.0)
    seg_row = 8.0 * jnp.floor((r_row + c_row + 7.0) / 8.0)
    seg_col = jnp.sum(jnp.where(sub == lane_e, seg_row, 0.0), axis=1, keepdims=True)
    r_col = jnp.sum(jnp.where(sub == lane_e, r_row, 0.0), axis=1, keepdims=True)
    loff_row = jnp.sum(jnp.where(sub < lane_e, seg_col, 0.0), axis=0, keepdims=True)
    loff_col = jnp.sum(jnp.where(lane_e < sub, seg_row, 0.0), axis=1, keepdims=True)
    first_slot = loff_col + r_col
    slots = []
    for k in range(TOP_K):
        here = eio == idxs[k]
        slots.append(jnp.sum(jnp.where(here, first_slot + before, 0.0), axis=0, keepdims=True))
        slot_ref[k:k + 1, :] = slots[k].astype(I32)
    slot_ref[TOP_K:DEST_ROWS, :] = jnp.zeros((DEST_ROWS - TOP_K, tn), I32)
    gp = jnp.concatenate([e / den for e in es] + slots + [jnp.zeros((HEAD_W - 2 * TOP_K, tn), F32)], axis=0)
    gt_ref[...] = gp.T
    tab_ref[0:1, :] = c_row.astype(I32)
    tab_ref[1:2, :] = loff_row.astype(I32)
    tab_ref[2:3, :] = base_row.astype(I32)
    tab_ref[3:DEST_ROWS, :] = jnp.zeros((DEST_ROWS - 3, HEAD_W), I32)
    carry_ref[...] = carry_ref[...] + tile_cnt
    cnt_ref[...] = carry_ref[...]


def _route_plan_kernel(cnt_ref, be_ref, nv_ref, pad_ref, *, n_blk_pad, n_rows):
    cnt = cnt_ref[...]
    padded = jnp.floor((cnt + (MOE_BM - 1)) / MOE_BM) * MOE_BM
    r = lax.broadcasted_iota(I32, (N_EXPERTS, N_EXPERTS), 0)
    c = lax.broadcasted_iota(I32, (N_EXPERTS, N_EXPERTS), 1)
    padded_row = jnp.sum(jnp.where(r == c, padded, 0.0), axis=0, keepdims=True)
    pend = jnp.sum(jnp.where(c <= r, padded_row, 0.0), axis=1, keepdims=True)
    pstart = pend - padded
    blk_start = (lax.broadcasted_iota(I32, (N_EXPERTS, n_blk_pad), 1) * MOE_BM).astype(F32)
    be = jnp.sum(jnp.where(pend <= blk_start, 1.0, 0.0), axis=0, keepdims=True)
    be_ref[...] = jnp.minimum(be, N_EXPERTS - 1.0).astype(I32)
    total = jnp.sum(padded, axis=0, keepdims=True)
    nv_ref[...] = jnp.broadcast_to(total / MOE_BM, nv_ref.shape).astype(I32)
    on_lane = (lax.broadcasted_iota(I32, (N_EXPERTS, HEAD_W), 0) == lax.broadcasted_iota(I32, (N_EXPERTS, HEAD_W), 1))
    lane = lax.broadcasted_iota(I32, (1, HEAD_W), 1)
    pad_lo = jnp.sum(jnp.where(on_lane, pstart + cnt, 0.0), axis=0, keepdims=True)
    pad_hi = jnp.sum(jnp.where(on_lane, pend, 0.0), axis=0, keepdims=True)
    pad_ref[0:1, :] = jnp.where(lane == N_EXPERTS, total, pad_lo).astype(I32)
    pad_ref[1:2, :] = jnp.where(lane == N_EXPERTS, float(n_rows), pad_hi).astype(I32)
    pad_ref[2:3, :] = jnp.sum(jnp.where(on_lane, pstart, 0.0), axis=0, keepdims=True).astype(I32)
    pad_ref[3:DEST_ROWS, :] = jnp.zeros((DEST_ROWS - 3, HEAD_W), I32)


def _route(logits_t, n_blk_pad, n_rows):
    nt = logits_t.shape[1]
    tn = TM
    blk = lambda i: (0, i)
    tile = lambda i: (i, 0)
    whole = lambda i: (0, 0)
    n_tiles = nt // tn
    slot, gate_t, tab, cnt = pl.pallas_call(
        _route_a_kernel,
        out_shape=(jax.ShapeDtypeStruct((n_tiles * DEST_ROWS, tn), I32), jax.ShapeDtypeStruct((nt, HEAD_W), F32),
                   jax.ShapeDtypeStruct((n_tiles * DEST_ROWS, HEAD_W), I32),
                   jax.ShapeDtypeStruct((N_EXPERTS, 1), F32)),
        grid=(n_tiles,),
        in_specs=[pl.BlockSpec((N_EXPERTS, tn), blk)],
        out_specs=(pl.BlockSpec((DEST_ROWS, tn), tile), pl.BlockSpec((tn, HEAD_W), tile),
                   pl.BlockSpec((DEST_ROWS, HEAD_W), tile), pl.BlockSpec((N_EXPERTS, 1), whole)),
        scratch_shapes=[pltpu.VMEM((N_EXPERTS, 1), F32)],
        compiler_params=_params(("arbitrary",)),
        name="route_topk_slots",
    )(logits_t)
    blk_exp, n_valid, pad = pl.pallas_call(
        functools.partial(_route_plan_kernel, n_blk_pad=n_blk_pad, n_rows=n_rows),
        out_shape=(jax.ShapeDtypeStruct((1, n_blk_pad), I32), jax.ShapeDtypeStruct((1, HEAD_W), I32),
                   jax.ShapeDtypeStruct((DEST_ROWS, HEAD_W), I32)),
        grid=(1,),
        in_specs=[pl.BlockSpec((N_EXPERTS, 1), whole)],
        out_specs=(pl.BlockSpec((1, n_blk_pad), whole), pl.BlockSpec((1, HEAD_W), whole),
                   pl.BlockSpec((DEST_ROWS, HEAD_W), whole)),
        compiler_params=_params(("arbitrary",)),
        name="route_plan",
    )(cnt)
    return slot, gate_t, tab, blk_exp.reshape(n_blk_pad), n_valid[0, 0:1], pad


def _for_each_chunk(n8, fn):
    for bi, b in enumerate(GROUP_BITS):
        @pl.when((n8 & b) != 0)
        def _(bi=bi, b=b):
            first = pl.multiple_of((n8 - (n8 & (2 * b - 1))) * 8, 8)
            fn(first, 8 * b, bi % 2)


def _wait_rows(n8, make_copy):
    for b in WAIT_BITS:
        @pl.when((n8 & b) != 0)
        def _(b=b):
            make_copy(8 * b).wait()


def _dispatch_kernel(h_ref, slot_ref, tab_ref, pad_ref, xs_ref, tsm_ref, psm_ref, sent_ref, xl2_ref, part_ref,
                     zero_ref, sem_tab, sem_pad, sem_rows2):
    i = pl.program_id(0)
    last = pl.num_programs(0) - 1
    tn = h_ref.shape[0]
    buf = i % 2
    xl_ref, sem_rows = xl2_ref.at[buf], sem_rows2.at[buf]
    cpt = pltpu.make_async_copy(tab_ref, tsm_ref, sem_tab)
    cpt.start()

    def out_copy(src_ref, src0, dst0, rows, sem=sem_rows):
        return pltpu.make_async_copy(src_ref.at[pl.ds(src0, rows)], xs_ref.at[pl.ds(dst0, rows)], sem)

    @pl.when(i == 0)
    def _first_step():
        part_ref[...] = jnp.zeros(part_ref.shape, F32)
        zero_ref[...] = jnp.zeros(zero_ref.shape, F32)
        cpp = pltpu.make_async_copy(pad_ref, psm_ref, sem_pad)
        cpp.start()
        cpp.wait()

        def per_span(e, carry):
            lo8, hi8 = (psm_ref[0, e] + 7) >> 3, psm_ref[1, e] >> 3

            def zero_copy(u):
                return out_copy(zero_ref, 0, pl.multiple_of(u * 8, 8), 8)

            lax.fori_loop(lo8, hi8, lambda u, c: (zero_copy(u).start(), c)[1], 0)
            lax.fori_loop(lo8, hi8, lambda u, c: (zero_copy(u).wait(), c)[1], 0)
            return carry

        lax.fori_loop(0, N_EXPERTS + 1, per_span, 0)

    slot = lax.broadcasted_iota(I32, (L_ROWS, tn), 0)
    sl = slot_ref[...]
    hit = jnp.where(slot == sl[0:1], 1.0, jnp.where(slot == sl[1:2], 1.0, jnp.where(
        slot == sl[2:3], 1.0, jnp.where(slot == sl[3:4], 1.0, 0.0))))
    xl_ref[...] = jnp.dot(hit.astype(BF16), h_ref[...].astype(BF16), preferred_element_type=F32)
    cpt.wait()
    sub8 = lax.broadcasted_iota(I32, (8, 1), 0)

    def per_expert(e, sent8):
        cnt, lo, g0 = tsm_ref[0, e], pl.multiple_of(tsm_ref[1, e], 8), psm_ref[2, e] + tsm_ref[2, e]
        r = g0 & 7
        g_tile = pl.multiple_of(g0 - r, 8)
        head = pl.ds(lo, 8)
        xl_ref[head, :] = xl_ref[head, :] + part_ref[e]
        full8, rem = (r + cnt) >> 3, (r + cnt) & 7
        _for_each_chunk(full8, lambda first, rows, prio: out_copy(
            xl_ref, pl.multiple_of(lo + first, 8), pl.multiple_of(g_tile + first, 8), rows).start(priority=prio))
        tail = xl_ref[pl.ds(pl.multiple_of(lo + full8 * 8, 8), 8), :]
        part_ref[e] = jnp.where(sub8 < rem, tail, 0.0)
        return sent8 + full8

    sent8 = lax.fori_loop(0, N_EXPERTS, per_expert, 0)

    sent_ref[buf] = sent8

    @pl.when(i > 0)
    def _previous_step_copies():
        _wait_rows(sent_ref[1 - buf], lambda rows: out_copy(xl_ref, 0, 0, rows, sem_rows2.at[1 - buf]))

    @pl.when(i == last)
    def _last_step():
        def flush(e, n8):
            end = psm_ref[0, e]
            due = (end & 7) != 0

            @pl.when(due)
            def _():
                pltpu.make_async_copy(part_ref.at[e], xs_ref.at[pl.ds(pl.multiple_of(end - (end & 7), 8), 8)],
                                      sem_rows).start()

            return n8 + jnp.where(due, 1, 0)

        _wait_rows(lax.fori_loop(0, N_EXPERTS, flush, sent8), lambda rows: out_copy(xl_ref, 0, 0, rows))


def _dispatch(h2, slot, tab, pad, n_rows):
    nt = h2.shape[0]
    tile = lambda i: (i, 0)
    return pl.pallas_call(
        _dispatch_kernel,
        out_shape=jax.ShapeDtypeStruct((n_rows, D_MODEL), F32),
        grid=(nt // TM,),
        in_specs=[pl.BlockSpec((TM, D_MODEL), tile),
                  pl.BlockSpec((DEST_ROWS, TM), tile),
                  pl.BlockSpec((DEST_ROWS, HEAD_W), tile),
                  pl.BlockSpec((DEST_ROWS, HEAD_W), lambda i: (0, 0))],
        out_specs=pl.BlockSpec(memory_space=pl.ANY),
        scratch_shapes=[pltpu.SMEM((DEST_ROWS, HEAD_W), I32), pltpu.SMEM((DEST_ROWS, HEAD_W), I32),
                        pltpu.SMEM((2,), I32),
                        pltpu.VMEM((2, L_ROWS, D_MODEL), F32), pltpu.VMEM((N_EXPERTS, 8, D_MODEL), F32),
                        pltpu.VMEM((8, D_MODEL), F32),
                        pltpu.SemaphoreType.DMA, pltpu.SemaphoreType.DMA, pltpu.SemaphoreType.DMA((2,))],
        compiler_params=_params(("arbitrary",)),
        name="moe_dispatch",
    )(h2, slot, tab, pad)


def _expert_kernel(be_ref, nv_ref, xs_ref, w1_ref, b1_ref, w2_ref, b2_ref, ys_ref, w1b_ref, w2b_ref):
    i = pl.program_id(0)
    valid = i < nv_ref[0]
    prev = be_ref[jnp.maximum(i - 1, 0)]
    changed = jnp.logical_or(i == 0, be_ref[i] != prev)

    @pl.when(jnp.logical_and(valid, changed))
    def _cast_weights():
        step = 128
        def cast(j, carry):
            rows = pl.ds(pl.multiple_of(j * step, step), step)
            w1b_ref[rows, :] = w1_ref[rows, :].astype(BF16)
            w2b_ref[rows, :] = w2_ref[rows, :].astype(BF16)
            return carry
        lax.fori_loop(0, w1_ref.shape[0] // step, cast, 0)

    @pl.when(jnp.logical_not(valid))
    def _unused_block():
        ys_ref[...] = jnp.zeros(ys_ref.shape, F32)

    @pl.when(valid)
    def _mlp():
        h = jnp.dot(xs_ref[...].astype(BF16), w1b_ref[...], preferred_element_type=F32) + b1_ref[...]
        half = h.shape[1] // 2
        a = jnp.minimum(h[:, 0:half], SWIGLU_LIMIT)
        lin = jnp.clip(h[:, half:2 * half], -SWIGLU_LIMIT, SWIGLU_LIMIT)
        act = a * _sigmoid(SWIGLU_ALPHA * a) * (lin + 1.0)
        ys_ref[...] = jnp.dot(act.astype(BF16), w2b_ref[...], preferred_element_type=F32) + b2_ref[...]


def _experts(xs, blk_exp, n_valid, w1, b1, w2, b2, layer):
    n_rows = xs.shape[0]
    n_blk = n_rows // MOE_BM
    depth, d_e2 = w1.shape[0], w1.shape[3]

    def blk(i, be, nv):
        return jnp.minimum(i, nv[0] - 1)

    def expert(i, be, nv):
        return (layer, be[blk(i, be, nv)], 0, 0)

    grid_spec = pltpu.PrefetchScalarGridSpec(
        num_scalar_prefetch=2,
        grid=(n_blk,),
        in_specs=[
            pl.BlockSpec((MOE_BM, D_MODEL), lambda i, be, nv: (blk(i, be, nv), 0)),
            pl.BlockSpec((None, None, D_MODEL, d_e2), expert),
            pl.BlockSpec((None, None, 1, d_e2), expert),
            pl.BlockSpec((None, None, d_e2 // 2, D_MODEL), expert),
            pl.BlockSpec((None, None, 1, D_MODEL), expert),
        ],
        out_specs=pl.BlockSpec((MOE_BM, D_MODEL), lambda i, be, nv: (i, 0)),
        scratch_shapes=[pltpu.VMEM((D_MODEL, d_e2), BF16), pltpu.VMEM((d_e2 // 2, D_MODEL), BF16)],
    )
    return pl.pallas_call(
        _expert_kernel,
        out_shape=jax.ShapeDtypeStruct((n_rows, D_MODEL), F32),
        grid_spec=grid_spec,
        compiler_params=_params(("arbitrary",)),
        name="moe_experts",
    )(blk_exp, n_valid, xs, w1, b1.reshape(depth, N_EXPERTS, 1, d_e2), w2, b2.reshape(depth, N_EXPERTS, 1, D_MODEL))


def _combine_kernel(x_ref, gt_ref, tab_ref, tab_next_ref, pad_ref, m_ref, fg_ref, ys_ref, *out_and_scratch, npt):
    *o_ref, tsm_ref, psm_ref, got_ref, yl2_ref, sem_tab, sem_rows2 = out_and_scratch
    o_ref = o_ref[0] if npt is None else o_ref
    i = pl.program_id(0)
    tn = x_ref.shape[0]
    buf = i % 2

    def in_copy(b, src0, dst0, rows):
        return pltpu.make_async_copy(ys_ref.at[pl.ds(src0, rows)], yl2_ref.at[b, pl.ds(dst0, rows)],
                                     sem_rows2.at[b])

    def fetch(table_ref, b):
        cpt = pltpu.make_async_copy(table_ref, tsm_ref, sem_tab)
        cpt.start()
        cpt.wait()

        def per_expert(e, got8):
            cnt, lo, g0 = tsm_ref[0, e], pl.multiple_of(tsm_ref[1, e], 8), psm_ref[2, e] + tsm_ref[2, e]
            r = g0 & 7
            g_tile = pl.multiple_of(g0 - r, 8)
            cover8 = jnp.where(cnt > 0, (r + cnt + 7) >> 3, 0)
            _for_each_chunk(cover8, lambda first, rows, prio: in_copy(
                b, pl.multiple_of(g_tile + first, 8), pl.multiple_of(lo + first, 8), rows).start(priority=prio))
            return got8 + cover8

        got_ref[b] = lax.fori_loop(0, N_EXPERTS, per_expert, 0)

    @pl.when(i == 0)
    def _first_step():
        yl2_ref[...] = jnp.zeros(yl2_ref.shape, F32)
        cpp = pltpu.make_async_copy(pad_ref, psm_ref, sem_tab)
        cpp.start()
        cpp.wait()
        fetch(tab_ref, 0)

    @pl.when(i + 1 < pl.num_programs(0))
    def _next_tile():
        fetch(tab_next_ref, 1 - buf)

    gt = gt_ref[...]
    slot = lax.broadcasted_iota(I32, (tn, L_ROWS), 1).astype(F32)
    pick = jnp.where(slot == gt[:, TOP_K:TOP_K + 1], gt[:, 0:1], 0.0)
    for k in range(1, TOP_K):
        pick = pick + jnp.where(slot == gt[:, TOP_K + k:TOP_K + k + 1], gt[:, k:k + 1], 0.0)
    _wait_rows(got_ref[buf], lambda rows: in_copy(buf, 0, 0, rows))
    acc = jnp.dot(pick.astype(BF16), yl2_ref[buf].astype(BF16), preferred_element_type=F32)
    xn = x_ref[...] + m_ref[5] * acc
    if npt is None:
        o_ref[...] = xn
    else:
        xn = _rms(xn, fg_ref[...])
        op_ref, os_ref = o_ref

        @pl.when(i < npt)
        def _():
            op_ref[...] = xn

        @pl.when(i >= npt)
        def _():
            os_ref[...] = xn


def _combine(x, gate_t, tab, pad, m, final_g, ys, mod_spec, n_prompt, final):
    nt = x.shape[0]
    tile = lambda i: (i, 0)
    npt = n_prompt // TM
    if final:
        out_shape = (jax.ShapeDtypeStruct((n_prompt, D_MODEL), F32), jax.ShapeDtypeStruct((nt - n_prompt, D_MODEL), F32))
        out_specs = (pl.BlockSpec((TM, D_MODEL), lambda i: (jnp.minimum(i, npt - 1), 0)),
                     pl.BlockSpec((TM, D_MODEL), lambda i: (jnp.maximum(i - npt, 0), 0)))
    else:
        out_shape, out_specs = jax.ShapeDtypeStruct((nt, D_MODEL), F32), pl.BlockSpec((TM, D_MODEL), tile)
    return pl.pallas_call(
        functools.partial(_combine_kernel, npt=npt if final else None),
        out_shape=out_shape,
        grid=(nt // TM,),
        in_specs=[pl.BlockSpec((TM, D_MODEL), tile),
                  pl.BlockSpec((TM, HEAD_W), tile),
                  pl.BlockSpec((DEST_ROWS, HEAD_W), tile),
                  pl.BlockSpec((DEST_ROWS, HEAD_W), lambda i: (jnp.minimum(i + 1, nt // TM - 1), 0)),
                  pl.BlockSpec((DEST_ROWS, HEAD_W), lambda i: (0, 0)),
                  mod_spec,
                  pl.BlockSpec((1, D_MODEL), lambda i: (0, 0)),
                  pl.BlockSpec(memory_space=pl.ANY)],
        out_specs=out_specs,
        scratch_shapes=[pltpu.SMEM((DEST_ROWS, HEAD_W), I32), pltpu.SMEM((DEST_ROWS, HEAD_W), I32),
                        pltpu.SMEM((2,), I32), pltpu.VMEM((2, L_ROWS, D_MODEL), F32),
                        pltpu.SemaphoreType.DMA, pltpu.SemaphoreType.DMA((2,))],
        compiler_params=_params(("arbitrary",)),
        name="moe_combine_final" if final else "moe_combine",
    )(x, gate_t, tab, tab, pad, m, final_g, ys)


def _moe(x, h2, logits_t, m, final_g, w1, b1, w2, b2, mod_spec, n_prompt, layer, final):
    nt = x.shape[0]
    n_blk = nt * TOP_K // MOE_BM + N_EXPERTS
    n_blk_pad = -(-n_blk // HEAD_W) * HEAD_W
    slot, gate_t, tab, blk_exp, n_valid, pad = _route(logits_t, n_blk_pad, n_blk * MOE_BM)
    xs = _dispatch(h2, slot, tab, pad, n_blk * MOE_BM)
    ys = _experts(xs, blk_exp, n_valid, w1, b1, w2, b2, layer)
    return _combine(x, gate_t, tab, pad, m, final_g, ys, mod_spec, n_prompt, final)


def _rope_tables(seq):
    t = jnp.arange(seq)
    inv = 1.0 / (ROPE_THETA ** (jnp.arange(ROPE_HALF, dtype=F32) / ROPE_HALF))
    ang_r = (t // GRID_W).astype(F32)[:, None] * inv
    ang_c = (t % GRID_W).astype(F32)[:, None] * inv
    cr, sr, cc, sc = jnp.cos(ang_r), jnp.sin(ang_r), jnp.cos(ang_c), jnp.sin(ang_c)
    cos = jnp.concatenate([cr, cr, cc, cc] * 2, axis=1)
    sin = jnp.concatenate([-sr, sr, -sc, sc] * 2, axis=1)
    return cos, sin


def kernel(x_prompt, x_sample, c, cache_k, cache_v, state_hgrn, c_ctx, mod_w, mod_b, norm_g, final_norm_g,
           w_in_even, w_out_even, da_lambda, da_subln_g, hg_norm_g, hg_lb, w_in_odd, sgu_w, sgu_b, w_out_odd,
           router_w, router_b, ex_w1, ex_b1, ex_w2, ex_b2):
    bp, lp, _ = x_prompt.shape
    bs, ls, _ = x_sample.shape
    n_p, n_s = bp * lp, bs * ls
    assert lp % TM == 0 and ls % TD == 0 and n_p % TD == 0 and n_p % ls == 0 and bs < MOD_ROWS
    depth = mod_w.shape[0]

    x = (x_prompt.reshape(n_p, D_MODEL), x_sample.reshape(n_s, D_MODEL))
    cond = jnp.zeros((MOD_ROWS, D_MODEL), F32).at[0:bs].set(c).at[bs].set(c_ctx)
    mod = _modulation(cond, mod_w, mod_b)
    mod_spec = _mod_spec(n_p, ls, bs, TM)
    mod_spec_d = _mod_spec(n_p, ls, bs, TD)
    rope_tabs = _rope_tables(ls)
    hg_lb_h = hg_lb.reshape(2, depth + 1, HG_HEADS, 1, HG_DK)
    final_g = final_norm_g.reshape(1, D_MODEL)

    new_k, new_v, new_s = [], [], []
    for l in range(depth):
        m = mod[l]
        g1 = norm_g[l, 0].reshape(1, D_MODEL)
        g2 = norm_g[l, 1].reshape(1, D_MODEL)
        if l % 2 == 0:
            e = l // 2
            lam_init = 0.8 - 0.6 * math.exp(-0.3 * l)
            proj, mix = _inproj(x, n_p, m, g1, w_in_even[e].astype(BF16), mod_spec_d)
            subln = da_subln_g[e].reshape(1, HEAD_W)
            hgn = hg_norm_g[e].reshape(1, HEAD_W)
            mix, k_new, v_new = _diff_attention(proj, mix, da_lambda[e], subln, row0=0, n_batch=bp, seq=lp,
                                                lam_init=lam_init)
            mix = _diff_attention(proj, mix, da_lambda[e], subln, row0=n_p, n_batch=bs, seq=ls, lam_init=lam_init,
                                  rope_tabs=rope_tabs, ctx_k=cache_k, ctx_v=cache_v, layer_e=e)
            mix, s_p = _hgrn(proj, mix, hg_lb_h, hgn, row0=0, n_batch=bp, seq=lp, layer=l)
            mix, _ = _hgrn(proj, mix, hg_lb_h, hgn, row0=n_p, n_batch=bs, seq=ls, layer=l, s0=state_hgrn[:, e])
            w_out = w_out_even[e].astype(BF16)
            new_k.append(k_new)
            new_v.append(v_new)
            new_s.append(s_p)
        else:
            o = l // 2
            mix = _odd_mixer(x, m, g1, w_in_odd[o].astype(BF16), sgu_w[o].astype(BF16), sgu_b[o].T, mod_spec_d)
            w_out = w_out_odd[o].astype(BF16)
        x, h2, logits_t = _post(x, n_p, mix, m, g2, w_out, router_w[l].T, router_b[l].reshape(N_EXPERTS, 1),
                                mod_spec_d)
        x = _moe(x, h2, logits_t, m, final_g, ex_w1, ex_b1, ex_w2, ex_b2, mod_spec, n_p, layer=l,
                 final=(l == depth - 1))

    y_prompt, y_sample = x
    return (y_prompt.reshape(bp, lp, D_MODEL), y_sample.reshape(bs, ls, D_MODEL),
            jnp.stack(new_k, axis=1), jnp.stack(new_v, axis=1), jnp.stack(new_s, axis=1))
```

```python
import functools
import math

import jax
import jax.numpy as jnp
import numpy as np
from jax import lax
from jax.experimental import pallas as pl
from jax.experimental.pallas import tpu as pltpu

F32 = jnp.float32
BF16 = jnp.bfloat16
I32 = jnp.int32

D_MODEL = 1024
DEPTH = 2
GRID_W = 64
DA_HEADS = 4
DA_HD = 64
HG_HEADS = 4
HG_DK = 128
ROPE_THETA = 10000.0
ROPE_HALF = DA_HD // 4
CM_CHUNK = 128
CM_GROUPS = 8
N_EXPERTS = 32
TOP_K = 4
SWIGLU_ALPHA = 1.702
SWIGLU_LIMIT = 7.0
EPS = 1e-6
EVEN_IN = 4096
HEAD_W = 128

TM = 256
TD = 512
ATTN_SUB = 2
ATTN_KEY_CHUNK = 2048
HG_C = 64
HG_SB = 16
HG_HPS = 2
HG_UNROLL = 4
HG_FIN = 256
HG_DOUBLE_BUFFER_BYTES = 2 * 1024 * 1024
L_ROWS = -(-(TOP_K * TM + N_EXPERTS * 14) // 128) * 128
GROUP_BITS = (32, 16, 8, 4, 2, 1)
WAIT_BITS = (128, 64, 32, 16, 8, 4, 2, 1)
MOE_BM = 512
DEST_ROWS = 8
MOD_ROWS = 16
V7X_VMEM_LIMIT = 56 * 1024 * 1024

NT_DIMS = (((1,), (1,)), ((), ()))
TN_DIMS = (((0,), (0,)), ((), ()))


def _params(sem):
    return pltpu.CompilerParams(dimension_semantics=sem, vmem_limit_bytes=V7X_VMEM_LIMIT)


def _sigmoid(x):
    return 1.0 / (1.0 + jnp.exp(-x))


def _silu(x):
    return x * _sigmoid(x)


def _rms(x, g):
    ms = jnp.mean(x * x, axis=-1, keepdims=True)
    return x * lax.rsqrt(ms + EPS) * g


def _norm_mod(x, g, shift, scale):
    return _rms(x, g) * (1.0 + scale) + shift


def _mod_kernel(c_ref, w_ref, b_ref, o_ref):
    s = _silu(c_ref[...])
    o_ref[...] = jnp.dot(s.astype(BF16), w_ref[...].astype(BF16), preferred_element_type=F32) + b_ref[...]


def _modulation(cond, mod_w, mod_b):
    depth = mod_w.shape[0]
    m = pl.pallas_call(
        _mod_kernel,
        out_shape=jax.ShapeDtypeStruct((depth, 6, MOD_ROWS, D_MODEL), F32),
        grid=(depth, 6),
        in_specs=[
            pl.BlockSpec((MOD_ROWS, D_MODEL), lambda l, j: (0, 0)),
            pl.BlockSpec((None, D_MODEL, D_MODEL), lambda l, j: (l, 0, j)),
            pl.BlockSpec((None, 1, D_MODEL), lambda l, j: (l, 0, j)),
        ],
        out_specs=pl.BlockSpec((None, None, MOD_ROWS, D_MODEL), lambda l, j: (l, j, 0, 0)),
        compiler_params=_params(("arbitrary", "arbitrary")),
        name="modulation",
    )(cond, mod_w, mod_b.reshape(depth, 1, 6 * D_MODEL))
    return m.transpose(0, 2, 1, 3)[:, :, :, None, :]


def _mod_spec(n_prompt, l_sample, ctx_row, tile):
    def index(i):
        t = i * tile
        return (jnp.where(t < n_prompt, ctx_row, (t - n_prompt) // l_sample), 0, 0, 0)
    return pl.BlockSpec((None, 6, 1, D_MODEL), index)


def _x_parts(x, n_prompt):
    npt = n_prompt // TD
    if isinstance(x, tuple):
        a, b = x
        spec_b = pl.BlockSpec((TD, D_MODEL), lambda i: (jnp.maximum(i - npt, 0), 0))
    else:
        a = b = x
        spec_b = pl.BlockSpec((TD, D_MODEL), lambda i: (jnp.maximum(i, npt), 0))
    spec_a = pl.BlockSpec((TD, D_MODEL), lambda i: (jnp.minimum(i, npt - 1), 0))
    n_rows = a.shape[0] + b.shape[0] if isinstance(x, tuple) else x.shape[0]
    return [spec_a, spec_b], [a, b], npt, n_rows


def _pick_x(xa_ref, xb_ref, npt):
    return jnp.where(pl.program_id(0) < npt, xa_ref[...], xb_ref[...])


def _inproj_kernel(xa_ref, xb_ref, m_ref, g_ref, w_ref, o_ref, mix_ref, *, npt):
    h = _norm_mod(_pick_x(xa_ref, xb_ref, npt), g_ref[...], m_ref[0], m_ref[1])
    o_ref[...] = jnp.dot(h.astype(BF16), w_ref[...], preferred_element_type=F32)
    mix_ref[...] = jnp.zeros(mix_ref.shape, F32)


def _inproj(x, n_prompt, m, g, w, mod_spec):
    x_specs, x_args, npt, nt = _x_parts(x, n_prompt)
    n_out = w.shape[1]
    return pl.pallas_call(
        functools.partial(_inproj_kernel, npt=npt),
        out_shape=(jax.ShapeDtypeStruct((nt, n_out), F32), jax.ShapeDtypeStruct((nt, D_MODEL), F32)),
        grid=(nt // TD,),
        in_specs=x_specs + [
            mod_spec,
            pl.BlockSpec((1, D_MODEL), lambda i: (0, 0)),
            pl.BlockSpec((D_MODEL, n_out), lambda i: (0, 0)),
        ],
        out_specs=(pl.BlockSpec((TD, n_out), lambda i: (i, 0)), pl.BlockSpec((TD, D_MODEL), lambda i: (i, 0))),
        compiler_params=_params(("arbitrary",)),
        name="even_inproj",
    )(*x_args, m, g, w)


def _rope(x, cos, sin_signed):
    lane = lax.broadcasted_iota(I32, x.shape, 1)
    first = ((lane // ROPE_HALF) % 2) == 0
    partner = jnp.where(first, pltpu.roll(x, HEAD_W - ROPE_HALF, 1), pltpu.roll(x, ROPE_HALF, 1))
    return x * cos + partner * sin_signed


def _attn_kernel(*refs, rope, ctx, bq, chunks, lam_init):
    it = iter(refs)
    lam_ref, g_ref, q_ref, k_ref, v_ref = (next(it) for _ in range(5))
    if rope:
        cq_ref, sq_ref, ck_ref, sk_ref = (next(it) for _ in range(4))
    if ctx:
        kc_ref, vc_ref = next(it), next(it)
    next(it)
    o_ref = next(it)
    if not ctx:
        ko_ref, vo_ref = next(it), next(it)
    kt_ref, vx_ref = next(it), next(it)
    l_new = k_ref.shape[0]
    l_ctx = kc_ref.shape[0] if ctx else 0

    @pl.when(pl.program_id(2) == 0)
    def _prepare_keys():
        if ctx:
            kt_ref[:, 0:l_ctx] = kc_ref[...].T.astype(BF16)
            vx_ref[0:l_ctx, 0:HEAD_W] = vc_ref[...].astype(BF16)
        else:
            ko_ref[...] = k_ref[...]
            vo_ref[...] = v_ref[...]
        step = min(512, l_new)
        for c0 in range(0, l_new, step):
            k = k_ref[c0:c0 + step, :]
            if rope:
                k = _rope(k, ck_ref[c0:c0 + step, :], sk_ref[c0:c0 + step, :])
            kt_ref[:, l_ctx + c0:l_ctx + c0 + step] = k.T.astype(BF16)
            vx_ref[l_ctx + c0:l_ctx + c0 + step, 0:HEAD_W] = v_ref[c0:c0 + step, :].astype(BF16)
        vx_ref[:, HEAD_W:2 * HEAD_W] = jnp.ones((l_ctx + l_new, HEAD_W), BF16)

    n_sub = q_ref.shape[0] // bq
    lane = lax.broadcasted_iota(I32, (bq, HEAD_W), 1)
    lo = lane < DA_HD
    zero = jnp.zeros((bq, HEAD_W), F32)
    qss = []
    for u in range(n_sub):
        rows = slice(u * bq, (u + 1) * bq)
        q = q_ref[rows, :]
        if rope:
            q = _rope(q, cq_ref[rows, :], sq_ref[rows, :])
        q = q * (DA_HD ** -0.5)
        qss.append(jnp.concatenate([jnp.where(lo, q, zero), jnp.where(lo, zero, q)], axis=0).astype(BF16))

    def scores(qs, chunk):
        return jnp.dot(qs, kt_ref[:, chunk[0]:chunk[0] + chunk[1]], preferred_element_type=F32)

    m = [jnp.full((2 * bq, 1), -jnp.inf, F32)] * n_sub
    l = [jnp.zeros((2 * bq, 1), F32)] * n_sub
    acc = [jnp.zeros((2 * bq, HEAD_W), F32)] * n_sub
    s_next = [scores(qs, chunks[0]) for qs in qss]
    for ci, (c0, cs) in enumerate(chunks):
        s_cur = s_next
        if ci + 1 < len(chunks):
            s_next = [scores(qs, chunks[ci + 1]) for qs in qss]
        for u in range(n_sub):
            s = s_cur[u]
            mn = jnp.maximum(m[u], jnp.max(s, axis=-1, keepdims=True))
            alpha = jnp.exp(m[u] - mn)
            p = jnp.exp(s - mn).astype(BF16)
            pv = jnp.dot(p, vx_ref[c0:c0 + cs, :], preferred_element_type=F32)
            acc[u] = alpha * acc[u] + pv[:, 0:HEAD_W]
            l[u] = alpha * l[u] + pv[:, HEAD_W:HEAD_W + 1]
            m[u] = mn
    lp = lam_ref[...]
    lam = (jnp.exp(jnp.sum(lp[0:1] * lp[1:2], axis=-1, keepdims=True))
           - jnp.exp(jnp.sum(lp[2:3] * lp[3:4], axis=-1, keepdims=True)) + lam_init)
    for u in range(n_sub):
        o2 = acc[u] / l[u]
        o = o2[0:bq] - lam * o2[bq:2 * bq]
        o_ref[u * bq:(u + 1) * bq, :] = _rms(o, g_ref[...]) * (1.0 - lam_init)


def _attn_chunks(l_ctx, l_new):
    total = l_ctx + l_new
    if l_new <= ATTN_KEY_CHUNK:
        return ((0, total),)
    chunks, c0 = [(0, l_ctx + ATTN_KEY_CHUNK)], l_ctx + ATTN_KEY_CHUNK
    while c0 < total:
        cs = min(ATTN_KEY_CHUNK, total - c0)
        chunks.append((c0, cs))
        c0 += cs
    return tuple(chunks)


def _diff_attention(proj, mix, lam_p, subln_g, *, row0, n_batch, seq, lam_init, rope_tabs=None,
                    ctx_k=None, ctx_v=None, layer_e=0):
    rope, ctx = rope_tabs is not None, ctx_k is not None
    n_rows = proj.shape[0]
    bq = min(ATTN_SUB * 256, seq)
    nq = seq // bq
    l_ctx = ctx_k.shape[3] if ctx else 0
    qb0, kb0 = row0 // bq, row0 // seq
    in_specs = [
        pl.BlockSpec((4, DA_HD), lambda b, h, i: (0, 0)),
        pl.BlockSpec((1, HEAD_W), lambda b, h, i: (0, 0)),
        pl.BlockSpec((bq, HEAD_W), lambda b, h, i: (qb0 + b * nq + i, h)),
        pl.BlockSpec((seq, HEAD_W), lambda b, h, i: (kb0 + b, DA_HEADS + h)),
        pl.BlockSpec((seq, HEAD_W), lambda b, h, i: (kb0 + b, 2 * DA_HEADS + h)),
    ]
    args = [lam_p, subln_g, proj, proj, proj]
    if rope:
        cos, sin = rope_tabs
        in_specs += [
            pl.BlockSpec((bq, HEAD_W), lambda b, h, i: (i, 0)),
            pl.BlockSpec((bq, HEAD_W), lambda b, h, i: (i, 0)),
            pl.BlockSpec((seq, HEAD_W), lambda b, h, i: (0, 0)),
            pl.BlockSpec((seq, HEAD_W), lambda b, h, i: (0, 0)),
        ]
        args += [cos, sin, cos, sin]
    if ctx:
        in_specs += [
            pl.BlockSpec((None, None, None, l_ctx, HEAD_W), lambda b, h, i: (b, layer_e, h, 0, 0)),
            pl.BlockSpec((None, None, None, l_ctx, HEAD_W), lambda b, h, i: (b, layer_e, h, 0, 0)),
        ]
        args += [ctx_k, ctx_v]
    mix_sds = jax.ShapeDtypeStruct((n_rows, D_MODEL), F32)
    mix_spec = pl.BlockSpec((bq, HEAD_W), lambda b, h, i: (qb0 + b * nq + i, h))
    in_specs.append(pl.BlockSpec(memory_space=pl.ANY))
    args.append(mix)
    aliases = {len(args) - 1: 0}
    if ctx:
        out_shape, out_specs = mix_sds, mix_spec
    else:
        cache_sds = jax.ShapeDtypeStruct((n_batch, DA_HEADS, seq, HEAD_W), F32)
        cache_spec = pl.BlockSpec((None, None, seq, HEAD_W), lambda b, h, i: (b, h, 0, 0))
        out_shape, out_specs = (mix_sds, cache_sds, cache_sds), (mix_spec, cache_spec, cache_spec)
    kernel = functools.partial(_attn_kernel, rope=rope, ctx=ctx, bq=bq // ATTN_SUB,
                               chunks=_attn_chunks(l_ctx, seq), lam_init=lam_init)
    return pl.pallas_call(
        kernel,
        out_shape=out_shape,
        grid=(n_batch, DA_HEADS, nq),
        in_specs=in_specs,
        out_specs=out_specs,
        scratch_shapes=[pltpu.VMEM((HEAD_W, l_ctx + seq), BF16), pltpu.VMEM((l_ctx + seq, 2 * HEAD_W), BF16)],
        input_output_aliases=aliases,
        compiler_params=_params(("arbitrary", "arbitrary", "arbitrary")),
        name="diff_attention_ctx" if ctx else "diff_attention",
    )(*args)


def _hgrn_chunks(streams):
    c, nsb = HG_C, HG_C // HG_SB
    row = lax.broadcasted_iota(I32, (c, c), 0)
    col = lax.broadcasted_iota(I32, (c, c), 1)
    tri = {r: jnp.where((col >= row) if r else (col <= row), 1.0, 0.0).astype(BF16) for r in (False, True)}
    ones = jnp.ones((HG_DK, HG_DK), BF16)
    t_idx = lax.broadcasted_iota(I32, (HG_SB, 1), 0)

    g3s = []
    for q, k, v, lf, rev, f in streams:
        hi = lf.astype(BF16)
        r1 = lf - hi.astype(F32)
        mid = r1.astype(BF16)
        lo = (r1 - mid.astype(F32)).astype(BF16)
        g3s.append(jnp.dot(tri[rev], jnp.concatenate([hi, mid, lo], axis=1), preferred_element_type=F32))
    gs = [g3[:, 0:HG_DK] + g3[:, HG_DK:2 * HG_DK] + g3[:, 2 * HG_DK:3 * HG_DK] for g3 in g3s]
    g_lasts = [g[0:1] if s[4] else g[c - 1:c] for g, s in zip(gs, streams)]

    qgs = [(s[0] * jnp.exp(g)).astype(BF16) for g, s in zip(gs, streams)]

    def outside(i, rev):
        r0 = i * HG_SB
        if rev:
            return (r0 + HG_SB, c, r0 + HG_SB) if r0 + HG_SB < c else None
        return (0, r0, r0 - 1) if r0 > 0 else None

    a_offs = {}
    for i in range(nsb):
        for n, (g, s) in enumerate(zip(gs, streams)):
            span = outside(i, s[4])
            if span is None:
                continue
            s_lo, s_hi, rb = span
            r0 = i * HG_SB
            ref_g = g[rb:rb + 1]
            qd = (s[0][r0:r0 + HG_SB] * jnp.exp(g[r0:r0 + HG_SB] - ref_g)).astype(BF16)
            kd = (s[1][s_lo:s_hi] * jnp.exp(ref_g - g[s_lo:s_hi])).astype(BF16)
            a_offs[i, n] = lax.dot_general(qd, kd, NT_DIMS, preferred_element_type=F32)
    o_offs = {}
    for (i, n), a in a_offs.items():
        s_lo, s_hi, _ = outside(i, streams[n][4])
        o_offs[i, n] = jnp.dot(a.astype(BF16), streams[n][2][s_lo:s_hi].astype(BF16), preferred_element_type=F32)

    def rows_of(j, rev):
        half = HG_SB // 2
        if rev:
            return (0, half) if j < half else (0, HG_SB)
        return (half, HG_SB) if j >= half else (0, HG_SB)

    a_reps = {}
    for i in range(nsb):
        r0 = i * HG_SB
        for n, (g, s) in enumerate(zip(gs, streams)):
            qi, ki, fi = s[0][r0:r0 + HG_SB], s[1][r0:r0 + HG_SB], s[5][r0:r0 + HG_SB]
            ws, qe = {}, None
            order = range(HG_SB) if s[4] else range(HG_SB - 1, -1, -1)
            for j in order:
                t0, t1 = rows_of(j, s[4])
                here = t_idx[t0:t1] == j
                if qe is None:
                    qe = jnp.where(here, qi[t0:t1], 0.0)
                else:
                    if qe.shape[0] != t1 - t0:
                        pad8 = jnp.zeros_like(qe)
                        qe = jnp.concatenate([qe, pad8] if s[4] else [pad8, qe], axis=0)
                    step = fi[j - 1:j] if s[4] else fi[j + 1:j + 2]
                    qe = jnp.where(here, qi[t0:t1], qe * step)
                ws[j] = qe * ki[j:j + 1]
            a_reps[i, n] = jnp.dot(jnp.concatenate([ws[j] for j in range(HG_SB)], axis=0).astype(BF16), ones,
                                   preferred_element_type=F32)

    results = []
    for n, (g, s) in enumerate(zip(gs, streams)):
        q, k, v, lf, rev, f = s
        outs = []
        for i in range(nsb):
            r0 = i * HG_SB
            vi = v[r0:r0 + HG_SB]
            zero8 = jnp.zeros((HG_SB // 2, v.shape[1]), F32)
            lo_acc, hi_acc, at = zero8, zero8, 0
            for j in range(HG_SB):
                t0, t1 = rows_of(j, rev)
                term = a_reps[i, n][at:at + (t1 - t0)] * vi[j:j + 1]
                at += t1 - t0
                if t1 - t0 == HG_SB:
                    lo_acc, hi_acc = lo_acc + term[0:HG_SB // 2], hi_acc + term[HG_SB // 2:HG_SB]
                elif t0 == 0:
                    lo_acc = lo_acc + term
                else:
                    hi_acc = hi_acc + term
            oi = jnp.concatenate([lo_acc, hi_acc], axis=0)
            if (i, n) in o_offs:
                oi = oi + o_offs[i, n]
            outs.append(oi)
        kd = (k * jnp.exp(g_lasts[n] - g)).astype(BF16)
        gain = lax.dot_general(v.astype(BF16), kd, TN_DIMS, preferred_element_type=F32)
        results.append((jnp.concatenate(outs, axis=0), qgs[n], jnp.exp(g_lasts[n]), gain))
    return results


def _hgrn_advance(part, st):
    o_local, qg, decay, gain = part
    o = o_local + lax.dot_general(qg, st.astype(BF16), NT_DIMS, preferred_element_type=F32)
    return o, st * decay + gain


def _hgrn_kernel(*refs, layer, has_s0):
    it = iter(refs)
    hq_ref, zf_ref, zb_ref, hi_ref, hg_ref, lbp_ref, g_ref = (next(it) for _ in range(7))
    s0_ref = next(it) if has_s0 else None
    next(it)
    o_ref, sout_ref, st_ref, ob_ref = next(it), next(it), next(it), next(it)
    seq = hq_ref.shape[0]
    n = seq // HG_C
    z_refs = (zf_ref, zb_ref)

    lbs = []
    for hh in range(HG_HPS):
        for d in range(2):
            p = lbp_ref[d, :, hh]
            e = jnp.exp(p - jnp.max(p, axis=0))
            lbs.append(jnp.sum(e[0:layer + 1], axis=0) / jnp.sum(e, axis=0))
            if has_s0:
                st_ref[2 * hh + d] = s0_ref[d, hh].T
            else:
                st_ref[2 * hh + d] = jnp.zeros((HEAD_W, HG_DK), F32)

    def body(ci, carry):
        streams, dests = [], []
        for hh in range(HG_HPS):
            lanes = slice(hh * HEAD_W, (hh + 1) * HEAD_W)
            for d in range(2):
                for u in range(HG_UNROLL):
                    c = ci * HG_UNROLL + u
                    c = (n - 1 - c) if d == 1 else c
                    rows = pl.ds(pl.multiple_of(c * HG_C, HG_C), HG_C)
                    lb = lbs[2 * hh + d]
                    f = lb + (1.0 - lb) * _sigmoid(z_refs[d][rows, lanes])
                    streams.append((_silu(hq_ref[rows, lanes]), 1.0 - f, hi_ref[rows, lanes], jnp.log(f), d == 1, f))
                    dests.append((o_ref if d == 0 else ob_ref, rows, lanes))
        parts = _hgrn_chunks(streams)
        for slot in range(2 * HG_HPS):
            st = st_ref[slot]
            for u in range(HG_UNROLL):
                o, st = _hgrn_advance(parts[slot * HG_UNROLL + u], st)
                dst_ref, rows, lanes = dests[slot * HG_UNROLL + u]
                dst_ref[rows, lanes] = o
            st_ref[slot] = st
        return carry

    lax.fori_loop(0, n // HG_UNROLL, body, 0)
    for hh in range(HG_HPS):
        for d in range(2):
            sout_ref[d, hh] = st_ref[2 * hh + d].T

    def finish(ci, carry):
        rows = pl.ds(pl.multiple_of(ci * HG_FIN, HG_FIN), HG_FIN)
        for hh in range(HG_HPS):
            lanes = slice(hh * HEAD_W, (hh + 1) * HEAD_W)
            tot = o_ref[rows, lanes] + ob_ref[rows, lanes]
            o_ref[rows, lanes] = _rms(tot, g_ref[...]) * _silu(hg_ref[rows, lanes])
        return carry

    lax.fori_loop(0, seq // HG_FIN, finish, 0)


def _hgrn(proj, mix, hg_lb_l, hg_norm_g, *, row0, n_batch, seq, layer, s0=None):
    has_s0 = s0 is not None
    rb0 = row0 // seq
    hpg = HG_HEADS // HG_HPS
    width = HG_HPS * HEAD_W
    col0 = 3 * DA_HEADS * HEAD_W // width
    once = pl.Buffered(1) if seq * width * 4 > HG_DOUBLE_BUFFER_BYTES else None

    def col_spec(j, mode=None):
        return pl.BlockSpec((seq, width), lambda b, h: (rb0 + b, col0 + j * hpg + h), pipeline_mode=mode)

    in_specs = [col_spec(0), col_spec(1), col_spec(2), col_spec(3), col_spec(4, once),
                pl.BlockSpec((2, DEPTH + 1, HG_HPS, 1, HG_DK), lambda b, h: (0, 0, h, 0, 0)),
                pl.BlockSpec((1, HEAD_W), lambda b, h: (0, 0))]
    args = [proj, proj, proj, proj, proj, hg_lb_l, hg_norm_g]
    state_spec = pl.BlockSpec((None, 2, HG_HPS, HG_DK, HEAD_W), lambda b, h: (b, 0, h, 0, 0))
    if has_s0:
        in_specs.append(state_spec)
        args.append(s0)
    in_specs.append(pl.BlockSpec(memory_space=pl.ANY))
    args.append(mix)
    return pl.pallas_call(
        functools.partial(_hgrn_kernel, layer=layer, has_s0=has_s0),
        out_shape=(jax.ShapeDtypeStruct(mix.shape, mix.dtype),
                   jax.ShapeDtypeStruct((n_batch, 2, HG_HEADS, HG_DK, HEAD_W), F32)),
        grid=(n_batch, hpg),
        in_specs=in_specs,
        out_specs=(pl.BlockSpec((seq, width), lambda b, h: (rb0 + b, DA_HEADS * HEAD_W // width + h)),
                   state_spec),
        scratch_shapes=[pltpu.VMEM((2 * HG_HPS, HEAD_W, HG_DK), F32), pltpu.VMEM((seq, width), F32)],
        input_output_aliases={len(args) - 1: 0},
        compiler_params=_params(("arbitrary", "arbitrary")),
        name="hgrn2_state" if has_s0 else "hgrn2",
    )(*args)


def _odd_kernel(x_ref, m_ref, g_ref, win_ref, ws_ref, sb_ref, o_ref):
    h = _norm_mod(x_ref[...], g_ref[...], m_ref[0], m_ref[1])
    z = jnp.dot(h.astype(BF16), win_ref[...], preferred_element_type=F32)
    z = 0.5 * z * (1.0 + lax.erf(z * (2.0 ** -0.5)))
    width = z.shape[1] // 2
    u, v = z[:, 0:width], z[:, width:2 * width]
    mu = jnp.mean(v, axis=-1, keepdims=True)
    vc = v - mu
    var = jnp.mean(vc * vc, axis=-1, keepdims=True)
    vn = (vc * lax.rsqrt(var + EPS)).astype(BF16)
    gd = width // CM_GROUPS
    for r in range(x_ref.shape[0] // CM_CHUNK):
        rows = slice(r * CM_CHUNK, (r + 1) * CM_CHUNK)
        for gi in range(CM_GROUPS):
            cols = slice(gi * gd, (gi + 1) * gd)
            sv = jnp.dot(ws_ref[gi], vn[rows, cols], preferred_element_type=F32) + sb_ref[:, gi:gi + 1]
            o_ref[rows, cols] = u[rows, cols] * sv


def _odd_mixer(x, m, g, w_in, sgu_w, sgu_bt, mod_spec):
    nt = x.shape[0]
    return pl.pallas_call(
        _odd_kernel,
        out_shape=jax.ShapeDtypeStruct((nt, D_MODEL), F32),
        grid=(nt // TD,),
        in_specs=[
            pl.BlockSpec((TD, D_MODEL), lambda i: (i, 0)),
            mod_spec,
            pl.BlockSpec((1, D_MODEL), lambda i: (0, 0)),
            pl.BlockSpec(w_in.shape, lambda i: (0, 0)),
            pl.BlockSpec(sgu_w.shape, lambda i: (0, 0, 0)),
            pl.BlockSpec(sgu_bt.shape, lambda i: (0, 0)),
        ],
        out_specs=pl.BlockSpec((TD, D_MODEL), lambda i: (i, 0)),
        compiler_params=_params(("arbitrary",)),
        name="odd_mixer",
    )(x, m, g, w_in, sgu_w, sgu_bt)


def _post_kernel(xa_ref, xb_ref, mix_ref, m_ref, g_ref, w_ref, rwt_ref, rb_ref, xo_ref, h2_ref, lg_ref, *, npt):
    mo = jnp.dot(mix_ref[...].astype(BF16), w_ref[...], preferred_element_type=F32)
    xn = _pick_x(xa_ref, xb_ref, npt) + m_ref[2] * mo
    xo_ref[...] = xn
    h2 = _norm_mod(xn, g_ref[...], m_ref[3], m_ref[4])
    h2_ref[...] = h2
    hh = h2.astype(BF16)
    hl = (h2 - hh.astype(F32)).astype(BF16)
    rw = rwt_ref[...]
    rh = rw.astype(BF16)
    rl = (rw - rh.astype(F32)).astype(BF16)
    lg = (lax.dot_general(rh, hh, NT_DIMS, preferred_element_type=F32)
          + lax.dot_general(rh, hl, NT_DIMS, preferred_element_type=F32)
          + lax.dot_general(rl, hh, NT_DIMS, preferred_element_type=F32))
    lg_ref[...] = lg + rb_ref[...]


def _post(x, n_prompt, mix, m, g2, w_out, router_wt, router_b, mod_spec):
    x_specs, x_args, npt, nt = _x_parts(x, n_prompt)
    row = lambda i: (i, 0)
    return pl.pallas_call(
        functools.partial(_post_kernel, npt=npt),
        out_shape=(jax.ShapeDtypeStruct((nt, D_MODEL), F32),
                   jax.ShapeDtypeStruct((nt, D_MODEL), F32),
                   jax.ShapeDtypeStruct((N_EXPERTS, nt), F32)),
        grid=(nt // TD,),
        in_specs=x_specs + [
            pl.BlockSpec((TD, D_MODEL), row),
            mod_spec,
            pl.BlockSpec((1, D_MODEL), lambda i: (0, 0)),
            pl.BlockSpec((D_MODEL, D_MODEL), lambda i: (0, 0)),
            pl.BlockSpec((N_EXPERTS, D_MODEL), lambda i: (0, 0)),
            pl.BlockSpec((N_EXPERTS, 1), lambda i: (0, 0)),
        ],
        out_specs=(pl.BlockSpec((TD, D_MODEL), row), pl.BlockSpec((TD, D_MODEL), row),
                   pl.BlockSpec((N_EXPERTS, TD), lambda i: (0, i))),
        compiler_params=_params(("arbitrary",)),
        name="mixer_out_router",
    )(*x_args, mix, m, g2, w_out, router_wt, router_b)


def _experts_on_lanes(col):
    on_lane = (lax.broadcasted_iota(I32, (N_EXPERTS, HEAD_W), 0) == lax.broadcasted_iota(I32, (N_EXPERTS, HEAD_W), 1))
    return jnp.sum(jnp.where(on_lane, col, 0.0), axis=0, keepdims=True)


def _route_a_kernel(lg_ref, slot_ref, gt_ref, tab_ref, cnt_ref, carry_ref):
    i = pl.program_id(0)

    @pl.when(i == 0)
    def _():
        carry_ref[...] = jnp.zeros(carry_ref.shape, F32)

    l = lg_ref[...]
    tn = l.shape[1]
    eio = lax.broadcasted_iota(I32, l.shape, 0)
    vals, idxs = [], []
    for _ in range(TOP_K):
        mk = jnp.max(l, axis=0, keepdims=True)
        ik = jnp.min(jnp.where(l == mk, eio, N_EXPERTS), axis=0, keepdims=True)
        vals.append(mk)
        idxs.append(ik)
        l = jnp.where(eio == ik, -jnp.inf, l)
    es = [jnp.exp(v - vals[0]) for v in vals]
    den = es[0] + es[1] + es[2] + es[3]
    sel = jnp.zeros(l.shape, F32)
    for ik in idxs:
        sel = sel + jnp.where(eio == ik, 1.0, 0.0)
    upper = jnp.where(lax.broadcasted_iota(I32, (tn, tn), 0) < lax.broadcasted_iota(I32, (tn, tn), 1),
                      1.0, 0.0).astype(BF16)
    before = jnp.dot(sel.astype(BF16), upper, preferred_element_type=F32)

    tile_cnt = jnp.sum(sel, axis=1, keepdims=True)
    c_row, base_row = _experts_on_lanes(tile_cnt), _experts_on_lanes(carry_ref[...])
    sub = lax.broadcasted_iota(I32, (N_EXPERTS, HEAD_W), 0)
    lane_e = lax.broadcasted_iota(I32, (N_EXPERTS, HEAD_W), 1)
    r_row = base_row - 8.0 * jnp.floor(base_row / 8.0)
    seg_row = 8.0 * jnp.floor((r_row + c_row + 7.0) / 8.0)
    seg_col = jnp.sum(jnp.where(sub == lane_e, seg_row, 0.0), axis=1, keepdims=True)
    r_col = jnp.sum(jnp.where(sub == lane_e, r_row, 0.0), axis=1, keepdims=True)
    loff_row = jnp.sum(jnp.where(sub < lane_e, seg_col, 0.0), axis=0, keepdims=True)
    loff_col = jnp.sum(jnp.where(lane_e < sub, seg_row, 0.0), axis=1, keepdims=True)
    first_slot = loff_col + r_col
    slots = []
    for k in range(TOP_K):
        here = eio == idxs[k]
        slots.append(jnp.sum(jnp.where(here, first_slot + before, 0.0), axis=0, keepdims=True))
        slot_ref[k:k + 1, :] = slots[k].astype(I32)
    slot_ref[TOP_K:DEST_ROWS, :] = jnp.zeros((DEST_ROWS - TOP_K, tn), I32)
    gp = jnp.concatenate([e / den for e in es] + slots + [jnp.zeros((HEAD_W - 2 * TOP_K, tn), F32)], axis=0)
    gt_ref[...] = gp.T
    tab_ref[0:1, :] = c_row.astype(I32)
    tab_ref[1:2, :] = loff_row.astype(I32)
    tab_ref[2:3, :] = base_row.astype(I32)
    tab_ref[3:DEST_ROWS, :] = jnp.zeros((DEST_ROWS - 3, HEAD_W), I32)
    carry_ref[...] = carry_ref[...] + tile_cnt
    cnt_ref[...] = carry_ref[...]


def _route_plan_kernel(cnt_ref, be_ref, nv_ref, pad_ref, *, n_blk_pad, n_rows):
    cnt = cnt_ref[...]
    padded = jnp.floor((cnt + (MOE_BM - 1)) / MOE_BM) * MOE_BM
    r = lax.broadcasted_iota(I32, (N_EXPERTS, N_EXPERTS), 0)
    c = lax.broadcasted_iota(I32, (N_EXPERTS, N_EXPERTS), 1)
    padded_row = jnp.sum(jnp.where(r == c, padded, 0.0), axis=0, keepdims=True)
    pend = jnp.sum(jnp.where(c <= r, padded_row, 0.0), axis=1, keepdims=True)
    pstart = pend - padded
    blk_start = (lax.broadcasted_iota(I32, (N_EXPERTS, n_blk_pad), 1) * MOE_BM).astype(F32)
    be = jnp.sum(jnp.where(pend <= blk_start, 1.0, 0.0), axis=0, keepdims=True)
    be_ref[...] = jnp.minimum(be, N_EXPERTS - 1.0).astype(I32)
    total = jnp.sum(padded, axis=0, keepdims=True)
    nv_ref[...] = jnp.broadcast_to(total / MOE_BM, nv_ref.shape).astype(I32)
    on_lane = (lax.broadcasted_iota(I32, (N_EXPERTS, HEAD_W), 0) == lax.broadcasted_iota(I32, (N_EXPERTS, HEAD_W), 1))
    lane = lax.broadcasted_iota(I32, (1, HEAD_W), 1)
    pad_lo = jnp.sum(jnp.where(on_lane, pstart + cnt, 0.0), axis=0, keepdims=True)
    pad_hi = jnp.sum(jnp.where(on_lane, pend, 0.0), axis=0, keepdims=True)
    pad_ref[0:1, :] = jnp.where(lane == N_EXPERTS, total, pad_lo).astype(I32)
    pad_ref[1:2, :] = jnp.where(lane == N_EXPERTS, float(n_rows), pad_hi).astype(I32)
    pad_ref[2:3, :] = jnp.sum(jnp.where(on_lane, pstart, 0.0), axis=0, keepdims=True).astype(I32)
    pad_ref[3:DEST_ROWS, :] = jnp.zeros((DEST_ROWS - 3, HEAD_W), I32)


def _route(logits_t, n_blk_pad, n_rows):
    nt = logits_t.shape[1]
    tn = TM
    blk = lambda i: (0, i)
    tile = lambda i: (i, 0)
    whole = lambda i: (0, 0)
    n_tiles = nt // tn
    slot, gate_t, tab, cnt = pl.pallas_call(
        _route_a_kernel,
        out_shape=(jax.ShapeDtypeStruct((n_tiles * DEST_ROWS, tn), I32), jax.ShapeDtypeStruct((nt, HEAD_W), F32),
                   jax.ShapeDtypeStruct((n_tiles * DEST_ROWS, HEAD_W), I32),
                   jax.ShapeDtypeStruct((N_EXPERTS, 1), F32)),
        grid=(n_tiles,),
        in_specs=[pl.BlockSpec((N_EXPERTS, tn), blk)],
        out_specs=(pl.BlockSpec((DEST_ROWS, tn), tile), pl.BlockSpec((tn, HEAD_W), tile),
                   pl.BlockSpec((DEST_ROWS, HEAD_W), tile), pl.BlockSpec((N_EXPERTS, 1), whole)),
        scratch_shapes=[pltpu.VMEM((N_EXPERTS, 1), F32)],
        compiler_params=_params(("arbitrary",)),
        name="route_topk_slots",
    )(logits_t)
    blk_exp, n_valid, pad = pl.pallas_call(
        functools.partial(_route_plan_kernel, n_blk_pad=n_blk_pad, n_rows=n_rows),
        out_shape=(jax.ShapeDtypeStruct((1, n_blk_pad), I32), jax.ShapeDtypeStruct((1, HEAD_W), I32),
                   jax.ShapeDtypeStruct((DEST_ROWS, HEAD_W), I32)),
        grid=(1,),
        in_specs=[pl.BlockSpec((N_EXPERTS, 1), whole)],
        out_specs=(pl.BlockSpec((1, n_blk_pad), whole), pl.BlockSpec((1, HEAD_W), whole),
                   pl.BlockSpec((DEST_ROWS, HEAD_W), whole)),
        compiler_params=_params(("arbitrary",)),
        name="route_plan",
    )(cnt)
    return slot, gate_t, tab, blk_exp.reshape(n_blk_pad), n_valid[0, 0:1], pad


def _for_each_chunk(n8, fn):
    for bi, b in enumerate(GROUP_BITS):
        @pl.when((n8 & b) != 0)
        def _(bi=bi, b=b):
            first = pl.multiple_of((n8 - (n8 & (2 * b - 1))) * 8, 8)
            fn(first, 8 * b, bi % 2)


def _wait_rows(n8, make_copy):
    for b in WAIT_BITS:
        @pl.when((n8 & b) != 0)
        def _(b=b):
            make_copy(8 * b).wait()


def _dispatch_kernel(h_ref, slot_ref, tab_ref, pad_ref, xs_ref, tsm_ref, psm_ref, sent_ref, xl2_ref, part_ref,
                     zero_ref, sem_tab, sem_pad, sem_rows2):
    i = pl.program_id(0)
    last = pl.num_programs(0) - 1
    tn = h_ref.shape[0]
    buf = i % 2
    xl_ref, sem_rows = xl2_ref.at[buf], sem_rows2.at[buf]
    cpt = pltpu.make_async_copy(tab_ref, tsm_ref, sem_tab)
    cpt.start()

    def out_copy(src_ref, src0, dst0, rows, sem=sem_rows):
        return pltpu.make_async_copy(src_ref.at[pl.ds(src0, rows)], xs_ref.at[pl.ds(dst0, rows)], sem)

    @pl.when(i == 0)
    def _first_step():
        part_ref[...] = jnp.zeros(part_ref.shape, F32)
        zero_ref[...] = jnp.zeros(zero_ref.shape, F32)
        cpp = pltpu.make_async_copy(pad_ref, psm_ref, sem_pad)
        cpp.start()
        cpp.wait()

        def per_span(e, carry):
            lo8, hi8 = (psm_ref[0, e] + 7) >> 3, psm_ref[1, e] >> 3

            def zero_copy(u):
                return out_copy(zero_ref, 0, pl.multiple_of(u * 8, 8), 8)

            lax.fori_loop(lo8, hi8, lambda u, c: (zero_copy(u).start(), c)[1], 0)
            lax.fori_loop(lo8, hi8, lambda u, c: (zero_copy(u).wait(), c)[1], 0)
            return carry

        lax.fori_loop(0, N_EXPERTS + 1, per_span, 0)

    slot = lax.broadcasted_iota(I32, (L_ROWS, tn), 0)
    sl = slot_ref[...]
    hit = jnp.where(slot == sl[0:1], 1.0, jnp.where(slot == sl[1:2], 1.0, jnp.where(
        slot == sl[2:3], 1.0, jnp.where(slot == sl[3:4], 1.0, 0.0))))
    xl_ref[...] = jnp.dot(hit.astype(BF16), h_ref[...].astype(BF16), preferred_element_type=F32)
    cpt.wait()
    sub8 = lax.broadcasted_iota(I32, (8, 1), 0)

    def per_expert(e, sent8):
        cnt, lo, g0 = tsm_ref[0, e], pl.multiple_of(tsm_ref[1, e], 8), psm_ref[2, e] + tsm_ref[2, e]
        r = g0 & 7
        g_tile = pl.multiple_of(g0 - r, 8)
        head = pl.ds(lo, 8)
        xl_ref[head, :] = xl_ref[head, :] + part_ref[e]
        full8, rem = (r + cnt) >> 3, (r + cnt) & 7
        _for_each_chunk(full8, lambda first, rows, prio: out_copy(
            xl_ref, pl.multiple_of(lo + first, 8), pl.multiple_of(g_tile + first, 8), rows).start(priority=prio))
        tail = xl_ref[pl.ds(pl.multiple_of(lo + full8 * 8, 8), 8), :]
        part_ref[e] = jnp.where(sub8 < rem, tail, 0.0)
        return sent8 + full8

    sent8 = lax.fori_loop(0, N_EXPERTS, per_expert, 0)

    sent_ref[buf] = sent8

    @pl.when(i > 0)
    def _previous_step_copies():
        _wait_rows(sent_ref[1 - buf], lambda rows: out_copy(xl_ref, 0, 0, rows, sem_rows2.at[1 - buf]))

    @pl.when(i == last)
    def _last_step():
        def flush(e, n8):
            end = psm_ref[0, e]
            due = (end & 7) != 0

            @pl.when(due)
            def _():
                pltpu.make_async_copy(part_ref.at[e], xs_ref.at[pl.ds(pl.multiple_of(end - (end & 7), 8), 8)],
                                      sem_rows).start()

            return n8 + jnp.where(due, 1, 0)

        _wait_rows(lax.fori_loop(0, N_EXPERTS, flush, sent8), lambda rows: out_copy(xl_ref, 0, 0, rows))


def _dispatch(h2, slot, tab, pad, n_rows):
    nt = h2.shape[0]
    tile = lambda i: (i, 0)
    return pl.pallas_call(
        _dispatch_kernel,
        out_shape=jax.ShapeDtypeStruct((n_rows, D_MODEL), F32),
        grid=(nt // TM,),
        in_specs=[pl.BlockSpec((TM, D_MODEL), tile),
                  pl.BlockSpec((DEST_ROWS, TM), tile),
                  pl.BlockSpec((DEST_ROWS, HEAD_W), tile),
                  pl.BlockSpec((DEST_ROWS, HEAD_W), lambda i: (0, 0))],
        out_specs=pl.BlockSpec(memory_space=pl.ANY),
        scratch_shapes=[pltpu.SMEM((DEST_ROWS, HEAD_W), I32), pltpu.SMEM((DEST_ROWS, HEAD_W), I32),
                        pltpu.SMEM((2,), I32),
                        pltpu.VMEM((2, L_ROWS, D_MODEL), F32), pltpu.VMEM((N_EXPERTS, 8, D_MODEL), F32),
                        pltpu.VMEM((8, D_MODEL), F32),
                        pltpu.SemaphoreType.DMA, pltpu.SemaphoreType.DMA, pltpu.SemaphoreType.DMA((2,))],
        compiler_params=_params(("arbitrary",)),
        name="moe_dispatch",
    )(h2, slot, tab, pad)


def _expert_kernel(be_ref, nv_ref, xs_ref, w1_ref, b1_ref, w2_ref, b2_ref, ys_ref, w1b_ref, w2b_ref):
    i = pl.program_id(0)
    valid = i < nv_ref[0]
    prev = be_ref[jnp.maximum(i - 1, 0)]
    changed = jnp.logical_or(i == 0, be_ref[i] != prev)

    @pl.when(jnp.logical_and(valid, changed))
    def _cast_weights():
        step = 128
        def cast(j, carry):
            rows = pl.ds(pl.multiple_of(j * step, step), step)
            w1b_ref[rows, :] = w1_ref[rows, :].astype(BF16)
            w2b_ref[rows, :] = w2_ref[rows, :].astype(BF16)
            return carry
        lax.fori_loop(0, w1_ref.shape[0] // step, cast, 0)

    @pl.when(jnp.logical_not(valid))
    def _unused_block():
        ys_ref[...] = jnp.zeros(ys_ref.shape, F32)

    @pl.when(valid)
    def _mlp():
        h = jnp.dot(xs_ref[...].astype(BF16), w1b_ref[...], preferred_element_type=F32) + b1_ref[...]
        half = h.shape[1] // 2
        a = jnp.minimum(h[:, 0:half], SWIGLU_LIMIT)
        lin = jnp.clip(h[:, half:2 * half], -SWIGLU_LIMIT, SWIGLU_LIMIT)
        act = a * _sigmoid(SWIGLU_ALPHA * a) * (lin + 1.0)
        ys_ref[...] = jnp.dot(act.astype(BF16), w2b_ref[...], preferred_element_type=F32) + b2_ref[...]


def _experts(xs, blk_exp, n_valid, w1, b1, w2, b2, layer):
    n_rows = xs.shape[0]
    n_blk = n_rows // MOE_BM
    depth, d_e2 = w1.shape[0], w1.shape[3]

    def blk(i, be, nv):
        return jnp.minimum(i, nv[0] - 1)

    def expert(i, be, nv):
        return (layer, be[blk(i, be, nv)], 0, 0)

    grid_spec = pltpu.PrefetchScalarGridSpec(
        num_scalar_prefetch=2,
        grid=(n_blk,),
        in_specs=[
            pl.BlockSpec((MOE_BM, D_MODEL), lambda i, be, nv: (blk(i, be, nv), 0)),
            pl.BlockSpec((None, None, D_MODEL, d_e2), expert),
            pl.BlockSpec((None, None, 1, d_e2), expert),
            pl.BlockSpec((None, None, d_e2 // 2, D_MODEL), expert),
            pl.BlockSpec((None, None, 1, D_MODEL), expert),
        ],
        out_specs=pl.BlockSpec((MOE_BM, D_MODEL), lambda i, be, nv: (i, 0)),
        scratch_shapes=[pltpu.VMEM((D_MODEL, d_e2), BF16), pltpu.VMEM((d_e2 // 2, D_MODEL), BF16)],
    )
    return pl.pallas_call(
        _expert_kernel,
        out_shape=jax.ShapeDtypeStruct((n_rows, D_MODEL), F32),
        grid_spec=grid_spec,
        compiler_params=_params(("arbitrary",)),
        name="moe_experts",
    )(blk_exp, n_valid, xs, w1, b1.reshape(depth, N_EXPERTS, 1, d_e2), w2, b2.reshape(depth, N_EXPERTS, 1, D_MODEL))


def _combine_kernel(x_ref, gt_ref, tab_ref, tab_next_ref, pad_ref, m_ref, fg_ref, ys_ref, *out_and_scratch, npt):
    *o_ref, tsm_ref, psm_ref, got_ref, yl2_ref, sem_tab, sem_rows2 = out_and_scratch
    o_ref = o_ref[0] if npt is None else o_ref
    i = pl.program_id(0)
    tn = x_ref.shape[0]
    buf = i % 2

    def in_copy(b, src0, dst0, rows):
        return pltpu.make_async_copy(ys_ref.at[pl.ds(src0, rows)], yl2_ref.at[b, pl.ds(dst0, rows)],
                                     sem_rows2.at[b])

    def fetch(table_ref, b):
        cpt = pltpu.make_async_copy(table_ref, tsm_ref, sem_tab)
        cpt.start()
        cpt.wait()

        def per_expert(e, got8):
            cnt, lo, g0 = tsm_ref[0, e], pl.multiple_of(tsm_ref[1, e], 8), psm_ref[2, e] + tsm_ref[2, e]
            r = g0 & 7
            g_tile = pl.multiple_of(g0 - r, 8)
            cover8 = jnp.where(cnt > 0, (r + cnt + 7) >> 3, 0)
            _for_each_chunk(cover8, lambda first, rows, prio: in_copy(
                b, pl.multiple_of(g_tile + first, 8), pl.multiple_of(lo + first, 8), rows).start(priority=prio))
            return got8 + cover8

        got_ref[b] = lax.fori_loop(0, N_EXPERTS, per_expert, 0)

    @pl.when(i == 0)
    def _first_step():
        yl2_ref[...] = jnp.zeros(yl2_ref.shape, F32)
        cpp = pltpu.make_async_copy(pad_ref, psm_ref, sem_tab)
        cpp.start()
        cpp.wait()
        fetch(tab_ref, 0)

    @pl.when(i + 1 < pl.num_programs(0))
    def _next_tile():
        fetch(tab_next_ref, 1 - buf)

    gt = gt_ref[...]
    slot = lax.broadcasted_iota(I32, (tn, L_ROWS), 1).astype(F32)
    pick = jnp.where(slot == gt[:, TOP_K:TOP_K + 1], gt[:, 0:1], 0.0)
    for k in range(1, TOP_K):
        pick = pick + jnp.where(slot == gt[:, TOP_K + k:TOP_K + k + 1], gt[:, k:k + 1], 0.0)
    _wait_rows(got_ref[buf], lambda rows: in_copy(buf, 0, 0, rows))
    acc = jnp.dot(pick.astype(BF16), yl2_ref[buf].astype(BF16), preferred_element_type=F32)
    xn = x_ref[...] + m_ref[5] * acc
    if npt is None:
        o_ref[...] = xn
    else:
        xn = _rms(xn, fg_ref[...])
        op_ref, os_ref = o_ref

        @pl.when(i < npt)
        def _():
            op_ref[...] = xn

        @pl.when(i >= npt)
        def _():
            os_ref[...] = xn


def _combine(x, gate_t, tab, pad, m, final_g, ys, mod_spec, n_prompt, final):
    nt = x.shape[0]
    tile = lambda i: (i, 0)
    npt = n_prompt // TM
    if final:
        out_shape = (jax.ShapeDtypeStruct((n_prompt, D_MODEL), F32), jax.ShapeDtypeStruct((nt - n_prompt, D_MODEL), F32))
        out_specs = (pl.BlockSpec((TM, D_MODEL), lambda i: (jnp.minimum(i, npt - 1), 0)),
                     pl.BlockSpec((TM, D_MODEL), lambda i: (jnp.maximum(i - npt, 0), 0)))
    else:
        out_shape, out_specs = jax.ShapeDtypeStruct((nt, D_MODEL), F32), pl.BlockSpec((TM, D_MODEL), tile)
    return pl.pallas_call(
        functools.partial(_combine_kernel, npt=npt if final else None),
        out_shape=out_shape,
        grid=(nt // TM,),
        in_specs=[pl.BlockSpec((TM, D_MODEL), tile),
                  pl.BlockSpec((TM, HEAD_W), tile),
                  pl.BlockSpec((DEST_ROWS, HEAD_W), tile),
                  pl.BlockSpec((DEST_ROWS, HEAD_W), lambda i: (jnp.minimum(i + 1, nt // TM - 1), 0)),
                  pl.BlockSpec((DEST_ROWS, HEAD_W), lambda i: (0, 0)),
                  mod_spec,
                  pl.BlockSpec((1, D_MODEL), lambda i: (0, 0)),
                  pl.BlockSpec(memory_space=pl.ANY)],
        out_specs=out_specs,
        scratch_shapes=[pltpu.SMEM((DEST_ROWS, HEAD_W), I32), pltpu.SMEM((DEST_ROWS, HEAD_W), I32),
                        pltpu.SMEM((2,), I32), pltpu.VMEM((2, L_ROWS, D_MODEL), F32),
                        pltpu.SemaphoreType.DMA, pltpu.SemaphoreType.DMA((2,))],
        compiler_params=_params(("arbitrary",)),
        name="moe_combine_final" if final else "moe_combine",
    )(x, gate_t, tab, tab, pad, m, final_g, ys)


def _moe(x, h2, logits_t, m, final_g, w1, b1, w2, b2, mod_spec, n_prompt, layer, final):
    nt = x.shape[0]
    n_blk = nt * TOP_K // MOE_BM + N_EXPERTS
    n_blk_pad = -(-n_blk // HEAD_W) * HEAD_W
    slot, gate_t, tab, blk_exp, n_valid, pad = _route(logits_t, n_blk_pad, n_blk * MOE_BM)
    xs = _dispatch(h2, slot, tab, pad, n_blk * MOE_BM)
    ys = _experts(xs, blk_exp, n_valid, w1, b1, w2, b2, layer)
    return _combine(x, gate_t, tab, pad, m, final_g, ys, mod_spec, n_prompt, final)


def _rope_tables(seq):
    t = np.arange(seq)
    inv = (1.0 / (np.float32(ROPE_THETA) ** (np.arange(ROPE_HALF, dtype=np.float32) / np.float32(ROPE_HALF))))
    inv = inv.astype(np.float32)
    ang_r = (t // GRID_W).astype(np.float32)[:, None] * inv
    ang_c = (t % GRID_W).astype(np.float32)[:, None] * inv
    cr, sr, cc, sc = np.cos(ang_r), np.sin(ang_r), np.cos(ang_c), np.sin(ang_c)
    cos = np.concatenate([cr, cr, cc, cc] * 2, axis=1).astype(np.float32)
    sin = np.concatenate([-sr, sr, -sc, sc] * 2, axis=1).astype(np.float32)
    return jnp.asarray(cos), jnp.asarray(sin)


def kernel(x_prompt, x_sample, c, cache_k, cache_v, state_hgrn, c_ctx, mod_w, mod_b, norm_g, final_norm_g,
           w_in_even, w_out_even, da_lambda, da_subln_g, hg_norm_g, hg_lb, w_in_odd, sgu_w, sgu_b, w_out_odd,
           router_w, router_b, ex_w1, ex_b1, ex_w2, ex_b2):
    bp, lp, _ = x_prompt.shape
    bs, ls, _ = x_sample.shape
    n_p, n_s = bp * lp, bs * ls
    assert lp % TM == 0 and ls % TD == 0 and n_p % TD == 0 and n_p % ls == 0 and bs < MOD_ROWS
    depth = mod_w.shape[0]

    x = (x_prompt.reshape(n_p, D_MODEL), x_sample.reshape(n_s, D_MODEL))
    cond = jnp.zeros((MOD_ROWS, D_MODEL), F32).at[0:bs].set(c).at[bs].set(c_ctx)
    mod = _modulation(cond, mod_w, mod_b)
    mod_spec = _mod_spec(n_p, ls, bs, TM)
    mod_spec_d = _mod_spec(n_p, ls, bs, TD)
    rope_tabs = _rope_tables(ls)
    hg_lb_h = hg_lb.reshape(2, depth + 1, HG_HEADS, 1, HG_DK)
    final_g = final_norm_g.reshape(1, D_MODEL)

    new_k, new_v, new_s = [], [], []
    for l in range(depth):
        m = mod[l]
        g1 = norm_g[l, 0].reshape(1, D_MODEL)
        g2 = norm_g[l, 1].reshape(1, D_MODEL)
        if l % 2 == 0:
            e = l // 2
            lam_init = 0.8 - 0.6 * math.exp(-0.3 * l)
            proj, mix = _inproj(x, n_p, m, g1, w_in_even[e].astype(BF16), mod_spec_d)
            subln = da_subln_g[e].reshape(1, HEAD_W)
            hgn = hg_norm_g[e].reshape(1, HEAD_W)
            mix, k_new, v_new = _diff_attention(proj, mix, da_lambda[e], subln, row0=0, n_batch=bp, seq=lp,
                                                lam_init=lam_init)
            mix = _diff_attention(proj, mix, da_lambda[e], subln, row0=n_p, n_batch=bs, seq=ls, lam_init=lam_init,
                                  rope_tabs=rope_tabs, ctx_k=cache_k, ctx_v=cache_v, layer_e=e)
            mix, s_p = _hgrn(proj, mix, hg_lb_h, hgn, row0=0, n_batch=bp, seq=lp, layer=l)
            mix, _ = _hgrn(proj, mix, hg_lb_h, hgn, row0=n_p, n_batch=bs, seq=ls, layer=l, s0=state_hgrn[:, e])
            w_out = w_out_even[e].astype(BF16)
            new_k.append(k_new)
            new_v.append(v_new)
            new_s.append(s_p)
        else:
            o = l // 2
            mix = _odd_mixer(x, m, g1, w_in_odd[o].astype(BF16), sgu_w[o].astype(BF16), sgu_b[o].T, mod_spec_d)
            w_out = w_out_odd[o].astype(BF16)
        x, h2, logits_t = _post(x, n_p, mix, m, g2, w_out, router_w[l].T, router_b[l].reshape(N_EXPERTS, 1),
                                mod_spec_d)
        x = _moe(x, h2, logits_t, m, final_g, ex_w1, ex_b1, ex_w2, ex_b2, mod_spec, n_p, layer=l,
                 final=(l == depth - 1))

    y_prompt, y_sample = x
    return (y_prompt.reshape(bp, lp, D_MODEL), y_sample.reshape(bs, ls, D_MODEL),
            jnp.stack(new_k, axis=1), jnp.stack(new_v, axis=1), jnp.stack(new_s, axis=1))
```

```python
import functools
import math

import jax
import jax.numpy as jnp
import numpy as np
from jax import lax
from jax.experimental import pallas as pl
from jax.experimental.pallas import tpu as pltpu

F32 = jnp.float32
BF16 = jnp.bfloat16
I32 = jnp.int32

D_MODEL = 1024
DEPTH = 2
GRID_W = 64
DA_HEADS = 4
DA_HD = 64
HG_HEADS = 4
HG_DK = 128
ROPE_THETA = 10000.0
ROPE_HALF = DA_HD // 4
CM_CHUNK = 128
CM_GROUPS = 8
N_EXPERTS = 32
TOP_K = 4
SWIGLU_ALPHA = 1.702
SWIGLU_LIMIT = 7.0
EPS = 1e-6
HEAD_W = 128

TM = 256
TD = 512
ATTN_SUB = 2
ATTN_KEY_CHUNK = 2048
HG_C = 64
HG_SB = 16
HG_HPS = 2
HG_UNROLL = 4
HG_FIN = 256
HG_DOUBLE_BUFFER_BYTES = 2 * 1024 * 1024
V7X_TILE_ROWS = 8
V7X_LANES = 128
L_ROWS = -(-(TOP_K * TM + N_EXPERTS * 2 * (V7X_TILE_ROWS - 1)) // V7X_LANES) * V7X_LANES
GROUP_BITS = (32, 16, 8, 4, 2, 1)
WAIT_BITS = (128, 64, 32, 16, 8, 4, 2, 1)
MOE_BM = 512
DEST_ROWS = V7X_TILE_ROWS
MOD_ROWS = 16
V7X_VMEM_LIMIT = 56 * 1024 * 1024

NT_DIMS = (((1,), (1,)), ((), ()))
TN_DIMS = (((0,), (0,)), ((), ()))


def _params(sem):
    return pltpu.CompilerParams(dimension_semantics=sem, vmem_limit_bytes=V7X_VMEM_LIMIT)


def _sigmoid(x):
    return 1.0 / (1.0 + jnp.exp(-x))


def _silu(x):
    return x * _sigmoid(x)


def _rms(x, g):
    ms = jnp.mean(x * x, axis=-1, keepdims=True)
    return x * lax.rsqrt(ms + EPS) * g


def _norm_mod(x, g, shift, scale):
    return _rms(x, g) * (1.0 + scale) + shift


def _mod_kernel(c_ref, w_ref, b_ref, o_ref):
    s = _silu(c_ref[...])
    o_ref[...] = jnp.dot(s.astype(BF16), w_ref[...].astype(BF16), preferred_element_type=F32) + b_ref[...]


def _modulation(cond, mod_w, mod_b):
    depth = mod_w.shape[0]
    m = pl.pallas_call(
        _mod_kernel,
        out_shape=jax.ShapeDtypeStruct((depth, 6, MOD_ROWS, D_MODEL), F32),
        grid=(depth, 6),
        in_specs=[
            pl.BlockSpec((MOD_ROWS, D_MODEL), lambda l, j: (0, 0)),
            pl.BlockSpec((None, D_MODEL, D_MODEL), lambda l, j: (l, 0, j)),
            pl.BlockSpec((None, 1, D_MODEL), lambda l, j: (l, 0, j)),
        ],
        out_specs=pl.BlockSpec((None, None, MOD_ROWS, D_MODEL), lambda l, j: (l, j, 0, 0)),
        compiler_params=_params(("arbitrary", "arbitrary")),
        name="modulation",
    )(cond, mod_w, mod_b.reshape(depth, 1, 6 * D_MODEL))
    return m.transpose(0, 2, 1, 3)[:, :, :, None, :]


def _mod_spec(n_prompt, l_sample, ctx_row, tile):
    def index(i):
        t = i * tile
        return (jnp.where(t < n_prompt, ctx_row, (t - n_prompt) // l_sample), 0, 0, 0)
    return pl.BlockSpec((None, 6, 1, D_MODEL), index)


def _x_parts(x, n_prompt):
    npt = n_prompt // TD
    if isinstance(x, tuple):
        a, b = x
        spec_b = pl.BlockSpec((TD, D_MODEL), lambda i: (jnp.maximum(i - npt, 0), 0))
    else:
        a = b = x
        spec_b = pl.BlockSpec((TD, D_MODEL), lambda i: (jnp.maximum(i, npt), 0))
    spec_a = pl.BlockSpec((TD, D_MODEL), lambda i: (jnp.minimum(i, npt - 1), 0))
    n_rows = a.shape[0] + b.shape[0] if isinstance(x, tuple) else x.shape[0]
    return [spec_a, spec_b], [a, b], npt, n_rows


def _pick_x(xa_ref, xb_ref, npt):
    return jnp.where(pl.program_id(0) < npt, xa_ref[...], xb_ref[...])


def _inproj_kernel(xa_ref, xb_ref, m_ref, g_ref, w_ref, o_ref, mix_ref, *, npt):
    h = _norm_mod(_pick_x(xa_ref, xb_ref, npt), g_ref[...], m_ref[0], m_ref[1])
    o_ref[...] = jnp.dot(h.astype(BF16), w_ref[...], preferred_element_type=F32)
    mix_ref[...] = jnp.zeros(mix_ref.shape, F32)


def _inproj(x, n_prompt, m, g, w, mod_spec):
    x_specs, x_args, npt, nt = _x_parts(x, n_prompt)
    n_out = w.shape[1]
    return pl.pallas_call(
        functools.partial(_inproj_kernel, npt=npt),
        out_shape=(jax.ShapeDtypeStruct((nt, n_out), F32), jax.ShapeDtypeStruct((nt, D_MODEL), F32)),
        grid=(nt // TD,),
        in_specs=x_specs + [
            mod_spec,
            pl.BlockSpec((1, D_MODEL), lambda i: (0, 0)),
            pl.BlockSpec((D_MODEL, n_out), lambda i: (0, 0)),
        ],
        out_specs=(pl.BlockSpec((TD, n_out), lambda i: (i, 0)), pl.BlockSpec((TD, D_MODEL), lambda i: (i, 0))),
        compiler_params=_params(("arbitrary",)),
        name="even_inproj",
    )(*x_args, m, g, w)


def _rope(x, cos, sin_signed):
    lane = lax.broadcasted_iota(I32, x.shape, 1)
    first = ((lane // ROPE_HALF) % 2) == 0
    partner = jnp.where(first, pltpu.roll(x, HEAD_W - ROPE_HALF, 1), pltpu.roll(x, ROPE_HALF, 1))
    return x * cos + partner * sin_signed


def _attn_kernel(*refs, rope, ctx, bq, chunks, lam_init):
    it = iter(refs)
    lam_ref, g_ref, q_ref, k_ref, v_ref = (next(it) for _ in range(5))
    if rope:
        cq_ref, sq_ref, ck_ref, sk_ref = (next(it) for _ in range(4))
    if ctx:
        kc_ref, vc_ref = next(it), next(it)
    next(it)
    o_ref = next(it)
    if not ctx:
        ko_ref, vo_ref = next(it), next(it)
    kt_ref, vx_ref = next(it), next(it)
    l_new = k_ref.shape[0]
    l_ctx = kc_ref.shape[0] if ctx else 0

    @pl.when(pl.program_id(2) == 0)
    def _prepare_keys():
        if ctx:
            kt_ref[:, 0:l_ctx] = kc_ref[...].T.astype(BF16)
            vx_ref[0:l_ctx, 0:HEAD_W] = vc_ref[...].astype(BF16)
        else:
            ko_ref[...] = k_ref[...]
            vo_ref[...] = v_ref[...]
        step = min(512, l_new)
        for c0 in range(0, l_new, step):
            k = k_ref[c0:c0 + step, :]
            if rope:
                k = _rope(k, ck_ref[c0:c0 + step, :], sk_ref[c0:c0 + step, :])
            kt_ref[:, l_ctx + c0:l_ctx + c0 + step] = k.T.astype(BF16)
            vx_ref[l_ctx + c0:l_ctx + c0 + step, 0:HEAD_W] = v_ref[c0:c0 + step, :].astype(BF16)
        vx_ref[:, HEAD_W:2 * HEAD_W] = jnp.ones((l_ctx + l_new, HEAD_W), BF16)

    n_sub = q_ref.shape[0] // bq
    lane = lax.broadcasted_iota(I32, (bq, HEAD_W), 1)
    lo = lane < DA_HD
    zero = jnp.zeros((bq, HEAD_W), F32)
    qss = []
    for u in range(n_sub):
        rows = slice(u * bq, (u + 1) * bq)
        q = q_ref[rows, :]
        if rope:
            q = _rope(q, cq_ref[rows, :], sq_ref[rows, :])
        q = q * (DA_HD ** -0.5)
        qss.append(jnp.concatenate([jnp.where(lo, q, zero), jnp.where(lo, zero, q)], axis=0).astype(BF16))

    def scores(qs, chunk):
        return jnp.dot(qs, kt_ref[:, chunk[0]:chunk[0] + chunk[1]], preferred_element_type=F32)

    m = [jnp.full((2 * bq, 1), -jnp.inf, F32)] * n_sub
    l = [jnp.zeros((2 * bq, 1), F32)] * n_sub
    acc = [jnp.zeros((2 * bq, HEAD_W), F32)] * n_sub
    s_next = [scores(qs, chunks[0]) for qs in qss]
    for ci, (c0, cs) in enumerate(chunks):
        s_cur = s_next
        if ci + 1 < len(chunks):
            s_next = [scores(qs, chunks[ci + 1]) for qs in qss]
        for u in range(n_sub):
            s = s_cur[u]
            mn = jnp.maximum(m[u], jnp.max(s, axis=-1, keepdims=True))
            alpha = jnp.exp(m[u] - mn)
            p = jnp.exp(s - mn).astype(BF16)
            pv = jnp.dot(p, vx_ref[c0:c0 + cs, :], preferred_element_type=F32)
            acc[u] = alpha * acc[u] + pv[:, 0:HEAD_W]
            l[u] = alpha * l[u] + pv[:, HEAD_W:HEAD_W + 1]
            m[u] = mn
    lp = lam_ref[...]
    lam = (jnp.exp(jnp.sum(lp[0:1] * lp[1:2], axis=-1, keepdims=True))
           - jnp.exp(jnp.sum(lp[2:3] * lp[3:4], axis=-1, keepdims=True)) + lam_init)
    for u in range(n_sub):
        o2 = acc[u] / l[u]
        o = o2[0:bq] - lam * o2[bq:2 * bq]
        o_ref[u * bq:(u + 1) * bq, :] = _rms(o, g_ref[...]) * (1.0 - lam_init)


def _attn_chunks(l_ctx, l_new):
    total = l_ctx + l_new
    if l_new <= ATTN_KEY_CHUNK:
        return ((0, total),)
    chunks, c0 = [(0, l_ctx + ATTN_KEY_CHUNK)], l_ctx + ATTN_KEY_CHUNK
    while c0 < total:
        cs = min(ATTN_KEY_CHUNK, total - c0)
        chunks.append((c0, cs))
        c0 += cs
    return tuple(chunks)


def _diff_attention(proj, mix, lam_p, subln_g, *, row0, n_batch, seq, lam_init, rope_tabs=None,
                    ctx_k=None, ctx_v=None, layer_e=0):
    rope, ctx = rope_tabs is not None, ctx_k is not None
    n_rows = proj.shape[0]
    bq = min(ATTN_SUB * 256, seq)
    nq = seq // bq
    l_ctx = ctx_k.shape[3] if ctx else 0
    qb0, kb0 = row0 // bq, row0 // seq
    in_specs = [
        pl.BlockSpec((4, DA_HD), lambda b, h, i: (0, 0)),
        pl.BlockSpec((1, HEAD_W), lambda b, h, i: (0, 0)),
        pl.BlockSpec((bq, HEAD_W), lambda b, h, i: (qb0 + b * nq + i, h)),
        pl.BlockSpec((seq, HEAD_W), lambda b, h, i: (kb0 + b, DA_HEADS + h)),
        pl.BlockSpec((seq, HEAD_W), lambda b, h, i: (kb0 + b, 2 * DA_HEADS + h)),
    ]
    args = [lam_p, subln_g, proj, proj, proj]
    if rope:
        cos, sin = rope_tabs
        in_specs += [
            pl.BlockSpec((bq, HEAD_W), lambda b, h, i: (i, 0)),
            pl.BlockSpec((bq, HEAD_W), lambda b, h, i: (i, 0)),
            pl.BlockSpec((seq, HEAD_W), lambda b, h, i: (0, 0)),
            pl.BlockSpec((seq, HEAD_W), lambda b, h, i: (0, 0)),
        ]
        args += [cos, sin, cos, sin]
    if ctx:
        in_specs += [
            pl.BlockSpec((None, None, None, l_ctx, HEAD_W), lambda b, h, i: (b, layer_e, h, 0, 0)),
            pl.BlockSpec((None, None, None, l_ctx, HEAD_W), lambda b, h, i: (b, layer_e, h, 0, 0)),
        ]
        args += [ctx_k, ctx_v]
    mix_sds = jax.ShapeDtypeStruct((n_rows, D_MODEL), F32)
    mix_spec = pl.BlockSpec((bq, HEAD_W), lambda b, h, i: (qb0 + b * nq + i, h))
    in_specs.append(pl.BlockSpec(memory_space=pl.ANY))
    args.append(mix)
    aliases = {len(args) - 1: 0}
    if ctx:
        out_shape, out_specs = mix_sds, mix_spec
    else:
        cache_sds = jax.ShapeDtypeStruct((n_batch, DA_HEADS, seq, HEAD_W), F32)
        cache_spec = pl.BlockSpec((None, None, seq, HEAD_W), lambda b, h, i: (b, h, 0, 0))
        out_shape, out_specs = (mix_sds, cache_sds, cache_sds), (mix_spec, cache_spec, cache_spec)
    kernel = functools.partial(_attn_kernel, rope=rope, ctx=ctx, bq=bq // ATTN_SUB,
                               chunks=_attn_chunks(l_ctx, seq), lam_init=lam_init)
    return pl.pallas_call(
        kernel,
        out_shape=out_shape,
        grid=(n_batch, DA_HEADS, nq),
        in_specs=in_specs,
        out_specs=out_specs,
        scratch_shapes=[pltpu.VMEM((HEAD_W, l_ctx + seq), BF16), pltpu.VMEM((l_ctx + seq, 2 * HEAD_W), BF16)],
        input_output_aliases=aliases,
        compiler_params=_params(("arbitrary", "arbitrary", "arbitrary")),
        name="diff_attention_ctx" if ctx else "diff_attention",
    )(*args)


def _hgrn_chunks(streams):
    c, nsb = HG_C, HG_C // HG_SB
    row = lax.broadcasted_iota(I32, (c, c), 0)
    col = lax.broadcasted_iota(I32, (c, c), 1)
    tri = {r: jnp.where((col >= row) if r else (col <= row), 1.0, 0.0).astype(BF16) for r in (False, True)}
    ones = jnp.ones((HG_DK, HG_DK), BF16)
    t_idx = lax.broadcasted_iota(I32, (HG_SB, 1), 0)

    g3s = []
    for q, k, v, lf, rev, f in streams:
        hi = lf.astype(BF16)
        r1 = lf - hi.astype(F32)
        mid = r1.astype(BF16)
        lo = (r1 - mid.astype(F32)).astype(BF16)
        g3s.append(jnp.dot(tri[rev], jnp.concatenate([hi, mid, lo], axis=1), preferred_element_type=F32))
    gs = [g3[:, 0:HG_DK] + g3[:, HG_DK:2 * HG_DK] + g3[:, 2 * HG_DK:3 * HG_DK] for g3 in g3s]
    g_lasts = [g[0:1] if s[4] else g[c - 1:c] for g, s in zip(gs, streams)]

    qgs = [(s[0] * jnp.exp(g)).astype(BF16) for g, s in zip(gs, streams)]

    def outside(i, rev):
        r0 = i * HG_SB
        if rev:
            return (r0 + HG_SB, c, r0 + HG_SB) if r0 + HG_SB < c else None
        return (0, r0, r0 - 1) if r0 > 0 else None

    a_offs = {}
    for i in range(nsb):
        for n, (g, s) in enumerate(zip(gs, streams)):
            span = outside(i, s[4])
            if span is None:
                continue
            s_lo, s_hi, rb = span
            r0 = i * HG_SB
            ref_g = g[rb:rb + 1]
            qd = (s[0][r0:r0 + HG_SB] * jnp.exp(g[r0:r0 + HG_SB] - ref_g)).astype(BF16)
            kd = (s[1][s_lo:s_hi] * jnp.exp(ref_g - g[s_lo:s_hi])).astype(BF16)
            a_offs[i, n] = lax.dot_general(qd, kd, NT_DIMS, preferred_element_type=F32)
    o_offs = {}
    for (i, n), a in a_offs.items():
        s_lo, s_hi, _ = outside(i, streams[n][4])
        o_offs[i, n] = jnp.dot(a.astype(BF16), streams[n][2][s_lo:s_hi].astype(BF16), preferred_element_type=F32)

    def rows_of(j, rev):
        half = HG_SB // 2
        if rev:
            return (0, half) if j < half else (0, HG_SB)
        return (half, HG_SB) if j >= half else (0, HG_SB)

    a_reps = {}
    for i in range(nsb):
        r0 = i * HG_SB
        for n, (g, s) in enumerate(zip(gs, streams)):
            qi, ki, fi = s[0][r0:r0 + HG_SB], s[1][r0:r0 + HG_SB], s[5][r0:r0 + HG_SB]
            ws, qe = {}, None
            order = range(HG_SB) if s[4] else range(HG_SB - 1, -1, -1)
            for j in order:
                t0, t1 = rows_of(j, s[4])
                here = t_idx[t0:t1] == j
                if qe is None:
                    qe = jnp.where(here, qi[t0:t1], 0.0)
                else:
                    if qe.shape[0] != t1 - t0:
                        pad8 = jnp.zeros_like(qe)
                        qe = jnp.concatenate([qe, pad8] if s[4] else [pad8, qe], axis=0)
                    step = fi[j - 1:j] if s[4] else fi[j + 1:j + 2]
                    qe = jnp.where(here, qi[t0:t1], qe * step)
                ws[j] = qe * ki[j:j + 1]
            a_reps[i, n] = jnp.dot(jnp.concatenate([ws[j] for j in range(HG_SB)], axis=0).astype(BF16), ones,
                                   preferred_element_type=F32)

    results = []
    for n, (g, s) in enumerate(zip(gs, streams)):
        q, k, v, lf, rev, f = s
        outs = []
        for i in range(nsb):
            r0 = i * HG_SB
            vi = v[r0:r0 + HG_SB]
            zero8 = jnp.zeros((HG_SB // 2, v.shape[1]), F32)
            lo_acc, hi_acc, at = zero8, zero8, 0
            for j in range(HG_SB):
                t0, t1 = rows_of(j, rev)
                term = a_reps[i, n][at:at + (t1 - t0)] * vi[j:j + 1]
                at += t1 - t0
                if t1 - t0 == HG_SB:
                    lo_acc, hi_acc = lo_acc + term[0:HG_SB // 2], hi_acc + term[HG_SB // 2:HG_SB]
                elif t0 == 0:
                    lo_acc = lo_acc + term
                else:
                    hi_acc = hi_acc + term
            oi = jnp.concatenate([lo_acc, hi_acc], axis=0)
            if (i, n) in o_offs:
                oi = oi + o_offs[i, n]
            outs.append(oi)
        kd = (k * jnp.exp(g_lasts[n] - g)).astype(BF16)
        gain = lax.dot_general(v.astype(BF16), kd, TN_DIMS, preferred_element_type=F32)
        results.append((jnp.concatenate(outs, axis=0), qgs[n], jnp.exp(g_lasts[n]), gain))
    return results


def _hgrn_advance(part, st):
    o_local, qg, decay, gain = part
    o = o_local + lax.dot_general(qg, st.astype(BF16), NT_DIMS, preferred_element_type=F32)
    return o, st * decay + gain


def _hgrn_kernel(*refs, layer, has_s0):
    it = iter(refs)
    hq_ref, zf_ref, zb_ref, hi_ref, hg_ref, lbp_ref, g_ref = (next(it) for _ in range(7))
    s0_ref = next(it) if has_s0 else None
    next(it)
    o_ref, sout_ref, st_ref, ob_ref = next(it), next(it), next(it), next(it)
    seq = hq_ref.shape[0]
    n = seq // HG_C
    z_refs = (zf_ref, zb_ref)

    lbs = []
    for hh in range(HG_HPS):
        for d in range(2):
            p = lbp_ref[d, :, hh]
            e = jnp.exp(p - jnp.max(p, axis=0))
            lbs.append(jnp.sum(e[0:layer + 1], axis=0) / jnp.sum(e, axis=0))
            if has_s0:
                st_ref[2 * hh + d] = s0_ref[d, hh].T
            else:
                st_ref[2 * hh + d] = jnp.zeros((HEAD_W, HG_DK), F32)

    def body(ci, carry):
        streams, dests = [], []
        for hh in range(HG_HPS):
            lanes = slice(hh * HEAD_W, (hh + 1) * HEAD_W)
            for d in range(2):
                for u in range(HG_UNROLL):
                    c = ci * HG_UNROLL + u
                    c = (n - 1 - c) if d == 1 else c
                    rows = pl.ds(pl.multiple_of(c * HG_C, HG_C), HG_C)
                    lb = lbs[2 * hh + d]
                    f = lb + (1.0 - lb) * _sigmoid(z_refs[d][rows, lanes])
                    streams.append((_silu(hq_ref[rows, lanes]), 1.0 - f, hi_ref[rows, lanes], jnp.log(f), d == 1, f))
                    dests.append((o_ref if d == 0 else ob_ref, rows, lanes))
        parts = _hgrn_chunks(streams)
        for slot in range(2 * HG_HPS):
            st = st_ref[slot]
            for u in range(HG_UNROLL):
                o, st = _hgrn_advance(parts[slot * HG_UNROLL + u], st)
                dst_ref, rows, lanes = dests[slot * HG_UNROLL + u]
                dst_ref[rows, lanes] = o
            st_ref[slot] = st
        return carry

    lax.fori_loop(0, n // HG_UNROLL, body, 0)
    for hh in range(HG_HPS):
        for d in range(2):
            sout_ref[d, hh] = st_ref[2 * hh + d].T

    def finish(ci, carry):
        rows = pl.ds(pl.multiple_of(ci * HG_FIN, HG_FIN), HG_FIN)
        for hh in range(HG_HPS):
            lanes = slice(hh * HEAD_W, (hh + 1) * HEAD_W)
            tot = o_ref[rows, lanes] + ob_ref[rows, lanes]
            o_ref[rows, lanes] = _rms(tot, g_ref[...]) * _silu(hg_ref[rows, lanes])
        return carry

    lax.fori_loop(0, seq // HG_FIN, finish, 0)


def _hgrn(proj, mix, hg_lb_l, hg_norm_g, *, row0, n_batch, seq, layer, s0=None):
    has_s0 = s0 is not None
    rb0 = row0 // seq
    hpg = HG_HEADS // HG_HPS
    width = HG_HPS * HEAD_W
    col0 = 3 * DA_HEADS * HEAD_W // width
    once = pl.Buffered(1) if seq * width * 4 > HG_DOUBLE_BUFFER_BYTES else None

    def col_spec(j, mode=None):
        return pl.BlockSpec((seq, width), lambda b, h: (rb0 + b, col0 + j * hpg + h), pipeline_mode=mode)

    in_specs = [col_spec(0), col_spec(1), col_spec(2), col_spec(3), col_spec(4, once),
                pl.BlockSpec((2, DEPTH + 1, HG_HPS, 1, HG_DK), lambda b, h: (0, 0, h, 0, 0)),
                pl.BlockSpec((1, HEAD_W), lambda b, h: (0, 0))]
    args = [proj, proj, proj, proj, proj, hg_lb_l, hg_norm_g]
    state_spec = pl.BlockSpec((None, 2, HG_HPS, HG_DK, HEAD_W), lambda b, h: (b, 0, h, 0, 0))
    if has_s0:
        in_specs.append(state_spec)
        args.append(s0)
    in_specs.append(pl.BlockSpec(memory_space=pl.ANY))
    args.append(mix)
    return pl.pallas_call(
        functools.partial(_hgrn_kernel, layer=layer, has_s0=has_s0),
        out_shape=(jax.ShapeDtypeStruct(mix.shape, mix.dtype),
                   jax.ShapeDtypeStruct((n_batch, 2, HG_HEADS, HG_DK, HEAD_W), F32)),
        grid=(n_batch, hpg),
        in_specs=in_specs,
        out_specs=(pl.BlockSpec((seq, width), lambda b, h: (rb0 + b, DA_HEADS * HEAD_W // width + h)),
                   state_spec),
        scratch_shapes=[pltpu.VMEM((2 * HG_HPS, HEAD_W, HG_DK), F32), pltpu.VMEM((seq, width), F32)],
        input_output_aliases={len(args) - 1: 0},
        compiler_params=_params(("arbitrary", "arbitrary")),
        name="hgrn2_state" if has_s0 else "hgrn2",
    )(*args)


def _odd_kernel(x_ref, m_ref, g_ref, win_ref, ws_ref, sb_ref, o_ref):
    h = _norm_mod(x_ref[...], g_ref[...], m_ref[0], m_ref[1])
    z = jnp.dot(h.astype(BF16), win_ref[...], preferred_element_type=F32)
    z = 0.5 * z * (1.0 + lax.erf(z * (2.0 ** -0.5)))
    width = z.shape[1] // 2
    u, v = z[:, 0:width], z[:, width:2 * width]
    mu = jnp.mean(v, axis=-1, keepdims=True)
    vc = v - mu
    var = jnp.mean(vc * vc, axis=-1, keepdims=True)
    vn = (vc * lax.rsqrt(var + EPS)).astype(BF16)
    gd = width // CM_GROUPS
    for r in range(x_ref.shape[0] // CM_CHUNK):
        rows = slice(r * CM_CHUNK, (r + 1) * CM_CHUNK)
        for gi in range(CM_GROUPS):
            cols = slice(gi * gd, (gi + 1) * gd)
            sv = jnp.dot(ws_ref[gi], vn[rows, cols], preferred_element_type=F32) + sb_ref[:, gi:gi + 1]
            o_ref[rows, cols] = u[rows, cols] * sv


def _odd_mixer(x, m, g, w_in, sgu_w, sgu_bt, mod_spec):
    nt = x.shape[0]
    return pl.pallas_call(
        _odd_kernel,
        out_shape=jax.ShapeDtypeStruct((nt, D_MODEL), F32),
        grid=(nt // TD,),
        in_specs=[
            pl.BlockSpec((TD, D_MODEL), lambda i: (i, 0)),
            mod_spec,
            pl.BlockSpec((1, D_MODEL), lambda i: (0, 0)),
            pl.BlockSpec(w_in.shape, lambda i: (0, 0)),
            pl.BlockSpec(sgu_w.shape, lambda i: (0, 0, 0)),
            pl.BlockSpec(sgu_bt.shape, lambda i: (0, 0)),
        ],
        out_specs=pl.BlockSpec((TD, D_MODEL), lambda i: (i, 0)),
        compiler_params=_params(("arbitrary",)),
        name="odd_mixer",
    )(x, m, g, w_in, sgu_w, sgu_bt)


def _post_kernel(xa_ref, xb_ref, mix_ref, m_ref, g_ref, w_ref, rwt_ref, rb_ref, xo_ref, h2_ref, lg_ref, *, npt):
    mo = jnp.dot(mix_ref[...].astype(BF16), w_ref[...], preferred_element_type=F32)
    xn = _pick_x(xa_ref, xb_ref, npt) + m_ref[2] * mo
    xo_ref[...] = xn
    h2 = _norm_mod(xn, g_ref[...], m_ref[3], m_ref[4])
    h2_ref[...] = h2
    hh = h2.astype(BF16)
    hl = (h2 - hh.astype(F32)).astype(BF16)
    rw = rwt_ref[...]
    rh = rw.astype(BF16)
    rl = (rw - rh.astype(F32)).astype(BF16)
    lg = (lax.dot_general(rh, hh, NT_DIMS, preferred_element_type=F32)
          + lax.dot_general(rh, hl, NT_DIMS, preferred_element_type=F32)
          + lax.dot_general(rl, hh, NT_DIMS, preferred_element_type=F32))
    lg_ref[...] = lg + rb_ref[...]


def _post(x, n_prompt, mix, m, g2, w_out, router_wt, router_b, mod_spec):
    x_specs, x_args, npt, nt = _x_parts(x, n_prompt)
    row = lambda i: (i, 0)
    return pl.pallas_call(
        functools.partial(_post_kernel, npt=npt),
        out_shape=(jax.ShapeDtypeStruct((nt, D_MODEL), F32),
                   jax.ShapeDtypeStruct((nt, D_MODEL), F32),
                   jax.ShapeDtypeStruct((N_EXPERTS, nt), F32)),
        grid=(nt // TD,),
        in_specs=x_specs + [
            pl.BlockSpec((TD, D_MODEL), row),
            mod_spec,
            pl.BlockSpec((1, D_MODEL), lambda i: (0, 0)),
            pl.BlockSpec((D_MODEL, D_MODEL), lambda i: (0, 0)),
            pl.BlockSpec((N_EXPERTS, D_MODEL), lambda i: (0, 0)),
            pl.BlockSpec((N_EXPERTS, 1), lambda i: (0, 0)),
        ],
        out_specs=(pl.BlockSpec((TD, D_MODEL), row), pl.BlockSpec((TD, D_MODEL), row),
                   pl.BlockSpec((N_EXPERTS, TD), lambda i: (0, i))),
        compiler_params=_params(("arbitrary",)),
        name="mixer_out_router",
    )(*x_args, mix, m, g2, w_out, router_wt, router_b)


def _experts_on_lanes(col):
    on_lane = (lax.broadcasted_iota(I32, (N_EXPERTS, HEAD_W), 0) == lax.broadcasted_iota(I32, (N_EXPERTS, HEAD_W), 1))
    return jnp.sum(jnp.where(on_lane, col, 0.0), axis=0, keepdims=True)


def _route_a_kernel(lg_ref, slot_ref, gt_ref, tab_ref, cnt_ref, carry_ref):
    i = pl.program_id(0)

    @pl.when(i == 0)
    def _():
        carry_ref[...] = jnp.zeros(carry_ref.shape, F32)

    l = lg_ref[...]
    tn = l.shape[1]
    eio = lax.broadcasted_iota(I32, l.shape, 0)
    vals, idxs = [], []
    for _ in range(TOP_K):
        mk = jnp.max(l, axis=0, keepdims=True)
        ik = jnp.min(jnp.where(l == mk, eio, N_EXPERTS), axis=0, keepdims=True)
        vals.append(mk)
        idxs.append(ik)
        l = jnp.where(eio == ik, -jnp.inf, l)
    es = [jnp.exp(v - vals[0]) for v in vals]
    den = es[0] + es[1] + es[2] + es[3]
    sel = jnp.zeros(l.shape, F32)
    for ik in idxs:
        sel = sel + jnp.where(eio == ik, 1.0, 0.0)
    upper = jnp.where(lax.broadcasted_iota(I32, (tn, tn), 0) < lax.broadcasted_iota(I32, (tn, tn), 1),
                      1.0, 0.0).astype(BF16)
    before = jnp.dot(sel.astype(BF16), upper, preferred_element_type=F32)

    tile_cnt = jnp.sum(sel, axis=1, keepdims=True)
    c_row, base_row = _experts_on_lanes(tile_cnt), _experts_on_lanes(carry_ref[...])
    sub = lax.broadcasted_iota(I32, (N_EXPERTS, HEAD_W), 0)
    lane_e = lax.broadcasted_iota(I32, (N_EXPERTS, HEAD_W), 1)
    r_row = base_row - 8.0 * jnp.floor(base_row / 8.0)
    seg_row = 8.0 * jnp.floor((r_row + c_row + 7.0) / 8.0)
    seg_col = jnp.sum(jnp.where(sub == lane_e, seg_row, 0.0), axis=1, keepdims=True)
    r_col = jnp.sum(jnp.where(sub == lane_e, r_row, 0.0), axis=1, keepdims=True)
    loff_row = jnp.sum(jnp.where(sub < lane_e, seg_col, 0.0), axis=0, keepdims=True)
    loff_col = jnp.sum(jnp.where(lane_e < sub, seg_row, 0.0), axis=1, keepdims=True)
    first_slot = loff_col + r_col
    slots = []
    for k in range(TOP_K):
        here = eio == idxs[k]
        slots.append(jnp.sum(jnp.where(here, first_slot + before, 0.0), axis=0, keepdims=True))
        slot_ref[k:k + 1, :] = slots[k].astype(I32)
    slot_ref[TOP_K:DEST_ROWS, :] = jnp.zeros((DEST_ROWS - TOP_K, tn), I32)
    gp = jnp.concatenate([e / den for e in es] + slots + [jnp.zeros((HEAD_W - 2 * TOP_K, tn), F32)], axis=0)
    gt_ref[...] = gp.T
    tab_ref[0:1, :] = c_row.astype(I32)
    tab_ref[1:2, :] = loff_row.astype(I32)
    tab_ref[2:3, :] = base_row.astype(I32)
    tab_ref[3:DEST_ROWS, :] = jnp.zeros((DEST_ROWS - 3, HEAD_W), I32)
    carry_ref[...] = carry_ref[...] + tile_cnt
    cnt_ref[...] = carry_ref[...]


def _route_plan_kernel(cnt_ref, be_ref, nv_ref, bv_ref, pad_ref, *, n_blk_pad, n_rows):
    cnt = cnt_ref[...]
    padded = jnp.floor((cnt + (MOE_BM - 1)) / MOE_BM) * MOE_BM
    r = lax.broadcasted_iota(I32, (N_EXPERTS, N_EXPERTS), 0)
    c = lax.broadcasted_iota(I32, (N_EXPERTS, N_EXPERTS), 1)
    padded_row = jnp.sum(jnp.where(r == c, padded, 0.0), axis=0, keepdims=True)
    pend = jnp.sum(jnp.where(c <= r, padded_row, 0.0), axis=1, keepdims=True)
    pstart = pend - padded
    blk_start = (lax.broadcasted_iota(I32, (N_EXPERTS, n_blk_pad), 1) * MOE_BM).astype(F32)
    be = jnp.sum(jnp.where(pend <= blk_start, 1.0, 0.0), axis=0, keepdims=True)
    be = jnp.minimum(be, N_EXPERTS - 1.0)
    be_ref[...] = be.astype(I32)
    owner = lax.broadcasted_iota(I32, (N_EXPERTS, n_blk_pad), 0).astype(F32) == be
    last_row = jnp.sum(jnp.where(owner, pstart + cnt, 0.0), axis=0, keepdims=True)
    bv_ref[...] = jnp.clip(last_row - blk_start[0:1, :], 0.0, float(MOE_BM)).astype(I32)
    total = jnp.sum(padded, axis=0, keepdims=True)
    nv_ref[...] = jnp.broadcast_to(total / MOE_BM, nv_ref.shape).astype(I32)
    on_lane = (lax.broadcasted_iota(I32, (N_EXPERTS, HEAD_W), 0) == lax.broadcasted_iota(I32, (N_EXPERTS, HEAD_W), 1))
    lane = lax.broadcasted_iota(I32, (1, HEAD_W), 1)
    pad_lo = jnp.sum(jnp.where(on_lane, pstart + cnt, 0.0), axis=0, keepdims=True)
    pad_hi = jnp.sum(jnp.where(on_lane, pend, 0.0), axis=0, keepdims=True)
    pad_ref[0:1, :] = jnp.where(lane == N_EXPERTS, total, pad_lo).astype(I32)
    pad_ref[1:2, :] = jnp.where(lane == N_EXPERTS, float(n_rows), pad_hi).astype(I32)
    pad_ref[2:3, :] = jnp.sum(jnp.where(on_lane, pstart, 0.0), axis=0, keepdims=True).astype(I32)
    pad_ref[3:DEST_ROWS, :] = jnp.zeros((DEST_ROWS - 3, HEAD_W), I32)


def _route(logits_t, n_blk_pad, n_rows):
    nt = logits_t.shape[1]
    tn = TM
    blk = lambda i: (0, i)
    tile = lambda i: (i, 0)
    whole = lambda i: (0, 0)
    n_tiles = nt // tn
    slot, gate_t, tab, cnt = pl.pallas_call(
        _route_a_kernel,
        out_shape=(jax.ShapeDtypeStruct((n_tiles * DEST_ROWS, tn), I32), jax.ShapeDtypeStruct((nt, HEAD_W), F32),
                   jax.ShapeDtypeStruct((n_tiles * DEST_ROWS, HEAD_W), I32),
                   jax.ShapeDtypeStruct((N_EXPERTS, 1), F32)),
        grid=(n_tiles,),
        in_specs=[pl.BlockSpec((N_EXPERTS, tn), blk)],
        out_specs=(pl.BlockSpec((DEST_ROWS, tn), tile), pl.BlockSpec((tn, HEAD_W), tile),
                   pl.BlockSpec((DEST_ROWS, HEAD_W), tile), pl.BlockSpec((N_EXPERTS, 1), whole)),
        scratch_shapes=[pltpu.VMEM((N_EXPERTS, 1), F32)],
        compiler_params=_params(("arbitrary",)),
        name="route_topk_slots",
    )(logits_t)
    blk_exp, n_valid, blk_rows, pad = pl.pallas_call(
        functools.partial(_route_plan_kernel, n_blk_pad=n_blk_pad, n_rows=n_rows),
        out_shape=(jax.ShapeDtypeStruct((1, n_blk_pad), I32), jax.ShapeDtypeStruct((1, HEAD_W), I32),
                   jax.ShapeDtypeStruct((1, n_blk_pad), I32), jax.ShapeDtypeStruct((DEST_ROWS, HEAD_W), I32)),
        grid=(1,),
        in_specs=[pl.BlockSpec((N_EXPERTS, 1), whole)],
        out_specs=(pl.BlockSpec((1, n_blk_pad), whole), pl.BlockSpec((1, HEAD_W), whole),
                   pl.BlockSpec((1, n_blk_pad), whole), pl.BlockSpec((DEST_ROWS, HEAD_W), whole)),
        compiler_params=_params(("arbitrary",)),
        name="route_plan",
    )(cnt)
    return slot, gate_t, tab, (blk_exp.reshape(n_blk_pad), n_valid[0, 0:1], blk_rows.reshape(n_blk_pad)), pad


def _for_each_chunk(n8, fn):
    for bi, b in enumerate(GROUP_BITS):
        @pl.when((n8 & b) != 0)
        def _(bi=bi, b=b):
            first = pl.multiple_of((n8 - (n8 & (2 * b - 1))) * 8, 8)
            fn(first, 8 * b, bi % 2)


def _wait_rows(n8, make_copy):
    for b in WAIT_BITS:
        @pl.when((n8 & b) != 0)
        def _(b=b):
            make_copy(8 * b).wait()


def _dispatch_kernel(h_ref, slot_ref, tab_ref, pad_ref, xs_ref, tsm_ref, psm_ref, sent_ref, xl2_ref, part_ref,
                     zero_ref, sem_tab, sem_pad, sem_rows2):
    i = pl.program_id(0)
    last = pl.num_programs(0) - 1
    tn = h_ref.shape[0]
    buf = i % 2
    xl_ref, sem_rows = xl2_ref.at[buf], sem_rows2.at[buf]
    cpt = pltpu.make_async_copy(tab_ref, tsm_ref, sem_tab)
    cpt.start()

    def out_copy(src_ref, src0, dst0, rows, sem=sem_rows):
        return pltpu.make_async_copy(src_ref.at[pl.ds(src0, rows)], xs_ref.at[pl.ds(dst0, rows)], sem)

    @pl.when(i == 0)
    def _first_step():
        part_ref[...] = jnp.zeros(part_ref.shape, F32)
        zero_ref[...] = jnp.zeros(zero_ref.shape, F32)
        cpp = pltpu.make_async_copy(pad_ref, psm_ref, sem_pad)
        cpp.start()
        cpp.wait()

        def per_span(e, carry):
            lo8, hi8 = (psm_ref[0, e] + 7) >> 3, psm_ref[1, e] >> 3

            def zero_copy(u):
                return out_copy(zero_ref, 0, pl.multiple_of(u * 8, 8), 8)

            lax.fori_loop(lo8, hi8, lambda u, c: (zero_copy(u).start(), c)[1], 0)
            lax.fori_loop(lo8, hi8, lambda u, c: (zero_copy(u).wait(), c)[1], 0)
            return carry

        lax.fori_loop(0, N_EXPERTS + 1, per_span, 0)

    slot = lax.broadcasted_iota(I32, (L_ROWS, tn), 0)
    sl = slot_ref[...]
    hit = jnp.where(slot == sl[0:1], 1.0, jnp.where(slot == sl[1:2], 1.0, jnp.where(
        slot == sl[2:3], 1.0, jnp.where(slot == sl[3:4], 1.0, 0.0))))
    xl_ref[...] = jnp.dot(hit.astype(BF16), h_ref[...].astype(BF16), preferred_element_type=F32)
    cpt.wait()
    sub8 = lax.broadcasted_iota(I32, (8, 1), 0)

    def per_expert(e, sent8):
        cnt, lo, g0 = tsm_ref[0, e], pl.multiple_of(tsm_ref[1, e], 8), psm_ref[2, e] + tsm_ref[2, e]
        r = g0 & 7
        g_tile = pl.multiple_of(g0 - r, 8)
        head = pl.ds(lo, 8)
        xl_ref[head, :] = xl_ref[head, :] + part_ref[e]
        full8, rem = (r + cnt) >> 3, (r + cnt) & 7
        _for_each_chunk(full8, lambda first, rows, prio: out_copy(
            xl_ref, pl.multiple_of(lo + first, 8), pl.multiple_of(g_tile + first, 8), rows).start(priority=prio))
        tail = xl_ref[pl.ds(pl.multiple_of(lo + full8 * 8, 8), 8), :]
        part_ref[e] = jnp.where(sub8 < rem, tail, 0.0)
        return sent8 + full8

    sent8 = lax.fori_loop(0, N_EXPERTS, per_expert, 0)

    sent_ref[buf] = sent8

    @pl.when(i > 0)
    def _previous_step_copies():
        _wait_rows(sent_ref[1 - buf], lambda rows: out_copy(xl_ref, 0, 0, rows, sem_rows2.at[1 - buf]))

    @pl.when(i == last)
    def _last_step():
        def flush(e, n8):
            end = psm_ref[0, e]
            due = (end & 7) != 0

            @pl.when(due)
            def _():
                pltpu.make_async_copy(part_ref.at[e], xs_ref.at[pl.ds(pl.multiple_of(end - (end & 7), 8), 8)],
                                      sem_rows).start()

            return n8 + jnp.where(due, 1, 0)

        _wait_rows(lax.fori_loop(0, N_EXPERTS, flush, sent8), lambda rows: out_copy(xl_ref, 0, 0, rows))


def _dispatch(h2, slot, tab, pad, n_rows):
    nt = h2.shape[0]
    tile = lambda i: (i, 0)
    return pl.pallas_call(
        _dispatch_kernel,
        out_shape=jax.ShapeDtypeStruct((n_rows, D_MODEL), F32),
        grid=(nt // TM,),
        in_specs=[pl.BlockSpec((TM, D_MODEL), tile),
                  pl.BlockSpec((DEST_ROWS, TM), tile),
                  pl.BlockSpec((DEST_ROWS, HEAD_W), tile),
                  pl.BlockSpec((DEST_ROWS, HEAD_W), lambda i: (0, 0))],
        out_specs=pl.BlockSpec(memory_space=pl.ANY),
        scratch_shapes=[pltpu.SMEM((DEST_ROWS, HEAD_W), I32), pltpu.SMEM((DEST_ROWS, HEAD_W), I32),
                        pltpu.SMEM((2,), I32),
                        pltpu.VMEM((2, L_ROWS, D_MODEL), F32), pltpu.VMEM((N_EXPERTS, 8, D_MODEL), F32),
                        pltpu.VMEM((8, D_MODEL), F32),
                        pltpu.SemaphoreType.DMA, pltpu.SemaphoreType.DMA, pltpu.SemaphoreType.DMA((2,))],
        compiler_params=_params(("arbitrary",)),
        name="moe_dispatch",
    )(h2, slot, tab, pad)


def _expert_kernel(be_ref, nv_ref, rows_ref, xs_ref, w1_ref, b1_ref, w2_ref, b2_ref, ys_ref, w1b_ref, w2b_ref):
    i = pl.program_id(0)
    valid = i < nv_ref[0]
    half_rows = xs_ref.shape[0] // 2
    needs_all = rows_ref[i] > half_rows
    prev = be_ref[jnp.maximum(i - 1, 0)]
    changed = jnp.logical_or(i == 0, be_ref[i] != prev)

    @pl.when(jnp.logical_and(valid, changed))
    def _cast_weights():
        step = 128
        def cast(j, carry):
            rows = pl.ds(pl.multiple_of(j * step, step), step)
            w1b_ref[rows, :] = w1_ref[rows, :].astype(BF16)
            w2b_ref[rows, :] = w2_ref[rows, :].astype(BF16)
            return carry
        lax.fori_loop(0, w1_ref.shape[0] // step, cast, 0)

    @pl.when(jnp.logical_not(valid))
    def _unused_block():
        ys_ref[...] = jnp.zeros(ys_ref.shape, F32)

    def mlp(rows):
        h = jnp.dot(xs_ref[0:rows, :].astype(BF16), w1b_ref[...], preferred_element_type=F32) + b1_ref[...]
        half = h.shape[1] // 2
        a = jnp.minimum(h[:, 0:half], SWIGLU_LIMIT)
        lin = jnp.clip(h[:, half:2 * half], -SWIGLU_LIMIT, SWIGLU_LIMIT)
        act = a * _sigmoid(SWIGLU_ALPHA * a) * (lin + 1.0)
        ys_ref[0:rows, :] = jnp.dot(act.astype(BF16), w2b_ref[...], preferred_element_type=F32) + b2_ref[...]

    @pl.when(jnp.logical_and(valid, needs_all))
    def _whole_block():
        mlp(2 * half_rows)

    @pl.when(jnp.logical_and(valid, jnp.logical_not(needs_all)))
    def _first_half():
        mlp(half_rows)
        ys_ref[half_rows:2 * half_rows, :] = jnp.zeros((half_rows, ys_ref.shape[1]), F32)


def _experts(xs, plan, w1, b1, w2, b2, layer):
    n_rows = xs.shape[0]
    n_blk = n_rows // MOE_BM
    depth, d_e2 = w1.shape[0], w1.shape[3]

    def blk(i, be, nv):
        return jnp.minimum(i, nv[0] - 1)

    def expert(i, be, nv, rows):
        return (layer, be[blk(i, be, nv)], 0, 0)

    grid_spec = pltpu.PrefetchScalarGridSpec(
        num_scalar_prefetch=3,
        grid=(n_blk,),
        in_specs=[
            pl.BlockSpec((MOE_BM, D_MODEL), lambda i, be, nv, rows: (blk(i, be, nv), 0)),
            pl.BlockSpec((None, None, D_MODEL, d_e2), expert),
            pl.BlockSpec((None, None, 1, d_e2), expert),
            pl.BlockSpec((None, None, d_e2 // 2, D_MODEL), expert),
            pl.BlockSpec((None, None, 1, D_MODEL), expert),
        ],
        out_specs=pl.BlockSpec((MOE_BM, D_MODEL), lambda i, be, nv, rows: (i, 0)),
        scratch_shapes=[pltpu.VMEM((D_MODEL, d_e2), BF16), pltpu.VMEM((d_e2 // 2, D_MODEL), BF16)],
    )
    return pl.pallas_call(
        _expert_kernel,
        out_shape=jax.ShapeDtypeStruct((n_rows, D_MODEL), F32),
        grid_spec=grid_spec,
        compiler_params=_params(("arbitrary",)),
        name="moe_experts",
    )(*plan, xs, w1, b1.reshape(depth, N_EXPERTS, 1, d_e2), w2, b2.reshape(depth, N_EXPERTS, 1, D_MODEL))


def _combine_kernel(x_ref, gt_ref, tab_ref, tab_next_ref, pad_ref, m_ref, fg_ref, ys_ref, *out_and_scratch, npt):
    *o_ref, tsm_ref, psm_ref, got_ref, yl2_ref, sem_tab, sem_rows2 = out_and_scratch
    o_ref = o_ref[0] if npt is None else o_ref
    i = pl.program_id(0)
    tn = x_ref.shape[0]
    buf = i % 2

    def in_copy(b, src0, dst0, rows):
        return pltpu.make_async_copy(ys_ref.at[pl.ds(src0, rows)], yl2_ref.at[b, pl.ds(dst0, rows)],
                                     sem_rows2.at[b])

    def fetch(table_ref, b):
        cpt = pltpu.make_async_copy(table_ref, tsm_ref, sem_tab)
        cpt.start()
        cpt.wait()

        def per_expert(e, got8):
            cnt, lo, g0 = tsm_ref[0, e], pl.multiple_of(tsm_ref[1, e], 8), psm_ref[2, e] + tsm_ref[2, e]
            r = g0 & 7
            g_tile = pl.multiple_of(g0 - r, 8)
            cover8 = jnp.where(cnt > 0, (r + cnt + 7) >> 3, 0)
            _for_each_chunk(cover8, lambda first, rows, prio: in_copy(
                b, pl.multiple_of(g_tile + first, 8), pl.multiple_of(lo + first, 8), rows).start(priority=prio))
            return got8 + cover8

        got_ref[b] = lax.fori_loop(0, N_EXPERTS, per_expert, 0)

    @pl.when(i == 0)
    def _first_step():
        yl2_ref[...] = jnp.zeros(yl2_ref.shape, F32)
        cpp = pltpu.make_async_copy(pad_ref, psm_ref, sem_tab)
        cpp.start()
        cpp.wait()
        fetch(tab_ref, 0)

    @pl.when(i + 1 < pl.num_programs(0))
    def _next_tile():
        fetch(tab_next_ref, 1 - buf)

    gt = gt_ref[...]
    slot = lax.broadcasted_iota(I32, (tn, L_ROWS), 1).astype(F32)
    pick = jnp.where(slot == gt[:, TOP_K:TOP_K + 1], gt[:, 0:1], 0.0)
    for k in range(1, TOP_K):
        pick = pick + jnp.where(slot == gt[:, TOP_K + k:TOP_K + k + 1], gt[:, k:k + 1], 0.0)
    _wait_rows(got_ref[buf], lambda rows: in_copy(buf, 0, 0, rows))
    acc = jnp.dot(pick.astype(BF16), yl2_ref[buf].astype(BF16), preferred_element_type=F32)
    xn = x_ref[...] + m_ref[5] * acc
    if npt is None:
        o_ref[...] = xn
    else:
        xn = _rms(xn, fg_ref[...])
        op_ref, os_ref = o_ref

        @pl.when(i < npt)
        def _():
            op_ref[...] = xn

        @pl.when(i >= npt)
        def _():
            os_ref[...] = xn


def _combine(x, gate_t, tab, pad, m, final_g, ys, mod_spec, n_prompt, final):
    nt = x.shape[0]
    tile = lambda i: (i, 0)
    npt = n_prompt // TM
    if final:
        out_shape = (jax.ShapeDtypeStruct((n_prompt, D_MODEL), F32), jax.ShapeDtypeStruct((nt - n_prompt, D_MODEL), F32))
        out_specs = (pl.BlockSpec((TM, D_MODEL), lambda i: (jnp.minimum(i, npt - 1), 0)),
                     pl.BlockSpec((TM, D_MODEL), lambda i: (jnp.maximum(i - npt, 0), 0)))
    else:
        out_shape, out_specs = jax.ShapeDtypeStruct((nt, D_MODEL), F32), pl.BlockSpec((TM, D_MODEL), tile)
    return pl.pallas_call(
        functools.partial(_combine_kernel, npt=npt if final else None),
        out_shape=out_shape,
        grid=(nt // TM,),
        in_specs=[pl.BlockSpec((TM, D_MODEL), tile),
                  pl.BlockSpec((TM, HEAD_W), tile),
                  pl.BlockSpec((DEST_ROWS, HEAD_W), tile),
                  pl.BlockSpec((DEST_ROWS, HEAD_W), lambda i: (jnp.minimum(i + 1, nt // TM - 1), 0)),
                  pl.BlockSpec((DEST_ROWS, HEAD_W), lambda i: (0, 0)),
                  mod_spec,
                  pl.BlockSpec((1, D_MODEL), lambda i: (0, 0)),
                  pl.BlockSpec(memory_space=pl.ANY)],
        out_specs=out_specs,
        scratch_shapes=[pltpu.SMEM((DEST_ROWS, HEAD_W), I32), pltpu.SMEM((DEST_ROWS, HEAD_W), I32),
                        pltpu.SMEM((2,), I32), pltpu.VMEM((2, L_ROWS, D_MODEL), F32),
                        pltpu.SemaphoreType.DMA, pltpu.SemaphoreType.DMA((2,))],
        compiler_params=_params(("arbitrary",)),
        name="moe_combine_final" if final else "moe_combine",
    )(x, gate_t, tab, tab, pad, m, final_g, ys)


def _moe(x, h2, logits_t, m, final_g, w1, b1, w2, b2, mod_spec, n_prompt, layer, final):
    nt = x.shape[0]
    n_blk = nt * TOP_K // MOE_BM + N_EXPERTS
    n_blk_pad = -(-n_blk // HEAD_W) * HEAD_W
    slot, gate_t, tab, plan, pad = _route(logits_t, n_blk_pad, n_blk * MOE_BM)
    xs = _dispatch(h2, slot, tab, pad, n_blk * MOE_BM)
    ys = _experts(xs, plan, w1, b1, w2, b2, layer)
    return _combine(x, gate_t, tab, pad, m, final_g, ys, mod_spec, n_prompt, final)


def _rope_tables(seq):
    t = np.arange(seq)
    inv = (1.0 / (np.float32(ROPE_THETA) ** (np.arange(ROPE_HALF, dtype=np.float32) / np.float32(ROPE_HALF))))
    inv = inv.astype(np.float32)
    ang_r = (t // GRID_W).astype(np.float32)[:, None] * inv
    ang_c = (t % GRID_W).astype(np.float32)[:, None] * inv
    cr, sr, cc, sc = np.cos(ang_r), np.sin(ang_r), np.cos(ang_c), np.sin(ang_c)
    cos = np.concatenate([cr, cr, cc, cc] * 2, axis=1).astype(np.float32)
    sin = np.concatenate([-sr, sr, -sc, sc] * 2, axis=1).astype(np.float32)
    return jnp.asarray(cos), jnp.asarray(sin)


def kernel(x_prompt, x_sample, c, cache_k, cache_v, state_hgrn, c_ctx, mod_w, mod_b, norm_g, final_norm_g,
           w_in_even, w_out_even, da_lambda, da_subln_g, hg_norm_g, hg_lb, w_in_odd, sgu_w, sgu_b, w_out_odd,
           router_w, router_b, ex_w1, ex_b1, ex_w2, ex_b2):
    bp, lp, _ = x_prompt.shape
    bs, ls, _ = x_sample.shape
    n_p, n_s = bp * lp, bs * ls
    assert lp % TM == 0 and ls % TD == 0 and n_p % TD == 0 and n_p % ls == 0 and bs < MOD_ROWS
    depth = mod_w.shape[0]

    x = (x_prompt.reshape(n_p, D_MODEL), x_sample.reshape(n_s, D_MODEL))
    cond = jnp.zeros((MOD_ROWS, D_MODEL), F32).at[0:bs].set(c).at[bs].set(c_ctx)
    mod = _modulation(cond, mod_w, mod_b)
    mod_spec = _mod_spec(n_p, ls, bs, TM)
    mod_spec_d = _mod_spec(n_p, ls, bs, TD)
    rope_tabs = _rope_tables(ls)
    hg_lb_h = hg_lb.reshape(2, depth + 1, HG_HEADS, 1, HG_DK)
    final_g = final_norm_g.reshape(1, D_MODEL)

    new_k, new_v, new_s = [], [], []
    for l in range(depth):
        m = mod[l]
        g1 = norm_g[l, 0].reshape(1, D_MODEL)
        g2 = norm_g[l, 1].reshape(1, D_MODEL)
        if l % 2 == 0:
            e = l // 2
            lam_init = 0.8 - 0.6 * math.exp(-0.3 * l)
            proj, mix = _inproj(x, n_p, m, g1, w_in_even[e].astype(BF16), mod_spec_d)
            subln = da_subln_g[e].reshape(1, HEAD_W)
            hgn = hg_norm_g[e].reshape(1, HEAD_W)
            mix, k_new, v_new = _diff_attention(proj, mix, da_lambda[e], subln, row0=0, n_batch=bp, seq=lp,
                                                lam_init=lam_init)
            mix = _diff_attention(proj, mix, da_lambda[e], subln, row0=n_p, n_batch=bs, seq=ls, lam_init=lam_init,
                                  rope_tabs=rope_tabs, ctx_k=cache_k, ctx_v=cache_v, layer_e=e)
            mix, s_p = _hgrn(proj, mix, hg_lb_h, hgn, row0=0, n_batch=bp, seq=lp, layer=l)
            mix, _ = _hgrn(proj, mix, hg_lb_h, hgn, row0=n_p, n_batch=bs, seq=ls, layer=l, s0=state_hgrn[:, e])
            w_out = w_out_even[e].astype(BF16)
            new_k.append(k_new)
            new_v.append(v_new)
            new_s.append(s_p)
        else:
            o = l // 2
            mix = _odd_mixer(x, m, g1, w_in_odd[o].astype(BF16), sgu_w[o].astype(BF16), sgu_b[o].T, mod_spec_d)
            w_out = w_out_odd[o].astype(BF16)
        x, h2, logits_t = _post(x, n_p, mix, m, g2, w_out, router_w[l].T, router_b[l].reshape(N_EXPERTS, 1),
                                mod_spec_d)
        x = _moe(x, h2, logits_t, m, final_g, ex_w1, ex_b1, ex_w2, ex_b2, mod_spec, n_p, layer=l,
                 final=(l == depth - 1))

    y_prompt, y_sample = x
    return (y_prompt.reshape(bp, lp, D_MODEL), y_sample.reshape(bs, ls, D_MODEL),
            jnp.stack(new_k, axis=1), jnp.stack(new_v, axis=1), jnp.stack(new_s, axis=1))
```

```python
import functools
import math

import jax
import jax.numpy as jnp
import numpy as np
from jax import lax
from jax.experimental import pallas as pl
from jax.experimental.pallas import tpu as pltpu

F32 = jnp.float32
BF16 = jnp.bfloat16
I32 = jnp.int32

D_MODEL = 1024
DEPTH = 2
GRID_W = 64
DA_HEADS = 4
DA_HD = 64
HG_HEADS = 4
HG_DK = 128
ROPE_THETA = 10000.0
ROPE_HALF = DA_HD // 4
CM_CHUNK = 128
CM_GROUPS = 8
N_EXPERTS = 32
TOP_K = 4
SWIGLU_ALPHA = 1.702
SWIGLU_LIMIT = 7.0
EPS = 1e-6
HEAD_W = 128

TM = 256
TD = 512
ATTN_SUB = 2
ATTN_KEY_CHUNK = 2048
HG_C = 64
HG_SB = 16
HG_HPS = 2
HG_UNROLL = 4
HG_FIN = 256
HG_DOUBLE_BUFFER_BYTES = 2 * 1024 * 1024
V7X_TILE_ROWS = 8
V7X_LANES = 128
L_ROWS = -(-(TOP_K * TM + N_EXPERTS * 2 * (V7X_TILE_ROWS - 1)) // V7X_LANES) * V7X_LANES
GROUP_BITS = (32, 16, 8, 4, 2, 1)
WAIT_BITS = (128, 64, 32, 16, 8, 4, 2, 1)
MOE_BM = 512
DEST_ROWS = V7X_TILE_ROWS
MOD_ROWS = 16
V7X_VMEM_LIMIT = 56 * 1024 * 1024

NT_DIMS = (((1,), (1,)), ((), ()))
TN_DIMS = (((0,), (0,)), ((), ()))


def _params(sem):
    return pltpu.CompilerParams(dimension_semantics=sem, vmem_limit_bytes=V7X_VMEM_LIMIT)


def _sigmoid(x):
    return 1.0 / (1.0 + jnp.exp(-x))


def _silu(x):
    return x * _sigmoid(x)


def _rms(x, g):
    ms = jnp.mean(x * x, axis=-1, keepdims=True)
    return x * lax.rsqrt(ms + EPS) * g


def _norm_mod(x, g, shift, scale):
    return _rms(x, g) * (1.0 + scale) + shift


def _mod_kernel(c_ref, w_ref, b_ref, o_ref):
    s = _silu(c_ref[...])
    o_ref[...] = jnp.dot(s.astype(BF16), w_ref[...].astype(BF16), preferred_element_type=F32) + b_ref[...]


def _modulation(cond, mod_w, mod_b):
    depth = mod_w.shape[0]
    m = pl.pallas_call(
        _mod_kernel,
        out_shape=jax.ShapeDtypeStruct((depth, 6, MOD_ROWS, D_MODEL), F32),
        grid=(depth, 6),
        in_specs=[
            pl.BlockSpec((MOD_ROWS, D_MODEL), lambda l, j: (0, 0)),
            pl.BlockSpec((None, D_MODEL, D_MODEL), lambda l, j: (l, 0, j)),
            pl.BlockSpec((None, 1, D_MODEL), lambda l, j: (l, 0, j)),
        ],
        out_specs=pl.BlockSpec((None, None, MOD_ROWS, D_MODEL), lambda l, j: (l, j, 0, 0)),
        compiler_params=_params(("arbitrary", "arbitrary")),
        name="modulation",
    )(cond, mod_w, mod_b.reshape(depth, 1, 6 * D_MODEL))
    return m.transpose(0, 2, 1, 3)[:, :, :, None, :]


def _mod_spec(n_prompt, l_sample, ctx_row, tile):
    def index(i):
        t = i * tile
        return (jnp.where(t < n_prompt, ctx_row, (t - n_prompt) // l_sample), 0, 0, 0)
    return pl.BlockSpec((None, 6, 1, D_MODEL), index)


def _x_parts(x, n_prompt):
    npt = n_prompt // TD
    if isinstance(x, tuple):
        a, b = x
        spec_b = pl.BlockSpec((TD, D_MODEL), lambda i: (jnp.maximum(i - npt, 0), 0))
    else:
        a = b = x
        spec_b = pl.BlockSpec((TD, D_MODEL), lambda i: (jnp.maximum(i, npt), 0))
    spec_a = pl.BlockSpec((TD, D_MODEL), lambda i: (jnp.minimum(i, npt - 1), 0))
    n_rows = a.shape[0] + b.shape[0] if isinstance(x, tuple) else x.shape[0]
    return [spec_a, spec_b], [a, b], npt, n_rows


def _pick_x(xa_ref, xb_ref, npt):
    return jnp.where(pl.program_id(0) < npt, xa_ref[...], xb_ref[...])


def _inproj_kernel(xa_ref, xb_ref, m_ref, g_ref, w_ref, o_ref, mix_ref, *, npt):
    h = _norm_mod(_pick_x(xa_ref, xb_ref, npt), g_ref[...], m_ref[0], m_ref[1])
    o_ref[...] = jnp.dot(h.astype(BF16), w_ref[...], preferred_element_type=F32)
    mix_ref[...] = jnp.zeros(mix_ref.shape, F32)


def _inproj(x, n_prompt, m, g, w, mod_spec):
    x_specs, x_args, npt, nt = _x_parts(x, n_prompt)
    n_out = w.shape[1]
    return pl.pallas_call(
        functools.partial(_inproj_kernel, npt=npt),
        out_shape=(jax.ShapeDtypeStruct((nt, n_out), F32), jax.ShapeDtypeStruct((nt, D_MODEL), F32)),
        grid=(nt // TD,),
        in_specs=x_specs + [
            mod_spec,
            pl.BlockSpec((1, D_MODEL), lambda i: (0, 0)),
            pl.BlockSpec((D_MODEL, n_out), lambda i: (0, 0)),
        ],
        out_specs=(pl.BlockSpec((TD, n_out), lambda i: (i, 0)), pl.BlockSpec((TD, D_MODEL), lambda i: (i, 0))),
        compiler_params=_params(("arbitrary",)),
        name="even_inproj",
    )(*x_args, m, g, w)


def _rope(x, cos, sin_signed):
    lane = lax.broadcasted_iota(I32, x.shape, 1)
    first = ((lane // ROPE_HALF) % 2) == 0
    partner = jnp.where(first, pltpu.roll(x, HEAD_W - ROPE_HALF, 1), pltpu.roll(x, ROPE_HALF, 1))
    return x * cos + partner * sin_signed


def _attn_kernel(*refs, rope, ctx, bq, chunks, lam_init):
    it = iter(refs)
    lam_ref, g_ref, q_ref, k_ref, v_ref = (next(it) for _ in range(5))
    if rope:
        cq_ref, sq_ref, ck_ref, sk_ref = (next(it) for _ in range(4))
    if ctx:
        kc_ref, vc_ref = next(it), next(it)
    next(it)
    o_ref = next(it)
    if not ctx:
        ko_ref, vo_ref = next(it), next(it)
    kt_ref, vx_ref = next(it), next(it)
    l_new = k_ref.shape[0]
    l_ctx = kc_ref.shape[0] if ctx else 0

    @pl.when(pl.program_id(2) == 0)
    def _prepare_keys():
        if ctx:
            kt_ref[:, 0:l_ctx] = kc_ref[...].T.astype(BF16)
            vx_ref[0:l_ctx, 0:HEAD_W] = vc_ref[...].astype(BF16)
        else:
            ko_ref[...] = k_ref[...]
            vo_ref[...] = v_ref[...]
        step = min(512, l_new)
        for c0 in range(0, l_new, step):
            k = k_ref[c0:c0 + step, :]
            if rope:
                k = _rope(k, ck_ref[c0:c0 + step, :], sk_ref[c0:c0 + step, :])
            kt_ref[:, l_ctx + c0:l_ctx + c0 + step] = k.T.astype(BF16)
            vx_ref[l_ctx + c0:l_ctx + c0 + step, 0:HEAD_W] = v_ref[c0:c0 + step, :].astype(BF16)
        vx_ref[:, HEAD_W:2 * HEAD_W] = jnp.ones((l_ctx + l_new, HEAD_W), BF16)

    n_sub = q_ref.shape[0] // bq
    lane = lax.broadcasted_iota(I32, (bq, HEAD_W), 1)
    lo = lane < DA_HD
    zero = jnp.zeros((bq, HEAD_W), F32)
    qss = []
    for u in range(n_sub):
        rows = slice(u * bq, (u + 1) * bq)
        q = q_ref[rows, :]
        if rope:
            q = _rope(q, cq_ref[rows, :], sq_ref[rows, :])
        q = q * (DA_HD ** -0.5)
        qss.append(jnp.concatenate([jnp.where(lo, q, zero), jnp.where(lo, zero, q)], axis=0).astype(BF16))

    def scores(qs, chunk):
        return jnp.dot(qs, kt_ref[:, chunk[0]:chunk[0] + chunk[1]], preferred_element_type=F32)

    m = [jnp.full((2 * bq, 1), -jnp.inf, F32)] * n_sub
    l = [jnp.zeros((2 * bq, 1), F32)] * n_sub
    acc = [jnp.zeros((2 * bq, HEAD_W), F32)] * n_sub
    s_next = [scores(qs, chunks[0]) for qs in qss]
    for ci, (c0, cs) in enumerate(chunks):
        s_cur = s_next
        if ci + 1 < len(chunks):
            s_next = [scores(qs, chunks[ci + 1]) for qs in qss]
        for u in range(n_sub):
            s = s_cur[u]
            mn = jnp.maximum(m[u], jnp.max(s, axis=-1, keepdims=True))
            alpha = jnp.exp(m[u] - mn)
            p = jnp.exp(s - mn).astype(BF16)
            pv = jnp.dot(p, vx_ref[c0:c0 + cs, :], preferred_element_type=F32)
            acc[u] = alpha * acc[u] + pv[:, 0:HEAD_W]
            l[u] = alpha * l[u] + pv[:, HEAD_W:HEAD_W + 1]
            m[u] = mn
    lp = lam_ref[...]
    lam = (jnp.exp(jnp.sum(lp[0:1] * lp[1:2], axis=-1, keepdims=True))
           - jnp.exp(jnp.sum(lp[2:3] * lp[3:4], axis=-1, keepdims=True)) + lam_init)
    for u in range(n_sub):
        o2 = acc[u] / l[u]
        o = o2[0:bq] - lam * o2[bq:2 * bq]
        o_ref[u * bq:(u + 1) * bq, :] = _rms(o, g_ref[...]) * (1.0 - lam_init)


def _attn_chunks(l_ctx, l_new):
    total = l_ctx + l_new
    if l_new <= ATTN_KEY_CHUNK:
        return ((0, total),)
    chunks, c0 = [(0, l_ctx + ATTN_KEY_CHUNK)], l_ctx + ATTN_KEY_CHUNK
    while c0 < total:
        cs = min(ATTN_KEY_CHUNK, total - c0)
        chunks.append((c0, cs))
        c0 += cs
    return tuple(chunks)


def _diff_attention(proj, mix, lam_p, subln_g, *, row0, n_batch, seq, lam_init, rope_tabs=None,
                    ctx_k=None, ctx_v=None, layer_e=0):
    rope, ctx = rope_tabs is not None, ctx_k is not None
    n_rows = proj.shape[0]
    bq = min(ATTN_SUB * 256, seq)
    nq = seq // bq
    l_ctx = ctx_k.shape[3] if ctx else 0
    qb0, kb0 = row0 // bq, row0 // seq
    in_specs = [
        pl.BlockSpec((4, DA_HD), lambda b, h, i: (0, 0)),
        pl.BlockSpec((1, HEAD_W), lambda b, h, i: (0, 0)),
        pl.BlockSpec((bq, HEAD_W), lambda b, h, i: (qb0 + b * nq + i, h)),
        pl.BlockSpec((seq, HEAD_W), lambda b, h, i: (kb0 + b, DA_HEADS + h)),
        pl.BlockSpec((seq, HEAD_W), lambda b, h, i: (kb0 + b, 2 * DA_HEADS + h)),
    ]
    args = [lam_p, subln_g, proj, proj, proj]
    if rope:
        cos, sin = rope_tabs
        in_specs += [
            pl.BlockSpec((bq, HEAD_W), lambda b, h, i: (i, 0)),
            pl.BlockSpec((bq, HEAD_W), lambda b, h, i: (i, 0)),
            pl.BlockSpec((seq, HEAD_W), lambda b, h, i: (0, 0)),
            pl.BlockSpec((seq, HEAD_W), lambda b, h, i: (0, 0)),
        ]
        args += [cos, sin, cos, sin]
    if ctx:
        in_specs += [
            pl.BlockSpec((None, None, None, l_ctx, HEAD_W), lambda b, h, i: (b, layer_e, h, 0, 0)),
            pl.BlockSpec((None, None, None, l_ctx, HEAD_W), lambda b, h, i: (b, layer_e, h, 0, 0)),
        ]
        args += [ctx_k, ctx_v]
    mix_sds = jax.ShapeDtypeStruct((n_rows, D_MODEL), F32)
    mix_spec = pl.BlockSpec((bq, HEAD_W), lambda b, h, i: (qb0 + b * nq + i, h))
    in_specs.append(pl.BlockSpec(memory_space=pl.ANY))
    args.append(mix)
    aliases = {len(args) - 1: 0}
    if ctx:
        out_shape, out_specs = mix_sds, mix_spec
    else:
        cache_sds = jax.ShapeDtypeStruct((n_batch, DA_HEADS, seq, HEAD_W), F32)
        cache_spec = pl.BlockSpec((None, None, seq, HEAD_W), lambda b, h, i: (b, h, 0, 0))
        out_shape, out_specs = (mix_sds, cache_sds, cache_sds), (mix_spec, cache_spec, cache_spec)
    kernel = functools.partial(_attn_kernel, rope=rope, ctx=ctx, bq=bq // ATTN_SUB,
                               chunks=_attn_chunks(l_ctx, seq), lam_init=lam_init)
    return pl.pallas_call(
        kernel,
        out_shape=out_shape,
        grid=(n_batch, DA_HEADS, nq),
        in_specs=in_specs,
        out_specs=out_specs,
        scratch_shapes=[pltpu.VMEM((HEAD_W, l_ctx + seq), BF16), pltpu.VMEM((l_ctx + seq, 2 * HEAD_W), BF16)],
        input_output_aliases=aliases,
        compiler_params=_params(("arbitrary", "arbitrary", "arbitrary")),
        name="diff_attention_ctx" if ctx else "diff_attention",
    )(*args)


def _hgrn_chunks(streams):
    c, nsb = HG_C, HG_C // HG_SB
    row = lax.broadcasted_iota(I32, (c, c), 0)
    col = lax.broadcasted_iota(I32, (c, c), 1)
    tri = {r: jnp.where((col >= row) if r else (col <= row), 1.0, 0.0).astype(BF16) for r in (False, True)}
    ones = jnp.ones((HG_DK, HG_DK), BF16)
    t_idx = lax.broadcasted_iota(I32, (HG_SB, 1), 0)

    g3s = []
    for q, k, v, lf, rev, f in streams:
        hi = lf.astype(BF16)
        r1 = lf - hi.astype(F32)
        mid = r1.astype(BF16)
        lo = (r1 - mid.astype(F32)).astype(BF16)
        g3s.append(jnp.dot(tri[rev], jnp.concatenate([hi, mid, lo], axis=1), preferred_element_type=F32))
    gs = [g3[:, 0:HG_DK] + g3[:, HG_DK:2 * HG_DK] + g3[:, 2 * HG_DK:3 * HG_DK] for g3 in g3s]
    g_lasts = [g[0:1] if s[4] else g[c - 1:c] for g, s in zip(gs, streams)]

    qgs = [(s[0] * jnp.exp(g)).astype(BF16) for g, s in zip(gs, streams)]

    def outside(i, rev):
        r0 = i * HG_SB
        if rev:
            return (r0 + HG_SB, c, r0 + HG_SB) if r0 + HG_SB < c else None
        return (0, r0, r0 - 1) if r0 > 0 else None

    a_offs = {}
    for i in range(nsb):
        for n, (g, s) in enumerate(zip(gs, streams)):
            span = outside(i, s[4])
            if span is None:
                continue
            s_lo, s_hi, rb = span
            r0 = i * HG_SB
            ref_g = g[rb:rb + 1]
            qd = (s[0][r0:r0 + HG_SB] * jnp.exp(g[r0:r0 + HG_SB] - ref_g)).astype(BF16)
            kd = (s[1][s_lo:s_hi] * jnp.exp(ref_g - g[s_lo:s_hi])).astype(BF16)
            a_offs[i, n] = lax.dot_general(qd, kd, NT_DIMS, preferred_element_type=F32)
    o_offs = {}
    for (i, n), a in a_offs.items():
        s_lo, s_hi, _ = outside(i, streams[n][4])
        o_offs[i, n] = jnp.dot(a.astype(BF16), streams[n][2][s_lo:s_hi].astype(BF16), preferred_element_type=F32)

    def rows_of(j, rev):
        half = HG_SB // 2
        if rev:
            return (0, half) if j < half else (0, HG_SB)
        return (half, HG_SB) if j >= half else (0, HG_SB)

    a_reps = {}
    for i in range(nsb):
        r0 = i * HG_SB
        for n, (g, s) in enumerate(zip(gs, streams)):
            qi, ki, fi = s[0][r0:r0 + HG_SB], s[1][r0:r0 + HG_SB], s[5][r0:r0 + HG_SB]
            ws, qe = {}, None
            order = range(HG_SB) if s[4] else range(HG_SB - 1, -1, -1)
            for j in order:
                t0, t1 = rows_of(j, s[4])
                here = t_idx[t0:t1] == j
                if qe is None:
                    qe = jnp.where(here, qi[t0:t1], 0.0)
                else:
                    if qe.shape[0] != t1 - t0:
                        pad8 = jnp.zeros_like(qe)
                        qe = jnp.concatenate([qe, pad8] if s[4] else [pad8, qe], axis=0)
                    step = fi[j - 1:j] if s[4] else fi[j + 1:j + 2]
                    qe = jnp.where(here, qi[t0:t1], qe * step)
                ws[j] = qe * ki[j:j + 1]
            a_reps[i, n] = jnp.dot(jnp.concatenate([ws[j] for j in range(HG_SB)], axis=0).astype(BF16), ones,
                                   preferred_element_type=F32)

    results = []
    for n, (g, s) in enumerate(zip(gs, streams)):
        q, k, v, lf, rev, f = s
        outs = []
        for i in range(nsb):
            r0 = i * HG_SB
            vi = v[r0:r0 + HG_SB]
            zero8 = jnp.zeros((HG_SB // 2, v.shape[1]), F32)
            lo_acc, hi_acc, at = zero8, zero8, 0
            for j in range(HG_SB):
                t0, t1 = rows_of(j, rev)
                term = a_reps[i, n][at:at + (t1 - t0)] * vi[j:j + 1]
                at += t1 - t0
                if t1 - t0 == HG_SB:
                    lo_acc, hi_acc = lo_acc + term[0:HG_SB // 2], hi_acc + term[HG_SB // 2:HG_SB]
                elif t0 == 0:
                    lo_acc = lo_acc + term
                else:
                    hi_acc = hi_acc + term
            oi = jnp.concatenate([lo_acc, hi_acc], axis=0)
            if (i, n) in o_offs:
                oi = oi + o_offs[i, n]
            outs.append(oi)
        kd = (k * jnp.exp(g_lasts[n] - g)).astype(BF16)
        gain = lax.dot_general(v.astype(BF16), kd, TN_DIMS, preferred_element_type=F32)
        results.append((jnp.concatenate(outs, axis=0), qgs[n], jnp.exp(g_lasts[n]), gain))
    return results


def _hgrn_advance(part, st):
    o_local, qg, decay, gain = part
    o = o_local + lax.dot_general(qg, st.astype(BF16), NT_DIMS, preferred_element_type=F32)
    return o, st * decay + gain


def _hgrn_kernel(*refs, layer, has_s0):
    it = iter(refs)
    hq_ref, zf_ref, zb_ref, hi_ref, hg_ref, lbp_ref, g_ref = (next(it) for _ in range(7))
    s0_ref = next(it) if has_s0 else None
    next(it)
    o_ref, sout_ref, st_ref, ob_ref = next(it), next(it), next(it), next(it)
    seq = hq_ref.shape[0]
    n = seq // HG_C
    z_refs = (zf_ref, zb_ref)

    lbs = []
    for hh in range(HG_HPS):
        for d in range(2):
            p = lbp_ref[d, :, hh]
            e = jnp.exp(p - jnp.max(p, axis=0))
            lbs.append(jnp.sum(e[0:layer + 1], axis=0) / jnp.sum(e, axis=0))
            if has_s0:
                st_ref[2 * hh + d] = s0_ref[d, hh].T
            else:
                st_ref[2 * hh + d] = jnp.zeros((HEAD_W, HG_DK), F32)

    def body(ci, carry):
        streams, dests = [], []
        for hh in range(HG_HPS):
            lanes = slice(hh * HEAD_W, (hh + 1) * HEAD_W)
            for d in range(2):
                for u in range(HG_UNROLL):
                    c = ci * HG_UNROLL + u
                    c = (n - 1 - c) if d == 1 else c
                    rows = pl.ds(pl.multiple_of(c * HG_C, HG_C), HG_C)
                    lb = lbs[2 * hh + d]
                    f = lb + (1.0 - lb) * _sigmoid(z_refs[d][rows, lanes])
                    streams.append((_silu(hq_ref[rows, lanes]), 1.0 - f, hi_ref[rows, lanes], jnp.log(f), d == 1, f))
                    dests.append((o_ref if d == 0 else ob_ref, rows, lanes))
        parts = _hgrn_chunks(streams)
        for slot in range(2 * HG_HPS):
            st = st_ref[slot]
            for u in range(HG_UNROLL):
                o, st = _hgrn_advance(parts[slot * HG_UNROLL + u], st)
                dst_ref, rows, lanes = dests[slot * HG_UNROLL + u]
                dst_ref[rows, lanes] = o
            st_ref[slot] = st
        return carry

    lax.fori_loop(0, n // HG_UNROLL, body, 0)
    for hh in range(HG_HPS):
        for d in range(2):
            sout_ref[d, hh] = st_ref[2 * hh + d].T

    def finish(ci, carry):
        rows = pl.ds(pl.multiple_of(ci * HG_FIN, HG_FIN), HG_FIN)
        for hh in range(HG_HPS):
            lanes = slice(hh * HEAD_W, (hh + 1) * HEAD_W)
            tot = o_ref[rows, lanes] + ob_ref[rows, lanes]
            o_ref[rows, lanes] = _rms(tot, g_ref[...]) * _silu(hg_ref[rows, lanes])
        return carry

    lax.fori_loop(0, seq // HG_FIN, finish, 0)


def _hgrn(proj, mix, hg_lb_l, hg_norm_g, *, row0, n_batch, seq, layer, s0=None):
    has_s0 = s0 is not None
    rb0 = row0 // seq
    hpg = HG_HEADS // HG_HPS
    width = HG_HPS * HEAD_W
    col0 = 3 * DA_HEADS * HEAD_W // width
    once = pl.Buffered(1) if seq * width * 4 > HG_DOUBLE_BUFFER_BYTES else None

    def col_spec(j, mode=None):
        return pl.BlockSpec((seq, width), lambda b, h: (rb0 + b, col0 + j * hpg + h), pipeline_mode=mode)

    in_specs = [col_spec(0), col_spec(1), col_spec(2), col_spec(3), col_spec(4, once),
                pl.BlockSpec((2, DEPTH + 1, HG_HPS, 1, HG_DK), lambda b, h: (0, 0, h, 0, 0)),
                pl.BlockSpec((1, HEAD_W), lambda b, h: (0, 0))]
    args = [proj, proj, proj, proj, proj, hg_lb_l, hg_norm_g]
    state_spec = pl.BlockSpec((None, 2, HG_HPS, HG_DK, HEAD_W), lambda b, h: (b, 0, h, 0, 0))
    if has_s0:
        in_specs.append(state_spec)
        args.append(s0)
    in_specs.append(pl.BlockSpec(memory_space=pl.ANY))
    args.append(mix)
    return pl.pallas_call(
        functools.partial(_hgrn_kernel, layer=layer, has_s0=has_s0),
        out_shape=(jax.ShapeDtypeStruct(mix.shape, mix.dtype),
                   jax.ShapeDtypeStruct((n_batch, 2, HG_HEADS, HG_DK, HEAD_W), F32)),
        grid=(n_batch, hpg),
        in_specs=in_specs,
        out_specs=(pl.BlockSpec((seq, width), lambda b, h: (rb0 + b, DA_HEADS * HEAD_W // width + h)),
                   state_spec),
        scratch_shapes=[pltpu.VMEM((2 * HG_HPS, HEAD_W, HG_DK), F32), pltpu.VMEM((seq, width), F32)],
        input_output_aliases={len(args) - 1: 0},
        compiler_params=_params(("arbitrary", "arbitrary")),
        name="hgrn2_state" if has_s0 else "hgrn2",
    )(*args)


def _odd_kernel(x_ref, m_ref, g_ref, win_ref, ws_ref, sb_ref, o_ref):
    h = _norm_mod(x_ref[...], g_ref[...], m_ref[0], m_ref[1])
    z = jnp.dot(h.astype(BF16), win_ref[...], preferred_element_type=F32)
    z = 0.5 * z * (1.0 + lax.erf(z * (2.0 ** -0.5)))
    width = z.shape[1] // 2
    u, v = z[:, 0:width], z[:, width:2 * width]
    mu = jnp.mean(v, axis=-1, keepdims=True)
    vc = v - mu
    var = jnp.mean(vc * vc, axis=-1, keepdims=True)
    vn = (vc * lax.rsqrt(var + EPS)).astype(BF16)
    gd = width // CM_GROUPS
    for r in range(x_ref.shape[0] // CM_CHUNK):
        rows = slice(r * CM_CHUNK, (r + 1) * CM_CHUNK)
        for gi in range(CM_GROUPS):
            cols = slice(gi * gd, (gi + 1) * gd)
            sv = jnp.dot(ws_ref[gi], vn[rows, cols], preferred_element_type=F32) + sb_ref[:, gi:gi + 1]
            o_ref[rows, cols] = u[rows, cols] * sv


def _odd_mixer(x, m, g, w_in, sgu_w, sgu_bt, mod_spec):
    nt = x.shape[0]
    return pl.pallas_call(
        _odd_kernel,
        out_shape=jax.ShapeDtypeStruct((nt, D_MODEL), F32),
        grid=(nt // TD,),
        in_specs=[
            pl.BlockSpec((TD, D_MODEL), lambda i: (i, 0)),
            mod_spec,
            pl.BlockSpec((1, D_MODEL), lambda i: (0, 0)),
            pl.BlockSpec(w_in.shape, lambda i: (0, 0)),
            pl.BlockSpec(sgu_w.shape, lambda i: (0, 0, 0)),
            pl.BlockSpec(sgu_bt.shape, lambda i: (0, 0)),
        ],
        out_specs=pl.BlockSpec((TD, D_MODEL), lambda i: (i, 0)),
        compiler_params=_params(("arbitrary",)),
        name="odd_mixer",
    )(x, m, g, w_in, sgu_w, sgu_bt)


def _post_kernel(xa_ref, xb_ref, mix_ref, m_ref, g_ref, w_ref, rwt_ref, rb_ref, xo_ref, h2_ref, lg_ref, *, npt):
    mo = jnp.dot(mix_ref[...].astype(BF16), w_ref[...], preferred_element_type=F32)
    xn = _pick_x(xa_ref, xb_ref, npt) + m_ref[2] * mo
    xo_ref[...] = xn
    h2 = _norm_mod(xn, g_ref[...], m_ref[3], m_ref[4])
    h2_ref[...] = h2
    hh = h2.astype(BF16)
    hl = (h2 - hh.astype(F32)).astype(BF16)
    rw = rwt_ref[...]
    rh = rw.astype(BF16)
    rl = (rw - rh.astype(F32)).astype(BF16)
    lg = (lax.dot_general(rh, hh, NT_DIMS, preferred_element_type=F32)
          + lax.dot_general(rh, hl, NT_DIMS, preferred_element_type=F32)
          + lax.dot_general(rl, hh, NT_DIMS, preferred_element_type=F32))
    lg_ref[...] = lg + rb_ref[...]


def _post(x, n_prompt, mix, m, g2, w_out, router_wt, router_b, mod_spec):
    x_specs, x_args, npt, nt = _x_parts(x, n_prompt)
    row = lambda i: (i, 0)
    return pl.pallas_call(
        functools.partial(_post_kernel, npt=npt),
        out_shape=(jax.ShapeDtypeStruct((nt, D_MODEL), F32),
                   jax.ShapeDtypeStruct((nt, D_MODEL), F32),
                   jax.ShapeDtypeStruct((N_EXPERTS, nt), F32)),
        grid=(nt // TD,),
        in_specs=x_specs + [
            pl.BlockSpec((TD, D_MODEL), row),
            mod_spec,
            pl.BlockSpec((1, D_MODEL), lambda i: (0, 0)),
            pl.BlockSpec((D_MODEL, D_MODEL), lambda i: (0, 0)),
            pl.BlockSpec((N_EXPERTS, D_MODEL), lambda i: (0, 0)),
            pl.BlockSpec((N_EXPERTS, 1), lambda i: (0, 0)),
        ],
        out_specs=(pl.BlockSpec((TD, D_MODEL), row), pl.BlockSpec((TD, D_MODEL), row),
                   pl.BlockSpec((N_EXPERTS, TD), lambda i: (0, i))),
        compiler_params=_params(("arbitrary",)),
        name="mixer_out_router",
    )(*x_args, mix, m, g2, w_out, router_wt, router_b)


def _experts_on_lanes(col):
    on_lane = (lax.broadcasted_iota(I32, (N_EXPERTS, HEAD_W), 0) == lax.broadcasted_iota(I32, (N_EXPERTS, HEAD_W), 1))
    return jnp.sum(jnp.where(on_lane, col, 0.0), axis=0, keepdims=True)


def _route_a_kernel(lg_ref, slot_ref, gt_ref, tab_ref, cnt_ref, carry_ref):
    i = pl.program_id(0)

    @pl.when(i == 0)
    def _():
        carry_ref[...] = jnp.zeros(carry_ref.shape, F32)

    l = lg_ref[...]
    tn = l.shape[1]
    eio = lax.broadcasted_iota(I32, l.shape, 0)
    vals, idxs = [], []
    for _ in range(TOP_K):
        mk = jnp.max(l, axis=0, keepdims=True)
        ik = jnp.min(jnp.where(l == mk, eio, N_EXPERTS), axis=0, keepdims=True)
        vals.append(mk)
        idxs.append(ik)
        l = jnp.where(eio == ik, -jnp.inf, l)
    es = [jnp.exp(v - vals[0]) for v in vals]
    den = es[0] + es[1] + es[2] + es[3]
    sel = jnp.zeros(l.shape, F32)
    for ik in idxs:
        sel = sel + jnp.where(eio == ik, 1.0, 0.0)
    upper = jnp.where(lax.broadcasted_iota(I32, (tn, tn), 0) < lax.broadcasted_iota(I32, (tn, tn), 1),
                      1.0, 0.0).astype(BF16)
    before = jnp.dot(sel.astype(BF16), upper, preferred_element_type=F32)

    tile_cnt = jnp.sum(sel, axis=1, keepdims=True)
    c_row, base_row = _experts_on_lanes(tile_cnt), _experts_on_lanes(carry_ref[...])
    sub = lax.broadcasted_iota(I32, (N_EXPERTS, HEAD_W), 0)
    lane_e = lax.broadcasted_iota(I32, (N_EXPERTS, HEAD_W), 1)
    r_row = base_row - 8.0 * jnp.floor(base_row / 8.0)
    seg_row = 8.0 * jnp.floor((r_row + c_row + 7.0) / 8.0)
    seg_col = jnp.sum(jnp.where(sub == lane_e, seg_row, 0.0), axis=1, keepdims=True)
    r_col = jnp.sum(jnp.where(sub == lane_e, r_row, 0.0), axis=1, keepdims=True)
    loff_row = jnp.sum(jnp.where(sub < lane_e, seg_col, 0.0), axis=0, keepdims=True)
    loff_col = jnp.sum(jnp.where(lane_e < sub, seg_row, 0.0), axis=1, keepdims=True)
    first_slot = loff_col + r_col
    slots = []
    for k in range(TOP_K):
        here = eio == idxs[k]
        slots.append(jnp.sum(jnp.where(here, first_slot + before, 0.0), axis=0, keepdims=True))
        slot_ref[k:k + 1, :] = slots[k].astype(I32)
    slot_ref[TOP_K:DEST_ROWS, :] = jnp.zeros((DEST_ROWS - TOP_K, tn), I32)
    gp = jnp.concatenate([e / den for e in es] + slots + [jnp.zeros((HEAD_W - 2 * TOP_K, tn), F32)], axis=0)
    gt_ref[...] = gp.T
    tab_ref[0:1, :] = c_row.astype(I32)
    tab_ref[1:2, :] = loff_row.astype(I32)
    tab_ref[2:3, :] = base_row.astype(I32)
    tab_ref[3:DEST_ROWS, :] = jnp.zeros((DEST_ROWS - 3, HEAD_W), I32)
    carry_ref[...] = carry_ref[...] + tile_cnt
    cnt_ref[...] = carry_ref[...]


def _route_plan_kernel(cnt_ref, be_ref, nv_ref, pad_ref, *, n_blk_pad, n_rows):
    cnt = cnt_ref[...]
    padded = jnp.floor((cnt + (MOE_BM - 1)) / MOE_BM) * MOE_BM
    r = lax.broadcasted_iota(I32, (N_EXPERTS, N_EXPERTS), 0)
    c = lax.broadcasted_iota(I32, (N_EXPERTS, N_EXPERTS), 1)
    padded_row = jnp.sum(jnp.where(r == c, padded, 0.0), axis=0, keepdims=True)
    pend = jnp.sum(jnp.where(c <= r, padded_row, 0.0), axis=1, keepdims=True)
    pstart = pend - padded
    blk_start = (lax.broadcasted_iota(I32, (N_EXPERTS, n_blk_pad), 1) * MOE_BM).astype(F32)
    be = jnp.sum(jnp.where(pend <= blk_start, 1.0, 0.0), axis=0, keepdims=True)
    be_ref[...] = jnp.minimum(be, N_EXPERTS - 1.0).astype(I32)
    total = jnp.sum(padded, axis=0, keepdims=True)
    nv_ref[...] = jnp.broadcast_to(total / MOE_BM, nv_ref.shape).astype(I32)
    on_lane = (lax.broadcasted_iota(I32, (N_EXPERTS, HEAD_W), 0) == lax.broadcasted_iota(I32, (N_EXPERTS, HEAD_W), 1))
    lane = lax.broadcasted_iota(I32, (1, HEAD_W), 1)
    pad_lo = jnp.sum(jnp.where(on_lane, pstart + cnt, 0.0), axis=0, keepdims=True)
    pad_hi = jnp.sum(jnp.where(on_lane, pend, 0.0), axis=0, keepdims=True)
    pad_ref[0:1, :] = jnp.where(lane == N_EXPERTS, total, pad_lo).astype(I32)
    pad_ref[1:2, :] = jnp.where(lane == N_EXPERTS, float(n_rows), pad_hi).astype(I32)
    pad_ref[2:3, :] = jnp.sum(jnp.where(on_lane, pstart, 0.0), axis=0, keepdims=True).astype(I32)
    pad_ref[3:DEST_ROWS, :] = jnp.zeros((DEST_ROWS - 3, HEAD_W), I32)


def _route(logits_t, n_blk_pad, n_rows):
    nt = logits_t.shape[1]
    tn = TM
    blk = lambda i: (0, i)
    tile = lambda i: (i, 0)
    whole = lambda i: (0, 0)
    n_tiles = nt // tn
    slot, gate_t, tab, cnt = pl.pallas_call(
        _route_a_kernel,
        out_shape=(jax.ShapeDtypeStruct((n_tiles * DEST_ROWS, tn), I32), jax.ShapeDtypeStruct((nt, HEAD_W), F32),
                   jax.ShapeDtypeStruct((n_tiles * DEST_ROWS, HEAD_W), I32),
                   jax.ShapeDtypeStruct((N_EXPERTS, 1), F32)),
        grid=(n_tiles,),
        in_specs=[pl.BlockSpec((N_EXPERTS, tn), blk)],
        out_specs=(pl.BlockSpec((DEST_ROWS, tn), tile), pl.BlockSpec((tn, HEAD_W), tile),
                   pl.BlockSpec((DEST_ROWS, HEAD_W), tile), pl.BlockSpec((N_EXPERTS, 1), whole)),
        scratch_shapes=[pltpu.VMEM((N_EXPERTS, 1), F32)],
        compiler_params=_params(("arbitrary",)),
        name="route_topk_slots",
    )(logits_t)
    blk_exp, n_valid, pad = pl.pallas_call(
        functools.partial(_route_plan_kernel, n_blk_pad=n_blk_pad, n_rows=n_rows),
        out_shape=(jax.ShapeDtypeStruct((1, n_blk_pad), I32), jax.ShapeDtypeStruct((1, HEAD_W), I32),
                   jax.ShapeDtypeStruct((DEST_ROWS, HEAD_W), I32)),
        grid=(1,),
        in_specs=[pl.BlockSpec((N_EXPERTS, 1), whole)],
        out_specs=(pl.BlockSpec((1, n_blk_pad), whole), pl.BlockSpec((1, HEAD_W), whole),
                   pl.BlockSpec((DEST_ROWS, HEAD_W), whole)),
        compiler_params=_params(("arbitrary",)),
        name="route_plan",
    )(cnt)
    return slot, gate_t, tab, blk_exp.reshape(n_blk_pad), n_valid[0, 0:1], pad


def _for_each_chunk(n8, fn):
    for bi, b in enumerate(GROUP_BITS):
        @pl.when((n8 & b) != 0)
        def _(bi=bi, b=b):
            first = pl.multiple_of((n8 - (n8 & (2 * b - 1))) * 8, 8)
            fn(first, 8 * b, bi % 2)


def _wait_rows(n8, make_copy):
    for b in WAIT_BITS:
        @pl.when((n8 & b) != 0)
        def _(b=b):
            make_copy(8 * b).wait()


def _dispatch_kernel(h_ref, slot_ref, tab_ref, pad_ref, xs_ref, tsm_ref, psm_ref, sent_ref, xl2_ref, part_ref,
                     zero_ref, sem_tab, sem_pad, sem_rows2):
    i = pl.program_id(0)
    last = pl.num_programs(0) - 1
    tn = h_ref.shape[0]
    buf = i % 2
    xl_ref, sem_rows = xl2_ref.at[buf], sem_rows2.at[buf]
    cpt = pltpu.make_async_copy(tab_ref, tsm_ref, sem_tab)
    cpt.start()

    def out_copy(src_ref, src0, dst0, rows, sem=sem_rows):
        return pltpu.make_async_copy(src_ref.at[pl.ds(src0, rows)], xs_ref.at[pl.ds(dst0, rows)], sem)

    @pl.when(i == 0)
    def _first_step():
        part_ref[...] = jnp.zeros(part_ref.shape, F32)
        zero_ref[...] = jnp.zeros(zero_ref.shape, F32)
        cpp = pltpu.make_async_copy(pad_ref, psm_ref, sem_pad)
        cpp.start()
        cpp.wait()

        def per_span(e, carry):
            lo8, hi8 = (psm_ref[0, e] + 7) >> 3, psm_ref[1, e] >> 3

            def zero_copy(u):
                return out_copy(zero_ref, 0, pl.multiple_of(u * 8, 8), 8)

            lax.fori_loop(lo8, hi8, lambda u, c: (zero_copy(u).start(), c)[1], 0)
            lax.fori_loop(lo8, hi8, lambda u, c: (zero_copy(u).wait(), c)[1], 0)
            return carry

        lax.fori_loop(0, N_EXPERTS + 1, per_span, 0)

    slot = lax.broadcasted_iota(I32, (L_ROWS, tn), 0)
    sl = slot_ref[...]
    hit = jnp.where(slot == sl[0:1], 1.0, jnp.where(slot == sl[1:2], 1.0, jnp.where(
        slot == sl[2:3], 1.0, jnp.where(slot == sl[3:4], 1.0, 0.0))))
    xl_ref[...] = jnp.dot(hit.astype(BF16), h_ref[...].astype(BF16), preferred_element_type=F32)
    cpt.wait()
    sub8 = lax.broadcasted_iota(I32, (8, 1), 0)

    def per_expert(e, sent8):
        cnt, lo, g0 = tsm_ref[0, e], pl.multiple_of(tsm_ref[1, e], 8), psm_ref[2, e] + tsm_ref[2, e]
        r = g0 & 7
        g_tile = pl.multiple_of(g0 - r, 8)
        head = pl.ds(lo, 8)
        xl_ref[head, :] = xl_ref[head, :] + part_ref[e]
        full8, rem = (r + cnt) >> 3, (r + cnt) & 7
        _for_each_chunk(full8, lambda first, rows, prio: out_copy(
            xl_ref, pl.multiple_of(lo + first, 8), pl.multiple_of(g_tile + first, 8), rows).start(priority=prio))
        tail = xl_ref[pl.ds(pl.multiple_of(lo + full8 * 8, 8), 8), :]
        part_ref[e] = jnp.where(sub8 < rem, tail, 0.0)
        return sent8 + full8

    sent8 = lax.fori_loop(0, N_EXPERTS, per_expert, 0)

    sent_ref[buf] = sent8

    @pl.when(i > 0)
    def _previous_step_copies():
        _wait_rows(sent_ref[1 - buf], lambda rows: out_copy(xl_ref, 0, 0, rows, sem_rows2.at[1 - buf]))

    @pl.when(i == last)
    def _last_step():
        def flush(e, n8):
            end = psm_ref[0, e]
            due = (end & 7) != 0

            @pl.when(due)
            def _():
                pltpu.make_async_copy(part_ref.at[e], xs_ref.at[pl.ds(pl.multiple_of(end - (end & 7), 8), 8)],
                                      sem_rows).start()

            return n8 + jnp.where(due, 1, 0)

        _wait_rows(lax.fori_loop(0, N_EXPERTS, flush, sent8), lambda rows: out_copy(xl_ref, 0, 0, rows))


def _dispatch(h2, slot, tab, pad, n_rows):
    nt = h2.shape[0]
    tile = lambda i: (i, 0)
    return pl.pallas_call(
        _dispatch_kernel,
        out_shape=jax.ShapeDtypeStruct((n_rows, D_MODEL), F32),
        grid=(nt // TM,),
        in_specs=[pl.BlockSpec((TM, D_MODEL), tile),
                  pl.BlockSpec((DEST_ROWS, TM), tile),
                  pl.BlockSpec((DEST_ROWS, HEAD_W), tile),
                  pl.BlockSpec((DEST_ROWS, HEAD_W), lambda i: (0, 0))],
        out_specs=pl.BlockSpec(memory_space=pl.ANY),
        scratch_shapes=[pltpu.SMEM((DEST_ROWS, HEAD_W), I32), pltpu.SMEM((DEST_ROWS, HEAD_W), I32),
                        pltpu.SMEM((2,), I32),
                        pltpu.VMEM((2, L_ROWS, D_MODEL), F32), pltpu.VMEM((N_EXPERTS, 8, D_MODEL), F32),
                        pltpu.VMEM((8, D_MODEL), F32),
                        pltpu.SemaphoreType.DMA, pltpu.SemaphoreType.DMA, pltpu.SemaphoreType.DMA((2,))],
        compiler_params=_params(("arbitrary",)),
        name="moe_dispatch",
    )(h2, slot, tab, pad)


def _expert_kernel(be_ref, nv_ref, xs_ref, w1_ref, b1_ref, w2_ref, b2_ref, ys_ref, w1b_ref, w2b_ref):
    i = pl.program_id(0)
    valid = i < nv_ref[0]
    prev = be_ref[jnp.maximum(i - 1, 0)]
    changed = jnp.logical_or(i == 0, be_ref[i] != prev)

    @pl.when(jnp.logical_and(valid, changed))
    def _cast_weights():
        step = 128
        def cast(j, carry):
            rows = pl.ds(pl.multiple_of(j * step, step), step)
            w1b_ref[rows, :] = w1_ref[rows, :].astype(BF16)
            w2b_ref[rows, :] = w2_ref[rows, :].astype(BF16)
            return carry
        lax.fori_loop(0, w1_ref.shape[0] // step, cast, 0)

    @pl.when(jnp.logical_not(valid))
    def _unused_block():
        ys_ref[...] = jnp.zeros(ys_ref.shape, F32)

    @pl.when(valid)
    def _mlp():
        h = jnp.dot(xs_ref[...].astype(BF16), w1b_ref[...], preferred_element_type=F32) + b1_ref[...]
        half = h.shape[1] // 2
        a = jnp.minimum(h[:, 0:half], SWIGLU_LIMIT)
        lin = jnp.clip(h[:, half:2 * half], -SWIGLU_LIMIT, SWIGLU_LIMIT)
        act = a * _sigmoid(SWIGLU_ALPHA * a) * (lin + 1.0)
        ys_ref[...] = jnp.dot(act.astype(BF16), w2b_ref[...], preferred_element_type=F32) + b2_ref[...]


def _experts(xs, blk_exp, n_valid, w1, b1, w2, b2, layer):
    n_rows = xs.shape[0]
    n_blk = n_rows // MOE_BM
    depth, d_e2 = w1.shape[0], w1.shape[3]

    def blk(i, be, nv):
        return jnp.minimum(i, nv[0] - 1)

    def expert(i, be, nv):
        return (layer, be[blk(i, be, nv)], 0, 0)

    grid_spec = pltpu.PrefetchScalarGridSpec(
        num_scalar_prefetch=2,
        grid=(n_blk,),
        in_specs=[
            pl.BlockSpec((MOE_BM, D_MODEL), lambda i, be, nv: (blk(i, be, nv), 0)),
            pl.BlockSpec((None, None, D_MODEL, d_e2), expert),
            pl.BlockSpec((None, None, 1, d_e2), expert),
            pl.BlockSpec((None, None, d_e2 // 2, D_MODEL), expert),
            pl.BlockSpec((None, None, 1, D_MODEL), expert),
        ],
        out_specs=pl.BlockSpec((MOE_BM, D_MODEL), lambda i, be, nv: (i, 0)),
        scratch_shapes=[pltpu.VMEM((D_MODEL, d_e2), BF16), pltpu.VMEM((d_e2 // 2, D_MODEL), BF16)],
    )
    return pl.pallas_call(
        _expert_kernel,
        out_shape=jax.ShapeDtypeStruct((n_rows, D_MODEL), F32),
        grid_spec=grid_spec,
        compiler_params=_params(("arbitrary",)),
        name="moe_experts",
    )(blk_exp, n_valid, xs, w1, b1.reshape(depth, N_EXPERTS, 1, d_e2), w2, b2.reshape(depth, N_EXPERTS, 1, D_MODEL))


def _combine_kernel(x_ref, gt_ref, tab_ref, tab_next_ref, pad_ref, m_ref, fg_ref, ys_ref, *out_and_scratch, npt):
    *o_ref, tsm_ref, psm_ref, got_ref, yl2_ref, sem_tab, sem_rows2 = out_and_scratch
    o_ref = o_ref[0] if npt is None else o_ref
    i = pl.program_id(0)
    tn = x_ref.shape[0]
    buf = i % 2

    def in_copy(b, src0, dst0, rows):
        return pltpu.make_async_copy(ys_ref.at[pl.ds(src0, rows)], yl2_ref.at[b, pl.ds(dst0, rows)],
                                     sem_rows2.at[b])

    def fetch(table_ref, b):
        cpt = pltpu.make_async_copy(table_ref, tsm_ref, sem_tab)
        cpt.start()
        cpt.wait()

        def per_expert(e, got8):
            cnt, lo, g0 = tsm_ref[0, e], pl.multiple_of(tsm_ref[1, e], 8), psm_ref[2, e] + tsm_ref[2, e]
            r = g0 & 7
            g_tile = pl.multiple_of(g0 - r, 8)
            cover8 = jnp.where(cnt > 0, (r + cnt + 7) >> 3, 0)
            _for_each_chunk(cover8, lambda first, rows, prio: in_copy(
                b, pl.multiple_of(g_tile + first, 8), pl.multiple_of(lo + first, 8), rows).start(priority=prio))
            return got8 + cover8

        got_ref[b] = lax.fori_loop(0, N_EXPERTS, per_expert, 0)

    @pl.when(i == 0)
    def _first_step():
        yl2_ref[...] = jnp.zeros(yl2_ref.shape, F32)
        cpp = pltpu.make_async_copy(pad_ref, psm_ref, sem_tab)
        cpp.start()
        cpp.wait()
        fetch(tab_ref, 0)

    @pl.when(i + 1 < pl.num_programs(0))
    def _next_tile():
        fetch(tab_next_ref, 1 - buf)

    gt = gt_ref[...]
    slot = lax.broadcasted_iota(I32, (tn, L_ROWS), 1).astype(F32)
    pick = jnp.where(slot == gt[:, TOP_K:TOP_K + 1], gt[:, 0:1], 0.0)
    for k in range(1, TOP_K):
        pick = pick + jnp.where(slot == gt[:, TOP_K + k:TOP_K + k + 1], gt[:, k:k + 1], 0.0)
    _wait_rows(got_ref[buf], lambda rows: in_copy(buf, 0, 0, rows))
    acc = jnp.dot(pick.astype(BF16), yl2_ref[buf].astype(BF16), preferred_element_type=F32)
    xn = x_ref[...] + m_ref[5] * acc
    if npt is None:
        o_ref[...] = xn
    else:
        xn = _rms(xn, fg_ref[...])
        op_ref, os_ref = o_ref

        @pl.when(i < npt)
        def _():
            op_ref[...] = xn

        @pl.when(i >= npt)
        def _():
            os_ref[...] = xn


def _combine(x, gate_t, tab, pad, m, final_g, ys, mod_spec, n_prompt, final):
    nt = x.shape[0]
    tile = lambda i: (i, 0)
    npt = n_prompt // TM
    if final:
        out_shape = (jax.ShapeDtypeStruct((n_prompt, D_MODEL), F32), jax.ShapeDtypeStruct((nt - n_prompt, D_MODEL), F32))
        out_specs = (pl.BlockSpec((TM, D_MODEL), lambda i: (jnp.minimum(i, npt - 1), 0)),
                     pl.BlockSpec((TM, D_MODEL), lambda i: (jnp.maximum(i - npt, 0), 0)))
    else:
        out_shape, out_specs = jax.ShapeDtypeStruct((nt, D_MODEL), F32), pl.BlockSpec((TM, D_MODEL), tile)
    return pl.pallas_call(
        functools.partial(_combine_kernel, npt=npt if final else None),
        out_shape=out_shape,
        grid=(nt // TM,),
        in_specs=[pl.BlockSpec((TM, D_MODEL), tile),
                  pl.BlockSpec((TM, HEAD_W), tile),
                  pl.BlockSpec((DEST_ROWS, HEAD_W), tile),
                  pl.BlockSpec((DEST_ROWS, HEAD_W), lambda i: (jnp.minimum(i + 1, nt // TM - 1), 0)),
                  pl.BlockSpec((DEST_ROWS, HEAD_W), lambda i: (0, 0)),
                  mod_spec,
                  pl.BlockSpec((1, D_MODEL), lambda i: (0, 0)),
                  pl.BlockSpec(memory_space=pl.ANY)],
        out_specs=out_specs,
        scratch_shapes=[pltpu.SMEM((DEST_ROWS, HEAD_W), I32), pltpu.SMEM((DEST_ROWS, HEAD_W), I32),
                        pltpu.SMEM((2,), I32), pltpu.VMEM((2, L_ROWS, D_MODEL), F32),
                        pltpu.SemaphoreType.DMA, pltpu.SemaphoreType.DMA((2,))],
        compiler_params=_params(("arbitrary",)),
        name="moe_combine_final" if final else "moe_combine",
    )(x, gate_t, tab, tab, pad, m, final_g, ys)


def _moe(x, h2, logits_t, m, final_g, w1, b1, w2, b2, mod_spec, n_prompt, layer, final):
    nt = x.shape[0]
    n_blk = nt * TOP_K // MOE_BM + N_EXPERTS
    n_blk_pad = -(-n_blk // HEAD_W) * HEAD_W
    slot, gate_t, tab, blk_exp, n_valid, pad = _route(logits_t, n_blk_pad, n_blk * MOE_BM)
    xs = _dispatch(h2, slot, tab, pad, n_blk * MOE_BM)
    ys = _experts(xs, blk_exp, n_valid, w1, b1, w2, b2, layer)
    return _combine(x, gate_t, tab, pad, m, final_g, ys, mod_spec, n_prompt, final)


def _rope_tables(seq):
    t = np.arange(seq)
    inv = (1.0 / (np.float32(ROPE_THETA) ** (np.arange(ROPE_HALF, dtype=np.float32) / np.float32(ROPE_HALF))))
    inv = inv.astype(np.float32)
    ang_r = (t // GRID_W).astype(np.float32)[:, None] * inv
    ang_c = (t % GRID_W).astype(np.float32)[:, None] * inv
    cr, sr, cc, sc = np.cos(ang_r), np.sin(ang_r), np.cos(ang_c), np.sin(ang_c)
    cos = np.concatenate([cr, cr, cc, cc] * 2, axis=1).astype(np.float32)
    sin = np.concatenate([-sr, sr, -sc, sc] * 2, axis=1).astype(np.float32)
    return jnp.asarray(cos), jnp.asarray(sin)


def kernel(x_prompt, x_sample, c, cache_k, cache_v, state_hgrn, c_ctx, mod_w, mod_b, norm_g, final_norm_g,
           w_in_even, w_out_even, da_lambda, da_subln_g, hg_norm_g, hg_lb, w_in_odd, sgu_w, sgu_b, w_out_odd,
           router_w, router_b, ex_w1, ex_b1, ex_w2, ex_b2):
    bp, lp, _ = x_prompt.shape
    bs, ls, _ = x_sample.shape
    n_p, n_s = bp * lp, bs * ls
    assert lp % TM == 0 and ls % TD == 0 and n_p % TD == 0 and n_p % ls == 0 and bs < MOD_ROWS
    depth = mod_w.shape[0]

    x = (x_prompt.reshape(n_p, D_MODEL), x_sample.reshape(n_s, D_MODEL))
    cond = jnp.zeros((MOD_ROWS, D_MODEL), F32).at[0:bs].set(c).at[bs].set(c_ctx)
    mod = _modulation(cond, mod_w, mod_b)
    mod_spec = _mod_spec(n_p, ls, bs, TM)
    mod_spec_d = _mod_spec(n_p, ls, bs, TD)
    rope_tabs = _rope_tables(ls)
    hg_lb_h = hg_lb.reshape(2, depth + 1, HG_HEADS, 1, HG_DK)
    final_g = final_norm_g.reshape(1, D_MODEL)

    new_k, new_v, new_s = [], [], []
    for l in range(depth):
        m = mod[l]
        g1 = norm_g[l, 0].reshape(1, D_MODEL)
        g2 = norm_g[l, 1].reshape(1, D_MODEL)
        if l % 2 == 0:
            e = l // 2
            lam_init = 0.8 - 0.6 * math.exp(-0.3 * l)
            proj, mix = _inproj(x, n_p, m, g1, w_in_even[e].astype(BF16), mod_spec_d)
            subln = da_subln_g[e].reshape(1, HEAD_W)
            hgn = hg_norm_g[e].reshape(1, HEAD_W)
            mix, k_new, v_new = _diff_attention(proj, mix, da_lambda[e], subln, row0=0, n_batch=bp, seq=lp,
                                                lam_init=lam_init)
            mix = _diff_attention(proj, mix, da_lambda[e], subln, row0=n_p, n_batch=bs, seq=ls, lam_init=lam_init,
                                  rope_tabs=rope_tabs, ctx_k=cache_k, ctx_v=cache_v, layer_e=e)
            mix, s_p = _hgrn(proj, mix, hg_lb_h, hgn, row0=0, n_batch=bp, seq=lp, layer=l)
            mix, _ = _hgrn(proj, mix, hg_lb_h, hgn, row0=n_p, n_batch=bs, seq=ls, layer=l, s0=state_hgrn[:, e])
            w_out = w_out_even[e].astype(BF16)
            new_k.append(k_new)
            new_v.append(v_new)
            new_s.append(s_p)
        else:
            o = l // 2
            mix = _odd_mixer(x, m, g1, w_in_odd[o].astype(BF16), sgu_w[o].astype(BF16), sgu_b[o].T, mod_spec_d)
            w_out = w_out_odd[o].astype(BF16)
        x, h2, logits_t = _post(x, n_p, mix, m, g2, w_out, router_w[l].T, router_b[l].reshape(N_EXPERTS, 1),
                                mod_spec_d)
        x = _moe(x, h2, logits_t, m, final_g, ex_w1, ex_b1, ex_w2, ex_b2, mod_spec, n_p, layer=l,
                 final=(l == depth - 1))

    y_prompt, y_sample = x
    return (y_prompt.reshape(bp, lp, D_MODEL), y_sample.reshape(bs, ls, D_MODEL),
            jnp.stack(new_k, axis=1), jnp.stack(new_v, axis=1), jnp.stack(new_s, axis=1))
```
